```python
import math
import jax, jax.numpy as jnp
from jax import lax
import numpy as np

D_MODEL = 4096
BATCH = 2
SEQ = 4096
DEPTH = 1

NSA_HEADS = 16
NSA_GROUPS = 2
NSA_HPG = NSA_HEADS // NSA_GROUPS
HEAD_DIM = 128
ROT_DIM = HEAD_DIM // 4
ROPE_THETA = 500000.0
CMP_LEN = 32
CMP_STRIDE = 16
CMP_HIDDEN = 256
SLC_LEN = 64
SLC_TOPK = 16
WINDOW = 512
Q_BLOCK = 128
NSA_WIDTH = NSA_HEADS * HEAD_DIM
KV_WIDTH = NSA_GROUPS * HEAD_DIM
DN_HEADS = 16
DN_DK = 128
DN_DV = 128
DN_WIDTH = DN_HEADS * DN_DV
CONV_WIDTH = 4
DN_CHUNK = 64
EPS = 1e-6

IN_WIDTHS = (
    NSA_WIDTH,
    6 * KV_WIDTH,
    3 * NSA_HEADS,
    NSA_WIDTH,
    3 * DN_WIDTH,
    DN_HEADS,
    DN_HEADS,
    DN_WIDTH,
    2 * D_MODEL,
)
D_IN = sum(IN_WIDTHS)

kernel_name = "hybrid_nsa_gated_deltanet_block"


def rms_norm(x, gain):
    x32 = x.astype(jnp.float32)
    y = x32 * lax.rsqrt(jnp.mean(x32 * x32, axis=-1, keepdims=True) + EPS)
    return (y * gain.astype(jnp.float32)).astype(x.dtype)


def l2_norm(x):
    x32 = x.astype(jnp.float32)
    return (x32 * lax.rsqrt(jnp.sum(x32 * x32, axis=-1, keepdims=True) + EPS)).astype(x.dtype)


def partial_rope(x, pos):
    half = ROT_DIM // 2
    inv_freq = ROPE_THETA ** (-jnp.arange(half, dtype=jnp.float32) / half)
    ang = pos.astype(jnp.float32)[..., None, None] * inv_freq
    cos, sin = jnp.cos(ang), jnp.sin(ang)
    xr = x[..., :ROT_DIM].astype(jnp.float32)
    x1, x2 = xr[..., :half], xr[..., half:]
    rot = jnp.concatenate([x1 * cos - x2 * sin, x2 * cos + x1 * sin], axis=-1).astype(x.dtype)
    return jnp.concatenate([rot, x[..., ROT_DIM:]], axis=-1)


def masked_softmax(s, mask):
    s = jnp.where(mask, s.astype(jnp.float32), -jnp.inf)
    m = jnp.max(s, axis=-1, keepdims=True)
    m = jnp.where(jnp.isfinite(m), m, 0.0)
    p = jnp.exp(s - m)
    return p / jnp.maximum(jnp.sum(p, axis=-1, keepdims=True), 1e-30)


def compress_blocks(kv, pos_emb, w1, w2):
    B, S, G, DH = kv.shape
    n_cmp = (S - CMP_LEN) // CMP_STRIDE + 1
    idx = np.arange(n_cmp)[:, None] * CMP_STRIDE + np.arange(CMP_LEN)[None, :]
    blocks = kv[:, idx] + pos_emb[None, None, :, None, :]
    blocks = blocks.transpose(0, 1, 3, 2, 4).reshape(B, n_cmp, G, CMP_LEN * DH)
    return jax.nn.silu(blocks @ w1) @ w2


def nsa_attention(q, k_cmp, v_cmp, k_slc, v_slc, k_win, v_win, gates):
    B, S, H, DH = q.shape
    G, P = NSA_GROUPS, NSA_HPG
    n_cmp = k_cmp.shape[1]
    nb = S // SLC_LEN
    topk = min(SLC_TOPK, nb)
    cmp_start = np.arange(n_cmp) * CMP_STRIDE
    cmp_end = jnp.asarray(cmp_start + CMP_LEN - 1, dtype=jnp.int32)
    slc_start = np.arange(nb) * SLC_LEN
    overlap = jnp.asarray(((cmp_start[:, None] <= slc_start[None, :] + SLC_LEN - 1)
                           & (cmp_start[:, None] + CMP_LEN - 1 >= slc_start[None, :])).astype(np.float32))
    ks_blocks = k_slc.reshape(B, nb, SLC_LEN, G, DH).transpose(0, 3, 1, 2, 4)
    vs_blocks = v_slc.reshape(B, nb, SLC_LEN, G, DH).transpose(0, 3, 1, 2, 4)
    kw_pad = jnp.pad(k_win, ((0, 0), (WINDOW, 0), (0, 0), (0, 0)))
    vw_pad = jnp.pad(v_win, ((0, 0), (WINDOW, 0), (0, 0), (0, 0)))
    bi = jnp.arange(B)[:, None, None, None]
    gi = jnp.arange(G)[None, :, None, None]
    jblk = jnp.arange(nb)
    scale = HEAD_DIM ** -0.5
    dt = q.dtype

    def block_fn(qs):
        t = qs + jnp.arange(Q_BLOCK)
        qb = lax.dynamic_slice_in_dim(q, qs, Q_BLOCK, 1)
        qb = qb.reshape(B, Q_BLOCK, G, P, DH).transpose(0, 2, 3, 1, 4) * scale
        s_c = jnp.einsum('bgptd,bcgd->bgptc', qb, k_cmp)
        p_c = masked_softmax(s_c, cmp_end[None, :] <= t[:, None])
        o_c = jnp.einsum('bgptc,bcgd->bgptd', p_c.astype(dt), v_cmp)
        imp = jnp.einsum('bgptc,cj->bgtj', p_c, overlap)
        tb = t // SLC_LEN
        valid = jblk[None, :] * SLC_LEN <= t[:, None]
        forced = (jblk[None, :] == 0) | (jblk[None, :] == tb[:, None]) | (jblk[None, :] == tb[:, None] - 1)
        score = jnp.where(forced, 1e9, jnp.where(valid, imp, -jnp.inf))
        _, sel = lax.top_k(score, topk)
        ks = ks_blocks[bi, gi, sel]
        vs = vs_blocks[bi, gi, sel]
        tok = sel[..., None] * SLC_LEN + jnp.arange(SLC_LEN)
        mask_s = (tok <= t[None, None, :, None, None]).reshape(B, G, 1, Q_BLOCK, topk * SLC_LEN)
        s_s = jnp.einsum('bgptd,bgtkld->bgptkl', qb, ks).reshape(B, G, P, Q_BLOCK, topk * SLC_LEN)
        p_s = masked_softmax(s_s, mask_s).reshape(B, G, P, Q_BLOCK, topk, SLC_LEN)
        o_s = jnp.einsum('bgptkl,bgtkld->bgptd', p_s.astype(dt), vs)
        kw = lax.dynamic_slice_in_dim(kw_pad, qs, WINDOW + Q_BLOCK, 1)
        vw = lax.dynamic_slice_in_dim(vw_pad, qs, WINDOW + Q_BLOCK, 1)
        s_idx = qs - WINDOW + jnp.arange(WINDOW + Q_BLOCK)
        mask_w = (s_idx[None, :] >= 0) & (s_idx[None, :] <= t[:, None]) & (t[:, None] - s_idx[None, :] < WINDOW)
        s_w = jnp.einsum('bgptd,bsgd->bgpts', qb, kw)
        p_w = masked_softmax(s_w, mask_w)
        o_w = jnp.einsum('bgpts,bsgd->bgptd', p_w.astype(dt), vw)
        gb = lax.dynamic_slice_in_dim(gates, qs, Q_BLOCK, 1)
        gb = gb.reshape(B, Q_BLOCK, 3, G, P).transpose(2, 0, 3, 4, 1)[..., None]
        o = gb[0] * o_c + gb[1] * o_s + gb[2] * o_w
        return o.transpose(0, 3, 1, 2, 4).reshape(B, Q_BLOCK, H * DH)

    outs = lax.map(block_fn, jnp.arange(S // Q_BLOCK, dtype=jnp.int32) * Q_BLOCK)
    return outs.transpose(1, 0, 2, 3).reshape(B, S, H * DH)


def causal_conv(x, w):
    C = x.shape[-1]
    return lax.conv_general_dilated(x, w[:, None, :].astype(x.dtype), window_strides=(1,),
                                    padding=[(CONV_WIDTH - 1, 0)],
                                    dimension_numbers=('NWC', 'WIO', 'NWC'),
                                    feature_group_count=C)


def gated_delta_rule(q, k, v, g, beta):
    B, S, H, DK = q.shape
    DV = v.shape[-1]
    C = DN_CHUNK
    N = S // C

    def chunk(x):
        return x.reshape(B, N, C, H, x.shape[-1]).transpose(1, 0, 3, 2, 4)

    qc, kc, vc = chunk(q), chunk(k), chunk(v)
    gc = g.reshape(B, N, C, H).transpose(1, 0, 3, 2)
    bc = beta.reshape(B, N, C, H).transpose(1, 0, 3, 2)
    decay = jnp.cumsum(gc, axis=-1)
    tril = jnp.tril(jnp.ones((C, C), dtype=bool))
    strict = jnp.tril(jnp.ones((C, C), dtype=bool), -1)
    lmat = jnp.exp(jnp.where(tril, decay[..., :, None] - decay[..., None, :], -jnp.inf))
    kb = kc * bc[..., None]
    vb = vc * bc[..., None]
    a = jnp.where(strict, jnp.einsum('nbhid,nbhjd->nbhij', kb, kc) * lmat, 0.0)
    eye = jnp.eye(C, dtype=jnp.float32)
    tinv = lax.linalg.triangular_solve(a + eye, jnp.broadcast_to(eye, a.shape), left_side=True,
                                       lower=True, unit_diagonal=True)
    u = tinv @ vb
    w = tinv @ (kb * jnp.exp(decay)[..., None])
    attn = jnp.where(tril, jnp.einsum('nbhid,nbhjd->nbhij', qc, kc) * lmat, 0.0)
    q_dec = qc * jnp.exp(decay)[..., None]
    k_dec = kc * jnp.exp(decay[..., -1:] - decay)[..., None]
    g_last = jnp.exp(decay[..., -1])

    def step(state, xs):
        u_i, w_i, attn_i, qd_i, kd_i, gl_i = xs
        v_new = u_i - w_i @ state
        o_i = qd_i @ state + attn_i @ v_new
        state = state * gl_i[..., None, None] + jnp.swapaxes(kd_i, -1, -2) @ v_new
        return state, o_i

    s0 = jnp.zeros((B, H, DK, DV), dtype=jnp.float32)
    _, o = lax.scan(step, s0, (u, w, attn, q_dec, k_dec, g_last))
    return o.transpose(1, 0, 3, 2, 4).reshape(B, S, H, DV)


def setup_inputs(seed: int = 0) -> dict:
    key = jax.random.key(seed)
    ks = jax.random.split(key, 24)
    f32 = jnp.float32
    L, D = DEPTH, D_MODEL

    def nrm(k, shape, scale):
        return jax.random.normal(k, shape, f32) * scale

    dt = jnp.exp(jax.random.uniform(ks[13], (L, DN_HEADS), f32, math.log(1e-3), math.log(1e-1)))
    return {
        "x": nrm(ks[0], (BATCH, SEQ, D), 1.0),
        "c": nrm(ks[1], (BATCH, D), 1.0),
        "positions": jnp.tile(jnp.arange(SEQ, dtype=jnp.int32)[None, :], (BATCH, 1)),
        "w_ada": nrm(ks[2], (L, D, 3 * D), D ** -0.5),
        "b_ada": nrm(ks[3], (L, 3 * D), 0.01),
        "norm_gain": 1.0 + nrm(ks[4], (L, D), 0.02),
        "w_in": nrm(ks[5], (L, D, D_IN), D ** -0.5),
        "cmp_pos_k": nrm(ks[6], (L, CMP_LEN, HEAD_DIM), 0.02),
        "cmp_pos_v": nrm(ks[7], (L, CMP_LEN, HEAD_DIM), 0.02),
        "w_cmp_k1": nrm(ks[8], (L, CMP_LEN * HEAD_DIM, CMP_HIDDEN), (CMP_LEN * HEAD_DIM) ** -0.5),
        "w_cmp_k2": nrm(ks[9], (L, CMP_HIDDEN, HEAD_DIM), CMP_HIDDEN ** -0.5),
        "w_cmp_v1": nrm(ks[10], (L, CMP_LEN * HEAD_DIM, CMP_HIDDEN), (CMP_LEN * HEAD_DIM) ** -0.5),
        "w_cmp_v2": nrm(ks[11], (L, CMP_HIDDEN, HEAD_DIM), CMP_HIDDEN ** -0.5),
        "conv_w": nrm(ks[12], (L, CONV_WIDTH, 3 * DN_WIDTH), CONV_WIDTH ** -0.5),
        "dt_bias": jnp.log(jnp.expm1(dt)),
        "a_log": jnp.log(jax.random.uniform(ks[14], (L, DN_HEADS), f32, 1.0, 16.0)),
        "dn_norm_gain": 1.0 + nrm(ks[15], (L, DN_DV), 0.02),
        "w_proj_a": nrm(ks[16], (L, NSA_WIDTH, D), NSA_WIDTH ** -0.5),
        "w_proj_b": nrm(ks[17], (L, DN_WIDTH, D), DN_WIDTH ** -0.5),
        "w_out": nrm(ks[18], (L, D, D), D ** -0.5),
        "final_gain": 1.0 + nrm(ks[19], (D,), 0.02),
    }


def reference(x, c, positions, w_ada, b_ada, norm_gain, w_in, cmp_pos_k, cmp_pos_v,
              w_cmp_k1, w_cmp_k2, w_cmp_v1, w_cmp_v2, conv_w, dt_bias, a_log, dn_norm_gain,
              w_proj_a, w_proj_b, w_out, final_gain):
    B, S, D = x.shape
    split_at = np.cumsum(IN_WIDTHS)[:-1].tolist()
    n_cmp = (S - CMP_LEN) // CMP_STRIDE + 1
    cmp_end_idx = np.arange(n_cmp) * CMP_STRIDE + CMP_LEN - 1
    for l in range(DEPTH):
        mod = c @ w_ada[l] + b_ada[l]
        shift, scale, gate = jnp.split(mod, 3, axis=-1)
        h = rms_norm(x, norm_gain[l]) * (1.0 + scale[:, None, :]) + shift[:, None, :]
        proj = h @ w_in[l]
        (nsa_q, nsa_kv, nsa_g, nsa_z, dn_qkv, dn_a, dn_b, dn_z, merge_g) = jnp.split(proj, split_at, axis=-1)

        q = partial_rope(nsa_q.reshape(B, S, NSA_HEADS, HEAD_DIM), positions)
        k_c, v_c, k_s, v_s, k_w, v_w = [t.reshape(B, S, NSA_GROUPS, HEAD_DIM) for t in jnp.split(nsa_kv, 6, axis=-1)]
        k_s = partial_rope(k_s, positions)
        k_w = partial_rope(k_w, positions)
        kc = compress_blocks(k_c, cmp_pos_k[l], w_cmp_k1[l], w_cmp_k2[l])
        vc = compress_blocks(v_c, cmp_pos_v[l], w_cmp_v1[l], w_cmp_v2[l])
        kc = partial_rope(kc, positions[:, cmp_end_idx])
        branch_gates = jax.nn.sigmoid(nsa_g.reshape(B, S, 3, NSA_HEADS))
        o_a = nsa_attention(q, kc, vc, k_s, v_s, k_w, v_w, branch_gates) * jax.nn.silu(nsa_z)
        out_a = o_a @ w_proj_a[l]

        qkv = jax.nn.silu(causal_conv(dn_qkv, conv_w[l]))
        dq, dk, dv = jnp.split(qkv, 3, axis=-1)
        dq = l2_norm(dq.reshape(B, S, DN_HEADS, DN_DK)) * (DN_DK ** -0.5)
        dk = l2_norm(dk.reshape(B, S, DN_HEADS, DN_DK))
        dv = dv.reshape(B, S, DN_HEADS, DN_DV)
        g = -jnp.exp(a_log[l].astype(jnp.float32)) * jax.nn.softplus(dn_a.astype(jnp.float32) + dt_bias[l].astype(jnp.float32))
        beta = jax.nn.sigmoid(dn_b.astype(jnp.float32))
        o_b = gated_delta_rule(dq.astype(jnp.float32), dk.astype(jnp.float32), dv.astype(jnp.float32), g, beta)
        o_b = rms_norm(o_b.astype(x.dtype), dn_norm_gain[l]).reshape(B, S, DN_WIDTH) * jax.nn.silu(dn_z)
        out_b = o_b @ w_proj_b[l]

        gate_a, gate_b = jnp.split(jax.nn.sigmoid(merge_g), 2, axis=-1)
        mixed = (gate_a * out_a + gate_b * out_b) @ w_out[l]
        x = x + gate[:, None, :] * mixed
    return rms_norm(x, final_gain)
```

```python
import functools
import math

import numpy as np
import jax
import jax.numpy as jnp
from jax import lax
from jax.experimental import pallas as pl
from jax.experimental.pallas import tpu as pltpu

F32 = jnp.float32
BF16 = jnp.bfloat16

NSA_HEADS = 16
NSA_GROUPS = 2
NSA_HPG = NSA_HEADS // NSA_GROUPS
HEAD_DIM = 128
ROT_DIM = HEAD_DIM // 4
ROT_HALF = ROT_DIM // 2
ROPE_THETA = 500000.0
CMP_LEN = 32
CMP_STRIDE = 16
CMP_HIDDEN = 256
SLC_LEN = 64
SLC_TOPK = 16
WINDOW = 512
Q_BLOCK = 128
NSA_WIDTH = NSA_HEADS * HEAD_DIM
KV_WIDTH = NSA_GROUPS * HEAD_DIM
DN_HEADS = 16
DN_DK = 128
DN_DV = 128
DN_WIDTH = DN_HEADS * DN_DV
CONV_WIDTH = 4
EPS = 1e-6

LANES = 128
VMEM_LIMIT_BYTES = 56 * 1024 * 1024

DN_CHUNK = 128
SEL_TILE = 512
NEG = -1e30

TAIL_WIDTH = 1024 + 512 + 256 + 128


def _cparams(sem):
    return pltpu.CompilerParams(dimension_semantics=sem, vmem_limit_bytes=VMEM_LIMIT_BYTES)


def _sigmoid(x):
    return 1.0 / (1.0 + jnp.exp(-x))


def _silu(x):
    return x * _sigmoid(x)


def _dot(a, b):
    return jnp.dot(a, b, preferred_element_type=F32)


def _dot_nt(a, b):
    return lax.dot_general(a, b, (((1,), (1,)), ((), ())), preferred_element_type=F32)


def _dot_f32(a, b):
    return jnp.dot(a, b, preferred_element_type=F32, precision=lax.Precision.HIGHEST)


def _ada_kernel(c_ref, w_ref, b_ref, o_ref):
    c = c_ref[...]
    c_hi = c.astype(BF16)
    c_lo = (c - c_hi.astype(F32)).astype(BF16)
    w = w_ref[...]
    w_hi = w.astype(BF16)
    w_lo = (w - w_hi.astype(F32)).astype(BF16)
    acc = _dot(c_hi, w_hi) + _dot(c_lo, w_hi) + _dot(c_hi, w_lo)
    o_ref[...] = acc + b_ref[...]


def _ada(c8, w_ada, b_ada):
    D, N = w_ada.shape
    tn = min(512, N)
    return pl.pallas_call(
        _ada_kernel,
        grid=(N // tn,),
        in_specs=[pl.BlockSpec((8, D), lambda j: (0, 0)),
                  pl.BlockSpec((D, tn), lambda j: (0, j)),
                  pl.BlockSpec((1, tn), lambda j: (0, j))],
        out_specs=pl.BlockSpec((8, tn), lambda j: (0, j)),
        out_shape=jax.ShapeDtypeStruct((8, N), F32),
        compiler_params=_cparams(("arbitrary",)),
        name="ada",
    )(c8, w_ada, b_ada)


def _norm_kernel(x_ref, gain_ref, shift_ref, scale_ref, o_ref):
    x = x_ref[...]
    ms = jnp.mean(x * x, axis=-1, keepdims=True)
    y = x * lax.rsqrt(ms + EPS) * gain_ref[...]
    o_ref[...] = (y * (1.0 + scale_ref[0]) + shift_ref[0]).astype(o_ref.dtype)


def _norm(x2, gain, mod3, S):
    M, D = x2.shape
    tm = 256
    nb = S // tm
    return pl.pallas_call(
        _norm_kernel,
        grid=(M // tm,),
        in_specs=[pl.BlockSpec((tm, D), lambda i: (i, 0)),
                  pl.BlockSpec((1, D), lambda i: (0, 0)),
                  pl.BlockSpec((1, 1, D), lambda i: (i // nb, 0, 0)),
                  pl.BlockSpec((1, 1, D), lambda i: (i // nb, 0, 1))],
        out_specs=pl.BlockSpec((tm, D), lambda i: (i, 0)),
        out_shape=jax.ShapeDtypeStruct((M, D), BF16),
        compiler_params=_cparams(("arbitrary",)),
        name="norm",
    )(x2, gain, mod3, mod3)


def _mm_kernel(a_ref, w_ref, o_ref):
    o_ref[...] = _dot(a_ref[...], w_ref[...]).astype(o_ref.dtype)


def _matmul(a, w, out_dtype, tm, tn, name):
    M, K = a.shape
    N = w.shape[1]
    return pl.pallas_call(
        _mm_kernel,
        grid=(N // tn, M // tm),
        in_specs=[pl.BlockSpec((tm, K), lambda j, i: (i, 0)),
                  pl.BlockSpec((K, tn), lambda j, i: (0, j))],
        out_specs=pl.BlockSpec((tm, tn), lambda j, i: (i, j)),
        out_shape=jax.ShapeDtypeStruct((M, N), out_dtype),
        compiler_params=_cparams(("arbitrary", "arbitrary")),
        name=name,
    )(a, w)


def _rope_consts():
    inv = ROPE_THETA ** (-np.arange(ROT_HALF, dtype=np.float64) / ROT_HALF)
    invf = np.zeros((1, LANES), np.float32)
    invf[0, :ROT_HALF] = inv
    invf[0, ROT_HALF:ROT_DIM] = inv
    sgn = np.zeros((1, LANES), np.float32)
    sgn[0, :ROT_HALF] = -1.0
    sgn[0, ROT_HALF:ROT_DIM] = 1.0
    return jnp.asarray(invf), jnp.asarray(sgn)


def _rope_tables(pos_col, invf, sgn):
    ang = pos_col.astype(F32) * invf
    return jnp.cos(ang), jnp.sin(ang) * sgn


def _rope_apply(x, cos_t, sin_t):
    lane = lax.broadcasted_iota(jnp.int32, x.shape, 1)
    partner = jnp.where(lane < ROT_HALF,
                        pltpu.roll(x, LANES - ROT_HALF, 1),
                        pltpu.roll(x, ROT_HALF, 1))
    return x * cos_t + partner * sin_t


def _nsa_prep_kernel(q_ref, kv_ref, pos_ref, invf_ref, sgn_ref, qo_ref, kvo_ref):
    cos_t, sin_t = _rope_tables(pos_ref[...], invf_ref[...], sgn_ref[...])
    qscale = HEAD_DIM ** -0.5
    for h in range(NSA_HEADS):
        sl = slice(h * HEAD_DIM, (h + 1) * HEAD_DIM)
        xq = q_ref[:, sl].astype(F32)
        qo_ref[:, sl] = (_rope_apply(xq, cos_t, sin_t) * qscale).astype(qo_ref.dtype)
    for blk in range(8):
        sl = slice(blk * HEAD_DIM, (blk + 1) * HEAD_DIM)
        xk = kv_ref[:, sl]
        if blk in (0, 1, 4, 5):
            xk = _rope_apply(xk, cos_t, sin_t)
        kvo_ref[:, sl] = xk.astype(kvo_ref.dtype)


def _nsa_prep(pbig, ptail, pos_col, q_col_block):
    M = pbig.shape[0]
    tm = 256
    invf, sgn = _rope_consts()
    return pl.pallas_call(
        _nsa_prep_kernel,
        grid=(M // tm,),
        in_specs=[pl.BlockSpec((tm, NSA_WIDTH), lambda i: (i, q_col_block)),
                  pl.BlockSpec((tm, 1024), lambda i: (i, 0)),
                  pl.BlockSpec((tm, 1), lambda i: (i, 0)),
                  pl.BlockSpec((1, LANES), lambda i: (0, 0)),
                  pl.BlockSpec((1, LANES), lambda i: (0, 0))],
        out_specs=[pl.BlockSpec((tm, NSA_WIDTH), lambda i: (i, 0)),
                   pl.BlockSpec((tm, 1024), lambda i: (i, 0))],
        out_shape=[jax.ShapeDtypeStruct((M, NSA_WIDTH), BF16),
                   jax.ShapeDtypeStruct((M, 1024), BF16)],
        compiler_params=_cparams(("arbitrary",)),
        name="nsa_prep",
    )(pbig, ptail, pos_col, invf, sgn)


def _compress_kernel(x_ref, pe_ref, w1_ref, w2_ref, pos_ref, invf_ref, sgn_ref, o_ref, *, rope, n_rows):
    half = CMP_LEN // 2
    top = jnp.zeros((n_rows, CMP_HIDDEN), F32)
    bot = jnp.zeros((n_rows, CMP_HIDDEN), F32)
    for l in range(half):
        xl = x_ref[pl.ds(l, n_rows, stride=CMP_STRIDE), :]
        w_top = w1_ref[l * HEAD_DIM:(l + 1) * HEAD_DIM, :].astype(BF16)
        w_bot = w1_ref[(half + l) * HEAD_DIM:(half + l + 1) * HEAD_DIM, :].astype(BF16)
        top = top + _dot((xl + pe_ref[l:l + 1, :]).astype(BF16), w_top)
        bot = bot + _dot((xl + pe_ref[half + l:half + l + 1, :]).astype(BF16), w_bot)
    hid = top + pltpu.roll(bot, n_rows - 1, 0)
    out = _dot(_silu(hid).astype(BF16), w2_ref[...].astype(BF16))
    if rope:
        cos_t, sin_t = _rope_tables(pos_ref[0], invf_ref[...], sgn_ref[...])
        out = _rope_apply(out, cos_t, sin_t)
    o_ref[0, 0] = out.astype(o_ref.dtype)


def _compress(ptail, col_block0, pe, w1, w2, pos_cmp, B, S, rope):
    n_rows = S // CMP_STRIDE
    invf, sgn = _rope_consts()
    kern = functools.partial(_compress_kernel, rope=rope, n_rows=n_rows)
    return pl.pallas_call(
        kern,
        grid=(B, NSA_GROUPS),
        in_specs=[pl.BlockSpec((S, HEAD_DIM), lambda b, g: (b, col_block0 + g)),
                  pl.BlockSpec((CMP_LEN, HEAD_DIM), lambda b, g: (0, 0)),
                  pl.BlockSpec((CMP_LEN * HEAD_DIM, CMP_HIDDEN), lambda b, g: (0, 0)),
                  pl.BlockSpec((CMP_HIDDEN, HEAD_DIM), lambda b, g: (0, 0)),
                  pl.BlockSpec((1, n_rows, 1), lambda b, g: (b, 0, 0)),
                  pl.BlockSpec((1, LANES), lambda b, g: (0, 0)),
                  pl.BlockSpec((1, LANES), lambda b, g: (0, 0))],
        out_specs=pl.BlockSpec((1, 1, n_rows, HEAD_DIM), lambda b, g: (b, g, 0, 0)),
        out_shape=jax.ShapeDtypeStruct((B, NSA_GROUPS, n_rows, HEAD_DIM), BF16),
        compiler_params=_cparams(("arbitrary", "arbitrary")),
        name="compress_k" if rope else "compress_v",
    )(ptail, pe, w1, w2, pos_cmp, invf, sgn)


def _row_softmax_stats(s):
    m = jnp.max(s, axis=-1, keepdims=True)
    p = jnp.exp(s - m)
    l = jnp.sum(p, axis=-1, keepdims=True)
    return p, l


def _nsa_attn_kernel(q_ref, kc_ref, vc_ref, ks_ref, vs_ref, kw_ref, vw_ref, g_ref, z_ref,
                     ov_ref, e_ref, o_ref, m_s, l_s, acc_s, *, seq_len):
    T = Q_BLOCK
    P = NSA_HPG
    R = P * T
    n_cmp_pad = seq_len // CMP_STRIDE
    n_cmp = n_cmp_pad - 1
    qi = pl.program_id(2)
    qs = qi * T

    q = q_ref[...]
    qb = jnp.concatenate([q[:, p * HEAD_DIM:(p + 1) * HEAD_DIM] for p in range(P)], axis=0)

    kc = kc_ref[0, 0]
    vc = vc_ref[0, 0]
    s_c = _dot_nt(qb, kc)
    t_c = qs + (lax.broadcasted_iota(jnp.int32, s_c.shape, 0) & (T - 1))
    c_i = lax.broadcasted_iota(jnp.int32, s_c.shape, 1)
    mask_c = (c_i * CMP_STRIDE + (CMP_LEN - 1) <= t_c) & (c_i < n_cmp)
    s_cm = jnp.where(mask_c, s_c, NEG)
    m_c = jnp.max(s_cm, axis=-1, keepdims=True)
    m_c = jnp.where(m_c > 0.5 * NEG, m_c, 0.0)
    p_c = jnp.where(mask_c, jnp.exp(s_c - m_c), 0.0)
    l_c = jnp.sum(p_c, axis=-1, keepdims=True)
    p_c = p_c * (1.0 / jnp.maximum(l_c, 1e-30))
    o_c = _dot(p_c.astype(BF16), vc)

    p_sum = p_c[0:T]
    for p in range(1, P):
        p_sum = p_sum + p_c[p * T:(p + 1) * T]
    imp = _dot_f32(p_sum, ov_ref[...])
    nb = seq_len // SLC_LEN
    t_s = qs + lax.broadcasted_iota(jnp.int32, (T, LANES), 0)
    j_s = lax.broadcasted_iota(jnp.int32, (T, LANES), 1)
    tb = t_s // SLC_LEN
    valid = (j_s * SLC_LEN <= t_s) & (j_s < nb)
    forced = (j_s == 0) | (j_s == tb) | (j_s == tb - 1)
    score = jnp.where(forced, 1e9, jnp.where(valid, imp, -jnp.inf))
    score_t = score.T
    jrow = lax.broadcasted_iota(jnp.int32, (LANES, T), 0)
    cnt = jnp.zeros((LANES, T), F32)
    for k in range(nb):
        rk = score_t[k:k + 1, :]
        beats = (rk > score_t) | ((rk == score_t) & (k < jrow))
        cnt = cnt + jnp.where(beats, 1.0, 0.0)
    sel_t = jnp.where((cnt < float(min(SLC_TOPK, nb))) & (jrow < nb), 1.0, 0.0)
    sel = sel_t.T.astype(BF16)

    m_s[...] = jnp.full(m_s.shape, NEG, F32)
    l_s[...] = jnp.zeros(l_s.shape, F32)
    acc_s[...] = jnp.zeros(acc_s.shape, F32)
    n_tiles = (qs + T + SEL_TILE - 1) // SEL_TILE
    reps = SEL_TILE // LANES

    def sel_body(kt, carry):
        k0 = pl.multiple_of(kt * SEL_TILE, SEL_TILE)
        k_t = ks_ref[pl.ds(k0, SEL_TILE), :]
        v_t = vs_ref[pl.ds(k0, SEL_TILE), :]
        s = _dot_nt(qb, k_t)
        selmask = _dot(sel, e_ref[:, pl.ds(k0, SEL_TILE)])
        key = k0 + lax.broadcasted_iota(jnp.int32, (T, SEL_TILE), 1)
        tq = qs + lax.broadcasted_iota(jnp.int32, (T, SEL_TILE), 0)
        ok = (selmask > 0.5) & (key <= tq)
        s = jnp.where(ok[None], s.reshape(P, T, SEL_TILE), NEG).reshape(R, SEL_TILE)
        m_prev = m_s[...]
        m_new = jnp.maximum(m_prev, jnp.max(s, axis=-1, keepdims=True))
        alpha = jnp.exp(m_prev - m_new)
        p = jnp.exp(s - jnp.concatenate([m_new] * reps, axis=1))
        l_s[...] = alpha * l_s[...] + jnp.sum(p, axis=-1, keepdims=True)
        acc_s[...] = alpha * acc_s[...] + _dot(p.astype(BF16), v_t)
        m_s[...] = m_new
        return carry

    lax.fori_loop(0, n_tiles, sel_body, 0)
    o_s = acc_s[...] * (1.0 / l_s[...])

    wlen = WINDOW + T
    w0 = pl.multiple_of(jnp.maximum(qs - WINDOW, 0), T)
    k_w = kw_ref[pl.ds(w0, wlen), :]
    v_w = vw_ref[pl.ds(w0, wlen), :]
    s_w = _dot_nt(qb, k_w)
    key_w = w0 + lax.broadcasted_iota(jnp.int32, (T, wlen), 1)
    tq_w = qs + lax.broadcasted_iota(jnp.int32, (T, wlen), 0)
    ok_w = (key_w <= tq_w) & (tq_w - key_w < WINDOW)
    s_w = jnp.where(ok_w[None], s_w.reshape(P, T, wlen), NEG).reshape(R, wlen)
    p_w, l_w = _row_softmax_stats(s_w)
    o_w = _dot(p_w.astype(BF16), v_w) * (1.0 / l_w)

    gates = _sigmoid(g_ref[...])
    for p in range(P):
        rows = slice(p * T, (p + 1) * T)
        g0 = gates[:, 0 * P + p:0 * P + p + 1]
        g1 = gates[:, 1 * P + p:1 * P + p + 1]
        g2 = gates[:, 2 * P + p:2 * P + p + 1]
        o_p = g0 * o_c[rows] + g1 * o_s[rows] + g2 * o_w[rows]
        sl = slice(p * HEAD_DIM, (p + 1) * HEAD_DIM)
        o_ref[:, sl] = (o_p * _silu(z_ref[:, sl].astype(F32))).astype(o_ref.dtype)


def _nsa_consts(S):
    n_cmp_pad = S // CMP_STRIDE
    nb = S // SLC_LEN
    cmp_start = np.arange(n_cmp_pad) * CMP_STRIDE
    slc_start = np.arange(LANES) * SLC_LEN
    ov = ((cmp_start[:, None] <= slc_start[None, :] + SLC_LEN - 1)
          & (cmp_start[:, None] + CMP_LEN - 1 >= slc_start[None, :])
          & (np.arange(LANES)[None, :] < nb)
          & (np.arange(n_cmp_pad)[:, None] < n_cmp_pad - 1)).astype(np.float32)
    e = (np.arange(S)[None, :] // SLC_LEN == np.arange(LANES)[:, None]).astype(np.float32)
    return jnp.asarray(ov), jnp.asarray(e, dtype=BF16)


def _nsa_attn(q_r, kcmp, vcmp, kv16, ptail, pbig, z_col_block, B, S):
    M = B * S
    nq = S // Q_BLOCK
    gw = NSA_HPG * HEAD_DIM
    ov, e = _nsa_consts(S)
    n_cmp_pad = S // CMP_STRIDE
    R = NSA_HPG * Q_BLOCK
    kern = functools.partial(_nsa_attn_kernel, seq_len=S)
    kv_spec = lambda c: pl.BlockSpec((S, HEAD_DIM), lambda b, g, i: (b, c + g))
    return pl.pallas_call(
        kern,
        grid=(B, NSA_GROUPS, nq),
        in_specs=[pl.BlockSpec((Q_BLOCK, gw), lambda b, g, i: (b * nq + i, g)),
                  pl.BlockSpec((1, 1, n_cmp_pad, HEAD_DIM), lambda b, g, i: (b, g, 0, 0)),
                  pl.BlockSpec((1, 1, n_cmp_pad, HEAD_DIM), lambda b, g, i: (b, g, 0, 0)),
                  kv_spec(0), kv_spec(2), kv_spec(4), kv_spec(6),
                  pl.BlockSpec((Q_BLOCK, LANES), lambda b, g, i: (b * nq + i, 12 + g)),
                  pl.BlockSpec((Q_BLOCK, gw), lambda b, g, i: (b * nq + i, z_col_block * NSA_GROUPS + g)),
                  pl.BlockSpec((n_cmp_pad, LANES), lambda b, g, i: (0, 0)),
                  pl.BlockSpec((LANES, S), lambda b, g, i: (0, 0))],
        out_specs=pl.BlockSpec((Q_BLOCK, gw), lambda b, g, i: (b * nq + i, g)),
        out_shape=jax.ShapeDtypeStruct((M, NSA_WIDTH), BF16),
        scratch_shapes=[pltpu.VMEM((R, LANES), F32),
                        pltpu.VMEM((R, LANES), F32),
                        pltpu.VMEM((R, HEAD_DIM), F32)],
        compiler_params=_cparams(("arbitrary", "arbitrary", "arbitrary")),
        name="nsa_attn",
    )(q_r, kcmp, vcmp, kv16, kv16, kv16, kv16, ptail, pbig, ov, e)


def _dn_prep_kernel(x_ref, w_ref, o_ref, *rest, mode, seq_len):
    x = x_ref[...].astype(F32)
    w = w_ref[...]
    row = lax.broadcasted_iota(jnp.int32, x.shape, 0)
    y = x * w[CONV_WIDTH - 1:CONV_WIDTH, :]
    for k in range(1, CONV_WIDTH):
        xs = jnp.where(row >= k, pltpu.roll(x, k, 0), 0.0)
        y = y + xs * w[CONV_WIDTH - 1 - k:CONV_WIDTH - k, :]
    y = _silu(y)
    if mode in ("q", "k"):
        outs = []
        for h in range(x.shape[1] // DN_DK):
            yh = y[:, h * DN_DK:(h + 1) * DN_DK]
            ss = jnp.sum(yh * yh, axis=-1, keepdims=True)
            yh = yh * lax.rsqrt(ss + EPS)
            if mode == "q":
                yh = yh * (DN_DK ** -0.5)
            outs.append(yh)
        y = jnp.concatenate(outs, axis=1)
    o_ref[...] = y.astype(o_ref.dtype)
    if mode == "k":
        kt_ref = rest[0]
        kt_ref[...] = y.T.astype(kt_ref.dtype)


DN_PREP_COLS = 128


def _dn_prep(pbig, conv_w, col0, mode, B, S):
    M = B * S
    tw = DN_PREP_COLS
    col0_blocks = col0 // tw
    nj = DN_WIDTH // tw
    wcol0 = {"q": 0, "k": nj, "v": 2 * nj}[mode]
    kern = functools.partial(_dn_prep_kernel, mode=mode, seq_len=S)
    out_specs = [pl.BlockSpec((S, tw), lambda b, j: (b, j))]
    out_shape = [jax.ShapeDtypeStruct((M, DN_WIDTH), BF16)]
    if mode == "k":
        out_specs.append(pl.BlockSpec((tw, S), lambda b, j: (j, b)))
        out_shape.append(jax.ShapeDtypeStruct((DN_WIDTH, M), BF16))
    return pl.pallas_call(
        kern,
        grid=(B, nj),
        in_specs=[pl.BlockSpec((S, tw), lambda b, j: (b, col0_blocks + j)),
                  pl.BlockSpec((CONV_WIDTH, tw), lambda b, j: (0, wcol0 + j))],
        out_specs=out_specs,
        out_shape=out_shape,
        compiler_params=_cparams(("arbitrary", "arbitrary")),
        name="dn_prep_" + mode,
    )(pbig, conv_w)


def _dn_gate_kernel(ab_ref, alog_ref, dtb_ref, o_ref, ot_ref, *, tm):
    ab = ab_ref[...]
    x = ab + dtb_ref[...]
    softplus = jnp.maximum(x, 0.0) + jnp.log(1.0 + jnp.exp(-jnp.abs(x)))
    g = -jnp.exp(alog_ref[...]) * softplus
    beta = _sigmoid(ab)
    lane = lax.broadcasted_iota(jnp.int32, (DN_CHUNK, LANES), 1)
    r = lax.broadcasted_iota(jnp.int32, (DN_CHUNK, DN_CHUNK), 0)
    c = lax.broadcasted_iota(jnp.int32, (DN_CHUNK, DN_CHUNK), 1)
    tril = jnp.where(r >= c, 1.0, 0.0).astype(F32)
    for ci in range(tm // DN_CHUNK):
        rows = slice(ci * DN_CHUNK, (ci + 1) * DN_CHUNK)
        dec = _dot_f32(tril, g[rows])
        out = jnp.where(lane < DN_HEADS, dec, beta[rows])
        o_ref[rows, :] = out
        ot_ref[:, rows] = out.T


def _dn_gate(ptail, alog_row, dtb_row):
    M = ptail.shape[0]
    tm = 512
    kern = functools.partial(_dn_gate_kernel, tm=tm)
    return pl.pallas_call(
        kern,
        grid=(M // tm,),
        in_specs=[pl.BlockSpec((tm, LANES), lambda i: (i, 14)),
                  pl.BlockSpec((1, LANES), lambda i: (0, 0)),
                  pl.BlockSpec((1, LANES), lambda i: (0, 0))],
        out_specs=[pl.BlockSpec((tm, LANES), lambda i: (i, 0)),
                   pl.BlockSpec((LANES, tm), lambda i: (0, i))],
        out_shape=[jax.ShapeDtypeStruct((M, LANES), F32),
                   jax.ShapeDtypeStruct((LANES, M), F32)],
        compiler_params=_cparams(("arbitrary",)),
        name="dn_gate",
    )(ptail, alog_row, dtb_row)


def _dn_scan_kernel(q_ref, k_ref, v_ref, kt_ref, dec_ref, dect_ref, z_ref, gain_ref, o_ref, st_ref):
    C = DN_CHUNK

    @pl.when(pl.program_id(1) == 0)
    def _():
        st_ref[...] = jnp.zeros(st_ref.shape, F32)

    dec = dec_ref[...]
    dect = dect_ref[...]
    r = lax.broadcasted_iota(jnp.int32, (C, C), 0)
    c = lax.broadcasted_iota(jnp.int32, (C, C), 1)
    tril = r >= c
    strict = r > c
    gain = gain_ref[...]
    eye = jnp.where(r == c, 1.0, 0.0).astype(F32)
    lvl_masks = []
    for lg in range(int(math.log2(C))):
        lvl_masks.append(((r >> (lg + 1)) == (c >> (lg + 1))) & ((r >> lg) != (c >> lg)))

    for h in range(DN_HEADS):
        sl = slice(h * DN_DK, (h + 1) * DN_DK)
        qh = q_ref[:, sl].astype(F32)
        kh = k_ref[:, sl].astype(F32)
        vh = v_ref[:, sl].astype(F32)
        kth = kt_ref[sl, :]
        dcol = jnp.broadcast_to(dec[:, h:h + 1], (C, C))
        bcol = jnp.broadcast_to(dec[:, DN_HEADS + h:DN_HEADS + h + 1], (C, C))
        drow = jnp.broadcast_to(dect[h:h + 1, :], (C, C))
        dlast = dect[h:h + 1, C - 1:C]
        lmat = jnp.exp(jnp.where(tril, dcol - drow, NEG))
        e_d = jnp.exp(dcol)
        kb = kh * bcol
        vb = vh * bcol
        a = jnp.where(strict, _dot(kb.astype(BF16), kth) * lmat, 0.0)
        attn = jnp.where(tril, _dot(qh.astype(BF16), kth) * lmat, 0.0)
        tinv = eye - jnp.where(lvl_masks[0], a, 0.0)
        for lm in lvl_masks[1:]:
            t16 = tinv.astype(BF16)
            lo = jnp.where(lm, a, 0.0).astype(BF16)
            tinv = tinv - _dot(t16, _dot(lo, t16).astype(BF16))
        x = jnp.concatenate([vb, kb * e_d], axis=1)
        x = _dot(tinv.astype(BF16), x.astype(BF16))
        u = x[:, :DN_DV]
        w = x[:, DN_DV:]
        st = st_ref[h]
        st16 = st.astype(BF16)
        v_new = u - _dot(w.astype(BF16), st16)
        v_new16 = v_new.astype(BF16)
        o = _dot((qh * e_d).astype(BF16), st16) + _dot(attn.astype(BF16), v_new16)
        kdt = (kth.astype(F32) * jnp.exp(dlast - drow)).astype(BF16)
        st_ref[h] = st * jnp.exp(dlast) + _dot(kdt, v_new16)
        ms = jnp.mean(o * o, axis=-1, keepdims=True)
        y = o * lax.rsqrt(ms + EPS) * gain
        o_ref[:, sl] = (y * _silu(z_ref[:, sl].astype(F32))).astype(o_ref.dtype)


def _dn_scan(dq, dk, dv, dkt, dec, dect, pbig, z_col_block, gain_row, B, S):
    M = B * S
    C = DN_CHUNK
    nc = S // C
    tok = lambda b, n: (b * nc + n, 0)
    return pl.pallas_call(
        _dn_scan_kernel,
        grid=(B, nc),
        in_specs=[pl.BlockSpec((C, DN_WIDTH), tok),
                  pl.BlockSpec((C, DN_WIDTH), tok),
                  pl.BlockSpec((C, DN_WIDTH), tok),
                  pl.BlockSpec((DN_WIDTH, C), lambda b, n: (0, b * nc + n)),
                  pl.BlockSpec((C, LANES), tok),
                  pl.BlockSpec((LANES, C), lambda b, n: (0, b * nc + n)),
                  pl.BlockSpec((C, DN_WIDTH), lambda b, n: (b * nc + n, z_col_block)),
                  pl.BlockSpec((1, DN_DV), lambda b, n: (0, 0))],
        out_specs=pl.BlockSpec((C, DN_WIDTH), tok),
        out_shape=jax.ShapeDtypeStruct((M, DN_WIDTH), BF16),
        scratch_shapes=[pltpu.VMEM((DN_HEADS, DN_DK, DN_DV), F32)],
        compiler_params=_cparams(("arbitrary", "arbitrary")),
        name="dn_scan",
    )(dq, dk, dv, dkt, dec, dect, pbig, gain_row)


def _out1_kernel(oa_ref, ob_ref, wa_ref, wb_ref, ga_ref, gb_ref, o_ref):
    ya = _dot(oa_ref[...], wa_ref[...])
    yb = _dot(ob_ref[...], wb_ref[...])
    mix = _sigmoid(ga_ref[...].astype(F32)) * ya + _sigmoid(gb_ref[...].astype(F32)) * yb
    o_ref[...] = mix.astype(o_ref.dtype)


def _out1(o_a, o_b, wa, wb, pbig, D):
    M, K = o_a.shape
    tm = min(1024, M)
    tn = min(1024, D)
    nbd = D // tn
    return pl.pallas_call(
        _out1_kernel,
        grid=(D // tn, M // tm),
        in_specs=[pl.BlockSpec((tm, K), lambda j, i: (i, 0)),
                  pl.BlockSpec((tm, K), lambda j, i: (i, 0)),
                  pl.BlockSpec((K, tn), lambda j, i: (0, j)),
                  pl.BlockSpec((K, tn), lambda j, i: (0, j)),
                  pl.BlockSpec((tm, tn), lambda j, i: (i, j)),
                  pl.BlockSpec((tm, tn), lambda j, i: (i, nbd + j))],
        out_specs=pl.BlockSpec((tm, tn), lambda j, i: (i, j)),
        out_shape=jax.ShapeDtypeStruct((M, D), BF16),
        compiler_params=_cparams(("arbitrary", "arbitrary")),
        name="out1",
    )(o_a, o_b, wa, wb, pbig, pbig)


def _final_kernel(mix_ref, x_ref, gate_ref, fg_ref, o_ref):
    xn = x_ref[...] + gate_ref[0] * mix_ref[...].astype(F32)
    ms = jnp.mean(xn * xn, axis=-1, keepdims=True)
    o_ref[...] = xn * lax.rsqrt(ms + EPS) * fg_ref[...]


def _final(mixed, x2, mod3, final_gain, S):
    M, D = x2.shape
    tm = 256
    nb = S // tm
    return pl.pallas_call(
        _final_kernel,
        grid=(M // tm,),
        in_specs=[pl.BlockSpec((tm, D), lambda i: (i, 0)),
                  pl.BlockSpec((tm, D), lambda i: (i, 0)),
                  pl.BlockSpec((1, 1, D), lambda i: (i // nb, 0, 2)),
                  pl.BlockSpec((1, D), lambda i: (0, 0))],
        out_specs=pl.BlockSpec((tm, D), lambda i: (i, 0)),
        out_shape=jax.ShapeDtypeStruct((M, D), F32),
        compiler_params=_cparams(("arbitrary",)),
        name="final",
    )(mixed, x2, mod3, final_gain)


def _pad_cols(w, width):
    return jnp.pad(w, ((0, 0), (0, width - w.shape[1])))


def _permute_w_in(w, D):
    widths = (NSA_WIDTH, 6 * KV_WIDTH, 3 * NSA_HEADS, NSA_WIDTH, 3 * DN_WIDTH, DN_HEADS, DN_HEADS, DN_WIDTH, 2 * D)
    offs = np.concatenate([[0], np.cumsum(widths)])
    seg = lambda i: w[:, int(offs[i]):int(offs[i + 1])]
    q, kv, g, z, dqkv, da, db, dz, mg = [seg(i) for i in range(9)]
    big = jnp.concatenate([mg, dqkv, q, z, dz], axis=1).astype(BF16)
    gcols = []
    for gi in range(NSA_GROUPS):
        idx = np.array([r * NSA_HEADS + gi * NSA_HPG + p for r in range(3) for p in range(NSA_HPG)])
        gcols.append(_pad_cols(g[:, idx], LANES))
    ab = _pad_cols(jnp.concatenate([da, db], axis=1), LANES)
    tail = jnp.concatenate([kv[:, 2 * KV_WIDTH:], kv[:, :2 * KV_WIDTH]] + gcols + [ab], axis=1).astype(BF16)
    return big, tail


def kernel(x, c, positions, w_ada, b_ada, norm_gain, w_in, cmp_pos_k, cmp_pos_v, w_cmp_k1, w_cmp_k2,
           w_cmp_v1, w_cmp_v2, conv_w, dt_bias, a_log, dn_norm_gain, w_proj_a, w_proj_b, w_out, final_gain):
    B, S, D = x.shape
    M = B * S
    depth = w_in.shape[0]
    assert S % SEL_TILE == 0 and S % DN_CHUNK == 0 and S >= WINDOW + Q_BLOCK and D % 512 == 0 and B <= 8
    assert S // SLC_LEN <= LANES and D % 1024 == 0 and DN_DK == DN_CHUNK
    assert depth == 1, "the final RMSNorm is fused into the last layer's output kernel"

    off_dq = 2 * D
    off_q = off_dq + 3 * DN_WIDTH
    off_z = off_q + NSA_WIDTH
    off_dz = off_z + NSA_WIDTH

    x2 = x.reshape(M, D)
    pos_col = positions.reshape(M, 1)
    cmp_end = np.arange(S // CMP_STRIDE - 1) * CMP_STRIDE + CMP_LEN - 1
    pos_cmp = jnp.pad(positions[:, cmp_end], ((0, 0), (0, 1)))[:, :, None]
    c8 = jnp.pad(c, ((0, 8 - B), (0, 0)))

    for l in range(depth):
        mod = _ada(c8, w_ada[l], b_ada[l][None, :])
        mod3 = mod[:B].reshape(B, 1, 3 * D)
        h = _norm(x2, norm_gain[l][None, :], mod3, S)

        w_big, w_tail = _permute_w_in(w_in[l], D)
        tm = min(1024, M)
        pbig = _matmul(h, w_big, BF16, tm, 1024, "proj_big")
        ptail = _matmul(h, w_tail, F32, tm, 640, "proj_tail")

        q_r, kv16 = _nsa_prep(pbig, ptail, pos_col, off_q // NSA_WIDTH)
        kcmp = _compress(ptail, 8, cmp_pos_k[l], w_cmp_k1[l], w_cmp_k2[l], pos_cmp, B, S, True)
        vcmp = _compress(ptail, 10, cmp_pos_v[l], w_cmp_v1[l], w_cmp_v2[l], pos_cmp, B, S, False)
        o_a = _nsa_attn(q_r, kcmp, vcmp, kv16, ptail, pbig, off_z // NSA_WIDTH, B, S)

        cw = conv_w[l]
        dq = _dn_prep(pbig, cw, off_dq, "q", B, S)[0]
        dk, dkt = _dn_prep(pbig, cw, off_dq + DN_WIDTH, "k", B, S)
        dv = _dn_prep(pbig, cw, off_dq + 2 * DN_WIDTH, "v", B, S)[0]
        alog_row = _pad_cols(a_log[l][None, :].astype(F32), LANES)
        dtb_row = _pad_cols(dt_bias[l][None, :].astype(F32), LANES)
        dec, dect = _dn_gate(ptail, alog_row, dtb_row)
        o_b = _dn_scan(dq, dk, dv, dkt, dec, dect, pbig, off_dz // DN_WIDTH,
                       dn_norm_gain[l][None, :], B, S)

        mixed_in = _out1(o_a, o_b, w_proj_a[l].astype(BF16), w_proj_b[l].astype(BF16), pbig, D)
        mixed = _matmul(mixed_in, w_out[l].astype(BF16), BF16, min(1024, M), min(1024, D), "out2")
        x2 = _final(mixed, x2, mod3, final_gain[None, :], S)
    return x2.reshape(B, S, D)
```

```python
import functools
import math

import numpy as np
import jax
import jax.numpy as jnp
from jax import lax
from jax.experimental import pallas as pl
from jax.experimental.pallas import tpu as pltpu

F32 = jnp.float32
BF16 = jnp.bfloat16

NSA_HEADS = 16
NSA_GROUPS = 2
NSA_HPG = NSA_HEADS // NSA_GROUPS
HEAD_DIM = 128
ROT_DIM = HEAD_DIM // 4
ROT_HALF = ROT_DIM // 2
ROPE_THETA = 500000.0
CMP_LEN = 32
CMP_STRIDE = 16
CMP_HIDDEN = 256
SLC_LEN = 64
SLC_TOPK = 16
WINDOW = 512
Q_BLOCK = 128
NSA_WIDTH = NSA_HEADS * HEAD_DIM
KV_WIDTH = NSA_GROUPS * HEAD_DIM
DN_HEADS = 16
DN_DK = 128
DN_DV = 128
DN_WIDTH = DN_HEADS * DN_DV
CONV_WIDTH = 4
EPS = 1e-6

LANES = 128
VMEM_LIMIT_BYTES = 56 * 1024 * 1024

DN_CHUNK = 128
DN_HEAD_GROUP = 8
SEL_TILE = 512
NEG = -1e30

TAIL_WIDTH = 1024 + 512 + 256 + 128


def _cparams(sem):
    return pltpu.CompilerParams(dimension_semantics=sem, vmem_limit_bytes=VMEM_LIMIT_BYTES)


def _sigmoid(x):
    return 1.0 / (1.0 + jnp.exp(-x))


def _silu(x):
    return x * _sigmoid(x)


def _dot(a, b):
    return jnp.dot(a, b, preferred_element_type=F32)


def _dot_nt(a, b):
    return lax.dot_general(a, b, (((1,), (1,)), ((), ())), preferred_element_type=F32)


def _bmm(a, b):
    return lax.dot_general(a, b, (((2,), (1,)), ((0,), (0,))), preferred_element_type=F32)


def _dot_f32(a, b):
    return jnp.dot(a, b, preferred_element_type=F32, precision=lax.Precision.HIGHEST)


def _ada_kernel(c_ref, w_ref, b_ref, o_ref):
    c = c_ref[...]
    c_hi = c.astype(BF16)
    c_lo = (c - c_hi.astype(F32)).astype(BF16)
    w = w_ref[...]
    w_hi = w.astype(BF16)
    w_lo = (w - w_hi.astype(F32)).astype(BF16)
    acc = _dot(c_hi, w_hi) + _dot(c_lo, w_hi) + _dot(c_hi, w_lo)
    o_ref[...] = acc + b_ref[...]


def _ada(c8, w_ada, b_ada):
    D, N = w_ada.shape
    tn = min(512, N)
    return pl.pallas_call(
        _ada_kernel,
        grid=(N // tn,),
        in_specs=[pl.BlockSpec((8, D), lambda j: (0, 0)),
                  pl.BlockSpec((D, tn), lambda j: (0, j)),
                  pl.BlockSpec((1, tn), lambda j: (0, j))],
        out_specs=pl.BlockSpec((8, tn), lambda j: (0, j)),
        out_shape=jax.ShapeDtypeStruct((8, N), F32),
        compiler_params=_cparams(("arbitrary",)),
        name="ada",
    )(c8, w_ada, b_ada)


def _norm_kernel(x_ref, gain_ref, shift_ref, scale_ref, o_ref):
    x = x_ref[...]
    ms = jnp.mean(x * x, axis=-1, keepdims=True)
    y = x * lax.rsqrt(ms + EPS) * gain_ref[...]
    o_ref[...] = (y * (1.0 + scale_ref[0]) + shift_ref[0]).astype(o_ref.dtype)


def _norm(x2, gain, mod3, S):
    M, D = x2.shape
    tm = 256
    nb = S // tm
    return pl.pallas_call(
        _norm_kernel,
        grid=(M // tm,),
        in_specs=[pl.BlockSpec((tm, D), lambda i: (i, 0)),
                  pl.BlockSpec((1, D), lambda i: (0, 0)),
                  pl.BlockSpec((1, 1, D), lambda i: (i // nb, 0, 0)),
                  pl.BlockSpec((1, 1, D), lambda i: (i // nb, 0, 1))],
        out_specs=pl.BlockSpec((tm, D), lambda i: (i, 0)),
        out_shape=jax.ShapeDtypeStruct((M, D), BF16),
        compiler_params=_cparams(("arbitrary",)),
        name="norm",
    )(x2, gain, mod3, mod3)


def _mm_kernel(a_ref, w_ref, o_ref):
    o_ref[...] = _dot(a_ref[...], w_ref[...]).astype(o_ref.dtype)


def _matmul(a, w, out_dtype, tm, tn, name):
    M, K = a.shape
    N = w.shape[1]
    return pl.pallas_call(
        _mm_kernel,
        grid=(N // tn, M // tm),
        in_specs=[pl.BlockSpec((tm, K), lambda j, i: (i, 0)),
                  pl.BlockSpec((K, tn), lambda j, i: (0, j))],
        out_specs=pl.BlockSpec((tm, tn), lambda j, i: (i, j)),
        out_shape=jax.ShapeDtypeStruct((M, N), out_dtype),
        compiler_params=_cparams(("arbitrary", "arbitrary")),
        name=name,
    )(a, w)


def _rope_consts():
    inv = ROPE_THETA ** (-np.arange(ROT_HALF, dtype=np.float64) / ROT_HALF)
    invf = np.zeros((1, LANES), np.float32)
    invf[0, :ROT_HALF] = inv
    invf[0, ROT_HALF:ROT_DIM] = inv
    sgn = np.zeros((1, LANES), np.float32)
    sgn[0, :ROT_HALF] = -1.0
    sgn[0, ROT_HALF:ROT_DIM] = 1.0
    return jnp.asarray(invf), jnp.asarray(sgn)


def _rope_tables(pos_col, invf, sgn):
    ang = pos_col.astype(F32) * invf
    return jnp.cos(ang), jnp.sin(ang) * sgn


def _rope_apply(x, cos_t, sin_t):
    lane = lax.broadcasted_iota(jnp.int32, x.shape, 1)
    partner = jnp.where(lane < ROT_HALF,
                        pltpu.roll(x, LANES - ROT_HALF, 1),
                        pltpu.roll(x, ROT_HALF, 1))
    return x * cos_t + partner * sin_t


def _nsa_prep_kernel(q_ref, kv_ref, pos_ref, invf_ref, sgn_ref, qo_ref, kvo_ref):
    cos_t, sin_t = _rope_tables(pos_ref[...], invf_ref[...], sgn_ref[...])
    qscale = HEAD_DIM ** -0.5
    for h in range(NSA_HEADS):
        sl = slice(h * HEAD_DIM, (h + 1) * HEAD_DIM)
        xq = q_ref[:, sl].astype(F32)
        qo_ref[:, sl] = (_rope_apply(xq, cos_t, sin_t) * qscale).astype(qo_ref.dtype)
    for blk in range(8):
        sl = slice(blk * HEAD_DIM, (blk + 1) * HEAD_DIM)
        xk = kv_ref[:, sl]
        if blk in (0, 1, 4, 5):
            xk = _rope_apply(xk, cos_t, sin_t)
        kvo_ref[:, sl] = xk.astype(kvo_ref.dtype)


def _nsa_prep(pbig, ptail, pos_col, q_col_block):
    M = pbig.shape[0]
    tm = 256
    invf, sgn = _rope_consts()
    return pl.pallas_call(
        _nsa_prep_kernel,
        grid=(M // tm,),
        in_specs=[pl.BlockSpec((tm, NSA_WIDTH), lambda i: (i, q_col_block)),
                  pl.BlockSpec((tm, 1024), lambda i: (i, 0)),
                  pl.BlockSpec((tm, 1), lambda i: (i, 0)),
                  pl.BlockSpec((1, LANES), lambda i: (0, 0)),
                  pl.BlockSpec((1, LANES), lambda i: (0, 0))],
        out_specs=[pl.BlockSpec((tm, NSA_WIDTH), lambda i: (i, 0)),
                   pl.BlockSpec((tm, 1024), lambda i: (i, 0))],
        out_shape=[jax.ShapeDtypeStruct((M, NSA_WIDTH), BF16),
                   jax.ShapeDtypeStruct((M, 1024), BF16)],
        compiler_params=_cparams(("arbitrary",)),
        name="nsa_prep",
    )(pbig, ptail, pos_col, invf, sgn)


def _compress_kernel(x_ref, pe_ref, w1_ref, w2_ref, pos_ref, invf_ref, sgn_ref, o_ref, *, rope, n_rows):
    half = CMP_LEN // 2
    top = jnp.zeros((n_rows, CMP_HIDDEN), F32)
    bot = jnp.zeros((n_rows, CMP_HIDDEN), F32)
    for l in range(half):
        xl = x_ref[pl.ds(l, n_rows, stride=CMP_STRIDE), :]
        w_top = w1_ref[l * HEAD_DIM:(l + 1) * HEAD_DIM, :].astype(BF16)
        w_bot = w1_ref[(half + l) * HEAD_DIM:(half + l + 1) * HEAD_DIM, :].astype(BF16)
        top = top + _dot((xl + pe_ref[l:l + 1, :]).astype(BF16), w_top)
        bot = bot + _dot((xl + pe_ref[half + l:half + l + 1, :]).astype(BF16), w_bot)
    hid = top + pltpu.roll(bot, n_rows - 1, 0)
    out = _dot(_silu(hid).astype(BF16), w2_ref[...].astype(BF16))
    if rope:
        cos_t, sin_t = _rope_tables(pos_ref[0], invf_ref[...], sgn_ref[...])
        out = _rope_apply(out, cos_t, sin_t)
    o_ref[0, 0] = out.astype(o_ref.dtype)


def _compress(ptail, col_block0, pe, w1, w2, pos_cmp, B, S, rope):
    n_rows = S // CMP_STRIDE
    invf, sgn = _rope_consts()
    kern = functools.partial(_compress_kernel, rope=rope, n_rows=n_rows)
    return pl.pallas_call(
        kern,
        grid=(B, NSA_GROUPS),
        in_specs=[pl.BlockSpec((S, HEAD_DIM), lambda b, g: (b, col_block0 + g)),
                  pl.BlockSpec((CMP_LEN, HEAD_DIM), lambda b, g: (0, 0)),
                  pl.BlockSpec((CMP_LEN * HEAD_DIM, CMP_HIDDEN), lambda b, g: (0, 0)),
                  pl.BlockSpec((CMP_HIDDEN, HEAD_DIM), lambda b, g: (0, 0)),
                  pl.BlockSpec((1, n_rows, 1), lambda b, g: (b, 0, 0)),
                  pl.BlockSpec((1, LANES), lambda b, g: (0, 0)),
                  pl.BlockSpec((1, LANES), lambda b, g: (0, 0))],
        out_specs=pl.BlockSpec((1, 1, n_rows, HEAD_DIM), lambda b, g: (b, g, 0, 0)),
        out_shape=jax.ShapeDtypeStruct((B, NSA_GROUPS, n_rows, HEAD_DIM), BF16),
        compiler_params=_cparams(("arbitrary", "arbitrary")),
        name="compress_k" if rope else "compress_v",
    )(ptail, pe, w1, w2, pos_cmp, invf, sgn)


def _row_softmax_stats(s):
    m = jnp.max(s, axis=-1, keepdims=True)
    p = jnp.exp(s - m)
    l = jnp.sum(p, axis=-1, keepdims=True)
    return p, l


def _nsa_attn_kernel(q_ref, kc_ref, vc_ref, ks_ref, vs_ref, kw_ref, vw_ref, g_ref, z_ref,
                     ov_ref, e_ref, o_ref, m_s, l_s, acc_s, *, seq_len):
    T = Q_BLOCK
    P = NSA_HPG
    R = P * T
    n_cmp_pad = seq_len // CMP_STRIDE
    n_cmp = n_cmp_pad - 1
    qi = pl.program_id(2)
    qs = qi * T

    q = q_ref[...]
    qb = jnp.concatenate([q[:, p * HEAD_DIM:(p + 1) * HEAD_DIM] for p in range(P)], axis=0)

    kc = kc_ref[0, 0]
    vc = vc_ref[0, 0]
    s_c = _dot_nt(qb, kc)
    t_c = qs + (lax.broadcasted_iota(jnp.int32, s_c.shape, 0) & (T - 1))
    c_i = lax.broadcasted_iota(jnp.int32, s_c.shape, 1)
    mask_c = (c_i * CMP_STRIDE + (CMP_LEN - 1) <= t_c) & (c_i < n_cmp)
    s_cm = jnp.where(mask_c, s_c, NEG)
    m_c = jnp.max(s_cm, axis=-1, keepdims=True)
    m_c = jnp.where(m_c > 0.5 * NEG, m_c, 0.0)
    p_c = jnp.where(mask_c, jnp.exp(s_c - m_c), 0.0)
    l_c = jnp.sum(p_c, axis=-1, keepdims=True)
    p_c = p_c * (1.0 / jnp.maximum(l_c, 1e-30))
    o_c = _dot(p_c.astype(BF16), vc)

    p_sum = p_c[0:T]
    for p in range(1, P):
        p_sum = p_sum + p_c[p * T:(p + 1) * T]
    imp = _dot_f32(p_sum, ov_ref[...])
    nb = seq_len // SLC_LEN
    t_s = qs + lax.broadcasted_iota(jnp.int32, (T, LANES), 0)
    j_s = lax.broadcasted_iota(jnp.int32, (T, LANES), 1)
    tb = t_s // SLC_LEN
    valid = (j_s * SLC_LEN <= t_s) & (j_s < nb)
    forced = (j_s == 0) | (j_s == tb) | (j_s == tb - 1)
    score = jnp.where(forced, 1e9, jnp.where(valid, imp, -jnp.inf))
    score_t = score.T
    jrow = lax.broadcasted_iota(jnp.int32, (LANES, T), 0)
    cnt = jnp.zeros((LANES, T), F32)
    for k in range(nb):
        rk = score_t[k:k + 1, :]
        beats = (rk > score_t) | ((rk == score_t) & (k < jrow))
        cnt = cnt + jnp.where(beats, 1.0, 0.0)
    sel_t = jnp.where((cnt < float(min(SLC_TOPK, nb))) & (jrow < nb), 1.0, 0.0)
    sel = sel_t.T.astype(BF16)

    m_s[...] = jnp.full(m_s.shape, NEG, F32)
    l_s[...] = jnp.zeros(l_s.shape, F32)
    acc_s[...] = jnp.zeros(acc_s.shape, F32)
    n_tiles = (qs + T + SEL_TILE - 1) // SEL_TILE
    reps = SEL_TILE // LANES

    def sel_body(kt, carry):
        k0 = pl.multiple_of(kt * SEL_TILE, SEL_TILE)
        k_t = ks_ref[pl.ds(k0, SEL_TILE), :]
        v_t = vs_ref[pl.ds(k0, SEL_TILE), :]
        s = _dot_nt(qb, k_t)
        selmask = _dot(sel, e_ref[:, pl.ds(k0, SEL_TILE)])
        key = k0 + lax.broadcasted_iota(jnp.int32, (T, SEL_TILE), 1)
        tq = qs + lax.broadcasted_iota(jnp.int32, (T, SEL_TILE), 0)
        ok = (selmask > 0.5) & (key <= tq)
        s = jnp.where(ok[None], s.reshape(P, T, SEL_TILE), NEG).reshape(R, SEL_TILE)
        m_prev = m_s[...]
        m_new = jnp.maximum(m_prev, jnp.max(s, axis=-1, keepdims=True))
        alpha = jnp.exp(m_prev - m_new)
        p = jnp.exp(s - jnp.concatenate([m_new] * reps, axis=1))
        l_s[...] = alpha * l_s[...] + jnp.sum(p, axis=-1, keepdims=True)
        acc_s[...] = alpha * acc_s[...] + _dot(p.astype(BF16), v_t)
        m_s[...] = m_new
        return carry

    lax.fori_loop(0, n_tiles, sel_body, 0)
    o_s = acc_s[...] * (1.0 / l_s[...])

    wlen = WINDOW + T
    w0 = pl.multiple_of(jnp.maximum(qs - WINDOW, 0), T)
    k_w = kw_ref[pl.ds(w0, wlen), :]
    v_w = vw_ref[pl.ds(w0, wlen), :]
    s_w = _dot_nt(qb, k_w)
    key_w = w0 + lax.broadcasted_iota(jnp.int32, (T, wlen), 1)
    tq_w = qs + lax.broadcasted_iota(jnp.int32, (T, wlen), 0)
    ok_w = (key_w <= tq_w) & (tq_w - key_w < WINDOW)
    s_w = jnp.where(ok_w[None], s_w.reshape(P, T, wlen), NEG).reshape(R, wlen)
    p_w, l_w = _row_softmax_stats(s_w)
    o_w = _dot(p_w.astype(BF16), v_w) * (1.0 / l_w)

    gates = _sigmoid(g_ref[...])
    for p in range(P):
        rows = slice(p * T, (p + 1) * T)
        g0 = gates[:, 0 * P + p:0 * P + p + 1]
        g1 = gates[:, 1 * P + p:1 * P + p + 1]
        g2 = gates[:, 2 * P + p:2 * P + p + 1]
        o_p = g0 * o_c[rows] + g1 * o_s[rows] + g2 * o_w[rows]
        sl = slice(p * HEAD_DIM, (p + 1) * HEAD_DIM)
        o_ref[:, sl] = (o_p * _silu(z_ref[:, sl].astype(F32))).astype(o_ref.dtype)


def _nsa_consts(S):
    n_cmp_pad = S // CMP_STRIDE
    nb = S // SLC_LEN
    cmp_start = np.arange(n_cmp_pad) * CMP_STRIDE
    slc_start = np.arange(LANES) * SLC_LEN
    ov = ((cmp_start[:, None] <= slc_start[None, :] + SLC_LEN - 1)
          & (cmp_start[:, None] + CMP_LEN - 1 >= slc_start[None, :])
          & (np.arange(LANES)[None, :] < nb)
          & (np.arange(n_cmp_pad)[:, None] < n_cmp_pad - 1)).astype(np.float32)
    e = (np.arange(S)[None, :] // SLC_LEN == np.arange(LANES)[:, None]).astype(np.float32)
    return jnp.asarray(ov), jnp.asarray(e, dtype=BF16)


def _nsa_attn(q_r, kcmp, vcmp, kv16, ptail, pbig, z_col_block, B, S):
    M = B * S
    nq = S // Q_BLOCK
    gw = NSA_HPG * HEAD_DIM
    ov, e = _nsa_consts(S)
    n_cmp_pad = S // CMP_STRIDE
    R = NSA_HPG * Q_BLOCK
    kern = functools.partial(_nsa_attn_kernel, seq_len=S)
    kv_spec = lambda c: pl.BlockSpec((S, HEAD_DIM), lambda b, g, i: (b, c + g))
    return pl.pallas_call(
        kern,
        grid=(B, NSA_GROUPS, nq),
        in_specs=[pl.BlockSpec((Q_BLOCK, gw), lambda b, g, i: (b * nq + i, g)),
                  pl.BlockSpec((1, 1, n_cmp_pad, HEAD_DIM), lambda b, g, i: (b, g, 0, 0)),
                  pl.BlockSpec((1, 1, n_cmp_pad, HEAD_DIM), lambda b, g, i: (b, g, 0, 0)),
                  kv_spec(0), kv_spec(2), kv_spec(4), kv_spec(6),
                  pl.BlockSpec((Q_BLOCK, LANES), lambda b, g, i: (b * nq + i, 12 + g)),
                  pl.BlockSpec((Q_BLOCK, gw), lambda b, g, i: (b * nq + i, z_col_block * NSA_GROUPS + g)),
                  pl.BlockSpec((n_cmp_pad, LANES), lambda b, g, i: (0, 0)),
                  pl.BlockSpec((LANES, S), lambda b, g, i: (0, 0))],
        out_specs=pl.BlockSpec((Q_BLOCK, gw), lambda b, g, i: (b * nq + i, g)),
        out_shape=jax.ShapeDtypeStruct((M, NSA_WIDTH), BF16),
        scratch_shapes=[pltpu.VMEM((R, LANES), F32),
                        pltpu.VMEM((R, LANES), F32),
                        pltpu.VMEM((R, HEAD_DIM), F32)],
        compiler_params=_cparams(("arbitrary", "arbitrary", "arbitrary")),
        name="nsa_attn",
    )(q_r, kcmp, vcmp, kv16, kv16, kv16, kv16, ptail, pbig, ov, e)


def _dn_prep_kernel(x_ref, w_ref, o_ref, *rest, mode, seq_len):
    x = x_ref[...].astype(F32)
    w = w_ref[...]
    row = lax.broadcasted_iota(jnp.int32, x.shape, 0)
    y = x * w[CONV_WIDTH - 1:CONV_WIDTH, :]
    for k in range(1, CONV_WIDTH):
        xs = jnp.where(row >= k, pltpu.roll(x, k, 0), 0.0)
        y = y + xs * w[CONV_WIDTH - 1 - k:CONV_WIDTH - k, :]
    y = _silu(y)
    if mode in ("q", "k"):
        outs = []
        for h in range(x.shape[1] // DN_DK):
            yh = y[:, h * DN_DK:(h + 1) * DN_DK]
            ss = jnp.sum(yh * yh, axis=-1, keepdims=True)
            yh = yh * lax.rsqrt(ss + EPS)
            if mode == "q":
                yh = yh * (DN_DK ** -0.5)
            outs.append(yh)
        y = jnp.concatenate(outs, axis=1)
    o_ref[...] = y.astype(o_ref.dtype)
    if mode == "k":
        kt_ref = rest[0]
        kt_ref[...] = y.T.astype(kt_ref.dtype)


DN_PREP_COLS = 128


def _dn_prep(pbig, conv_w, col0, mode, B, S):
    M = B * S
    tw = DN_PREP_COLS
    col0_blocks = col0 // tw
    nj = DN_WIDTH // tw
    wcol0 = {"q": 0, "k": nj, "v": 2 * nj}[mode]
    kern = functools.partial(_dn_prep_kernel, mode=mode, seq_len=S)
    out_specs = [pl.BlockSpec((S, tw), lambda b, j: (b, j))]
    out_shape = [jax.ShapeDtypeStruct((M, DN_WIDTH), BF16)]
    if mode == "k":
        out_specs.append(pl.BlockSpec((tw, S), lambda b, j: (j, b)))
        out_shape.append(jax.ShapeDtypeStruct((DN_WIDTH, M), BF16))
    return pl.pallas_call(
        kern,
        grid=(B, nj),
        in_specs=[pl.BlockSpec((S, tw), lambda b, j: (b, col0_blocks + j)),
                  pl.BlockSpec((CONV_WIDTH, tw), lambda b, j: (0, wcol0 + j))],
        out_specs=out_specs,
        out_shape=out_shape,
        compiler_params=_cparams(("arbitrary", "arbitrary")),
        name="dn_prep_" + mode,
    )(pbig, conv_w)


def _dn_gate_kernel(ab_ref, alog_ref, dtb_ref, o_ref, ot_ref, *, tm):
    ab = ab_ref[...]
    x = ab + dtb_ref[...]
    softplus = jnp.maximum(x, 0.0) + jnp.log(1.0 + jnp.exp(-jnp.abs(x)))
    g = -jnp.exp(alog_ref[...]) * softplus
    beta = _sigmoid(ab)
    lane = lax.broadcasted_iota(jnp.int32, (DN_CHUNK, LANES), 1)
    r = lax.broadcasted_iota(jnp.int32, (DN_CHUNK, DN_CHUNK), 0)
    c = lax.broadcasted_iota(jnp.int32, (DN_CHUNK, DN_CHUNK), 1)
    tril = jnp.where(r >= c, 1.0, 0.0).astype(F32)
    for ci in range(tm // DN_CHUNK):
        rows = slice(ci * DN_CHUNK, (ci + 1) * DN_CHUNK)
        dec = _dot_f32(tril, g[rows])
        out = jnp.where(lane < DN_HEADS, dec, beta[rows])
        o_ref[rows, :] = out
        ot_ref[:, rows] = out.T


def _dn_gate(ptail, alog_row, dtb_row):
    M = ptail.shape[0]
    tm = 512
    kern = functools.partial(_dn_gate_kernel, tm=tm)
    return pl.pallas_call(
        kern,
        grid=(M // tm,),
        in_specs=[pl.BlockSpec((tm, LANES), lambda i: (i, 14)),
                  pl.BlockSpec((1, LANES), lambda i: (0, 0)),
                  pl.BlockSpec((1, LANES), lambda i: (0, 0))],
        out_specs=[pl.BlockSpec((tm, LANES), lambda i: (i, 0)),
                   pl.BlockSpec((LANES, tm), lambda i: (0, i))],
        out_shape=[jax.ShapeDtypeStruct((M, LANES), F32),
                   jax.ShapeDtypeStruct((LANES, M), F32)],
        compiler_params=_cparams(("arbitrary",)),
        name="dn_gate",
    )(ptail, alog_row, dtb_row)


def _dn_scan_kernel(q_ref, k_ref, v_ref, kt_ref, dec_ref, dect_ref, z_ref, gain_ref, o_ref, st_ref):
    C = DN_CHUNK

    @pl.when(pl.program_id(1) == 0)
    def _():
        st_ref[...] = jnp.zeros(st_ref.shape, F32)

    dec = dec_ref[...]
    dect = dect_ref[...]
    r = lax.broadcasted_iota(jnp.int32, (C, C), 0)
    c = lax.broadcasted_iota(jnp.int32, (C, C), 1)
    tril = r >= c
    strict = r > c
    gain = gain_ref[...]
    eye = jnp.where(r == c, 1.0, 0.0).astype(F32)
    lvl_masks = []
    for lg in range(int(math.log2(C))):
        lvl_masks.append(((r >> (lg + 1)) == (c >> (lg + 1))) & ((r >> lg) != (c >> lg)))

    for h0 in range(0, DN_HEADS, DN_HEAD_GROUP):
        heads = range(h0, h0 + DN_HEAD_GROUP)
        cols = lambda ref: jnp.stack([ref[:, h * DN_DK:(h + 1) * DN_DK] for h in heads])
        qh = cols(q_ref).astype(F32)
        kh = cols(k_ref).astype(F32)
        vh = cols(v_ref).astype(F32)
        kth = jnp.stack([kt_ref[h * DN_DK:(h + 1) * DN_DK, :] for h in heads])
        dcol = jnp.stack([jnp.broadcast_to(dec[:, h:h + 1], (C, C)) for h in heads])
        bcol = jnp.stack([jnp.broadcast_to(dec[:, DN_HEADS + h:DN_HEADS + h + 1], (C, C)) for h in heads])
        drow = jnp.stack([jnp.broadcast_to(dect[h:h + 1, :], (C, C)) for h in heads])
        dlast = jnp.stack([jnp.broadcast_to(dect[h:h + 1, C - 1:C], (C, C)) for h in heads])
        lmat = jnp.exp(jnp.where(tril[None], dcol - drow, NEG))
        e_d = jnp.exp(dcol)
        kb = kh * bcol
        vb = vh * bcol
        a = jnp.where(strict[None], _bmm(kb.astype(BF16), kth) * lmat, 0.0)
        attn = jnp.where(tril[None], _bmm(qh.astype(BF16), kth) * lmat, 0.0)
        tinv = eye[None] - jnp.where(lvl_masks[0][None], a, 0.0)
        for lm in lvl_masks[1:]:
            t16 = tinv.astype(BF16)
            lo = jnp.where(lm[None], a, 0.0).astype(BF16)
            tinv = tinv - _bmm(t16, _bmm(lo, t16).astype(BF16))
        t16 = tinv.astype(BF16)
        u = _bmm(t16, vb.astype(BF16))
        w = _bmm(t16, (kb * e_d).astype(BF16))
        st = st_ref[h0:h0 + DN_HEAD_GROUP]
        st16 = st.astype(BF16)
        v_new = u - _bmm(w.astype(BF16), st16)
        v_new16 = v_new.astype(BF16)
        o = _bmm((qh * e_d).astype(BF16), st16) + _bmm(attn.astype(BF16), v_new16)
        kdt = (kth.astype(F32) * jnp.exp(dlast - drow)).astype(BF16)
        st_ref[h0:h0 + DN_HEAD_GROUP] = st * jnp.exp(dlast) + _bmm(kdt, v_new16)
        ms = jnp.mean(o * o, axis=-1, keepdims=True)
        y = o * lax.rsqrt(ms + EPS) * gain[None]
        for i, h in enumerate(heads):
            sl = slice(h * DN_DV, (h + 1) * DN_DV)
            o_ref[:, sl] = (y[i] * _silu(z_ref[:, sl].astype(F32))).astype(o_ref.dtype)


def _dn_scan(dq, dk, dv, dkt, dec, dect, pbig, z_col_block, gain_row, B, S):
    M = B * S
    C = DN_CHUNK
    nc = S // C
    tok = lambda b, n: (b * nc + n, 0)
    return pl.pallas_call(
        _dn_scan_kernel,
        grid=(B, nc),
        in_specs=[pl.BlockSpec((C, DN_WIDTH), tok),
                  pl.BlockSpec((C, DN_WIDTH), tok),
                  pl.BlockSpec((C, DN_WIDTH), tok),
                  pl.BlockSpec((DN_WIDTH, C), lambda b, n: (0, b * nc + n)),
                  pl.BlockSpec((C, LANES), tok),
                  pl.BlockSpec((LANES, C), lambda b, n: (0, b * nc + n)),
                  pl.BlockSpec((C, DN_WIDTH), lambda b, n: (b * nc + n, z_col_block)),
                  pl.BlockSpec((1, DN_DV), lambda b, n: (0, 0))],
        out_specs=pl.BlockSpec((C, DN_WIDTH), tok),
        out_shape=jax.ShapeDtypeStruct((M, DN_WIDTH), BF16),
        scratch_shapes=[pltpu.VMEM((DN_HEADS, DN_DK, DN_DV), F32)],
        compiler_params=_cparams(("arbitrary", "arbitrary")),
        name="dn_scan",
    )(dq, dk, dv, dkt, dec, dect, pbig, gain_row)


def _out1_kernel(oa_ref, ob_ref, wa_ref, wb_ref, ga_ref, gb_ref, o_ref):
    ya = _dot(oa_ref[...], wa_ref[...])
    yb = _dot(ob_ref[...], wb_ref[...])
    mix = _sigmoid(ga_ref[...].astype(F32)) * ya + _sigmoid(gb_ref[...].astype(F32)) * yb
    o_ref[...] = mix.astype(o_ref.dtype)


def _out1(o_a, o_b, wa, wb, pbig, D):
    M, K = o_a.shape
    tm = min(1024, M)
    tn = min(1024, D)
    nbd = D // tn
    return pl.pallas_call(
        _out1_kernel,
        grid=(D // tn, M // tm),
        in_specs=[pl.BlockSpec((tm, K), lambda j, i: (i, 0)),
                  pl.BlockSpec((tm, K), lambda j, i: (i, 0)),
                  pl.BlockSpec((K, tn), lambda j, i: (0, j)),
                  pl.BlockSpec((K, tn), lambda j, i: (0, j)),
                  pl.BlockSpec((tm, tn), lambda j, i: (i, j)),
                  pl.BlockSpec((tm, tn), lambda j, i: (i, nbd + j))],
        out_specs=pl.BlockSpec((tm, tn), lambda j, i: (i, j)),
        out_shape=jax.ShapeDtypeStruct((M, D), BF16),
        compiler_params=_cparams(("arbitrary", "arbitrary")),
        name="out1",
    )(o_a, o_b, wa, wb, pbig, pbig)


def _final_kernel(mix_ref, x_ref, gate_ref, fg_ref, o_ref):
    xn = x_ref[...] + gate_ref[0] * mix_ref[...].astype(F32)
    ms = jnp.mean(xn * xn, axis=-1, keepdims=True)
    o_ref[...] = xn * lax.rsqrt(ms + EPS) * fg_ref[...]


def _final(mixed, x2, mod3, final_gain, S):
    M, D = x2.shape
    tm = 256
    nb = S // tm
    return pl.pallas_call(
        _final_kernel,
        grid=(M // tm,),
        in_specs=[pl.BlockSpec((tm, D), lambda i: (i, 0)),
                  pl.BlockSpec((tm, D), lambda i: (i, 0)),
                  pl.BlockSpec((1, 1, D), lambda i: (i // nb, 0, 2)),
                  pl.BlockSpec((1, D), lambda i: (0, 0))],
        out_specs=pl.BlockSpec((tm, D), lambda i: (i, 0)),
        out_shape=jax.ShapeDtypeStruct((M, D), F32),
        compiler_params=_cparams(("arbitrary",)),
        name="final",
    )(mixed, x2, mod3, final_gain)


def _pad_cols(w, width):
    return jnp.pad(w, ((0, 0), (0, width - w.shape[1])))


def _permute_w_in(w, D):
    widths = (NSA_WIDTH, 6 * KV_WIDTH, 3 * NSA_HEADS, NSA_WIDTH, 3 * DN_WIDTH, DN_HEADS, DN_HEADS, DN_WIDTH, 2 * D)
    offs = np.concatenate([[0], np.cumsum(widths)])
    seg = lambda i: w[:, int(offs[i]):int(offs[i + 1])]
    q, kv, g, z, dqkv, da, db, dz, mg = [seg(i) for i in range(9)]
    big = jnp.concatenate([mg, dqkv, q, z, dz], axis=1).astype(BF16)
    gcols = []
    for gi in range(NSA_GROUPS):
        idx = np.array([r * NSA_HEADS + gi * NSA_HPG + p for r in range(3) for p in range(NSA_HPG)])
        gcols.append(_pad_cols(g[:, idx], LANES))
    ab = _pad_cols(jnp.concatenate([da, db], axis=1), LANES)
    tail = jnp.concatenate([kv[:, 2 * KV_WIDTH:], kv[:, :2 * KV_WIDTH]] + gcols + [ab], axis=1).astype(BF16)
    return big, tail


def kernel(x, c, positions, w_ada, b_ada, norm_gain, w_in, cmp_pos_k, cmp_pos_v, w_cmp_k1, w_cmp_k2,
           w_cmp_v1, w_cmp_v2, conv_w, dt_bias, a_log, dn_norm_gain, w_proj_a, w_proj_b, w_out, final_gain):
    B, S, D = x.shape
    M = B * S
    depth = w_in.shape[0]
    assert S % SEL_TILE == 0 and S % DN_CHUNK == 0 and S >= WINDOW + Q_BLOCK and D % 512 == 0 and B <= 8
    assert S // SLC_LEN <= LANES and D % 1024 == 0 and DN_DK == DN_CHUNK
    assert depth == 1, "the final RMSNorm is fused into the last layer's output kernel"

    off_dq = 2 * D
    off_q = off_dq + 3 * DN_WIDTH
    off_z = off_q + NSA_WIDTH
    off_dz = off_z + NSA_WIDTH

    x2 = x.reshape(M, D)
    pos_col = positions.reshape(M, 1)
    cmp_end = np.arange(S // CMP_STRIDE - 1) * CMP_STRIDE + CMP_LEN - 1
    pos_cmp = jnp.pad(positions[:, cmp_end], ((0, 0), (0, 1)))[:, :, None]
    c8 = jnp.pad(c, ((0, 8 - B), (0, 0)))

    for l in range(depth):
        mod = _ada(c8, w_ada[l], b_ada[l][None, :])
        mod3 = mod[:B].reshape(B, 1, 3 * D)
        h = _norm(x2, norm_gain[l][None, :], mod3, S)

        w_big, w_tail = _permute_w_in(w_in[l], D)
        tm = min(1024, M)
        pbig = _matmul(h, w_big, BF16, tm, 1024, "proj_big")
        ptail = _matmul(h, w_tail, F32, tm, 640, "proj_tail")

        q_r, kv16 = _nsa_prep(pbig, ptail, pos_col, off_q // NSA_WIDTH)
        kcmp = _compress(ptail, 8, cmp_pos_k[l], w_cmp_k1[l], w_cmp_k2[l], pos_cmp, B, S, True)
        vcmp = _compress(ptail, 10, cmp_pos_v[l], w_cmp_v1[l], w_cmp_v2[l], pos_cmp, B, S, False)
        o_a = _nsa_attn(q_r, kcmp, vcmp, kv16, ptail, pbig, off_z // NSA_WIDTH, B, S)

        cw = conv_w[l]
        dq = _dn_prep(pbig, cw, off_dq, "q", B, S)[0]
        dk, dkt = _dn_prep(pbig, cw, off_dq + DN_WIDTH, "k", B, S)
        dv = _dn_prep(pbig, cw, off_dq + 2 * DN_WIDTH, "v", B, S)[0]
        alog_row = _pad_cols(a_log[l][None, :].astype(F32), LANES)
        dtb_row = _pad_cols(dt_bias[l][None, :].astype(F32), LANES)
        dec, dect = _dn_gate(ptail, alog_row, dtb_row)
        o_b = _dn_scan(dq, dk, dv, dkt, dec, dect, pbig, off_dz // DN_WIDTH,
                       dn_norm_gain[l][None, :], B, S)

        mixed_in = _out1(o_a, o_b, w_proj_a[l].astype(BF16), w_proj_b[l].astype(BF16), pbig, D)
        mixed = _matmul(mixed_in, w_out[l].astype(BF16), BF16, min(1024, M), min(1024, D), "out2")
        x2 = _final(mixed, x2, mod3, final_gain[None, :], S)
    return x2.reshape(B, S, D)
```

```python
import functools
import math

import numpy as np
import jax
import jax.numpy as jnp
from jax import lax
from jax.experimental import pallas as pl
from jax.experimental.pallas import tpu as pltpu

F32 = jnp.float32
BF16 = jnp.bfloat16

NSA_HEADS = 16
NSA_GROUPS = 2
NSA_HPG = NSA_HEADS // NSA_GROUPS
HEAD_DIM = 128
ROT_DIM = HEAD_DIM // 4
ROT_HALF = ROT_DIM // 2
ROPE_THETA = 500000.0
CMP_LEN = 32
CMP_STRIDE = 16
CMP_HIDDEN = 256
SLC_LEN = 64
SLC_TOPK = 16
WINDOW = 512
Q_BLOCK = 128
NSA_WIDTH = NSA_HEADS * HEAD_DIM
KV_WIDTH = NSA_GROUPS * HEAD_DIM
DN_HEADS = 16
DN_DK = 128
DN_DV = 128
DN_WIDTH = DN_HEADS * DN_DV
CONV_WIDTH = 4
EPS = 1e-6

LANES = 128
VMEM_LIMIT_BYTES = 56 * 1024 * 1024

DN_CHUNK = 128
DN_HEAD_GROUP = 8
SEL_TILE = 512
SEL_COL_CHUNKS = 1
NEG = -1e30

TAIL_WIDTH = 1024 + 512 + 256 + 128


def _cparams(sem):
    return pltpu.CompilerParams(dimension_semantics=sem, vmem_limit_bytes=VMEM_LIMIT_BYTES)


def _sigmoid(x):
    return 1.0 / (1.0 + jnp.exp(-x))


def _silu(x):
    return x * _sigmoid(x)


def _dot(a, b):
    return jnp.dot(a, b, preferred_element_type=F32)


def _dot_nt(a, b):
    return lax.dot_general(a, b, (((1,), (1,)), ((), ())), preferred_element_type=F32)


def _bmm(a, b):
    return lax.dot_general(a, b, (((2,), (1,)), ((0,), (0,))), preferred_element_type=F32)


def _dot_f32(a, b):
    return jnp.dot(a, b, preferred_element_type=F32, precision=lax.Precision.HIGHEST)


def _ada_kernel(c_ref, w_ref, b_ref, o_ref):
    c = c_ref[...]
    c_hi = c.astype(BF16)
    c_lo = (c - c_hi.astype(F32)).astype(BF16)
    w = w_ref[...]
    w_hi = w.astype(BF16)
    w_lo = (w - w_hi.astype(F32)).astype(BF16)
    acc = _dot(c_hi, w_hi) + _dot(c_lo, w_hi) + _dot(c_hi, w_lo)
    o_ref[...] = acc + b_ref[...]


def _ada(c8, w_ada, b_ada):
    D, N = w_ada.shape
    tn = min(512, N)
    return pl.pallas_call(
        _ada_kernel,
        grid=(N // tn,),
        in_specs=[pl.BlockSpec((8, D), lambda j: (0, 0)),
                  pl.BlockSpec((D, tn), lambda j: (0, j)),
                  pl.BlockSpec((1, tn), lambda j: (0, j))],
        out_specs=pl.BlockSpec((8, tn), lambda j: (0, j)),
        out_shape=jax.ShapeDtypeStruct((8, N), F32),
        compiler_params=_cparams(("arbitrary",)),
        name="ada",
    )(c8, w_ada, b_ada)


def _norm_kernel(x_ref, gain_ref, shift_ref, scale_ref, o_ref):
    x = x_ref[...]
    ms = jnp.mean(x * x, axis=-1, keepdims=True)
    y = x * lax.rsqrt(ms + EPS) * gain_ref[...]
    o_ref[...] = (y * (1.0 + scale_ref[0]) + shift_ref[0]).astype(o_ref.dtype)


def _norm(x2, gain, mod3, S):
    M, D = x2.shape
    tm = 256
    nb = S // tm
    return pl.pallas_call(
        _norm_kernel,
        grid=(M // tm,),
        in_specs=[pl.BlockSpec((tm, D), lambda i: (i, 0)),
                  pl.BlockSpec((1, D), lambda i: (0, 0)),
                  pl.BlockSpec((1, 1, D), lambda i: (i // nb, 0, 0)),
                  pl.BlockSpec((1, 1, D), lambda i: (i // nb, 0, 1))],
        out_specs=pl.BlockSpec((tm, D), lambda i: (i, 0)),
        out_shape=jax.ShapeDtypeStruct((M, D), BF16),
        compiler_params=_cparams(("arbitrary",)),
        name="norm",
    )(x2, gain, mod3, mod3)


def _mm_kernel(a_ref, w_ref, o_ref):
    o_ref[...] = _dot(a_ref[...], w_ref[...]).astype(o_ref.dtype)


def _matmul(a, w, out_dtype, tm, tn, name):
    M, K = a.shape
    N = w.shape[1]
    return pl.pallas_call(
        _mm_kernel,
        grid=(N // tn, M // tm),
        in_specs=[pl.BlockSpec((tm, K), lambda j, i: (i, 0)),
                  pl.BlockSpec((K, tn), lambda j, i: (0, j))],
        out_specs=pl.BlockSpec((tm, tn), lambda j, i: (i, j)),
        out_shape=jax.ShapeDtypeStruct((M, N), out_dtype),
        compiler_params=_cparams(("arbitrary", "arbitrary")),
        name=name,
    )(a, w)


def _rope_consts():
    inv = ROPE_THETA ** (-np.arange(ROT_HALF, dtype=np.float64) / ROT_HALF)
    invf = np.zeros((1, LANES), np.float32)
    invf[0, :ROT_HALF] = inv
    invf[0, ROT_HALF:ROT_DIM] = inv
    sgn = np.zeros((1, LANES), np.float32)
    sgn[0, :ROT_HALF] = -1.0
    sgn[0, ROT_HALF:ROT_DIM] = 1.0
    return jnp.asarray(invf), jnp.asarray(sgn)


def _rope_tables(pos_col, invf, sgn):
    ang = pos_col.astype(F32) * invf
    return jnp.cos(ang), jnp.sin(ang) * sgn


def _rope_apply(x, cos_t, sin_t):
    lane = lax.broadcasted_iota(jnp.int32, x.shape, 1)
    partner = jnp.where(lane < ROT_HALF,
                        pltpu.roll(x, LANES - ROT_HALF, 1),
                        pltpu.roll(x, ROT_HALF, 1))
    return x * cos_t + partner * sin_t


KEYS_WIDTH = 6 * HEAD_DIM


def _nsa_prep_kernel(q_ref, kv_ref, pos_ref, invf_ref, sgn_ref, qt_ref, keys_ref, vt_ref, *, seq_len):
    tm = q_ref.shape[0]
    cos_t, sin_t = _rope_tables(pos_ref[...], invf_ref[...], sgn_ref[...])
    qscale = (HEAD_DIM ** -0.5) * math.log2(math.e)
    for h in range(NSA_HEADS):
        sl = slice(h * HEAD_DIM, (h + 1) * HEAD_DIM)
        xq = _rope_apply(q_ref[:, sl].astype(F32), cos_t, sin_t) * qscale
        qt_ref[sl, :] = xq.T.astype(qt_ref.dtype)
    tok = (pl.program_id(0) * tm) % seq_len + lax.broadcasted_iota(jnp.int32, (tm, LANES), 0)
    lane = lax.broadcasted_iota(jnp.int32, (tm, LANES), 1)
    onehot = jnp.where(lane == (tok >> int(math.log2(SLC_LEN))), 1.0, 0.0).astype(keys_ref.dtype)
    blk = lambda i: kv_ref[:, i * HEAD_DIM:(i + 1) * HEAD_DIM]
    for g in range(NSA_GROUPS):
        ks = _rope_apply(blk(g), cos_t, sin_t)
        keys_ref[:, (2 * g) * HEAD_DIM:(2 * g + 1) * HEAD_DIM] = ks.astype(keys_ref.dtype)
        keys_ref[:, (2 * g + 1) * HEAD_DIM:(2 * g + 2) * HEAD_DIM] = onehot
        kw = _rope_apply(blk(4 + g), cos_t, sin_t)
        keys_ref[:, (4 + g) * HEAD_DIM:(5 + g) * HEAD_DIM] = kw.astype(keys_ref.dtype)
        vt_ref[g * HEAD_DIM:(g + 1) * HEAD_DIM, :] = blk(2 + g).T.astype(vt_ref.dtype)
        vt_ref[(2 + g) * HEAD_DIM:(3 + g) * HEAD_DIM, :] = blk(6 + g).T.astype(vt_ref.dtype)


def _nsa_prep(pbig, ptail, pos_col, q_col_block, S):
    M = pbig.shape[0]
    tm = 256
    invf, sgn = _rope_consts()
    return pl.pallas_call(
        functools.partial(_nsa_prep_kernel, seq_len=S),
        grid=(M // tm,),
        in_specs=[pl.BlockSpec((tm, NSA_WIDTH), lambda i: (i, q_col_block)),
                  pl.BlockSpec((tm, 1024), lambda i: (i, 0)),
                  pl.BlockSpec((tm, 1), lambda i: (i, 0)),
                  pl.BlockSpec((1, LANES), lambda i: (0, 0)),
                  pl.BlockSpec((1, LANES), lambda i: (0, 0))],
        out_specs=[pl.BlockSpec((NSA_WIDTH, tm), lambda i: (0, i)),
                   pl.BlockSpec((tm, KEYS_WIDTH), lambda i: (i, 0)),
                   pl.BlockSpec((4 * HEAD_DIM, tm), lambda i: (0, i))],
        out_shape=[jax.ShapeDtypeStruct((NSA_WIDTH, M), BF16),
                   jax.ShapeDtypeStruct((M, KEYS_WIDTH), BF16),
                   jax.ShapeDtypeStruct((4 * HEAD_DIM, M), BF16)],
        compiler_params=_cparams(("arbitrary",)),
        name="nsa_prep",
    )(pbig, ptail, pos_col, invf, sgn)


def _compress_kernel(x_ref, pe_ref, w1_ref, w2_ref, pos_ref, invf_ref, sgn_ref, o_ref, *, rope, n_rows):
    half = CMP_LEN // 2
    top = jnp.zeros((n_rows, CMP_HIDDEN), F32)
    bot = jnp.zeros((n_rows, CMP_HIDDEN), F32)
    for l in range(half):
        xl = x_ref[pl.ds(l, n_rows, stride=CMP_STRIDE), :]
        w_top = w1_ref[l * HEAD_DIM:(l + 1) * HEAD_DIM, :].astype(BF16)
        w_bot = w1_ref[(half + l) * HEAD_DIM:(half + l + 1) * HEAD_DIM, :].astype(BF16)
        top = top + _dot((xl + pe_ref[l:l + 1, :]).astype(BF16), w_top)
        bot = bot + _dot((xl + pe_ref[half + l:half + l + 1, :]).astype(BF16), w_bot)
    hid = top + pltpu.roll(bot, n_rows - 1, 0)
    act = _silu(hid).astype(BF16)
    if rope:
        out = _dot(act, w2_ref[...].astype(BF16))
        cos_t, sin_t = _rope_tables(pos_ref[0], invf_ref[...], sgn_ref[...])
        o_ref[0, 0] = _rope_apply(out, cos_t, sin_t).astype(o_ref.dtype)
    else:
        o_ref[0, 0] = _dot_nt(w2_ref[...].astype(BF16), act).astype(o_ref.dtype)


def _compress(ptail, col_block0, pe, w1, w2, pos_cmp, B, S, rope):
    n_rows = S // CMP_STRIDE
    invf, sgn = _rope_consts()
    kern = functools.partial(_compress_kernel, rope=rope, n_rows=n_rows)
    out_dims = (n_rows, HEAD_DIM) if rope else (HEAD_DIM, n_rows)
    return pl.pallas_call(
        kern,
        grid=(B, NSA_GROUPS),
        in_specs=[pl.BlockSpec((S, HEAD_DIM), lambda b, g: (b, col_block0 + g)),
                  pl.BlockSpec((CMP_LEN, HEAD_DIM), lambda b, g: (0, 0)),
                  pl.BlockSpec((CMP_LEN * HEAD_DIM, CMP_HIDDEN), lambda b, g: (0, 0)),
                  pl.BlockSpec(w2.shape, lambda b, g: (0, 0)),
                  pl.BlockSpec((1, n_rows, 1), lambda b, g: (b, 0, 0)),
                  pl.BlockSpec((1, LANES), lambda b, g: (0, 0)),
                  pl.BlockSpec((1, LANES), lambda b, g: (0, 0))],
        out_specs=pl.BlockSpec((1, 1) + out_dims, lambda b, g: (b, g, 0, 0)),
        out_shape=jax.ShapeDtypeStruct((B, NSA_GROUPS) + out_dims, BF16),
        compiler_params=_cparams(("arbitrary", "arbitrary")),
        name="compress_k" if rope else "compress_v",
    )(ptail, pe, w1, w2, pos_cmp, invf, sgn)


def _tile_lanes(x, n):
    return jnp.concatenate([x] * n, axis=1)


def _nsa_attn_kernel(qt_ref, kc_ref, vct_ref, ks_ref, vst_ref, kw_ref, vwt_ref, g_ref, z_ref,
                     ovt_ref, o_ref, m_s, l_s, acc_s, qa_s, *, seq_len):
    T = Q_BLOCK
    P = NSA_HPG
    R = P * T
    n_cmp_pad = seq_len // CMP_STRIDE
    n_cmp = n_cmp_pad - 1
    nb = seq_len // SLC_LEN
    qi = pl.program_id(2)
    qs = qi * T
    d0 = pl.multiple_of(qs, T)
    wlen = WINDOW + T
    w0 = pl.multiple_of(jnp.maximum(qs - WINDOW, 0), T)

    qt = jnp.concatenate([qt_ref[p * HEAD_DIM:(p + 1) * HEAD_DIM, :] for p in range(P)], axis=1)

    s_c = _dot(kc_ref[0, 0], qt)
    s_w = _dot(kw_ref[pl.ds(w0, wlen), :], qt)
    s_d = _dot(ks_ref[pl.ds(d0, T), 0:HEAD_DIM], qt)

    tok = lambda n: qs + lax.broadcasted_iota(jnp.int32, (n, T), 1)
    row = lambda n: lax.broadcasted_iota(jnp.int32, (n, T), 0)
    ok_c = (row(n_cmp_pad) * CMP_STRIDE + (CMP_LEN - 1) <= tok(n_cmp_pad)) & (row(n_cmp_pad) < n_cmp)
    key_w = w0 + row(wlen)
    ok_w = (key_w <= tok(wlen)) & (tok(wlen) - key_w < WINDOW)
    ok_d = row(T) <= lax.broadcasted_iota(jnp.int32, (T, T), 1)
    addmask = lambda ok: _tile_lanes(jnp.where(ok, 0.0, NEG).astype(F32), P)
    s_c = s_c + addmask(ok_c)
    s_w = s_w + addmask(ok_w)
    s_d = s_d + addmask(ok_d)

    m_c = jnp.max(s_c, axis=0, keepdims=True)
    m_w = jnp.max(s_w, axis=0, keepdims=True)
    m_d = jnp.max(s_d, axis=0, keepdims=True)
    m_c = jnp.where(m_c > 0.5 * NEG, m_c, 0.0)
    p_c = jnp.exp2(s_c - m_c)
    p_w = jnp.exp2(s_w - m_w)
    p_d = jnp.exp2(s_d - m_d)
    inv_c = 1.0 / jnp.maximum(jnp.sum(p_c, axis=0, keepdims=True), 1e-30)
    inv_w = 1.0 / jnp.sum(p_w, axis=0, keepdims=True)
    l_d = jnp.sum(p_d, axis=0, keepdims=True)
    o_c = _dot(vct_ref[0, 0], p_c.astype(BF16))
    o_w = _dot(vwt_ref[:, pl.ds(w0, wlen)], p_w.astype(BF16))
    m_s[...] = m_d
    l_s[...] = l_d
    acc_s[...] = _dot(vst_ref[:, pl.ds(d0, T)], p_d.astype(BF16))

    pn = p_c * inv_c
    p_sum = pn[:, 0:T]
    for p in range(1, P):
        p_sum = p_sum + pn[:, p * T:(p + 1) * T]
    imp = _dot_f32(ovt_ref[...], p_sum)[0:nb]
    jrow = row(nb)
    t_lane = tok(nb)
    tb = t_lane >> int(math.log2(SLC_LEN))
    visible = jrow * SLC_LEN <= t_lane
    forced = (jrow == 0) | (jrow == tb) | (jrow == tb - 1)
    score = jnp.where(forced, 1e9, jnp.where(visible, imp, -jnp.inf))
    cnt = jnp.zeros((nb, T), F32)
    for k in range(nb):
        rk = score[k:k + 1, :]
        cnt = cnt + jnp.where(jrow > k, jnp.where(rk >= score, 1.0, 0.0), jnp.where(rk > score, 1.0, 0.0))
    keep = (cnt < float(min(SLC_TOPK, nb))) & visible & (jrow < 2 * qi)
    bias = jnp.where(keep, 0.0, NEG).astype(F32)
    bias = jnp.concatenate([bias, jnp.full((LANES - nb, T), NEG, F32)], axis=0) if nb < LANES else bias
    qa_s[0:HEAD_DIM, :] = qt
    qa_s[HEAD_DIM:2 * HEAD_DIM, :] = _tile_lanes(bias.astype(BF16), P)

    cw = R // SEL_COL_CHUNKS

    def sel_body(kt, carry):
        k0 = pl.multiple_of(kt * SEL_TILE, SEL_TILE)
        k_t = ks_ref[pl.ds(k0, SEL_TILE), :]
        v_t = vst_ref[:, pl.ds(k0, SEL_TILE)]
        s_next = _dot(k_t, qa_s[:, 0:cw])
        for j in range(SEL_COL_CHUNKS):
            cs = slice(j * cw, (j + 1) * cw)
            s = s_next
            if j + 1 < SEL_COL_CHUNKS:
                s_next = _dot(k_t, qa_s[:, (j + 1) * cw:(j + 2) * cw])
            m_prev = m_s[:, cs]
            m_new = jnp.maximum(m_prev, jnp.max(s, axis=0, keepdims=True))
            alpha = jnp.exp2(m_prev - m_new)
            p = jnp.exp2(s - m_new)
            l_s[:, cs] = alpha * l_s[:, cs] + jnp.sum(p, axis=0, keepdims=True)
            acc_s[:, cs] = alpha * acc_s[:, cs] + _dot(v_t, p.astype(BF16))
            m_s[:, cs] = m_new
        return carry

    lax.fori_loop(0, (qs + SEL_TILE - 1) // SEL_TILE, sel_body, 0)

    gates_t = _sigmoid(g_ref[...]).T
    grow = lambda r: jnp.concatenate([gates_t[r * P + p:r * P + p + 1, :] for p in range(P)], axis=1)
    o_t = (o_c * (grow(0) * inv_c) + acc_s[...] * (grow(1) / l_s[...]) + o_w * (grow(2) * inv_w))
    for p in range(P):
        sl = slice(p * HEAD_DIM, (p + 1) * HEAD_DIM)
        o_p = o_t[:, p * T:(p + 1) * T].T
        o_ref[:, sl] = (o_p * _silu(z_ref[:, sl].astype(F32))).astype(o_ref.dtype)


def _nsa_consts(S):
    n_cmp_pad = S // CMP_STRIDE
    nb = S // SLC_LEN
    cmp_start = np.arange(n_cmp_pad) * CMP_STRIDE
    slc_start = np.arange(LANES) * SLC_LEN
    ov = ((cmp_start[:, None] <= slc_start[None, :] + SLC_LEN - 1)
          & (cmp_start[:, None] + CMP_LEN - 1 >= slc_start[None, :])
          & (np.arange(LANES)[None, :] < nb)
          & (np.arange(n_cmp_pad)[:, None] < n_cmp_pad - 1)).astype(np.float32)
    return jnp.asarray(ov.T)


def _nsa_attn(q_t, kcmp, vcmp_t, keys, v_t, ptail, pbig, z_col_block, B, S):
    M = B * S
    nq = S // Q_BLOCK
    gw = NSA_HPG * HEAD_DIM
    ovt = _nsa_consts(S)
    n_cmp_pad = S // CMP_STRIDE
    R = NSA_HPG * Q_BLOCK
    kern = functools.partial(_nsa_attn_kernel, seq_len=S)
    return pl.pallas_call(
        kern,
        grid=(B, NSA_GROUPS, nq),
        in_specs=[pl.BlockSpec((gw, Q_BLOCK), lambda b, g, i: (g, b * nq + i)),
                  pl.BlockSpec((1, 1, n_cmp_pad, HEAD_DIM), lambda b, g, i: (b, g, 0, 0)),
                  pl.BlockSpec((1, 1, HEAD_DIM, n_cmp_pad), lambda b, g, i: (b, g, 0, 0)),
                  pl.BlockSpec((S, 2 * HEAD_DIM), lambda b, g, i: (b, g)),
                  pl.BlockSpec((HEAD_DIM, S), lambda b, g, i: (g, b)),
                  pl.BlockSpec((S, HEAD_DIM), lambda b, g, i: (b, 4 + g)),
                  pl.BlockSpec((HEAD_DIM, S), lambda b, g, i: (2 + g, b)),
                  pl.BlockSpec((Q_BLOCK, LANES), lambda b, g, i: (b * nq + i, 12 + g)),
                  pl.BlockSpec((Q_BLOCK, gw), lambda b, g, i: (b * nq + i, z_col_block * NSA_GROUPS + g)),
                  pl.BlockSpec((LANES, n_cmp_pad), lambda b, g, i: (0, 0))],
        out_specs=pl.BlockSpec((Q_BLOCK, gw), lambda b, g, i: (b * nq + i, g)),
        out_shape=jax.ShapeDtypeStruct((M, NSA_WIDTH), BF16),
        scratch_shapes=[pltpu.VMEM((1, R), F32),
                        pltpu.VMEM((1, R), F32),
                        pltpu.VMEM((HEAD_DIM, R), F32),
                        pltpu.VMEM((2 * HEAD_DIM, R), BF16)],
        compiler_params=_cparams(("arbitrary", "arbitrary", "arbitrary")),
        name="nsa_attn",
    )(q_t, kcmp, vcmp_t, keys, v_t, keys, v_t, ptail, pbig, ovt)


def _dn_prep_kernel(x_ref, w_ref, o_ref, *rest, mode, seq_len):
    x = x_ref[...].astype(F32)
    w = w_ref[...]
    row = lax.broadcasted_iota(jnp.int32, x.shape, 0)
    y = x * w[CONV_WIDTH - 1:CONV_WIDTH, :]
    for k in range(1, CONV_WIDTH):
        xs = jnp.where(row >= k, pltpu.roll(x, k, 0), 0.0)
        y = y + xs * w[CONV_WIDTH - 1 - k:CONV_WIDTH - k, :]
    y = _silu(y)
    if mode in ("q", "k"):
        outs = []
        for h in range(x.shape[1] // DN_DK):
            yh = y[:, h * DN_DK:(h + 1) * DN_DK]
            ss = jnp.sum(yh * yh, axis=-1, keepdims=True)
            yh = yh * lax.rsqrt(ss + EPS)
            if mode == "q":
                yh = yh * (DN_DK ** -0.5)
            outs.append(yh)
        y = jnp.concatenate(outs, axis=1)
    o_ref[...] = y.astype(o_ref.dtype)
    if mode == "k":
        kt_ref = rest[0]
        kt_ref[...] = y.T.astype(kt_ref.dtype)


DN_PREP_COLS = 128


def _dn_prep(pbig, conv_w, col0, mode, B, S):
    M = B * S
    tw = DN_PREP_COLS
    col0_blocks = col0 // tw
    nj = DN_WIDTH // tw
    wcol0 = {"q": 0, "k": nj, "v": 2 * nj}[mode]
    kern = functools.partial(_dn_prep_kernel, mode=mode, seq_len=S)
    out_specs = [pl.BlockSpec((S, tw), lambda b, j: (b, j))]
    out_shape = [jax.ShapeDtypeStruct((M, DN_WIDTH), BF16)]
    if mode == "k":
        out_specs.append(pl.BlockSpec((tw, S), lambda b, j: (j, b)))
        out_shape.append(jax.ShapeDtypeStruct((DN_WIDTH, M), BF16))
    return pl.pallas_call(
        kern,
        grid=(B, nj),
        in_specs=[pl.BlockSpec((S, tw), lambda b, j: (b, col0_blocks + j)),
                  pl.BlockSpec((CONV_WIDTH, tw), lambda b, j: (0, wcol0 + j))],
        out_specs=out_specs,
        out_shape=out_shape,
        compiler_params=_cparams(("arbitrary", "arbitrary")),
        name="dn_prep_" + mode,
    )(pbig, conv_w)


def _dn_gate_kernel(ab_ref, alog_ref, dtb_ref, o_ref, ot_ref, *, tm):
    ab = ab_ref[...]
    x = ab + dtb_ref[...]
    softplus = jnp.maximum(x, 0.0) + jnp.log(1.0 + jnp.exp(-jnp.abs(x)))
    g = -jnp.exp(alog_ref[...]) * softplus
    beta = _sigmoid(ab)
    lane = lax.broadcasted_iota(jnp.int32, (DN_CHUNK, LANES), 1)
    r = lax.broadcasted_iota(jnp.int32, (DN_CHUNK, DN_CHUNK), 0)
    c = lax.broadcasted_iota(jnp.int32, (DN_CHUNK, DN_CHUNK), 1)
    tril = jnp.where(r >= c, 1.0, 0.0).astype(F32)
    for ci in range(tm // DN_CHUNK):
        rows = slice(ci * DN_CHUNK, (ci + 1) * DN_CHUNK)
        dec = _dot_f32(tril, g[rows])
        out = jnp.where(lane < DN_HEADS, dec, beta[rows])
        o_ref[rows, :] = out
        ot_ref[:, rows] = out.T


def _dn_gate(ptail, alog_row, dtb_row):
    M = ptail.shape[0]
    tm = 512
    kern = functools.partial(_dn_gate_kernel, tm=tm)
    return pl.pallas_call(
        kern,
        grid=(M // tm,),
        in_specs=[pl.BlockSpec((tm, LANES), lambda i: (i, 14)),
                  pl.BlockSpec((1, LANES), lambda i: (0, 0)),
                  pl.BlockSpec((1, LANES), lambda i: (0, 0))],
        out_specs=[pl.BlockSpec((tm, LANES), lambda i: (i, 0)),
                   pl.BlockSpec((LANES, tm), lambda i: (0, i))],
        out_shape=[jax.ShapeDtypeStruct((M, LANES), F32),
                   jax.ShapeDtypeStruct((LANES, M), F32)],
        compiler_params=_cparams(("arbitrary",)),
        name="dn_gate",
    )(ptail, alog_row, dtb_row)


def _dn_scan_kernel(q_ref, k_ref, v_ref, kt_ref, dec_ref, dect_ref, z_ref, gain_ref, o_ref, st_ref):
    C = DN_CHUNK

    @pl.when(pl.program_id(1) == 0)
    def _():
        st_ref[...] = jnp.zeros(st_ref.shape, F32)

    dec = dec_ref[...]
    dect = dect_ref[...]
    r = lax.broadcasted_iota(jnp.int32, (C, C), 0)
    c = lax.broadcasted_iota(jnp.int32, (C, C), 1)
    tril = r >= c
    strict = r > c
    gain = gain_ref[...]
    eye = jnp.where(r == c, 1.0, 0.0).astype(F32)
    lvl_masks = []
    for lg in range(int(math.log2(C))):
        lvl_masks.append(((r >> (lg + 1)) == (c >> (lg + 1))) & ((r >> lg) != (c >> lg)))

    for h0 in range(0, DN_HEADS, DN_HEAD_GROUP):
        heads = range(h0, h0 + DN_HEAD_GROUP)
        cols = lambda ref: jnp.stack([ref[:, h * DN_DK:(h + 1) * DN_DK] for h in heads])
        qh = cols(q_ref).astype(F32)
        kh = cols(k_ref).astype(F32)
        vh = cols(v_ref).astype(F32)
        kth = jnp.stack([kt_ref[h * DN_DK:(h + 1) * DN_DK, :] for h in heads])
        dcol = jnp.stack([jnp.broadcast_to(dec[:, h:h + 1], (C, C)) for h in heads])
        bcol = jnp.stack([jnp.broadcast_to(dec[:, DN_HEADS + h:DN_HEADS + h + 1], (C, C)) for h in heads])
        drow = jnp.stack([jnp.broadcast_to(dect[h:h + 1, :], (C, C)) for h in heads])
        dlast = jnp.stack([jnp.broadcast_to(dect[h:h + 1, C - 1:C], (C, C)) for h in heads])
        lmat = jnp.exp(jnp.where(tril[None], dcol - drow, NEG))
        e_d = jnp.exp(dcol)
        kb = kh * bcol
        vb = vh * bcol
        a = jnp.where(strict[None], _bmm(kb.astype(BF16), kth) * lmat, 0.0)
        attn = jnp.where(tril[None], _bmm(qh.astype(BF16), kth) * lmat, 0.0)
        tinv = eye[None] - jnp.where(lvl_masks[0][None], a, 0.0)
        for lm in lvl_masks[1:]:
            t16 = tinv.astype(BF16)
            lo = jnp.where(lm[None], a, 0.0).astype(BF16)
            tinv = tinv - _bmm(t16, _bmm(lo, t16).astype(BF16))
        t16 = tinv.astype(BF16)
        u = _bmm(t16, vb.astype(BF16))
        w = _bmm(t16, (kb * e_d).astype(BF16))
        st = st_ref[h0:h0 + DN_HEAD_GROUP]
        st16 = st.astype(BF16)
        v_new = u - _bmm(w.astype(BF16), st16)
        v_new16 = v_new.astype(BF16)
        o = _bmm((qh * e_d).astype(BF16), st16) + _bmm(attn.astype(BF16), v_new16)
        kdt = (kth.astype(F32) * jnp.exp(dlast - drow)).astype(BF16)
        st_ref[h0:h0 + DN_HEAD_GROUP] = st * jnp.exp(dlast) + _bmm(kdt, v_new16)
        ms = jnp.mean(o * o, axis=-1, keepdims=True)
        y = o * lax.rsqrt(ms + EPS) * gain[None]
        for i, h in enumerate(heads):
            sl = slice(h * DN_DV, (h + 1) * DN_DV)
            o_ref[:, sl] = (y[i] * _silu(z_ref[:, sl].astype(F32))).astype(o_ref.dtype)


def _dn_scan(dq, dk, dv, dkt, dec, dect, pbig, z_col_block, gain_row, B, S):
    M = B * S
    C = DN_CHUNK
    nc = S // C
    tok = lambda b, n: (b * nc + n, 0)
    return pl.pallas_call(
        _dn_scan_kernel,
        grid=(B, nc),
        in_specs=[pl.BlockSpec((C, DN_WIDTH), tok),
                  pl.BlockSpec((C, DN_WIDTH), tok),
                  pl.BlockSpec((C, DN_WIDTH), tok),
                  pl.BlockSpec((DN_WIDTH, C), lambda b, n: (0, b * nc + n)),
                  pl.BlockSpec((C, LANES), tok),
                  pl.BlockSpec((LANES, C), lambda b, n: (0, b * nc + n)),
                  pl.BlockSpec((C, DN_WIDTH), lambda b, n: (b * nc + n, z_col_block)),
                  pl.BlockSpec((1, DN_DV), lambda b, n: (0, 0))],
        out_specs=pl.BlockSpec((C, DN_WIDTH), tok),
        out_shape=jax.ShapeDtypeStruct((M, DN_WIDTH), BF16),
        scratch_shapes=[pltpu.VMEM((DN_HEADS, DN_DK, DN_DV), F32)],
        compiler_params=_cparams(("arbitrary", "arbitrary")),
        name="dn_scan",
    )(dq, dk, dv, dkt, dec, dect, pbig, gain_row)


def _out1_kernel(oa_ref, ob_ref, wa_ref, wb_ref, ga_ref, gb_ref, o_ref):
    ya = _dot(oa_ref[...], wa_ref[...])
    yb = _dot(ob_ref[...], wb_ref[...])
    mix = _sigmoid(ga_ref[...].astype(F32)) * ya + _sigmoid(gb_ref[...].astype(F32)) * yb
    o_ref[...] = mix.astype(o_ref.dtype)


def _out1(o_a, o_b, wa, wb, pbig, D):
    M, K = o_a.shape
    tm = min(1024, M)
    tn = min(1024, D)
    nbd = D // tn
    return pl.pallas_call(
        _out1_kernel,
        grid=(D // tn, M // tm),
        in_specs=[pl.BlockSpec((tm, K), lambda j, i: (i, 0)),
                  pl.BlockSpec((tm, K), lambda j, i: (i, 0)),
                  pl.BlockSpec((K, tn), lambda j, i: (0, j)),
                  pl.BlockSpec((K, tn), lambda j, i: (0, j)),
                  pl.BlockSpec((tm, tn), lambda j, i: (i, j)),
                  pl.BlockSpec((tm, tn), lambda j, i: (i, nbd + j))],
        out_specs=pl.BlockSpec((tm, tn), lambda j, i: (i, j)),
        out_shape=jax.ShapeDtypeStruct((M, D), BF16),
        compiler_params=_cparams(("arbitrary", "arbitrary")),
        name="out1",
    )(o_a, o_b, wa, wb, pbig, pbig)


def _final_kernel(mix_ref, x_ref, gate_ref, fg_ref, o_ref):
    xn = x_ref[...] + gate_ref[0] * mix_ref[...].astype(F32)
    ms = jnp.mean(xn * xn, axis=-1, keepdims=True)
    o_ref[...] = xn * lax.rsqrt(ms + EPS) * fg_ref[...]


def _final(mixed, x2, mod3, final_gain, S):
    M, D = x2.shape
    tm = 256
    nb = S // tm
    return pl.pallas_call(
        _final_kernel,
        grid=(M // tm,),
        in_specs=[pl.BlockSpec((tm, D), lambda i: (i, 0)),
                  pl.BlockSpec((tm, D), lambda i: (i, 0)),
                  pl.BlockSpec((1, 1, D), lambda i: (i // nb, 0, 2)),
                  pl.BlockSpec((1, D), lambda i: (0, 0))],
        out_specs=pl.BlockSpec((tm, D), lambda i: (i, 0)),
        out_shape=jax.ShapeDtypeStruct((M, D), F32),
        compiler_params=_cparams(("arbitrary",)),
        name="final",
    )(mixed, x2, mod3, final_gain)


def _pad_cols(w, width):
    return jnp.pad(w, ((0, 0), (0, width - w.shape[1])))


def _permute_w_in(w, D):
    widths = (NSA_WIDTH, 6 * KV_WIDTH, 3 * NSA_HEADS, NSA_WIDTH, 3 * DN_WIDTH, DN_HEADS, DN_HEADS, DN_WIDTH, 2 * D)
    offs = np.concatenate([[0], np.cumsum(widths)])
    seg = lambda i: w[:, int(offs[i]):int(offs[i + 1])]
    q, kv, g, z, dqkv, da, db, dz, mg = [seg(i) for i in range(9)]
    big = jnp.concatenate([mg, dqkv, q, z, dz], axis=1).astype(BF16)
    gcols = []
    for gi in range(NSA_GROUPS):
        idx = np.array([r * NSA_HEADS + gi * NSA_HPG + p for r in range(3) for p in range(NSA_HPG)])
        gcols.append(_pad_cols(g[:, idx], LANES))
    ab = _pad_cols(jnp.concatenate([da, db], axis=1), LANES)
    tail = jnp.concatenate([kv[:, 2 * KV_WIDTH:], kv[:, :2 * KV_WIDTH]] + gcols + [ab], axis=1).astype(BF16)
    return big, tail


def kernel(x, c, positions, w_ada, b_ada, norm_gain, w_in, cmp_pos_k, cmp_pos_v, w_cmp_k1, w_cmp_k2,
           w_cmp_v1, w_cmp_v2, conv_w, dt_bias, a_log, dn_norm_gain, w_proj_a, w_proj_b, w_out, final_gain):
    B, S, D = x.shape
    M = B * S
    depth = w_in.shape[0]
    assert S % SEL_TILE == 0 and S % DN_CHUNK == 0 and S >= WINDOW + Q_BLOCK and D % 512 == 0 and B <= 8
    assert S // SLC_LEN <= LANES and D % 1024 == 0 and DN_DK == DN_CHUNK
    assert depth == 1, "the final RMSNorm is fused into the last layer's output kernel"

    off_dq = 2 * D
    off_q = off_dq + 3 * DN_WIDTH
    off_z = off_q + NSA_WIDTH
    off_dz = off_z + NSA_WIDTH

    x2 = x.reshape(M, D)
    pos_col = positions.reshape(M, 1)
    cmp_end = np.arange(S // CMP_STRIDE - 1) * CMP_STRIDE + CMP_LEN - 1
    pos_cmp = jnp.pad(positions[:, cmp_end], ((0, 0), (0, 1)))[:, :, None]
    c8 = jnp.pad(c, ((0, 8 - B), (0, 0)))

    for l in range(depth):
        mod = _ada(c8, w_ada[l], b_ada[l][None, :])
        mod3 = mod[:B].reshape(B, 1, 3 * D)
        h = _norm(x2, norm_gain[l][None, :], mod3, S)

        w_big, w_tail = _permute_w_in(w_in[l], D)
        tm = min(1024, M)
        pbig = _matmul(h, w_big, BF16, tm, 1024, "proj_big")
        ptail = _matmul(h, w_tail, F32, tm, 640, "proj_tail")

        q_t, keys, v_t = _nsa_prep(pbig, ptail, pos_col, off_q // NSA_WIDTH, S)
        kcmp = _compress(ptail, 8, cmp_pos_k[l], w_cmp_k1[l], w_cmp_k2[l], pos_cmp, B, S, True)
        vcmp = _compress(ptail, 10, cmp_pos_v[l], w_cmp_v1[l], w_cmp_v2[l].T, pos_cmp, B, S, False)
        o_a = _nsa_attn(q_t, kcmp, vcmp, keys, v_t, ptail, pbig, off_z // NSA_WIDTH, B, S)

        cw = conv_w[l]
        dq = _dn_prep(pbig, cw, off_dq, "q", B, S)[0]
        dk, dkt = _dn_prep(pbig, cw, off_dq + DN_WIDTH, "k", B, S)
        dv = _dn_prep(pbig, cw, off_dq + 2 * DN_WIDTH, "v", B, S)[0]
        alog_row = _pad_cols(a_log[l][None, :].astype(F32), LANES)
        dtb_row = _pad_cols(dt_bias[l][None, :].astype(F32), LANES)
        dec, dect = _dn_gate(ptail, alog_row, dtb_row)
        o_b = _dn_scan(dq, dk, dv, dkt, dec, dect, pbig, off_dz // DN_WIDTH,
                       dn_norm_gain[l][None, :], B, S)

        mixed_in = _out1(o_a, o_b, w_proj_a[l].astype(BF16), w_proj_b[l].astype(BF16), pbig, D)
        mixed = _matmul(mixed_in, w_out[l].astype(BF16), BF16, min(1024, M), min(1024, D), "out2")
        x2 = _final(mixed, x2, mod3, final_gain[None, :], S)
    return x2.reshape(B, S, D)
```

```python
import functools
import math

import numpy as np
import jax
import jax.numpy as jnp
from jax import lax
from jax.experimental import pallas as pl
from jax.experimental.pallas import tpu as pltpu

F32 = jnp.float32
BF16 = jnp.bfloat16

NSA_HEADS = 16
NSA_GROUPS = 2
NSA_HPG = NSA_HEADS // NSA_GROUPS
HEAD_DIM = 128
ROT_DIM = HEAD_DIM // 4
ROT_HALF = ROT_DIM // 2
ROPE_THETA = 500000.0
CMP_LEN = 32
CMP_STRIDE = 16
CMP_HIDDEN = 256
SLC_LEN = 64
SLC_TOPK = 16
WINDOW = 512
Q_BLOCK = 128
NSA_WIDTH = NSA_HEADS * HEAD_DIM
KV_WIDTH = NSA_GROUPS * HEAD_DIM
DN_HEADS = 16
DN_DK = 128
DN_DV = 128
DN_WIDTH = DN_HEADS * DN_DV
CONV_WIDTH = 4
EPS = 1e-6

LANES = 128
VMEM_LIMIT_BYTES = 56 * 1024 * 1024

DN_CHUNK = 128
DN_HEAD_GROUP = 8
SEL_TILE = 512
SEL_COL_CHUNKS = 1
NEG = -1e30


def _cparams(sem):
    return pltpu.CompilerParams(dimension_semantics=sem, vmem_limit_bytes=VMEM_LIMIT_BYTES)


def _sigmoid(x):
    return 1.0 / (1.0 + jnp.exp(-x))


def _silu(x):
    return x * _sigmoid(x)


def _dot(a, b):
    return jnp.dot(a, b, preferred_element_type=F32)


def _dot_nt(a, b):
    return lax.dot_general(a, b, (((1,), (1,)), ((), ())), preferred_element_type=F32)


def _bmm(a, b):
    return lax.dot_general(a, b, (((2,), (1,)), ((0,), (0,))), preferred_element_type=F32)


def _dot_f32(a, b):
    return jnp.dot(a, b, preferred_element_type=F32, precision=lax.Precision.HIGHEST)


def _ada_kernel(c_ref, w_ref, b_ref, o_ref):
    c = c_ref[...]
    c_hi = c.astype(BF16)
    c_lo = (c - c_hi.astype(F32)).astype(BF16)
    w = w_ref[...]
    w_hi = w.astype(BF16)
    w_lo = (w - w_hi.astype(F32)).astype(BF16)
    acc = _dot(c_hi, w_hi) + _dot(c_lo, w_hi) + _dot(c_hi, w_lo)
    o_ref[...] = acc + b_ref[...]


def _ada(c8, w_ada, b_ada):
    D, N = w_ada.shape
    tn = min(512, N)
    return pl.pallas_call(
        _ada_kernel,
        grid=(N // tn,),
        in_specs=[pl.BlockSpec((8, D), lambda j: (0, 0)),
                  pl.BlockSpec((D, tn), lambda j: (0, j)),
                  pl.BlockSpec((1, tn), lambda j: (0, j))],
        out_specs=pl.BlockSpec((8, tn), lambda j: (0, j)),
        out_shape=jax.ShapeDtypeStruct((8, N), F32),
        compiler_params=_cparams(("arbitrary",)),
        name="ada",
    )(c8, w_ada, b_ada)


def _norm_kernel(x_ref, gain_ref, shift_ref, scale_ref, o_ref):
    x = x_ref[...]
    ms = jnp.mean(x * x, axis=-1, keepdims=True)
    y = x * lax.rsqrt(ms + EPS) * gain_ref[...]
    o_ref[...] = (y * (1.0 + scale_ref[0]) + shift_ref[0]).astype(o_ref.dtype)


def _norm(x2, gain, mod3, S):
    M, D = x2.shape
    tm = 256
    nb = S // tm
    return pl.pallas_call(
        _norm_kernel,
        grid=(M // tm,),
        in_specs=[pl.BlockSpec((tm, D), lambda i: (i, 0)),
                  pl.BlockSpec((1, D), lambda i: (0, 0)),
                  pl.BlockSpec((1, 1, D), lambda i: (i // nb, 0, 0)),
                  pl.BlockSpec((1, 1, D), lambda i: (i // nb, 0, 1))],
        out_specs=pl.BlockSpec((tm, D), lambda i: (i, 0)),
        out_shape=jax.ShapeDtypeStruct((M, D), BF16),
        compiler_params=_cparams(("arbitrary",)),
        name="norm",
    )(x2, gain, mod3, mod3)


MM_TM = 1024
MM_TN = 512


def _mm_kernel(a_ref, w_ref, o_ref, w16_ref, *, w_is_nk):
    @pl.when(pl.program_id(1) == 0)
    def _():
        w16_ref[...] = w_ref[...].astype(BF16)

    y = _dot_nt(a_ref[...], w16_ref[...]) if w_is_nk else _dot(a_ref[...], w16_ref[...])
    o_ref[...] = y.astype(o_ref.dtype)


def _matmul(a, w, out_dtype, name):
    M, K = a.shape
    N = w.shape[1]
    tm, tn = min(MM_TM, M), min(MM_TN, N)
    return pl.pallas_call(
        functools.partial(_mm_kernel, w_is_nk=False),
        grid=(N // tn, M // tm),
        in_specs=[pl.BlockSpec((tm, K), lambda j, i: (i, 0)),
                  pl.BlockSpec((K, tn), lambda j, i: (0, j))],
        out_specs=pl.BlockSpec((tm, tn), lambda j, i: (i, j)),
        out_shape=jax.ShapeDtypeStruct((M, N), out_dtype),
        scratch_shapes=[pltpu.VMEM((K, tn), BF16)],
        compiler_params=_cparams(("arbitrary", "arbitrary")),
        name=name,
    )(a, w)


W_ROW_ALIGN = 16


def _proj(h, w_nk, row_offsets, out_dtype, name):
    M, K = h.shape
    tm, tn = min(MM_TM, M), MM_TN
    assert all(int(o) % W_ROW_ALIGN == 0 and int(o) + tn <= w_nk.shape[0] for o in row_offsets)
    offs = jnp.asarray(np.asarray(row_offsets, np.int32) // W_ROW_ALIGN)
    grid_spec = pltpu.PrefetchScalarGridSpec(
        num_scalar_prefetch=1,
        grid=(len(row_offsets), M // tm),
        in_specs=[pl.BlockSpec((tm, K), lambda j, i, o: (i, 0)),
                  pl.BlockSpec((pl.Element(tn), pl.Element(K)), lambda j, i, o: (o[j] * W_ROW_ALIGN, 0))],
        out_specs=pl.BlockSpec((tm, tn), lambda j, i, o: (i, j)),
        scratch_shapes=[pltpu.VMEM((tn, K), BF16)],
    )

    def kern(o_smem, a_ref, w_ref, o_ref, w16_ref):
        _mm_kernel(a_ref, w_ref, o_ref, w16_ref, w_is_nk=True)

    return pl.pallas_call(
        kern,
        grid_spec=grid_spec,
        out_shape=jax.ShapeDtypeStruct((M, len(row_offsets) * tn), out_dtype),
        compiler_params=_cparams(("arbitrary", "arbitrary")),
        name=name,
    )(offs, h, w_nk)


def _rope_consts():
    inv = ROPE_THETA ** (-np.arange(ROT_HALF, dtype=np.float64) / ROT_HALF)
    invf = np.zeros((1, LANES), np.float32)
    invf[0, :ROT_HALF] = inv
    invf[0, ROT_HALF:ROT_DIM] = inv
    sgn = np.zeros((1, LANES), np.float32)
    sgn[0, :ROT_HALF] = -1.0
    sgn[0, ROT_HALF:ROT_DIM] = 1.0
    return jnp.asarray(invf), jnp.asarray(sgn)


def _rope_tables(pos_col, invf, sgn):
    ang = pos_col.astype(F32) * invf
    return jnp.cos(ang), jnp.sin(ang) * sgn


def _rope_apply(x, cos_t, sin_t):
    lane = lax.broadcasted_iota(jnp.int32, x.shape, 1)
    partner = jnp.where(lane < ROT_HALF,
                        pltpu.roll(x, LANES - ROT_HALF, 1),
                        pltpu.roll(x, ROT_HALF, 1))
    return x * cos_t + partner * sin_t


KEYS_WIDTH = 6 * HEAD_DIM


def _nsa_prep_kernel(q_ref, kv_ref, pos_ref, invf_ref, sgn_ref, qt_ref, keys_ref, vt_ref, *, seq_len):
    tm = q_ref.shape[0]
    cos_t, sin_t = _rope_tables(pos_ref[...], invf_ref[...], sgn_ref[...])
    qscale = (HEAD_DIM ** -0.5) * math.log2(math.e)
    for h in range(NSA_HEADS):
        sl = slice(h * HEAD_DIM, (h + 1) * HEAD_DIM)
        xq = _rope_apply(q_ref[:, sl].astype(F32), cos_t, sin_t) * qscale
        qt_ref[sl, :] = xq.T.astype(qt_ref.dtype)
    tok = (pl.program_id(0) * tm) % seq_len + lax.broadcasted_iota(jnp.int32, (tm, LANES), 0)
    lane = lax.broadcasted_iota(jnp.int32, (tm, LANES), 1)
    onehot = jnp.where(lane == (tok >> int(math.log2(SLC_LEN))), 1.0, 0.0).astype(keys_ref.dtype)
    blk = lambda i: kv_ref[:, i * HEAD_DIM:(i + 1) * HEAD_DIM]
    for g in range(NSA_GROUPS):
        ks = _rope_apply(blk(g), cos_t, sin_t)
        keys_ref[:, (2 * g) * HEAD_DIM:(2 * g + 1) * HEAD_DIM] = ks.astype(keys_ref.dtype)
        keys_ref[:, (2 * g + 1) * HEAD_DIM:(2 * g + 2) * HEAD_DIM] = onehot
        kw = _rope_apply(blk(4 + g), cos_t, sin_t)
        keys_ref[:, (4 + g) * HEAD_DIM:(5 + g) * HEAD_DIM] = kw.astype(keys_ref.dtype)
        vt_ref[g * HEAD_DIM:(g + 1) * HEAD_DIM, :] = blk(2 + g).T.astype(vt_ref.dtype)
        vt_ref[(2 + g) * HEAD_DIM:(3 + g) * HEAD_DIM, :] = blk(6 + g).T.astype(vt_ref.dtype)


def _nsa_prep(pbig, ptail, pos_col, q_col_block, S):
    M = pbig.shape[0]
    tm = 256
    invf, sgn = _rope_consts()
    return pl.pallas_call(
        functools.partial(_nsa_prep_kernel, seq_len=S),
        grid=(M // tm,),
        in_specs=[pl.BlockSpec((tm, NSA_WIDTH), lambda i: (i, q_col_block)),
                  pl.BlockSpec((tm, 1024), lambda i: (i, 0)),
                  pl.BlockSpec((tm, 1), lambda i: (i, 0)),
                  pl.BlockSpec((1, LANES), lambda i: (0, 0)),
                  pl.BlockSpec((1, LANES), lambda i: (0, 0))],
        out_specs=[pl.BlockSpec((NSA_WIDTH, tm), lambda i: (0, i)),
                   pl.BlockSpec((tm, KEYS_WIDTH), lambda i: (i, 0)),
                   pl.BlockSpec((4 * HEAD_DIM, tm), lambda i: (0, i))],
        out_shape=[jax.ShapeDtypeStruct((NSA_WIDTH, M), BF16),
                   jax.ShapeDtypeStruct((M, KEYS_WIDTH), BF16),
                   jax.ShapeDtypeStruct((4 * HEAD_DIM, M), BF16)],
        compiler_params=_cparams(("arbitrary",)),
        name="nsa_prep",
    )(pbig, ptail, pos_col, invf, sgn)


def _compress_kernel(x_ref, pe_ref, w1_ref, w2_ref, pos_ref, invf_ref, sgn_ref, o_ref, *, rope, n_rows):
    half = CMP_LEN // 2
    top = jnp.zeros((n_rows, CMP_HIDDEN), F32)
    bot = jnp.zeros((n_rows, CMP_HIDDEN), F32)
    for l in range(half):
        xl = x_ref[pl.ds(l, n_rows, stride=CMP_STRIDE), :]
        w_top = w1_ref[l * HEAD_DIM:(l + 1) * HEAD_DIM, :].astype(BF16)
        w_bot = w1_ref[(half + l) * HEAD_DIM:(half + l + 1) * HEAD_DIM, :].astype(BF16)
        top = top + _dot((xl + pe_ref[l:l + 1, :]).astype(BF16), w_top)
        bot = bot + _dot((xl + pe_ref[half + l:half + l + 1, :]).astype(BF16), w_bot)
    hid = top + pltpu.roll(bot, n_rows - 1, 0)
    act = _silu(hid).astype(BF16)
    if rope:
        out = _dot(act, w2_ref[...].astype(BF16))
        cos_t, sin_t = _rope_tables(pos_ref[0], invf_ref[...], sgn_ref[...])
        o_ref[0, 0] = _rope_apply(out, cos_t, sin_t).astype(o_ref.dtype)
    else:
        o_ref[0, 0] = _dot_nt(w2_ref[...].astype(BF16), act).astype(o_ref.dtype)


def _compress(ptail, col_block0, pe, w1, w2, pos_cmp, B, S, rope):
    n_rows = S // CMP_STRIDE
    invf, sgn = _rope_consts()
    kern = functools.partial(_compress_kernel, rope=rope, n_rows=n_rows)
    out_dims = (n_rows, HEAD_DIM) if rope else (HEAD_DIM, n_rows)
    return pl.pallas_call(
        kern,
        grid=(B, NSA_GROUPS),
        in_specs=[pl.BlockSpec((S, HEAD_DIM), lambda b, g: (b, col_block0 + g)),
                  pl.BlockSpec((CMP_LEN, HEAD_DIM), lambda b, g: (0, 0)),
                  pl.BlockSpec((CMP_LEN * HEAD_DIM, CMP_HIDDEN), lambda b, g: (0, 0)),
                  pl.BlockSpec(w2.shape, lambda b, g: (0, 0)),
                  pl.BlockSpec((1, n_rows, 1), lambda b, g: (b, 0, 0)),
                  pl.BlockSpec((1, LANES), lambda b, g: (0, 0)),
                  pl.BlockSpec((1, LANES), lambda b, g: (0, 0))],
        out_specs=pl.BlockSpec((1, 1) + out_dims, lambda b, g: (b, g, 0, 0)),
        out_shape=jax.ShapeDtypeStruct((B, NSA_GROUPS) + out_dims, BF16),
        compiler_params=_cparams(("arbitrary", "arbitrary")),
        name="compress_k" if rope else "compress_v",
    )(ptail, pe, w1, w2, pos_cmp, invf, sgn)


def _tile_lanes(x, n):
    return jnp.concatenate([x] * n, axis=1)


def _nsa_attn_kernel(qt_ref, kc_ref, vct_ref, ks_ref, vst_ref, kw_ref, vwt_ref, g_ref, z_ref,
                     ovt_ref, o_ref, m_s, l_s, acc_s, qa_s, *, seq_len):
    T = Q_BLOCK
    P = NSA_HPG
    R = P * T
    n_cmp_pad = seq_len // CMP_STRIDE
    n_cmp = n_cmp_pad - 1
    nb = seq_len // SLC_LEN
    qi = pl.program_id(2)
    qs = qi * T
    d0 = pl.multiple_of(qs, T)
    wlen = WINDOW + T
    w0 = pl.multiple_of(jnp.maximum(qs - WINDOW, 0), T)

    qt = jnp.concatenate([qt_ref[p * HEAD_DIM:(p + 1) * HEAD_DIM, :] for p in range(P)], axis=1)

    s_c = _dot(kc_ref[0, 0], qt)
    s_w = _dot(kw_ref[pl.ds(w0, wlen), :], qt)
    s_d = _dot(ks_ref[pl.ds(d0, T), 0:HEAD_DIM], qt)

    tok = lambda n: qs + lax.broadcasted_iota(jnp.int32, (n, T), 1)
    row = lambda n: lax.broadcasted_iota(jnp.int32, (n, T), 0)
    ok_c = (row(n_cmp_pad) * CMP_STRIDE + (CMP_LEN - 1) <= tok(n_cmp_pad)) & (row(n_cmp_pad) < n_cmp)
    key_w = w0 + row(wlen)
    ok_w = (key_w <= tok(wlen)) & (tok(wlen) - key_w < WINDOW)
    ok_d = row(T) <= lax.broadcasted_iota(jnp.int32, (T, T), 1)
    addmask = lambda ok: _tile_lanes(jnp.where(ok, 0.0, NEG).astype(F32), P)
    s_c = s_c + addmask(ok_c)
    s_w = s_w + addmask(ok_w)
    s_d = s_d + addmask(ok_d)

    m_c = jnp.max(s_c, axis=0, keepdims=True)
    m_w = jnp.max(s_w, axis=0, keepdims=True)
    m_d = jnp.max(s_d, axis=0, keepdims=True)
    m_c = jnp.where(m_c > 0.5 * NEG, m_c, 0.0)
    p_c = jnp.exp2(s_c - m_c)
    p_w = jnp.exp2(s_w - m_w)
    p_d = jnp.exp2(s_d - m_d)
    inv_c = 1.0 / jnp.maximum(jnp.sum(p_c, axis=0, keepdims=True), 1e-30)
    inv_w = 1.0 / jnp.sum(p_w, axis=0, keepdims=True)
    l_d = jnp.sum(p_d, axis=0, keepdims=True)
    o_c = _dot(vct_ref[0, 0], p_c.astype(BF16))
    o_w = _dot(vwt_ref[:, pl.ds(w0, wlen)], p_w.astype(BF16))
    m_s[...] = m_d
    l_s[...] = l_d
    acc_s[...] = _dot(vst_ref[:, pl.ds(d0, T)], p_d.astype(BF16))

    pn = p_c * inv_c
    p_sum = pn[:, 0:T]
    for p in range(1, P):
        p_sum = p_sum + pn[:, p * T:(p + 1) * T]
    imp = _dot_f32(ovt_ref[...], p_sum)[0:nb]
    jrow = row(nb)
    t_lane = tok(nb)
    tb = t_lane >> int(math.log2(SLC_LEN))
    visible = jrow * SLC_LEN <= t_lane
    forced = (jrow == 0) | (jrow == tb) | (jrow == tb - 1)
    score = jnp.where(forced, 1e9, jnp.where(visible, imp, -jnp.inf))
    cnt = jnp.zeros((nb, T), F32)
    for k in range(nb):
        rk = score[k:k + 1, :]
        cnt = cnt + jnp.where(jrow > k, jnp.where(rk >= score, 1.0, 0.0), jnp.where(rk > score, 1.0, 0.0))
    keep = (cnt < float(min(SLC_TOPK, nb))) & visible & (jrow < 2 * qi)
    bias = jnp.where(keep, 0.0, NEG).astype(F32)
    bias = jnp.concatenate([bias, jnp.full((LANES - nb, T), NEG, F32)], axis=0) if nb < LANES else bias
    qa_s[0:HEAD_DIM, :] = qt
    qa_s[HEAD_DIM:2 * HEAD_DIM, :] = _tile_lanes(bias.astype(BF16), P)

    cw = R // SEL_COL_CHUNKS

    def sel_body(kt, carry):
        k0 = pl.multiple_of(kt * SEL_TILE, SEL_TILE)
        k_t = ks_ref[pl.ds(k0, SEL_TILE), :]
        v_t = vst_ref[:, pl.ds(k0, SEL_TILE)]
        s_next = _dot(k_t, qa_s[:, 0:cw])
        for j in range(SEL_COL_CHUNKS):
            cs = slice(j * cw, (j + 1) * cw)
            s = s_next
            if j + 1 < SEL_COL_CHUNKS:
                s_next = _dot(k_t, qa_s[:, (j + 1) * cw:(j + 2) * cw])
            m_prev = m_s[:, cs]
            m_new = jnp.maximum(m_prev, jnp.max(s, axis=0, keepdims=True))
            alpha = jnp.exp2(m_prev - m_new)
            p = jnp.exp2(s - m_new)
            l_s[:, cs] = alpha * l_s[:, cs] + jnp.sum(p, axis=0, keepdims=True)
            acc_s[:, cs] = alpha * acc_s[:, cs] + _dot(v_t, p.astype(BF16))
            m_s[:, cs] = m_new
        return carry

    lax.fori_loop(0, (qs + SEL_TILE - 1) // SEL_TILE, sel_body, 0)

    gates_t = _sigmoid(g_ref[...]).T
    first_group = pl.program_id(1) == 0

    def grow(r):
        rows = []
        for p in range(P):
            h0 = r * NSA_HEADS + p
            rows.append(jnp.where(first_group, gates_t[h0:h0 + 1, :], gates_t[h0 + P:h0 + P + 1, :]))
        return jnp.concatenate(rows, axis=1)

    o_t = (o_c * (grow(0) * inv_c) + acc_s[...] * (grow(1) / l_s[...]) + o_w * (grow(2) * inv_w))
    for p in range(P):
        sl = slice(p * HEAD_DIM, (p + 1) * HEAD_DIM)
        o_p = o_t[:, p * T:(p + 1) * T].T
        o_ref[:, sl] = (o_p * _silu(z_ref[:, sl].astype(F32))).astype(o_ref.dtype)


def _nsa_consts(S):
    n_cmp_pad = S // CMP_STRIDE
    nb = S // SLC_LEN
    cmp_start = np.arange(n_cmp_pad) * CMP_STRIDE
    slc_start = np.arange(LANES) * SLC_LEN
    ov = ((cmp_start[:, None] <= slc_start[None, :] + SLC_LEN - 1)
          & (cmp_start[:, None] + CMP_LEN - 1 >= slc_start[None, :])
          & (np.arange(LANES)[None, :] < nb)
          & (np.arange(n_cmp_pad)[:, None] < n_cmp_pad - 1)).astype(np.float32)
    return jnp.asarray(ov.T)


def _nsa_attn(q_t, kcmp, vcmp_t, keys, v_t, ptail, pbig, z_col_block, B, S):
    M = B * S
    nq = S // Q_BLOCK
    gw = NSA_HPG * HEAD_DIM
    ovt = _nsa_consts(S)
    n_cmp_pad = S // CMP_STRIDE
    R = NSA_HPG * Q_BLOCK
    kern = functools.partial(_nsa_attn_kernel, seq_len=S)
    return pl.pallas_call(
        kern,
        grid=(B, NSA_GROUPS, nq),
        in_specs=[pl.BlockSpec((gw, Q_BLOCK), lambda b, g, i: (g, b * nq + i)),
                  pl.BlockSpec((1, 1, n_cmp_pad, HEAD_DIM), lambda b, g, i: (b, g, 0, 0)),
                  pl.BlockSpec((1, 1, HEAD_DIM, n_cmp_pad), lambda b, g, i: (b, g, 0, 0)),
                  pl.BlockSpec((S, 2 * HEAD_DIM), lambda b, g, i: (b, g)),
                  pl.BlockSpec((HEAD_DIM, S), lambda b, g, i: (g, b)),
                  pl.BlockSpec((S, HEAD_DIM), lambda b, g, i: (b, 4 + g)),
                  pl.BlockSpec((HEAD_DIM, S), lambda b, g, i: (2 + g, b)),
                  pl.BlockSpec((Q_BLOCK, LANES), lambda b, g, i: (b * nq + i, TAIL_GATE_BLOCK)),
                  pl.BlockSpec((Q_BLOCK, gw), lambda b, g, i: (b * nq + i, z_col_block * NSA_GROUPS + g)),
                  pl.BlockSpec((LANES, n_cmp_pad), lambda b, g, i: (0, 0))],
        out_specs=pl.BlockSpec((Q_BLOCK, gw), lambda b, g, i: (b * nq + i, g)),
        out_shape=jax.ShapeDtypeStruct((M, NSA_WIDTH), BF16),
        scratch_shapes=[pltpu.VMEM((1, R), F32),
                        pltpu.VMEM((1, R), F32),
                        pltpu.VMEM((HEAD_DIM, R), F32),
                        pltpu.VMEM((2 * HEAD_DIM, R), BF16)],
        compiler_params=_cparams(("arbitrary", "arbitrary", "arbitrary")),
        name="nsa_attn",
    )(q_t, kcmp, vcmp_t, keys, v_t, keys, v_t, ptail, pbig, ovt)


def _dn_prep_kernel(x_ref, w_ref, o_ref, *rest, mode, seq_len):
    x = x_ref[...].astype(F32)
    w = w_ref[...]
    row = lax.broadcasted_iota(jnp.int32, x.shape, 0)
    y = x * w[CONV_WIDTH - 1:CONV_WIDTH, :]
    for k in range(1, CONV_WIDTH):
        xs = jnp.where(row >= k, pltpu.roll(x, k, 0), 0.0)
        y = y + xs * w[CONV_WIDTH - 1 - k:CONV_WIDTH - k, :]
    y = _silu(y)
    if mode in ("q", "k"):
        outs = []
        for h in range(x.shape[1] // DN_DK):
            yh = y[:, h * DN_DK:(h + 1) * DN_DK]
            ss = jnp.sum(yh * yh, axis=-1, keepdims=True)
            yh = yh * lax.rsqrt(ss + EPS)
            if mode == "q":
                yh = yh * (DN_DK ** -0.5)
            outs.append(yh)
        y = jnp.concatenate(outs, axis=1)
    o_ref[...] = y.astype(o_ref.dtype)
    if mode == "k":
        kt_ref = rest[0]
        kt_ref[...] = y.T.astype(kt_ref.dtype)


DN_PREP_COLS = 128


def _dn_prep(pbig, conv_w, col0, mode, B, S):
    M = B * S
    tw = DN_PREP_COLS
    col0_blocks = col0 // tw
    nj = DN_WIDTH // tw
    wcol0 = {"q": 0, "k": nj, "v": 2 * nj}[mode]
    kern = functools.partial(_dn_prep_kernel, mode=mode, seq_len=S)
    out_specs = [pl.BlockSpec((S, tw), lambda b, j: (b, j))]
    out_shape = [jax.ShapeDtypeStruct((M, DN_WIDTH), BF16)]
    if mode == "k":
        out_specs.append(pl.BlockSpec((tw, S), lambda b, j: (j, b)))
        out_shape.append(jax.ShapeDtypeStruct((DN_WIDTH, M), BF16))
    return pl.pallas_call(
        kern,
        grid=(B, nj),
        in_specs=[pl.BlockSpec((S, tw), lambda b, j: (b, col0_blocks + j)),
                  pl.BlockSpec((CONV_WIDTH, tw), lambda b, j: (0, wcol0 + j))],
        out_specs=out_specs,
        out_shape=out_shape,
        compiler_params=_cparams(("arbitrary", "arbitrary")),
        name="dn_prep_" + mode,
    )(pbig, conv_w)


def _dn_gate_kernel(ab_ref, alog_ref, dtb_ref, o_ref, ot_ref, *, tm):
    ab = ab_ref[...]
    x = ab + dtb_ref[...]
    softplus = jnp.maximum(x, 0.0) + jnp.log(1.0 + jnp.exp(-jnp.abs(x)))
    g = -jnp.exp(alog_ref[...]) * softplus
    beta = _sigmoid(ab)
    lane = lax.broadcasted_iota(jnp.int32, (DN_CHUNK, LANES), 1)
    r = lax.broadcasted_iota(jnp.int32, (DN_CHUNK, DN_CHUNK), 0)
    c = lax.broadcasted_iota(jnp.int32, (DN_CHUNK, DN_CHUNK), 1)
    tril = jnp.where(r >= c, 1.0, 0.0).astype(F32)
    for ci in range(tm // DN_CHUNK):
        rows = slice(ci * DN_CHUNK, (ci + 1) * DN_CHUNK)
        dec = _dot_f32(tril, g[rows])
        out = jnp.where(lane < DN_HEADS, dec, beta[rows])
        o_ref[rows, :] = out
        ot_ref[:, rows] = out.T


def _dn_gate(ptail, alog_row, dtb_row):
    M = ptail.shape[0]
    tm = 512
    kern = functools.partial(_dn_gate_kernel, tm=tm)
    return pl.pallas_call(
        kern,
        grid=(M // tm,),
        in_specs=[pl.BlockSpec((tm, LANES), lambda i: (i, TAIL_AB_BLOCK)),
                  pl.BlockSpec((1, LANES), lambda i: (0, 0)),
                  pl.BlockSpec((1, LANES), lambda i: (0, 0))],
        out_specs=[pl.BlockSpec((tm, LANES), lambda i: (i, 0)),
                   pl.BlockSpec((LANES, tm), lambda i: (0, i))],
        out_shape=[jax.ShapeDtypeStruct((M, LANES), F32),
                   jax.ShapeDtypeStruct((LANES, M), F32)],
        compiler_params=_cparams(("arbitrary",)),
        name="dn_gate",
    )(ptail, alog_row, dtb_row)


def _dn_scan_kernel(q_ref, k_ref, v_ref, kt_ref, dec_ref, dect_ref, z_ref, gain_ref, o_ref, st_ref):
    C = DN_CHUNK

    @pl.when(pl.program_id(1) == 0)
    def _():
        st_ref[...] = jnp.zeros(st_ref.shape, F32)

    dec = dec_ref[...]
    dect = dect_ref[...]
    r = lax.broadcasted_iota(jnp.int32, (C, C), 0)
    c = lax.broadcasted_iota(jnp.int32, (C, C), 1)
    tril = r >= c
    strict = r > c
    gain = gain_ref[...]
    eye = jnp.where(r == c, 1.0, 0.0).astype(F32)
    lvl_masks = []
    for lg in range(int(math.log2(C))):
        lvl_masks.append(((r >> (lg + 1)) == (c >> (lg + 1))) & ((r >> lg) != (c >> lg)))

    for h0 in range(0, DN_HEADS, DN_HEAD_GROUP):
        heads = range(h0, h0 + DN_HEAD_GROUP)
        cols = lambda ref: jnp.stack([ref[:, h * DN_DK:(h + 1) * DN_DK] for h in heads])
        qh = cols(q_ref).astype(F32)
        kh = cols(k_ref).astype(F32)
        vh = cols(v_ref).astype(F32)
        kth = jnp.stack([kt_ref[h * DN_DK:(h + 1) * DN_DK, :] for h in heads])
        dcol = jnp.stack([jnp.broadcast_to(dec[:, h:h + 1], (C, C)) for h in heads])
        bcol = jnp.stack([jnp.broadcast_to(dec[:, DN_HEADS + h:DN_HEADS + h + 1], (C, C)) for h in heads])
        drow = jnp.stack([jnp.broadcast_to(dect[h:h + 1, :], (C, C)) for h in heads])
        dlast = jnp.stack([jnp.broadcast_to(dect[h:h + 1, C - 1:C], (C, C)) for h in heads])
        lmat = jnp.exp(jnp.where(tril[None], dcol - drow, NEG))
        e_d = jnp.exp(dcol)
        kb = kh * bcol
        vb = vh * bcol
        a = jnp.where(strict[None], _bmm(kb.astype(BF16), kth) * lmat, 0.0)
        attn = jnp.where(tril[None], _bmm(qh.astype(BF16), kth) * lmat, 0.0)
        tinv = eye[None] - jnp.where(lvl_masks[0][None], a, 0.0)
        for lm in lvl_masks[1:]:
            t16 = tinv.astype(BF16)
            lo = jnp.where(lm[None], a, 0.0).astype(BF16)
            tinv = tinv - _bmm(t16, _bmm(lo, t16).astype(BF16))
        t16 = tinv.astype(BF16)
        u = _bmm(t16, vb.astype(BF16))
        w = _bmm(t16, (kb * e_d).astype(BF16))
        st = st_ref[h0:h0 + DN_HEAD_GROUP]
        st16 = st.astype(BF16)
        v_new = u - _bmm(w.astype(BF16), st16)
        v_new16 = v_new.astype(BF16)
        o = _bmm((qh * e_d).astype(BF16), st16) + _bmm(attn.astype(BF16), v_new16)
        kdt = (kth.astype(F32) * jnp.exp(dlast - drow)).astype(BF16)
        st_ref[h0:h0 + DN_HEAD_GROUP] = st * jnp.exp(dlast) + _bmm(kdt, v_new16)
        ms = jnp.mean(o * o, axis=-1, keepdims=True)
        y = o * lax.rsqrt(ms + EPS) * gain[None]
        for i, h in enumerate(heads):
            sl = slice(h * DN_DV, (h + 1) * DN_DV)
            o_ref[:, sl] = (y[i] * _silu(z_ref[:, sl].astype(F32))).astype(o_ref.dtype)


def _dn_scan(dq, dk, dv, dkt, dec, dect, pbig, z_col_block, gain_row, B, S):
    M = B * S
    C = DN_CHUNK
    nc = S // C
    tok = lambda b, n: (b * nc + n, 0)
    return pl.pallas_call(
        _dn_scan_kernel,
        grid=(B, nc),
        in_specs=[pl.BlockSpec((C, DN_WIDTH), tok),
                  pl.BlockSpec((C, DN_WIDTH), tok),
                  pl.BlockSpec((C, DN_WIDTH), tok),
                  pl.BlockSpec((DN_WIDTH, C), lambda b, n: (0, b * nc + n)),
                  pl.BlockSpec((C, LANES), tok),
                  pl.BlockSpec((LANES, C), lambda b, n: (0, b * nc + n)),
                  pl.BlockSpec((C, DN_WIDTH), lambda b, n: (b * nc + n, z_col_block)),
                  pl.BlockSpec((1, DN_DV), lambda b, n: (0, 0))],
        out_specs=pl.BlockSpec((C, DN_WIDTH), tok),
        out_shape=jax.ShapeDtypeStruct((M, DN_WIDTH), BF16),
        scratch_shapes=[pltpu.VMEM((DN_HEADS, DN_DK, DN_DV), F32)],
        compiler_params=_cparams(("arbitrary", "arbitrary")),
        name="dn_scan",
    )(dq, dk, dv, dkt, dec, dect, pbig, gain_row)


def _out1_kernel(oa_ref, ob_ref, wa_ref, wb_ref, ga_ref, gb_ref, o_ref, wa16_ref, wb16_ref):
    @pl.when(pl.program_id(1) == 0)
    def _():
        wa16_ref[...] = wa_ref[...].astype(BF16)
        wb16_ref[...] = wb_ref[...].astype(BF16)

    ya = _dot(oa_ref[...], wa16_ref[...])
    yb = _dot(ob_ref[...], wb16_ref[...])
    mix = _sigmoid(ga_ref[...].astype(F32)) * ya + _sigmoid(gb_ref[...].astype(F32)) * yb
    o_ref[...] = mix.astype(o_ref.dtype)


def _out1(o_a, o_b, wa, wb, pbig, D):
    M, K = o_a.shape
    tm = min(MM_TM, M)
    tn = min(MM_TN, D)
    nbd = D // tn
    return pl.pallas_call(
        _out1_kernel,
        grid=(D // tn, M // tm),
        in_specs=[pl.BlockSpec((tm, K), lambda j, i: (i, 0)),
                  pl.BlockSpec((tm, K), lambda j, i: (i, 0)),
                  pl.BlockSpec((K, tn), lambda j, i: (0, j)),
                  pl.BlockSpec((K, tn), lambda j, i: (0, j)),
                  pl.BlockSpec((tm, tn), lambda j, i: (i, j)),
                  pl.BlockSpec((tm, tn), lambda j, i: (i, nbd + j))],
        out_specs=pl.BlockSpec((tm, tn), lambda j, i: (i, j)),
        out_shape=jax.ShapeDtypeStruct((M, D), BF16),
        scratch_shapes=[pltpu.VMEM((K, tn), BF16), pltpu.VMEM((K, tn), BF16)],
        compiler_params=_cparams(("arbitrary", "arbitrary")),
        name="out1",
    )(o_a, o_b, wa, wb, pbig, pbig)


def _final_kernel(mix_ref, x_ref, gate_ref, fg_ref, o_ref):
    xn = x_ref[...] + gate_ref[0] * mix_ref[...].astype(F32)
    ms = jnp.mean(xn * xn, axis=-1, keepdims=True)
    o_ref[...] = xn * lax.rsqrt(ms + EPS) * fg_ref[...]


def _final(mixed, x2, mod3, final_gain, S):
    M, D = x2.shape
    tm = 256
    nb = S // tm
    return pl.pallas_call(
        _final_kernel,
        grid=(M // tm,),
        in_specs=[pl.BlockSpec((tm, D), lambda i: (i, 0)),
                  pl.BlockSpec((tm, D), lambda i: (i, 0)),
                  pl.BlockSpec((1, 1, D), lambda i: (i // nb, 0, 2)),
                  pl.BlockSpec((1, D), lambda i: (0, 0))],
        out_specs=pl.BlockSpec((tm, D), lambda i: (i, 0)),
        out_shape=jax.ShapeDtypeStruct((M, D), F32),
        compiler_params=_cparams(("arbitrary",)),
        name="final",
    )(mixed, x2, mod3, final_gain)


def _pad_cols(w, width):
    return jnp.pad(w, ((0, 0), (0, width - w.shape[1])))


def _proj_row_offsets(D):
    widths = (NSA_WIDTH, 6 * KV_WIDTH, 3 * NSA_HEADS, NSA_WIDTH, 3 * DN_WIDTH, DN_HEADS, DN_HEADS, DN_WIDTH, 2 * D)
    offs = [int(o) for o in np.concatenate([[0], np.cumsum(widths)])]
    tiles = lambda seg: [offs[seg] + t * MM_TN for t in range(widths[seg] // MM_TN)]
    big = tiles(8) + tiles(4) + tiles(0) + tiles(3) + tiles(7)
    kv0 = offs[1]
    tail = [kv0 + 2 * KV_WIDTH, kv0 + 4 * KV_WIDTH, kv0, offs[2], offs[5]]
    return big, tail


TAIL_GATE_BLOCK = 3 * MM_TN // LANES
TAIL_AB_BLOCK = 4 * MM_TN // LANES
TAIL_CMP_BLOCK = 2 * MM_TN // LANES


def kernel(x, c, positions, w_ada, b_ada, norm_gain, w_in, cmp_pos_k, cmp_pos_v, w_cmp_k1, w_cmp_k2,
           w_cmp_v1, w_cmp_v2, conv_w, dt_bias, a_log, dn_norm_gain, w_proj_a, w_proj_b, w_out, final_gain):
    B, S, D = x.shape
    M = B * S
    depth = w_in.shape[0]
    assert S % SEL_TILE == 0 and S % DN_CHUNK == 0 and S >= WINDOW + Q_BLOCK and D % 512 == 0 and B <= 8
    assert S // SLC_LEN <= LANES and D % 1024 == 0 and DN_DK == DN_CHUNK
    assert depth == 1, "the final RMSNorm is fused into the last layer's output kernel"

    off_dq = 2 * D
    off_q = off_dq + 3 * DN_WIDTH
    off_z = off_q + NSA_WIDTH
    off_dz = off_z + NSA_WIDTH

    x2 = x.reshape(M, D)
    pos_col = positions.reshape(M, 1)
    cmp_end = np.arange(S // CMP_STRIDE - 1) * CMP_STRIDE + CMP_LEN - 1
    pos_cmp = jnp.pad(positions[:, cmp_end], ((0, 0), (0, 1)))[:, :, None]
    c8 = jnp.pad(c, ((0, 8 - B), (0, 0)))

    for l in range(depth):
        mod = _ada(c8, w_ada[l], b_ada[l][None, :])
        mod3 = mod[:B].reshape(B, 1, 3 * D)
        h = _norm(x2, norm_gain[l][None, :], mod3, S)

        w_nk = jnp.transpose(w_in[l])
        big_rows, tail_rows = _proj_row_offsets(D)
        pbig = _proj(h, w_nk, big_rows, BF16, "proj_big")
        ptail = _proj(h, w_nk, tail_rows, F32, "proj_tail")

        q_t, keys, v_t = _nsa_prep(pbig, ptail, pos_col, off_q // NSA_WIDTH, S)
        kcmp = _compress(ptail, TAIL_CMP_BLOCK, cmp_pos_k[l], w_cmp_k1[l], w_cmp_k2[l], pos_cmp, B, S, True)
        vcmp = _compress(ptail, TAIL_CMP_BLOCK + NSA_GROUPS, cmp_pos_v[l], w_cmp_v1[l], w_cmp_v2[l].T,
                         pos_cmp, B, S, False)
        o_a = _nsa_attn(q_t, kcmp, vcmp, keys, v_t, ptail, pbig, off_z // NSA_WIDTH, B, S)

        cw = conv_w[l]
        dq = _dn_prep(pbig, cw, off_dq, "q", B, S)[0]
        dk, dkt = _dn_prep(pbig, cw, off_dq + DN_WIDTH, "k", B, S)
        dv = _dn_prep(pbig, cw, off_dq + 2 * DN_WIDTH, "v", B, S)[0]
        alog_row = _pad_cols(a_log[l][None, :].astype(F32), LANES)
        dtb_row = _pad_cols(dt_bias[l][None, :].astype(F32), LANES)
        dec, dect = _dn_gate(ptail, alog_row, dtb_row)
        o_b = _dn_scan(dq, dk, dv, dkt, dec, dect, pbig, off_dz // DN_WIDTH,
                       dn_norm_gain[l][None, :], B, S)

        mixed_in = _out1(o_a, o_b, w_proj_a[l], w_proj_b[l], pbig, D)
        mixed = _matmul(mixed_in, w_out[l], BF16, "out2")
        x2 = _final(mixed, x2, mod3, final_gain[None, :], S)
    return x2.reshape(B, S, D)
```

```python
import functools
import math

import numpy as np
import jax
import jax.numpy as jnp
from jax import lax
from jax.experimental import pallas as pl
from jax.experimental.pallas import tpu as pltpu

F32 = jnp.float32
BF16 = jnp.bfloat16

NSA_HEADS = 16
NSA_GROUPS = 2
NSA_HPG = NSA_HEADS // NSA_GROUPS
HEAD_DIM = 128
ROT_DIM = HEAD_DIM // 4
ROT_HALF = ROT_DIM // 2
ROPE_THETA = 500000.0
CMP_LEN = 32
CMP_STRIDE = 16
CMP_HIDDEN = 256
SLC_LEN = 64
SLC_TOPK = 16
WINDOW = 512
Q_BLOCK = 128
NSA_WIDTH = NSA_HEADS * HEAD_DIM
KV_WIDTH = NSA_GROUPS * HEAD_DIM
DN_HEADS = 16
DN_DK = 128
DN_DV = 128
DN_WIDTH = DN_HEADS * DN_DV
CONV_WIDTH = 4
EPS = 1e-6

LANES = 128
VMEM_LIMIT_BYTES = 56 * 1024 * 1024

DN_CHUNK = 128
DN_HEAD_GROUP = 8
SEL_TILE = 512
VT_ROWS = HEAD_DIM + 16
NEG = -1e30


def _cparams(sem):
    return pltpu.CompilerParams(dimension_semantics=sem, vmem_limit_bytes=VMEM_LIMIT_BYTES)


def _sigmoid(x):
    return 1.0 / (1.0 + jnp.exp(-x))


def _silu(x):
    return x * _sigmoid(x)


def _dot(a, b):
    return jnp.dot(a, b, preferred_element_type=F32)


def _dot_nt(a, b):
    return lax.dot_general(a, b, (((1,), (1,)), ((), ())), preferred_element_type=F32)


def _bmm(a, b):
    return lax.dot_general(a, b, (((2,), (1,)), ((0,), (0,))), preferred_element_type=F32)


def _dot_f32(a, b):
    return jnp.dot(a, b, preferred_element_type=F32, precision=lax.Precision.HIGHEST)


def _ada_kernel(c_ref, w_ref, b_ref, o_ref):
    c = c_ref[...]
    c_hi = c.astype(BF16)
    c_lo = (c - c_hi.astype(F32)).astype(BF16)
    w = w_ref[...]
    w_hi = w.astype(BF16)
    w_lo = (w - w_hi.astype(F32)).astype(BF16)
    acc = _dot(c_hi, w_hi) + _dot(c_lo, w_hi) + _dot(c_hi, w_lo)
    o_ref[...] = acc + b_ref[...]


def _ada(c8, w_ada, b_ada):
    D, N = w_ada.shape
    tn = min(512, N)
    return pl.pallas_call(
        _ada_kernel,
        grid=(N // tn,),
        in_specs=[pl.BlockSpec((8, D), lambda j: (0, 0)),
                  pl.BlockSpec((D, tn), lambda j: (0, j)),
                  pl.BlockSpec((1, tn), lambda j: (0, j))],
        out_specs=pl.BlockSpec((8, tn), lambda j: (0, j)),
        out_shape=jax.ShapeDtypeStruct((8, N), F32),
        compiler_params=_cparams(("arbitrary",)),
        name="ada",
    )(c8, w_ada, b_ada)


def _norm_kernel(x_ref, gain_ref, shift_ref, scale_ref, o_ref):
    x = x_ref[...]
    ms = jnp.mean(x * x, axis=-1, keepdims=True)
    y = x * lax.rsqrt(ms + EPS) * gain_ref[...]
    o_ref[...] = (y * (1.0 + scale_ref[0]) + shift_ref[0]).astype(o_ref.dtype)


def _norm(x2, gain, mod3, S):
    M, D = x2.shape
    tm = 256
    nb = S // tm
    return pl.pallas_call(
        _norm_kernel,
        grid=(M // tm,),
        in_specs=[pl.BlockSpec((tm, D), lambda i: (i, 0)),
                  pl.BlockSpec((1, D), lambda i: (0, 0)),
                  pl.BlockSpec((1, 1, D), lambda i: (i // nb, 0, 0)),
                  pl.BlockSpec((1, 1, D), lambda i: (i // nb, 0, 1))],
        out_specs=pl.BlockSpec((tm, D), lambda i: (i, 0)),
        out_shape=jax.ShapeDtypeStruct((M, D), BF16),
        compiler_params=_cparams(("arbitrary",)),
        name="norm",
    )(x2, gain, mod3, mod3)


MM_TM = 512
MM_TN = 1024


def _mm_kernel(a_ref, w_ref, o_ref, w16_ref, *, w_is_nk):
    @pl.when(pl.program_id(1) == 0)
    def _():
        w16_ref[...] = w_ref[...].astype(BF16)

    y = _dot_nt(a_ref[...], w16_ref[...]) if w_is_nk else _dot(a_ref[...], w16_ref[...])
    o_ref[...] = y.astype(o_ref.dtype)


def _matmul(a, w, out_dtype, name):
    M, K = a.shape
    N = w.shape[1]
    tm, tn = min(MM_TM, M), min(MM_TN, N)
    return pl.pallas_call(
        functools.partial(_mm_kernel, w_is_nk=False),
        grid=(N // tn, M // tm),
        in_specs=[pl.BlockSpec((tm, K), lambda j, i: (i, 0)),
                  pl.BlockSpec((K, tn), lambda j, i: (0, j))],
        out_specs=pl.BlockSpec((tm, tn), lambda j, i: (i, j)),
        out_shape=jax.ShapeDtypeStruct((M, N), out_dtype),
        scratch_shapes=[pltpu.VMEM((K, tn), BF16)],
        compiler_params=_cparams(("arbitrary", "arbitrary")),
        name=name,
    )(a, w)


W_ROW_ALIGN = 16


def _proj(h, w_nk, row_offsets, tm, tn, out_dtype, name):
    M, K = h.shape
    tm = min(tm, M)
    assert all(int(o) % W_ROW_ALIGN == 0 and int(o) + tn <= w_nk.shape[0] for o in row_offsets)
    offs = jnp.asarray(np.asarray(row_offsets, np.int32) // W_ROW_ALIGN)
    grid_spec = pltpu.PrefetchScalarGridSpec(
        num_scalar_prefetch=1,
        grid=(len(row_offsets), M // tm),
        in_specs=[pl.BlockSpec((tm, K), lambda j, i, o: (i, 0)),
                  pl.BlockSpec((pl.Element(tn), pl.Element(K)), lambda j, i, o: (o[j] * W_ROW_ALIGN, 0))],
        out_specs=pl.BlockSpec((tm, tn), lambda j, i, o: (i, j)),
        scratch_shapes=[pltpu.VMEM((tn, K), BF16)],
    )

    def kern(o_smem, a_ref, w_ref, o_ref, w16_ref):
        _mm_kernel(a_ref, w_ref, o_ref, w16_ref, w_is_nk=True)

    return pl.pallas_call(
        kern,
        grid_spec=grid_spec,
        out_shape=jax.ShapeDtypeStruct((M, len(row_offsets) * tn), out_dtype),
        compiler_params=_cparams(("arbitrary", "arbitrary")),
        name=name,
    )(offs, h, w_nk)


def _rope_consts():
    inv = ROPE_THETA ** (-np.arange(ROT_HALF, dtype=np.float64) / ROT_HALF)
    invf = np.zeros((1, LANES), np.float32)
    invf[0, :ROT_HALF] = inv
    invf[0, ROT_HALF:ROT_DIM] = inv
    sgn = np.zeros((1, LANES), np.float32)
    sgn[0, :ROT_HALF] = -1.0
    sgn[0, ROT_HALF:ROT_DIM] = 1.0
    return jnp.asarray(invf), jnp.asarray(sgn)


def _rope_tables(pos_col, invf, sgn):
    ang = pos_col.astype(F32) * invf
    return jnp.cos(ang), jnp.sin(ang) * sgn


def _rope_apply(x, cos_t, sin_t):
    lane = lax.broadcasted_iota(jnp.int32, x.shape, 1)
    partner = jnp.where(lane < ROT_HALF,
                        pltpu.roll(x, LANES - ROT_HALF, 1),
                        pltpu.roll(x, ROT_HALF, 1))
    return x * cos_t + partner * sin_t


KEYS_WIDTH = 6 * HEAD_DIM


def _nsa_prep_kernel(q_ref, kv_ref, pos_ref, invf_ref, sgn_ref, qt_ref, keys_ref, vt_ref, *, seq_len):
    tm = q_ref.shape[0]
    cos_t, sin_t = _rope_tables(pos_ref[...], invf_ref[...], sgn_ref[...])
    qscale = (HEAD_DIM ** -0.5) * math.log2(math.e)
    for h in range(NSA_HEADS):
        sl = slice(h * HEAD_DIM, (h + 1) * HEAD_DIM)
        xq = _rope_apply(q_ref[:, sl].astype(F32), cos_t, sin_t) * qscale
        qt_ref[sl, :] = xq.T.astype(qt_ref.dtype)
    tok = (pl.program_id(0) * tm) % seq_len + lax.broadcasted_iota(jnp.int32, (tm, LANES), 0)
    lane = lax.broadcasted_iota(jnp.int32, (tm, LANES), 1)
    onehot = jnp.where(lane == (tok >> int(math.log2(SLC_LEN))), 1.0, 0.0).astype(keys_ref.dtype)
    blk = lambda i: kv_ref[:, i * HEAD_DIM:(i + 1) * HEAD_DIM]
    for g in range(NSA_GROUPS):
        ks = _rope_apply(blk(g), cos_t, sin_t)
        keys_ref[:, (2 * g) * HEAD_DIM:(2 * g + 1) * HEAD_DIM] = ks.astype(keys_ref.dtype)
        keys_ref[:, (2 * g + 1) * HEAD_DIM:(2 * g + 2) * HEAD_DIM] = onehot
        kw = _rope_apply(blk(4 + g), cos_t, sin_t)
        keys_ref[:, (4 + g) * HEAD_DIM:(5 + g) * HEAD_DIM] = kw.astype(keys_ref.dtype)
        ones = jnp.ones((VT_ROWS - HEAD_DIM, tm), vt_ref.dtype)
        for c, src in ((g, 2 + g), (2 + g, 6 + g)):
            vt_ref[c * VT_ROWS:c * VT_ROWS + HEAD_DIM, :] = blk(src).T.astype(vt_ref.dtype)
            vt_ref[c * VT_ROWS + HEAD_DIM:(c + 1) * VT_ROWS, :] = ones


def _nsa_prep(pbig, ptail, pos_col, q_col_block, S):
    M = pbig.shape[0]
    tm = 256
    invf, sgn = _rope_consts()
    return pl.pallas_call(
        functools.partial(_nsa_prep_kernel, seq_len=S),
        grid=(M // tm,),
        in_specs=[pl.BlockSpec((tm, NSA_WIDTH), lambda i: (i, q_col_block)),
                  pl.BlockSpec((tm, 1024), lambda i: (i, 0)),
                  pl.BlockSpec((tm, 1), lambda i: (i, 0)),
                  pl.BlockSpec((1, LANES), lambda i: (0, 0)),
                  pl.BlockSpec((1, LANES), lambda i: (0, 0))],
        out_specs=[pl.BlockSpec((NSA_WIDTH, tm), lambda i: (0, i)),
                   pl.BlockSpec((tm, KEYS_WIDTH), lambda i: (i, 0)),
                   pl.BlockSpec((4 * VT_ROWS, tm), lambda i: (0, i))],
        out_shape=[jax.ShapeDtypeStruct((NSA_WIDTH, M), BF16),
                   jax.ShapeDtypeStruct((M, KEYS_WIDTH), BF16),
                   jax.ShapeDtypeStruct((4 * VT_ROWS, M), BF16)],
        compiler_params=_cparams(("arbitrary",)),
        name="nsa_prep",
    )(pbig, ptail, pos_col, invf, sgn)


def _compress_kernel(x_ref, pe_ref, w1_ref, w2_ref, pos_ref, invf_ref, sgn_ref, o_ref, *, rope, n_rows):
    half = CMP_LEN // 2
    top = jnp.zeros((n_rows, CMP_HIDDEN), F32)
    bot = jnp.zeros((n_rows, CMP_HIDDEN), F32)
    for l in range(half):
        xl = x_ref[pl.ds(l, n_rows, stride=CMP_STRIDE), :]
        w_top = w1_ref[l * HEAD_DIM:(l + 1) * HEAD_DIM, :].astype(BF16)
        w_bot = w1_ref[(half + l) * HEAD_DIM:(half + l + 1) * HEAD_DIM, :].astype(BF16)
        top = top + _dot((xl + pe_ref[l:l + 1, :]).astype(BF16), w_top)
        bot = bot + _dot((xl + pe_ref[half + l:half + l + 1, :]).astype(BF16), w_bot)
    hid = top + pltpu.roll(bot, n_rows - 1, 0)
    act = _silu(hid).astype(BF16)
    if rope:
        out = _dot(act, w2_ref[...].astype(BF16))
        cos_t, sin_t = _rope_tables(pos_ref[0], invf_ref[...], sgn_ref[...])
        o_ref[0, 0] = _rope_apply(out, cos_t, sin_t).astype(o_ref.dtype)
    else:
        o_ref[0, 0, 0:HEAD_DIM, :] = _dot_nt(w2_ref[...].astype(BF16), act).astype(o_ref.dtype)
        o_ref[0, 0, HEAD_DIM:VT_ROWS, :] = jnp.ones((VT_ROWS - HEAD_DIM, n_rows), o_ref.dtype)


def _compress(ptail, col_block0, pe, w1, w2, pos_cmp, B, S, rope):
    n_rows = S // CMP_STRIDE
    invf, sgn = _rope_consts()
    kern = functools.partial(_compress_kernel, rope=rope, n_rows=n_rows)
    out_dims = (n_rows, HEAD_DIM) if rope else (VT_ROWS, n_rows)
    return pl.pallas_call(
        kern,
        grid=(B, NSA_GROUPS),
        in_specs=[pl.BlockSpec((S, HEAD_DIM), lambda b, g: (b, col_block0 + g)),
                  pl.BlockSpec((CMP_LEN, HEAD_DIM), lambda b, g: (0, 0)),
                  pl.BlockSpec((CMP_LEN * HEAD_DIM, CMP_HIDDEN), lambda b, g: (0, 0)),
                  pl.BlockSpec(w2.shape, lambda b, g: (0, 0)),
                  pl.BlockSpec((1, n_rows, 1), lambda b, g: (b, 0, 0)),
                  pl.BlockSpec((1, LANES), lambda b, g: (0, 0)),
                  pl.BlockSpec((1, LANES), lambda b, g: (0, 0))],
        out_specs=pl.BlockSpec((1, 1) + out_dims, lambda b, g: (b, g, 0, 0)),
        out_shape=jax.ShapeDtypeStruct((B, NSA_GROUPS) + out_dims, BF16),
        compiler_params=_cparams(("arbitrary", "arbitrary")),
        name="compress_k" if rope else "compress_v",
    )(ptail, pe, w1, w2, pos_cmp, invf, sgn)


def _tile_lanes(x, n):
    return jnp.concatenate([x] * n, axis=1)


def _nsa_attn_kernel(qt_ref, kc_ref, vct_ref, ks_ref, vst_ref, kw_ref, vwt_ref, g_ref, z_ref,
                     ovt_ref, o_ref, m_s, acc_s, qa_s, sa_s, sb_s, *, seq_len):
    T = Q_BLOCK
    P = NSA_HPG
    R = P * T
    n_cmp_pad = seq_len // CMP_STRIDE
    n_cmp = n_cmp_pad - 1
    nb = seq_len // SLC_LEN
    qi = pl.program_id(2)
    qs = qi * T
    d0 = pl.multiple_of(qs, T)
    wlen = WINDOW + T
    w0 = pl.multiple_of(jnp.maximum(qs - WINDOW, 0), T)

    qt = jnp.concatenate([qt_ref[p * HEAD_DIM:(p + 1) * HEAD_DIM, :] for p in range(P)], axis=1)

    s_c = _dot(kc_ref[0, 0], qt)
    s_w = _dot(kw_ref[pl.ds(w0, wlen), :], qt)
    s_d = _dot(ks_ref[pl.ds(d0, T), 0:HEAD_DIM], qt)

    tok = lambda n: qs + lax.broadcasted_iota(jnp.int32, (n, T), 1)
    row = lambda n: lax.broadcasted_iota(jnp.int32, (n, T), 0)
    ok_c = (row(n_cmp_pad) * CMP_STRIDE + (CMP_LEN - 1) <= tok(n_cmp_pad)) & (row(n_cmp_pad) < n_cmp)
    key_w = w0 + row(wlen)
    ok_w = (key_w <= tok(wlen)) & (tok(wlen) - key_w < WINDOW)
    ok_d = row(T) <= lax.broadcasted_iota(jnp.int32, (T, T), 1)
    addmask = lambda ok: _tile_lanes(jnp.where(ok, 0.0, NEG).astype(F32), P)
    s_c = s_c + addmask(ok_c)
    s_w = s_w + addmask(ok_w)
    s_d = s_d + addmask(ok_d)

    m_c = jnp.max(s_c, axis=0, keepdims=True)
    m_w = jnp.max(s_w, axis=0, keepdims=True)
    m_d = jnp.max(s_d, axis=0, keepdims=True)
    m_c = jnp.where(m_c > 0.5 * NEG, m_c, 0.0)
    p_c = jnp.exp2(s_c - m_c)
    p_w = jnp.exp2(s_w - m_w)
    p_d = jnp.exp2(s_d - m_d)
    o_c = _dot(vct_ref[0, 0], p_c.astype(BF16))
    o_w = _dot(vwt_ref[:, pl.ds(w0, wlen)], p_w.astype(BF16))
    inv_c = 1.0 / jnp.maximum(o_c[HEAD_DIM:HEAD_DIM + 1, :], 1e-30)
    inv_w = 1.0 / o_w[HEAD_DIM:HEAD_DIM + 1, :]
    m_s[...] = m_d
    acc_s[...] = _dot(vst_ref[:, pl.ds(d0, T)], p_d.astype(BF16))

    pn = p_c * inv_c
    p_sum = pn[:, 0:T]
    for p in range(1, P):
        p_sum = p_sum + pn[:, p * T:(p + 1) * T]
    imp = _dot_f32(ovt_ref[...], p_sum)[0:nb]
    jrow = row(nb)
    t_lane = tok(nb)
    tb = t_lane >> int(math.log2(SLC_LEN))
    visible = jrow * SLC_LEN <= t_lane
    forced = (jrow == 0) | (jrow == tb) | (jrow == tb - 1)
    score = jnp.where(forced, 1e9, jnp.where(visible, imp, -jnp.inf))
    cnt = jnp.zeros((nb, T), F32)
    for k in range(nb):
        rk = score[k:k + 1, :]
        cnt = cnt + jnp.where(jrow > k, jnp.where(rk >= score, 1.0, 0.0), jnp.where(rk > score, 1.0, 0.0))
    keep = (cnt < float(min(SLC_TOPK, nb))) & visible & (jrow < 2 * qi)
    bias = jnp.where(keep, 0.0, NEG).astype(F32)
    bias = jnp.concatenate([bias, jnp.full((LANES - nb, T), NEG, F32)], axis=0) if nb < LANES else bias
    qa_s[0:HEAD_DIM, :] = qt
    qa_s[HEAD_DIM:2 * HEAD_DIM, :] = _tile_lanes(bias.astype(BF16), P)

    n_t = (qs + SEL_TILE - 1) // SEL_TILE

    def scores_into(kt, dst):
        k0 = pl.multiple_of(kt * SEL_TILE, SEL_TILE)
        dst[...] = _dot(ks_ref[pl.ds(k0, SEL_TILE), :], qa_s[...])

    def consume(kt, src):
        k0 = pl.multiple_of(kt * SEL_TILE, SEL_TILE)
        s = src[...]
        m_prev = m_s[...]
        m_new = jnp.maximum(m_prev, jnp.max(s, axis=0, keepdims=True))
        alpha = jnp.exp2(m_prev - m_new)
        p = jnp.exp2(s - m_new)
        acc_s[...] = alpha * acc_s[...] + _dot(vst_ref[:, pl.ds(k0, SEL_TILE)], p.astype(BF16))
        m_s[...] = m_new

    @pl.when(n_t > 0)
    def _():
        scores_into(0, sa_s)

    def pair_body(i, carry):
        kt = 2 * i
        scores_into(kt + 1, sb_s)
        consume(kt, sa_s)
        scores_into(jnp.minimum(kt + 2, n_t - 1), sa_s)
        consume(kt + 1, sb_s)
        return carry

    lax.fori_loop(0, n_t // 2, pair_body, 0)

    @pl.when(n_t % 2 == 1)
    def _():
        consume(n_t - 1, sa_s)

    gates_t = _sigmoid(g_ref[...]).T
    first_group = pl.program_id(1) == 0

    def grow(r):
        rows = []
        for p in range(P):
            h0 = r * NSA_HEADS + p
            rows.append(jnp.where(first_group, gates_t[h0:h0 + 1, :], gates_t[h0 + P:h0 + P + 1, :]))
        return jnp.concatenate(rows, axis=1)

    o_s = acc_s[...]
    hd = slice(0, HEAD_DIM)
    o_t = (o_c[hd] * (grow(0) * inv_c) + o_s[hd] * (grow(1) / o_s[HEAD_DIM:HEAD_DIM + 1, :])
           + o_w[hd] * (grow(2) * inv_w))
    for p in range(P):
        sl = slice(p * HEAD_DIM, (p + 1) * HEAD_DIM)
        o_p = o_t[:, p * T:(p + 1) * T].T
        o_ref[:, sl] = (o_p * _silu(z_ref[:, sl].astype(F32))).astype(o_ref.dtype)


def _nsa_consts(S):
    n_cmp_pad = S // CMP_STRIDE
    nb = S // SLC_LEN
    cmp_start = np.arange(n_cmp_pad) * CMP_STRIDE
    slc_start = np.arange(LANES) * SLC_LEN
    ov = ((cmp_start[:, None] <= slc_start[None, :] + SLC_LEN - 1)
          & (cmp_start[:, None] + CMP_LEN - 1 >= slc_start[None, :])
          & (np.arange(LANES)[None, :] < nb)
          & (np.arange(n_cmp_pad)[:, None] < n_cmp_pad - 1)).astype(np.float32)
    return jnp.asarray(ov.T)


def _nsa_attn(q_t, kcmp, vcmp_t, keys, v_t, ptail, pbig, z_col_block, B, S):
    M = B * S
    nq = S // Q_BLOCK
    gw = NSA_HPG * HEAD_DIM
    ovt = _nsa_consts(S)
    n_cmp_pad = S // CMP_STRIDE
    R = NSA_HPG * Q_BLOCK
    kern = functools.partial(_nsa_attn_kernel, seq_len=S)
    return pl.pallas_call(
        kern,
        grid=(B, NSA_GROUPS, nq),
        in_specs=[pl.BlockSpec((gw, Q_BLOCK), lambda b, g, i: (g, b * nq + i)),
                  pl.BlockSpec((1, 1, n_cmp_pad, HEAD_DIM), lambda b, g, i: (b, g, 0, 0)),
                  pl.BlockSpec((1, 1, VT_ROWS, n_cmp_pad), lambda b, g, i: (b, g, 0, 0)),
                  pl.BlockSpec((S, 2 * HEAD_DIM), lambda b, g, i: (b, g)),
                  pl.BlockSpec((VT_ROWS, S), lambda b, g, i: (g, b)),
                  pl.BlockSpec((S, HEAD_DIM), lambda b, g, i: (b, 4 + g)),
                  pl.BlockSpec((VT_ROWS, S), lambda b, g, i: (2 + g, b)),
                  pl.BlockSpec((Q_BLOCK, LANES), lambda b, g, i: (b * nq + i, TAIL_GATE_BLOCK)),
                  pl.BlockSpec((Q_BLOCK, gw), lambda b, g, i: (b * nq + i, z_col_block * NSA_GROUPS + g)),
                  pl.BlockSpec((LANES, n_cmp_pad), lambda b, g, i: (0, 0))],
        out_specs=pl.BlockSpec((Q_BLOCK, gw), lambda b, g, i: (b * nq + i, g)),
        out_shape=jax.ShapeDtypeStruct((M, NSA_WIDTH), BF16),
        scratch_shapes=[pltpu.VMEM((1, R), F32),
                        pltpu.VMEM((VT_ROWS, R), F32),
                        pltpu.VMEM((2 * HEAD_DIM, R), BF16),
                        pltpu.VMEM((SEL_TILE, R), F32),
                        pltpu.VMEM((SEL_TILE, R), F32)],
        compiler_params=_cparams(("arbitrary", "arbitrary", "arbitrary")),
        name="nsa_attn",
    )(q_t, kcmp, vcmp_t, keys, v_t, keys, v_t, ptail, pbig, ovt)


def _dn_prep_kernel(x_ref, w_ref, o_ref, *rest, mode, seq_len):
    x = x_ref[...].astype(F32)
    w = w_ref[...]
    row = lax.broadcasted_iota(jnp.int32, x.shape, 0)
    y = x * w[CONV_WIDTH - 1:CONV_WIDTH, :]
    for k in range(1, CONV_WIDTH):
        xs = jnp.where(row >= k, pltpu.roll(x, k, 0), 0.0)
        y = y + xs * w[CONV_WIDTH - 1 - k:CONV_WIDTH - k, :]
    y = _silu(y)
    if mode in ("q", "k"):
        outs = []
        for h in range(x.shape[1] // DN_DK):
            yh = y[:, h * DN_DK:(h + 1) * DN_DK]
            ss = jnp.sum(yh * yh, axis=-1, keepdims=True)
            yh = yh * lax.rsqrt(ss + EPS)
            if mode == "q":
                yh = yh * (DN_DK ** -0.5)
            outs.append(yh)
        y = jnp.concatenate(outs, axis=1)
    o_ref[...] = y.astype(o_ref.dtype)
    if mode == "k":
        kt_ref = rest[0]
        kt_ref[...] = y.T.astype(kt_ref.dtype)


DN_PREP_COLS = 128


def _dn_prep(pbig, conv_w, col0, mode, B, S):
    M = B * S
    tw = DN_PREP_COLS
    col0_blocks = col0 // tw
    nj = DN_WIDTH // tw
    wcol0 = {"q": 0, "k": nj, "v": 2 * nj}[mode]
    kern = functools.partial(_dn_prep_kernel, mode=mode, seq_len=S)
    out_specs = [pl.BlockSpec((S, tw), lambda b, j: (b, j))]
    out_shape = [jax.ShapeDtypeStruct((M, DN_WIDTH), BF16)]
    if mode == "k":
        out_specs.append(pl.BlockSpec((tw, S), lambda b, j: (j, b)))
        out_shape.append(jax.ShapeDtypeStruct((DN_WIDTH, M), BF16))
    return pl.pallas_call(
        kern,
        grid=(B, nj),
        in_specs=[pl.BlockSpec((S, tw), lambda b, j: (b, col0_blocks + j)),
                  pl.BlockSpec((CONV_WIDTH, tw), lambda b, j: (0, wcol0 + j))],
        out_specs=out_specs,
        out_shape=out_shape,
        compiler_params=_cparams(("arbitrary", "arbitrary")),
        name="dn_prep_" + mode,
    )(pbig, conv_w)


def _dn_gate_kernel(ab_ref, alog_ref, dtb_ref, o_ref, ot_ref, *, tm):
    ab = ab_ref[...]
    x = ab + dtb_ref[...]
    softplus = jnp.maximum(x, 0.0) + jnp.log(1.0 + jnp.exp(-jnp.abs(x)))
    g = -jnp.exp(alog_ref[...]) * softplus
    beta = _sigmoid(ab)
    lane = lax.broadcasted_iota(jnp.int32, (DN_CHUNK, LANES), 1)
    r = lax.broadcasted_iota(jnp.int32, (DN_CHUNK, DN_CHUNK), 0)
    c = lax.broadcasted_iota(jnp.int32, (DN_CHUNK, DN_CHUNK), 1)
    tril = jnp.where(r >= c, 1.0, 0.0).astype(F32)
    for ci in range(tm // DN_CHUNK):
        rows = slice(ci * DN_CHUNK, (ci + 1) * DN_CHUNK)
        dec = _dot_f32(tril, g[rows])
        out = jnp.where(lane < DN_HEADS, dec, beta[rows])
        o_ref[rows, :] = out
        ot_ref[:, rows] = out.T


def _dn_gate(ptail, alog_row, dtb_row):
    M = ptail.shape[0]
    tm = 512
    kern = functools.partial(_dn_gate_kernel, tm=tm)
    return pl.pallas_call(
        kern,
        grid=(M // tm,),
        in_specs=[pl.BlockSpec((tm, LANES), lambda i: (i, TAIL_AB_BLOCK)),
                  pl.BlockSpec((1, LANES), lambda i: (0, 0)),
                  pl.BlockSpec((1, LANES), lambda i: (0, 0))],
        out_specs=[pl.BlockSpec((tm, LANES), lambda i: (i, 0)),
                   pl.BlockSpec((LANES, tm), lambda i: (0, i))],
        out_shape=[jax.ShapeDtypeStruct((M, LANES), F32),
                   jax.ShapeDtypeStruct((LANES, M), F32)],
        compiler_params=_cparams(("arbitrary",)),
        name="dn_gate",
    )(ptail, alog_row, dtb_row)


def _dn_scan_kernel(q_ref, k_ref, v_ref, kt_ref, dec_ref, dect_ref, z_ref, gain_ref, o_ref, st_ref):
    C = DN_CHUNK

    @pl.when(pl.program_id(1) == 0)
    def _():
        st_ref[...] = jnp.zeros(st_ref.shape, F32)

    dec = dec_ref[...]
    dect = dect_ref[...]
    r = lax.broadcasted_iota(jnp.int32, (C, C), 0)
    c = lax.broadcasted_iota(jnp.int32, (C, C), 1)
    tril = r >= c
    strict = r > c
    gain = gain_ref[...]
    eye = jnp.where(r == c, 1.0, 0.0).astype(F32)
    lvl_masks = []
    for lg in range(int(math.log2(C))):
        lvl_masks.append(((r >> (lg + 1)) == (c >> (lg + 1))) & ((r >> lg) != (c >> lg)))

    for h0 in range(0, DN_HEADS, DN_HEAD_GROUP):
        heads = range(h0, h0 + DN_HEAD_GROUP)
        cols = lambda ref: jnp.stack([ref[:, h * DN_DK:(h + 1) * DN_DK] for h in heads])
        qh = cols(q_ref).astype(F32)
        kh = cols(k_ref).astype(F32)
        vh = cols(v_ref).astype(F32)
        kth = jnp.stack([kt_ref[h * DN_DK:(h + 1) * DN_DK, :] for h in heads])
        dcol = jnp.stack([jnp.broadcast_to(dec[:, h:h + 1], (C, C)) for h in heads])
        bcol = jnp.stack([jnp.broadcast_to(dec[:, DN_HEADS + h:DN_HEADS + h + 1], (C, C)) for h in heads])
        drow = jnp.stack([jnp.broadcast_to(dect[h:h + 1, :], (C, C)) for h in heads])
        dlast = jnp.stack([jnp.broadcast_to(dect[h:h + 1, C - 1:C], (C, C)) for h in heads])
        lmat = jnp.exp(jnp.where(tril[None], dcol - drow, NEG))
        e_d = jnp.exp(dcol)
        kb = kh * bcol
        vb = vh * bcol
        a = jnp.where(strict[None], _bmm(kb.astype(BF16), kth) * lmat, 0.0)
        attn = jnp.where(tril[None], _bmm(qh.astype(BF16), kth) * lmat, 0.0)
        tinv = eye[None] - jnp.where(lvl_masks[0][None], a, 0.0)
        for lm in lvl_masks[1:]:
            t16 = tinv.astype(BF16)
            lo = jnp.where(lm[None], a, 0.0).astype(BF16)
            tinv = tinv - _bmm(t16, _bmm(lo, t16).astype(BF16))
        t16 = tinv.astype(BF16)
        u = _bmm(t16, vb.astype(BF16))
        w = _bmm(t16, (kb * e_d).astype(BF16))
        st = st_ref[h0:h0 + DN_HEAD_GROUP]
        st16 = st.astype(BF16)
        v_new = u - _bmm(w.astype(BF16), st16)
        v_new16 = v_new.astype(BF16)
        o = _bmm((qh * e_d).astype(BF16), st16) + _bmm(attn.astype(BF16), v_new16)
        kdt = (kth.astype(F32) * jnp.exp(dlast - drow)).astype(BF16)
        st_ref[h0:h0 + DN_HEAD_GROUP] = st * jnp.exp(dlast) + _bmm(kdt, v_new16)
        ms = jnp.mean(o * o, axis=-1, keepdims=True)
        y = o * lax.rsqrt(ms + EPS) * gain[None]
        for i, h in enumerate(heads):
            sl = slice(h * DN_DV, (h + 1) * DN_DV)
            o_ref[:, sl] = (y[i] * _silu(z_ref[:, sl].astype(F32))).astype(o_ref.dtype)


def _dn_scan(dq, dk, dv, dkt, dec, dect, pbig, z_col_block, gain_row, B, S):
    M = B * S
    C = DN_CHUNK
    nc = S // C
    tok = lambda b, n: (b * nc + n, 0)
    return pl.pallas_call(
        _dn_scan_kernel,
        grid=(B, nc),
        in_specs=[pl.BlockSpec((C, DN_WIDTH), tok),
                  pl.BlockSpec((C, DN_WIDTH), tok),
                  pl.BlockSpec((C, DN_WIDTH), tok),
                  pl.BlockSpec((DN_WIDTH, C), lambda b, n: (0, b * nc + n)),
                  pl.BlockSpec((C, LANES), tok),
                  pl.BlockSpec((LANES, C), lambda b, n: (0, b * nc + n)),
                  pl.BlockSpec((C, DN_WIDTH), lambda b, n: (b * nc + n, z_col_block)),
                  pl.BlockSpec((1, DN_DV), lambda b, n: (0, 0))],
        out_specs=pl.BlockSpec((C, DN_WIDTH), tok),
        out_shape=jax.ShapeDtypeStruct((M, DN_WIDTH), BF16),
        scratch_shapes=[pltpu.VMEM((DN_HEADS, DN_DK, DN_DV), F32)],
        compiler_params=_cparams(("arbitrary", "arbitrary")),
        name="dn_scan",
    )(dq, dk, dv, dkt, dec, dect, pbig, gain_row)


def _out1_kernel(oa_ref, ob_ref, wa_ref, wb_ref, ga_ref, gb_ref, o_ref, wa16_ref, wb16_ref):
    @pl.when(pl.program_id(1) == 0)
    def _():
        wa16_ref[...] = wa_ref[...].astype(BF16)
        wb16_ref[...] = wb_ref[...].astype(BF16)

    ya = _dot(oa_ref[...], wa16_ref[...])
    yb = _dot(ob_ref[...], wb16_ref[...])
    mix = _sigmoid(ga_ref[...].astype(F32)) * ya + _sigmoid(gb_ref[...].astype(F32)) * yb
    o_ref[...] = mix.astype(o_ref.dtype)


def _out1(o_a, o_b, wa, wb, pbig, D):
    M, K = o_a.shape
    tm = min(MM_TM, M)
    tn = min(MM_TN, D)
    nbd = D // tn
    return pl.pallas_call(
        _out1_kernel,
        grid=(D // tn, M // tm),
        in_specs=[pl.BlockSpec((tm, K), lambda j, i: (i, 0)),
                  pl.BlockSpec((tm, K), lambda j, i: (i, 0)),
                  pl.BlockSpec((K, tn), lambda j, i: (0, j), pipeline_mode=pl.Buffered(1)),
                  pl.BlockSpec((K, tn), lambda j, i: (0, j), pipeline_mode=pl.Buffered(1)),
                  pl.BlockSpec((tm, tn), lambda j, i: (i, j)),
                  pl.BlockSpec((tm, tn), lambda j, i: (i, nbd + j))],
        out_specs=pl.BlockSpec((tm, tn), lambda j, i: (i, j)),
        out_shape=jax.ShapeDtypeStruct((M, D), BF16),
        scratch_shapes=[pltpu.VMEM((K, tn), BF16), pltpu.VMEM((K, tn), BF16)],
        compiler_params=_cparams(("arbitrary", "arbitrary")),
        name="out1",
    )(o_a, o_b, wa, wb, pbig, pbig)


def _final_kernel(mix_ref, x_ref, gate_ref, fg_ref, o_ref):
    xn = x_ref[...] + gate_ref[0] * mix_ref[...].astype(F32)
    ms = jnp.mean(xn * xn, axis=-1, keepdims=True)
    o_ref[...] = xn * lax.rsqrt(ms + EPS) * fg_ref[...]


def _final(mixed, x2, mod3, final_gain, S):
    M, D = x2.shape
    tm = 256
    nb = S // tm
    return pl.pallas_call(
        _final_kernel,
        grid=(M // tm,),
        in_specs=[pl.BlockSpec((tm, D), lambda i: (i, 0)),
                  pl.BlockSpec((tm, D), lambda i: (i, 0)),
                  pl.BlockSpec((1, 1, D), lambda i: (i // nb, 0, 2)),
                  pl.BlockSpec((1, D), lambda i: (0, 0))],
        out_specs=pl.BlockSpec((tm, D), lambda i: (i, 0)),
        out_shape=jax.ShapeDtypeStruct((M, D), F32),
        compiler_params=_cparams(("arbitrary",)),
        name="final",
    )(mixed, x2, mod3, final_gain)


def _pad_cols(w, width):
    return jnp.pad(w, ((0, 0), (0, width - w.shape[1])))


def _proj_row_offsets(D):
    widths = (NSA_WIDTH, 6 * KV_WIDTH, 3 * NSA_HEADS, NSA_WIDTH, 3 * DN_WIDTH, DN_HEADS, DN_HEADS, DN_WIDTH, 2 * D)
    offs = [int(o) for o in np.concatenate([[0], np.cumsum(widths)])]
    tiles = lambda seg: [offs[seg] + t * MM_TN for t in range(widths[seg] // MM_TN)]
    big = tiles(8) + tiles(4) + tiles(0) + tiles(3) + tiles(7)
    kv0 = offs[1]
    tail = [kv0 + 2 * KV_WIDTH, kv0 + 4 * KV_WIDTH, kv0, offs[2], offs[5]]
    return big, tail


TAIL_TN = 512
TAIL_TM = 1024
TAIL_GATE_BLOCK = 3 * TAIL_TN // LANES
TAIL_AB_BLOCK = 4 * TAIL_TN // LANES
TAIL_CMP_BLOCK = 2 * TAIL_TN // LANES


def kernel(x, c, positions, w_ada, b_ada, norm_gain, w_in, cmp_pos_k, cmp_pos_v, w_cmp_k1, w_cmp_k2,
           w_cmp_v1, w_cmp_v2, conv_w, dt_bias, a_log, dn_norm_gain, w_proj_a, w_proj_b, w_out, final_gain):
    B, S, D = x.shape
    M = B * S
    depth = w_in.shape[0]
    assert S % SEL_TILE == 0 and S % DN_CHUNK == 0 and S >= WINDOW + Q_BLOCK and D % 512 == 0 and B <= 8
    assert S // SLC_LEN <= LANES and D % 1024 == 0 and DN_DK == DN_CHUNK
    assert depth == 1, "the final RMSNorm is fused into the last layer's output kernel"

    off_dq = 2 * D
    off_q = off_dq + 3 * DN_WIDTH
    off_z = off_q + NSA_WIDTH
    off_dz = off_z + NSA_WIDTH

    x2 = x.reshape(M, D)
    pos_col = positions.reshape(M, 1)
    cmp_end = np.arange(S // CMP_STRIDE - 1) * CMP_STRIDE + CMP_LEN - 1
    pos_cmp = jnp.pad(positions[:, cmp_end], ((0, 0), (0, 1)))[:, :, None]
    c8 = jnp.pad(c, ((0, 8 - B), (0, 0)))

    for l in range(depth):
        mod = _ada(c8, w_ada[l], b_ada[l][None, :])
        mod3 = mod[:B].reshape(B, 1, 3 * D)
        h = _norm(x2, norm_gain[l][None, :], mod3, S)

        w_nk = jnp.transpose(w_in[l])
        big_rows, tail_rows = _proj_row_offsets(D)
        pbig = _proj(h, w_nk, big_rows, MM_TM, MM_TN, BF16, "proj_big")
        ptail = _proj(h, w_nk, tail_rows, TAIL_TM, TAIL_TN, F32, "proj_tail")

        q_t, keys, v_t = _nsa_prep(pbig, ptail, pos_col, off_q // NSA_WIDTH, S)
        kcmp = _compress(ptail, TAIL_CMP_BLOCK, cmp_pos_k[l], w_cmp_k1[l], w_cmp_k2[l], pos_cmp, B, S, True)
        vcmp = _compress(ptail, TAIL_CMP_BLOCK + NSA_GROUPS, cmp_pos_v[l], w_cmp_v1[l], w_cmp_v2[l].T,
                         pos_cmp, B, S, False)
        o_a = _nsa_attn(q_t, kcmp, vcmp, keys, v_t, ptail, pbig, off_z // NSA_WIDTH, B, S)

        cw = conv_w[l]
        dq = _dn_prep(pbig, cw, off_dq, "q", B, S)[0]
        dk, dkt = _dn_prep(pbig, cw, off_dq + DN_WIDTH, "k", B, S)
        dv = _dn_prep(pbig, cw, off_dq + 2 * DN_WIDTH, "v", B, S)[0]
        alog_row = _pad_cols(a_log[l][None, :].astype(F32), LANES)
        dtb_row = _pad_cols(dt_bias[l][None, :].astype(F32), LANES)
        dec, dect = _dn_gate(ptail, alog_row, dtb_row)
        o_b = _dn_scan(dq, dk, dv, dkt, dec, dect, pbig, off_dz // DN_WIDTH,
                       dn_norm_gain[l][None, :], B, S)

        mixed_in = _out1(o_a, o_b, w_proj_a[l], w_proj_b[l], pbig, D)
        mixed = _matmul(mixed_in, w_out[l], BF16, "out2")
        x2 = _final(mixed, x2, mod3, final_gain[None, :], S)
    return x2.reshape(B, S, D)
```

```python
import functools
import math

import numpy as np
import jax
import jax.numpy as jnp
from jax import lax
from jax.experimental import pallas as pl
from jax.experimental.pallas import tpu as pltpu

F32 = jnp.float32
BF16 = jnp.bfloat16

NSA_HEADS = 16
NSA_GROUPS = 2
NSA_HPG = NSA_HEADS // NSA_GROUPS
HEAD_DIM = 128
ROT_DIM = HEAD_DIM // 4
ROT_HALF = ROT_DIM // 2
ROPE_THETA = 500000.0
CMP_LEN = 32
CMP_STRIDE = 16
CMP_HIDDEN = 256
SLC_LEN = 64
SLC_TOPK = 16
WINDOW = 512
Q_BLOCK = 128
NSA_WIDTH = NSA_HEADS * HEAD_DIM
KV_WIDTH = NSA_GROUPS * HEAD_DIM
DN_HEADS = 16
DN_DK = 128
DN_DV = 128
DN_WIDTH = DN_HEADS * DN_DV
CONV_WIDTH = 4
EPS = 1e-6

LANES = 128
VMEM_LIMIT_BYTES = 56 * 1024 * 1024

DN_CHUNK = 128
DN_HEAD_GROUP = 16
SEL_TILE = 512
VT_ROWS = HEAD_DIM + 16
NEG = -1e30


def _cparams(sem):
    return pltpu.CompilerParams(dimension_semantics=sem, vmem_limit_bytes=VMEM_LIMIT_BYTES)


def _sigmoid(x):
    return 1.0 / (1.0 + jnp.exp(-x))


def _silu(x):
    return x * _sigmoid(x)


def _dot(a, b):
    return jnp.dot(a, b, preferred_element_type=F32)


def _dot_nt(a, b):
    return lax.dot_general(a, b, (((1,), (1,)), ((), ())), preferred_element_type=F32)


def _bmm(a, b):
    return lax.dot_general(a, b, (((2,), (1,)), ((0,), (0,))), preferred_element_type=F32)


def _dot_f32(a, b):
    return jnp.dot(a, b, preferred_element_type=F32, precision=lax.Precision.HIGHEST)


def _ada_kernel(c_ref, w_ref, b_ref, o_ref):
    c = c_ref[...]
    c_hi = c.astype(BF16)
    c_lo = (c - c_hi.astype(F32)).astype(BF16)
    w = w_ref[...]
    w_hi = w.astype(BF16)
    w_lo = (w - w_hi.astype(F32)).astype(BF16)
    acc = _dot(c_hi, w_hi) + _dot(c_lo, w_hi) + _dot(c_hi, w_lo)
    o_ref[...] = acc + b_ref[...]


def _ada(c8, w_ada, b_ada):
    D, N = w_ada.shape
    tn = min(512, N)
    return pl.pallas_call(
        _ada_kernel,
        grid=(N // tn,),
        in_specs=[pl.BlockSpec((8, D), lambda j: (0, 0)),
                  pl.BlockSpec((D, tn), lambda j: (0, j)),
                  pl.BlockSpec((1, tn), lambda j: (0, j))],
        out_specs=pl.BlockSpec((8, tn), lambda j: (0, j)),
        out_shape=jax.ShapeDtypeStruct((8, N), F32),
        compiler_params=_cparams(("arbitrary",)),
        name="ada",
    )(c8, w_ada, b_ada)


def _norm_kernel(x_ref, gain_ref, shift_ref, scale_ref, o_ref):
    x = x_ref[...]
    ms = jnp.mean(x * x, axis=-1, keepdims=True)
    y = x * lax.rsqrt(ms + EPS) * gain_ref[...]
    o_ref[...] = (y * (1.0 + scale_ref[0]) + shift_ref[0]).astype(o_ref.dtype)


def _norm(x2, gain, mod3, S):
    M, D = x2.shape
    tm = 256
    nb = S // tm
    return pl.pallas_call(
        _norm_kernel,
        grid=(M // tm,),
        in_specs=[pl.BlockSpec((tm, D), lambda i: (i, 0)),
                  pl.BlockSpec((1, D), lambda i: (0, 0)),
                  pl.BlockSpec((1, 1, D), lambda i: (i // nb, 0, 0)),
                  pl.BlockSpec((1, 1, D), lambda i: (i // nb, 0, 1))],
        out_specs=pl.BlockSpec((tm, D), lambda i: (i, 0)),
        out_shape=jax.ShapeDtypeStruct((M, D), BF16),
        compiler_params=_cparams(("arbitrary",)),
        name="norm",
    )(x2, gain, mod3, mod3)


MM_TM = 512
MM_TN = 1024


def _mm_kernel(a_ref, w_ref, o_ref, w16_ref, *, w_is_nk):
    @pl.when(pl.program_id(1) == 0)
    def _():
        if w_is_nk:
            for c in range(0, w_ref.shape[0], WT_CHUNK):
                w16_ref[:, c:c + WT_CHUNK] = w_ref[c:c + WT_CHUNK, :].T.astype(BF16)
        else:
            w16_ref[...] = w_ref[...].astype(BF16)

    o_ref[...] = _dot(a_ref[...], w16_ref[...]).astype(o_ref.dtype)


def _matmul(a, w, out_dtype, name):
    M, K = a.shape
    N = w.shape[1]
    tm, tn = min(MM_TM, M), min(MM_TN, N)
    return pl.pallas_call(
        functools.partial(_mm_kernel, w_is_nk=False),
        grid=(N // tn, M // tm),
        in_specs=[pl.BlockSpec((tm, K), lambda j, i: (i, 0)),
                  pl.BlockSpec((K, tn), lambda j, i: (0, j))],
        out_specs=pl.BlockSpec((tm, tn), lambda j, i: (i, j)),
        out_shape=jax.ShapeDtypeStruct((M, N), out_dtype),
        scratch_shapes=[pltpu.VMEM((K, tn), BF16)],
        compiler_params=_cparams(("arbitrary", "arbitrary")),
        name=name,
    )(a, w)


W_ROW_ALIGN = 16
WT_CHUNK = 128


def _proj(h, w_nk, row_offsets, tm, tn, out_dtype, name):
    M, K = h.shape
    tm = min(tm, M)
    assert all(int(o) % W_ROW_ALIGN == 0 and int(o) + tn <= w_nk.shape[0] for o in row_offsets)
    offs = jnp.asarray(np.asarray(row_offsets, np.int32) // W_ROW_ALIGN)
    grid_spec = pltpu.PrefetchScalarGridSpec(
        num_scalar_prefetch=1,
        grid=(len(row_offsets), M // tm),
        in_specs=[pl.BlockSpec((tm, K), lambda j, i, o: (i, 0)),
                  pl.BlockSpec((pl.Element(tn), pl.Element(K)), lambda j, i, o: (o[j] * W_ROW_ALIGN, 0))],
        out_specs=pl.BlockSpec((tm, tn), lambda j, i, o: (i, j)),
        scratch_shapes=[pltpu.VMEM((K, tn), BF16)],
    )

    def kern(o_smem, a_ref, w_ref, o_ref, w16_ref):
        _mm_kernel(a_ref, w_ref, o_ref, w16_ref, w_is_nk=True)

    return pl.pallas_call(
        kern,
        grid_spec=grid_spec,
        out_shape=jax.ShapeDtypeStruct((M, len(row_offsets) * tn), out_dtype),
        compiler_params=_cparams(("arbitrary", "arbitrary")),
        name=name,
    )(offs, h, w_nk)


def _rope_consts():
    inv = ROPE_THETA ** (-np.arange(ROT_HALF, dtype=np.float64) / ROT_HALF)
    invf = np.zeros((1, LANES), np.float32)
    invf[0, :ROT_HALF] = inv
    invf[0, ROT_HALF:ROT_DIM] = inv
    sgn = np.zeros((1, LANES), np.float32)
    sgn[0, :ROT_HALF] = -1.0
    sgn[0, ROT_HALF:ROT_DIM] = 1.0
    return jnp.asarray(invf), jnp.asarray(sgn)


def _rope_tables(pos_col, invf, sgn):
    ang = pos_col.astype(F32) * invf
    return jnp.cos(ang), jnp.sin(ang) * sgn


def _rope_apply(x, cos_t, sin_t):
    lane = lax.broadcasted_iota(jnp.int32, x.shape, 1)
    partner = jnp.where(lane < ROT_HALF,
                        pltpu.roll(x, LANES - ROT_HALF, 1),
                        pltpu.roll(x, ROT_HALF, 1))
    return x * cos_t + partner * sin_t


KEYS_WIDTH = 6 * HEAD_DIM


def _nsa_prep_kernel(q_ref, kv_ref, pos_ref, invf_ref, sgn_ref, qt_ref, keys_ref, vt_ref, *, seq_len):
    tm = q_ref.shape[0]
    cos_t, sin_t = _rope_tables(pos_ref[...], invf_ref[...], sgn_ref[...])
    qscale = (HEAD_DIM ** -0.5) * math.log2(math.e)
    for h in range(NSA_HEADS):
        sl = slice(h * HEAD_DIM, (h + 1) * HEAD_DIM)
        xq = _rope_apply(q_ref[:, sl].astype(F32), cos_t, sin_t) * qscale
        qt_ref[sl, :] = xq.T.astype(qt_ref.dtype)
    tok = (pl.program_id(0) * tm) % seq_len + lax.broadcasted_iota(jnp.int32, (tm, LANES), 0)
    lane = lax.broadcasted_iota(jnp.int32, (tm, LANES), 1)
    onehot = jnp.where(lane == (tok >> int(math.log2(SLC_LEN))), 1.0, 0.0).astype(keys_ref.dtype)
    blk = lambda i: kv_ref[:, i * HEAD_DIM:(i + 1) * HEAD_DIM]
    for g in range(NSA_GROUPS):
        ks = _rope_apply(blk(g), cos_t, sin_t)
        keys_ref[:, (2 * g) * HEAD_DIM:(2 * g + 1) * HEAD_DIM] = ks.astype(keys_ref.dtype)
        keys_ref[:, (2 * g + 1) * HEAD_DIM:(2 * g + 2) * HEAD_DIM] = onehot
        kw = _rope_apply(blk(4 + g), cos_t, sin_t)
        keys_ref[:, (4 + g) * HEAD_DIM:(5 + g) * HEAD_DIM] = kw.astype(keys_ref.dtype)
        ones = jnp.ones((VT_ROWS - HEAD_DIM, tm), vt_ref.dtype)
        for c, src in ((g, 2 + g), (2 + g, 6 + g)):
            vt_ref[c * VT_ROWS:c * VT_ROWS + HEAD_DIM, :] = blk(src).T.astype(vt_ref.dtype)
            vt_ref[c * VT_ROWS + HEAD_DIM:(c + 1) * VT_ROWS, :] = ones


def _nsa_prep(pbig, ptail, pos_col, q_col_block, S):
    M = pbig.shape[0]
    tm = 256
    invf, sgn = _rope_consts()
    return pl.pallas_call(
        functools.partial(_nsa_prep_kernel, seq_len=S),
        grid=(M // tm,),
        in_specs=[pl.BlockSpec((tm, NSA_WIDTH), lambda i: (i, q_col_block)),
                  pl.BlockSpec((tm, 1024), lambda i: (i, 0)),
                  pl.BlockSpec((tm, 1), lambda i: (i, 0)),
                  pl.BlockSpec((1, LANES), lambda i: (0, 0)),
                  pl.BlockSpec((1, LANES), lambda i: (0, 0))],
        out_specs=[pl.BlockSpec((NSA_WIDTH, tm), lambda i: (0, i)),
                   pl.BlockSpec((tm, KEYS_WIDTH), lambda i: (i, 0)),
                   pl.BlockSpec((4 * VT_ROWS, tm), lambda i: (0, i))],
        out_shape=[jax.ShapeDtypeStruct((NSA_WIDTH, M), BF16),
                   jax.ShapeDtypeStruct((M, KEYS_WIDTH), BF16),
                   jax.ShapeDtypeStruct((4 * VT_ROWS, M), BF16)],
        compiler_params=_cparams(("arbitrary",)),
        name="nsa_prep",
    )(pbig, ptail, pos_col, invf, sgn)


def _compress_kernel(x_ref, pe_ref, w1_ref, w2_ref, pos_ref, invf_ref, sgn_ref, o_ref, *, rope, n_rows):
    half = CMP_LEN // 2
    top = jnp.zeros((n_rows, CMP_HIDDEN), F32)
    bot = jnp.zeros((n_rows, CMP_HIDDEN), F32)
    for l in range(half):
        xl = x_ref[pl.ds(l, n_rows, stride=CMP_STRIDE), :]
        w_top = w1_ref[l * HEAD_DIM:(l + 1) * HEAD_DIM, :].astype(BF16)
        w_bot = w1_ref[(half + l) * HEAD_DIM:(half + l + 1) * HEAD_DIM, :].astype(BF16)
        top = top + _dot((xl + pe_ref[l:l + 1, :]).astype(BF16), w_top)
        bot = bot + _dot((xl + pe_ref[half + l:half + l + 1, :]).astype(BF16), w_bot)
    hid = top + pltpu.roll(bot, n_rows - 1, 0)
    act = _silu(hid).astype(BF16)
    if rope:
        out = _dot(act, w2_ref[...].astype(BF16))
        cos_t, sin_t = _rope_tables(pos_ref[0], invf_ref[...], sgn_ref[...])
        o_ref[0, 0] = _rope_apply(out, cos_t, sin_t).astype(o_ref.dtype)
    else:
        o_ref[0, 0, 0:HEAD_DIM, :] = _dot_nt(w2_ref[...].astype(BF16), act).astype(o_ref.dtype)
        o_ref[0, 0, HEAD_DIM:VT_ROWS, :] = jnp.ones((VT_ROWS - HEAD_DIM, n_rows), o_ref.dtype)


def _compress(ptail, col_block0, pe, w1, w2, pos_cmp, B, S, rope):
    n_rows = S // CMP_STRIDE
    invf, sgn = _rope_consts()
    kern = functools.partial(_compress_kernel, rope=rope, n_rows=n_rows)
    out_dims = (n_rows, HEAD_DIM) if rope else (VT_ROWS, n_rows)
    return pl.pallas_call(
        kern,
        grid=(B, NSA_GROUPS),
        in_specs=[pl.BlockSpec((S, HEAD_DIM), lambda b, g: (b, col_block0 + g)),
                  pl.BlockSpec((CMP_LEN, HEAD_DIM), lambda b, g: (0, 0)),
                  pl.BlockSpec((CMP_LEN * HEAD_DIM, CMP_HIDDEN), lambda b, g: (0, 0)),
                  pl.BlockSpec(w2.shape, lambda b, g: (0, 0)),
                  pl.BlockSpec((1, n_rows, 1), lambda b, g: (b, 0, 0)),
                  pl.BlockSpec((1, LANES), lambda b, g: (0, 0)),
                  pl.BlockSpec((1, LANES), lambda b, g: (0, 0))],
        out_specs=pl.BlockSpec((1, 1) + out_dims, lambda b, g: (b, g, 0, 0)),
        out_shape=jax.ShapeDtypeStruct((B, NSA_GROUPS) + out_dims, BF16),
        compiler_params=_cparams(("arbitrary", "arbitrary")),
        name="compress_k" if rope else "compress_v",
    )(ptail, pe, w1, w2, pos_cmp, invf, sgn)


def _tile_lanes(x, n):
    return jnp.concatenate([x] * n, axis=1)


def _nsa_attn_kernel(qt_ref, kc_ref, vct_ref, ks_ref, vst_ref, kw_ref, vwt_ref, g_ref, z_ref,
                     ovt_ref, o_ref, m_s, acc_s, qa_s, sa_s, sb_s, *, seq_len):
    T = Q_BLOCK
    P = NSA_HPG
    R = P * T
    n_cmp_pad = seq_len // CMP_STRIDE
    n_cmp = n_cmp_pad - 1
    nb = seq_len // SLC_LEN
    qi = pl.program_id(2)
    qs = qi * T
    d0 = pl.multiple_of(qs, T)
    wlen = WINDOW + T
    w0 = pl.multiple_of(jnp.maximum(qs - WINDOW, 0), T)

    qt = jnp.concatenate([qt_ref[p * HEAD_DIM:(p + 1) * HEAD_DIM, :] for p in range(P)], axis=1)

    s_c = _dot(kc_ref[0, 0], qt)
    s_w = _dot(kw_ref[pl.ds(w0, wlen), :], qt)
    s_d = _dot(ks_ref[pl.ds(d0, T), 0:HEAD_DIM], qt)

    tok = lambda n: qs + lax.broadcasted_iota(jnp.int32, (n, T), 1)
    row = lambda n: lax.broadcasted_iota(jnp.int32, (n, T), 0)
    ok_c = (row(n_cmp_pad) * CMP_STRIDE + (CMP_LEN - 1) <= tok(n_cmp_pad)) & (row(n_cmp_pad) < n_cmp)
    key_w = w0 + row(wlen)
    ok_w = (key_w <= tok(wlen)) & (tok(wlen) - key_w < WINDOW)
    ok_d = row(T) <= lax.broadcasted_iota(jnp.int32, (T, T), 1)
    addmask = lambda ok: _tile_lanes(jnp.where(ok, 0.0, NEG).astype(F32), P)
    s_c = s_c + addmask(ok_c)
    s_w = s_w + addmask(ok_w)
    s_d = s_d + addmask(ok_d)

    m_c = jnp.max(s_c, axis=0, keepdims=True)
    m_w = jnp.max(s_w, axis=0, keepdims=True)
    m_d = jnp.max(s_d, axis=0, keepdims=True)
    m_c = jnp.where(m_c > 0.5 * NEG, m_c, 0.0)
    p_c = jnp.exp2(s_c - m_c)
    p_w = jnp.exp2(s_w - m_w)
    p_d = jnp.exp2(s_d - m_d)
    o_c = _dot(vct_ref[0, 0], p_c.astype(BF16))
    o_w = _dot(vwt_ref[:, pl.ds(w0, wlen)], p_w.astype(BF16))
    inv_c = 1.0 / jnp.maximum(o_c[HEAD_DIM:HEAD_DIM + 1, :], 1e-30)
    inv_w = 1.0 / o_w[HEAD_DIM:HEAD_DIM + 1, :]
    m_s[...] = m_d
    acc_s[...] = _dot(vst_ref[:, pl.ds(d0, T)], p_d.astype(BF16))

    pn = p_c * inv_c
    p_sum = pn[:, 0:T]
    for p in range(1, P):
        p_sum = p_sum + pn[:, p * T:(p + 1) * T]
    imp = _dot_f32(ovt_ref[...], p_sum)[0:nb]
    jrow = row(nb)
    t_lane = tok(nb)
    tb = t_lane >> int(math.log2(SLC_LEN))
    visible = jrow * SLC_LEN <= t_lane
    forced = (jrow == 0) | (jrow == tb) | (jrow == tb - 1)
    score = jnp.where(forced, 1e9, jnp.where(visible, imp, -jnp.inf))
    cnt = jnp.zeros((nb, T), F32)
    for k in range(nb):
        rk = score[k:k + 1, :]
        cnt = cnt + jnp.where(jrow > k, jnp.where(rk >= score, 1.0, 0.0), jnp.where(rk > score, 1.0, 0.0))
    keep = (cnt < float(min(SLC_TOPK, nb))) & visible & (jrow < 2 * qi)
    bias = jnp.where(keep, 0.0, NEG).astype(F32)
    bias = jnp.concatenate([bias, jnp.full((LANES - nb, T), NEG, F32)], axis=0) if nb < LANES else bias
    qa_s[0:HEAD_DIM, :] = qt
    qa_s[HEAD_DIM:2 * HEAD_DIM, :] = _tile_lanes(bias.astype(BF16), P)

    n_t = (qs + SEL_TILE - 1) // SEL_TILE

    def scores_into(kt, dst):
        k0 = pl.multiple_of(kt * SEL_TILE, SEL_TILE)
        dst[...] = _dot(ks_ref[pl.ds(k0, SEL_TILE), :], qa_s[...])

    def consume(kt, src):
        k0 = pl.multiple_of(kt * SEL_TILE, SEL_TILE)
        s = src[...]
        m_prev = m_s[...]
        m_new = jnp.maximum(m_prev, jnp.max(s, axis=0, keepdims=True))
        alpha = jnp.exp2(m_prev - m_new)
        p = jnp.exp2(s - m_new)
        acc_s[...] = alpha * acc_s[...] + _dot(vst_ref[:, pl.ds(k0, SEL_TILE)], p.astype(BF16))
        m_s[...] = m_new

    @pl.when(n_t > 0)
    def _():
        scores_into(0, sa_s)

    def pair_body(i, carry):
        kt = 2 * i
        scores_into(kt + 1, sb_s)
        consume(kt, sa_s)
        scores_into(jnp.minimum(kt + 2, n_t - 1), sa_s)
        consume(kt + 1, sb_s)
        return carry

    lax.fori_loop(0, n_t // 2, pair_body, 0)

    @pl.when(n_t % 2 == 1)
    def _():
        consume(n_t - 1, sa_s)

    gates_t = _sigmoid(g_ref[...]).T
    first_group = pl.program_id(1) == 0

    def grow(r):
        rows = []
        for p in range(P):
            h0 = r * NSA_HEADS + p
            rows.append(jnp.where(first_group, gates_t[h0:h0 + 1, :], gates_t[h0 + P:h0 + P + 1, :]))
        return jnp.concatenate(rows, axis=1)

    o_s = acc_s[...]
    hd = slice(0, HEAD_DIM)
    o_t = (o_c[hd] * (grow(0) * inv_c) + o_s[hd] * (grow(1) / o_s[HEAD_DIM:HEAD_DIM + 1, :])
           + o_w[hd] * (grow(2) * inv_w))
    for p in range(P):
        sl = slice(p * HEAD_DIM, (p + 1) * HEAD_DIM)
        o_p = o_t[:, p * T:(p + 1) * T].T
        o_ref[:, sl] = (o_p * _silu(z_ref[:, sl].astype(F32))).astype(o_ref.dtype)


def _nsa_consts(S):
    n_cmp_pad = S // CMP_STRIDE
    nb = S // SLC_LEN
    cmp_start = np.arange(n_cmp_pad) * CMP_STRIDE
    slc_start = np.arange(LANES) * SLC_LEN
    ov = ((cmp_start[:, None] <= slc_start[None, :] + SLC_LEN - 1)
          & (cmp_start[:, None] + CMP_LEN - 1 >= slc_start[None, :])
          & (np.arange(LANES)[None, :] < nb)
          & (np.arange(n_cmp_pad)[:, None] < n_cmp_pad - 1)).astype(np.float32)
    return jnp.asarray(ov.T)


def _nsa_attn(q_t, kcmp, vcmp_t, keys, v_t, ptail, pbig, z_col_block, B, S):
    M = B * S
    nq = S // Q_BLOCK
    gw = NSA_HPG * HEAD_DIM
    ovt = _nsa_consts(S)
    n_cmp_pad = S // CMP_STRIDE
    R = NSA_HPG * Q_BLOCK
    kern = functools.partial(_nsa_attn_kernel, seq_len=S)
    return pl.pallas_call(
        kern,
        grid=(B, NSA_GROUPS, nq),
        in_specs=[pl.BlockSpec((gw, Q_BLOCK), lambda b, g, i: (g, b * nq + i)),
                  pl.BlockSpec((1, 1, n_cmp_pad, HEAD_DIM), lambda b, g, i: (b, g, 0, 0)),
                  pl.BlockSpec((1, 1, VT_ROWS, n_cmp_pad), lambda b, g, i: (b, g, 0, 0)),
                  pl.BlockSpec((S, 2 * HEAD_DIM), lambda b, g, i: (b, g)),
                  pl.BlockSpec((VT_ROWS, S), lambda b, g, i: (g, b)),
                  pl.BlockSpec((S, HEAD_DIM), lambda b, g, i: (b, 4 + g)),
                  pl.BlockSpec((VT_ROWS, S), lambda b, g, i: (2 + g, b)),
                  pl.BlockSpec((Q_BLOCK, LANES), lambda b, g, i: (b * nq + i, TAIL_GATE_BLOCK)),
                  pl.BlockSpec((Q_BLOCK, gw), lambda b, g, i: (b * nq + i, z_col_block * NSA_GROUPS + g)),
                  pl.BlockSpec((LANES, n_cmp_pad), lambda b, g, i: (0, 0))],
        out_specs=pl.BlockSpec((Q_BLOCK, gw), lambda b, g, i: (b * nq + i, g)),
        out_shape=jax.ShapeDtypeStruct((M, NSA_WIDTH), BF16),
        scratch_shapes=[pltpu.VMEM((1, R), F32),
                        pltpu.VMEM((VT_ROWS, R), F32),
                        pltpu.VMEM((2 * HEAD_DIM, R), BF16),
                        pltpu.VMEM((SEL_TILE, R), F32),
                        pltpu.VMEM((SEL_TILE, R), F32)],
        compiler_params=_cparams(("arbitrary", "arbitrary", "arbitrary")),
        name="nsa_attn",
    )(q_t, kcmp, vcmp_t, keys, v_t, keys, v_t, ptail, pbig, ovt)


def _dn_prep_kernel(x_ref, w_ref, o_ref, *rest, mode, seq_len):
    x = x_ref[...].astype(F32)
    w = w_ref[...]
    row = lax.broadcasted_iota(jnp.int32, x.shape, 0)
    y = x * w[CONV_WIDTH - 1:CONV_WIDTH, :]
    for k in range(1, CONV_WIDTH):
        xs = jnp.where(row >= k, pltpu.roll(x, k, 0), 0.0)
        y = y + xs * w[CONV_WIDTH - 1 - k:CONV_WIDTH - k, :]
    y = _silu(y)
    if mode in ("q", "k"):
        outs = []
        for h in range(x.shape[1] // DN_DK):
            yh = y[:, h * DN_DK:(h + 1) * DN_DK]
            ss = jnp.sum(yh * yh, axis=-1, keepdims=True)
            yh = yh * lax.rsqrt(ss + EPS)
            if mode == "q":
                yh = yh * (DN_DK ** -0.5)
            outs.append(yh)
        y = jnp.concatenate(outs, axis=1)
    o_ref[...] = y.astype(o_ref.dtype)
    if mode == "k":
        kt_ref = rest[0]
        kt_ref[...] = y.T.astype(kt_ref.dtype)


DN_PREP_COLS = 128


def _dn_prep(pbig, conv_w, col0, mode, B, S):
    M = B * S
    tw = DN_PREP_COLS
    col0_blocks = col0 // tw
    nj = DN_WIDTH // tw
    wcol0 = {"q": 0, "k": nj, "v": 2 * nj}[mode]
    kern = functools.partial(_dn_prep_kernel, mode=mode, seq_len=S)
    out_specs = [pl.BlockSpec((S, tw), lambda b, j: (b, j))]
    out_shape = [jax.ShapeDtypeStruct((M, DN_WIDTH), BF16)]
    if mode == "k":
        out_specs.append(pl.BlockSpec((tw, S), lambda b, j: (j, b)))
        out_shape.append(jax.ShapeDtypeStruct((DN_WIDTH, M), BF16))
    return pl.pallas_call(
        kern,
        grid=(B, nj),
        in_specs=[pl.BlockSpec((S, tw), lambda b, j: (b, col0_blocks + j)),
                  pl.BlockSpec((CONV_WIDTH, tw), lambda b, j: (0, wcol0 + j))],
        out_specs=out_specs,
        out_shape=out_shape,
        compiler_params=_cparams(("arbitrary", "arbitrary")),
        name="dn_prep_" + mode,
    )(pbig, conv_w)


def _dn_gate_kernel(ab_ref, alog_ref, dtb_ref, o_ref, ot_ref, *, tm):
    ab = ab_ref[...]
    x = ab + dtb_ref[...]
    softplus = jnp.maximum(x, 0.0) + jnp.log(1.0 + jnp.exp(-jnp.abs(x)))
    g = -jnp.exp(alog_ref[...]) * softplus
    beta = _sigmoid(ab)
    lane = lax.broadcasted_iota(jnp.int32, (DN_CHUNK, LANES), 1)
    r = lax.broadcasted_iota(jnp.int32, (DN_CHUNK, DN_CHUNK), 0)
    c = lax.broadcasted_iota(jnp.int32, (DN_CHUNK, DN_CHUNK), 1)
    tril = jnp.where(r >= c, 1.0, 0.0).astype(F32)
    for ci in range(tm // DN_CHUNK):
        rows = slice(ci * DN_CHUNK, (ci + 1) * DN_CHUNK)
        dec = _dot_f32(tril, g[rows])
        out = jnp.where(lane < DN_HEADS, dec, beta[rows])
        o_ref[rows, :] = out
        ot_ref[:, rows] = out.T


def _dn_gate(ptail, alog_row, dtb_row):
    M = ptail.shape[0]
    tm = 512
    kern = functools.partial(_dn_gate_kernel, tm=tm)
    return pl.pallas_call(
        kern,
        grid=(M // tm,),
        in_specs=[pl.BlockSpec((tm, LANES), lambda i: (i, TAIL_AB_BLOCK)),
                  pl.BlockSpec((1, LANES), lambda i: (0, 0)),
                  pl.BlockSpec((1, LANES), lambda i: (0, 0))],
        out_specs=[pl.BlockSpec((tm, LANES), lambda i: (i, 0)),
                   pl.BlockSpec((LANES, tm), lambda i: (0, i))],
        out_shape=[jax.ShapeDtypeStruct((M, LANES), F32),
                   jax.ShapeDtypeStruct((LANES, M), F32)],
        compiler_params=_cparams(("arbitrary",)),
        name="dn_gate",
    )(ptail, alog_row, dtb_row)


def _dn_scan_kernel(q_ref, k_ref, v_ref, kt_ref, dec_ref, dect_ref, z_ref, gain_ref, o_ref, st_ref):
    C = DN_CHUNK

    @pl.when(pl.program_id(1) == 0)
    def _():
        st_ref[...] = jnp.zeros(st_ref.shape, F32)

    dec = dec_ref[...]
    dect = dect_ref[...]
    r = lax.broadcasted_iota(jnp.int32, (C, C), 0)
    c = lax.broadcasted_iota(jnp.int32, (C, C), 1)
    tril = r >= c
    strict = r > c
    gain = gain_ref[...]
    eye = jnp.where(r == c, 1.0, 0.0).astype(F32)
    lvl_masks = []
    for lg in range(int(math.log2(C))):
        lvl_masks.append(((r >> (lg + 1)) == (c >> (lg + 1))) & ((r >> lg) != (c >> lg)))

    for h0 in range(0, DN_HEADS, DN_HEAD_GROUP):
        heads = range(h0, h0 + DN_HEAD_GROUP)
        cols = lambda ref: jnp.stack([ref[:, h * DN_DK:(h + 1) * DN_DK] for h in heads])
        qh = cols(q_ref).astype(F32)
        kh = cols(k_ref).astype(F32)
        vh = cols(v_ref).astype(F32)
        kth = jnp.stack([kt_ref[h * DN_DK:(h + 1) * DN_DK, :] for h in heads])
        dcol = jnp.stack([jnp.broadcast_to(dec[:, h:h + 1], (C, C)) for h in heads])
        bcol = jnp.stack([jnp.broadcast_to(dec[:, DN_HEADS + h:DN_HEADS + h + 1], (C, C)) for h in heads])
        drow = jnp.stack([jnp.broadcast_to(dect[h:h + 1, :], (C, C)) for h in heads])
        dlast = jnp.stack([jnp.broadcast_to(dect[h:h + 1, C - 1:C], (C, C)) for h in heads])
        lmat = jnp.exp(jnp.where(tril[None], dcol - drow, NEG))
        e_d = jnp.exp(dcol)
        kb = kh * bcol
        vb = vh * bcol
        a = jnp.where(strict[None], _bmm(kb.astype(BF16), kth) * lmat, 0.0)
        attn = jnp.where(tril[None], _bmm(qh.astype(BF16), kth) * lmat, 0.0)
        tinv = eye[None] - jnp.where(lvl_masks[0][None], a, 0.0)
        for lm in lvl_masks[1:]:
            t16 = tinv.astype(BF16)
            lo = jnp.where(lm[None], a, 0.0).astype(BF16)
            tinv = tinv - _bmm(t16, _bmm(lo, t16).astype(BF16))
        t16 = tinv.astype(BF16)
        u = _bmm(t16, vb.astype(BF16))
        w = _bmm(t16, (kb * e_d).astype(BF16))
        st = st_ref[h0:h0 + DN_HEAD_GROUP]
        st16 = st.astype(BF16)
        v_new = u - _bmm(w.astype(BF16), st16)
        v_new16 = v_new.astype(BF16)
        o = _bmm((qh * e_d).astype(BF16), st16) + _bmm(attn.astype(BF16), v_new16)
        kdt = (kth.astype(F32) * jnp.exp(dlast - drow)).astype(BF16)
        st_ref[h0:h0 + DN_HEAD_GROUP] = st * jnp.exp(dlast) + _bmm(kdt, v_new16)
        ms = jnp.mean(o * o, axis=-1, keepdims=True)
        y = o * lax.rsqrt(ms + EPS) * gain[None]
        for i, h in enumerate(heads):
            sl = slice(h * DN_DV, (h + 1) * DN_DV)
            o_ref[:, sl] = (y[i] * _silu(z_ref[:, sl].astype(F32))).astype(o_ref.dtype)


def _dn_scan(dq, dk, dv, dkt, dec, dect, pbig, z_col_block, gain_row, B, S):
    M = B * S
    C = DN_CHUNK
    nc = S // C
    tok = lambda b, n: (b * nc + n, 0)
    return pl.pallas_call(
        _dn_scan_kernel,
        grid=(B, nc),
        in_specs=[pl.BlockSpec((C, DN_WIDTH), tok),
                  pl.BlockSpec((C, DN_WIDTH), tok),
                  pl.BlockSpec((C, DN_WIDTH), tok),
                  pl.BlockSpec((DN_WIDTH, C), lambda b, n: (0, b * nc + n)),
                  pl.BlockSpec((C, LANES), tok),
                  pl.BlockSpec((LANES, C), lambda b, n: (0, b * nc + n)),
                  pl.BlockSpec((C, DN_WIDTH), lambda b, n: (b * nc + n, z_col_block)),
                  pl.BlockSpec((1, DN_DV), lambda b, n: (0, 0))],
        out_specs=pl.BlockSpec((C, DN_WIDTH), tok),
        out_shape=jax.ShapeDtypeStruct((M, DN_WIDTH), BF16),
        scratch_shapes=[pltpu.VMEM((DN_HEADS, DN_DK, DN_DV), F32)],
        compiler_params=_cparams(("arbitrary", "arbitrary")),
        name="dn_scan",
    )(dq, dk, dv, dkt, dec, dect, pbig, gain_row)


def _out1_kernel(oa_ref, ob_ref, wa_ref, wb_ref, ga_ref, gb_ref, o_ref, wa16_ref, wb16_ref):
    @pl.when(pl.program_id(1) == 0)
    def _():
        wa16_ref[...] = wa_ref[...].astype(BF16)
        wb16_ref[...] = wb_ref[...].astype(BF16)

    ya = _dot(oa_ref[...], wa16_ref[...])
    yb = _dot(ob_ref[...], wb16_ref[...])
    mix = _sigmoid(ga_ref[...].astype(F32)) * ya + _sigmoid(gb_ref[...].astype(F32)) * yb
    o_ref[...] = mix.astype(o_ref.dtype)


def _out1(o_a, o_b, wa, wb, pbig, D):
    M, K = o_a.shape
    tm = min(MM_TM, M)
    tn = min(MM_TN, D)
    nbd = D // tn
    return pl.pallas_call(
        _out1_kernel,
        grid=(D // tn, M // tm),
        in_specs=[pl.BlockSpec((tm, K), lambda j, i: (i, 0)),
                  pl.BlockSpec((tm, K), lambda j, i: (i, 0)),
                  pl.BlockSpec((K, tn), lambda j, i: (0, j), pipeline_mode=pl.Buffered(1)),
                  pl.BlockSpec((K, tn), lambda j, i: (0, j), pipeline_mode=pl.Buffered(1)),
                  pl.BlockSpec((tm, tn), lambda j, i: (i, j)),
                  pl.BlockSpec((tm, tn), lambda j, i: (i, nbd + j))],
        out_specs=pl.BlockSpec((tm, tn), lambda j, i: (i, j)),
        out_shape=jax.ShapeDtypeStruct((M, D), BF16),
        scratch_shapes=[pltpu.VMEM((K, tn), BF16), pltpu.VMEM((K, tn), BF16)],
        compiler_params=_cparams(("arbitrary", "arbitrary")),
        name="out1",
    )(o_a, o_b, wa, wb, pbig, pbig)


def _final_kernel(mix_ref, x_ref, gate_ref, fg_ref, o_ref):
    xn = x_ref[...] + gate_ref[0] * mix_ref[...].astype(F32)
    ms = jnp.mean(xn * xn, axis=-1, keepdims=True)
    o_ref[...] = xn * lax.rsqrt(ms + EPS) * fg_ref[...]


def _final(mixed, x2, mod3, final_gain, S):
    M, D = x2.shape
    tm = 256
    nb = S // tm
    return pl.pallas_call(
        _final_kernel,
        grid=(M // tm,),
        in_specs=[pl.BlockSpec((tm, D), lambda i: (i, 0)),
                  pl.BlockSpec((tm, D), lambda i: (i, 0)),
                  pl.BlockSpec((1, 1, D), lambda i: (i // nb, 0, 2)),
                  pl.BlockSpec((1, D), lambda i: (0, 0))],
        out_specs=pl.BlockSpec((tm, D), lambda i: (i, 0)),
        out_shape=jax.ShapeDtypeStruct((M, D), F32),
        compiler_params=_cparams(("arbitrary",)),
        name="final",
    )(mixed, x2, mod3, final_gain)


def _pad_cols(w, width):
    return jnp.pad(w, ((0, 0), (0, width - w.shape[1])))


def _proj_row_offsets(D):
    widths = (NSA_WIDTH, 6 * KV_WIDTH, 3 * NSA_HEADS, NSA_WIDTH, 3 * DN_WIDTH, DN_HEADS, DN_HEADS, DN_WIDTH, 2 * D)
    offs = [int(o) for o in np.concatenate([[0], np.cumsum(widths)])]
    tiles = lambda seg: [offs[seg] + t * MM_TN for t in range(widths[seg] // MM_TN)]
    big = tiles(8) + tiles(4) + tiles(0) + tiles(3) + tiles(7)
    kv0 = offs[1]
    tail = [kv0 + 2 * KV_WIDTH, kv0 + 4 * KV_WIDTH, kv0, offs[2], offs[5]]
    return big, tail


TAIL_TN = 512
TAIL_TM = 1024
TAIL_GATE_BLOCK = 3 * TAIL_TN // LANES
TAIL_AB_BLOCK = 4 * TAIL_TN // LANES
TAIL_CMP_BLOCK = 2 * TAIL_TN // LANES


def kernel(x, c, positions, w_ada, b_ada, norm_gain, w_in, cmp_pos_k, cmp_pos_v, w_cmp_k1, w_cmp_k2,
           w_cmp_v1, w_cmp_v2, conv_w, dt_bias, a_log, dn_norm_gain, w_proj_a, w_proj_b, w_out, final_gain):
    B, S, D = x.shape
    M = B * S
    depth = w_in.shape[0]
    assert S % SEL_TILE == 0 and S % DN_CHUNK == 0 and S >= WINDOW + Q_BLOCK and D % 512 == 0 and B <= 8
    assert S // SLC_LEN <= LANES and D % 1024 == 0 and DN_DK == DN_CHUNK
    assert depth == 1, "the final RMSNorm is fused into the last layer's output kernel"

    off_dq = 2 * D
    off_q = off_dq + 3 * DN_WIDTH
    off_z = off_q + NSA_WIDTH
    off_dz = off_z + NSA_WIDTH

    x2 = x.reshape(M, D)
    pos_col = positions.reshape(M, 1)
    cmp_end = np.arange(S // CMP_STRIDE - 1) * CMP_STRIDE + CMP_LEN - 1
    pos_cmp = jnp.pad(positions[:, cmp_end], ((0, 0), (0, 1)))[:, :, None]
    c8 = jnp.pad(c, ((0, 8 - B), (0, 0)))

    for l in range(depth):
        mod = _ada(c8, w_ada[l], b_ada[l][None, :])
        mod3 = mod[:B].reshape(B, 1, 3 * D)
        h = _norm(x2, norm_gain[l][None, :], mod3, S)

        w_nk = jnp.transpose(w_in[l])
        big_rows, tail_rows = _proj_row_offsets(D)
        pbig = _proj(h, w_nk, big_rows, MM_TM, MM_TN, BF16, "proj_big")
        ptail = _proj(h, w_nk, tail_rows, TAIL_TM, TAIL_TN, F32, "proj_tail")

        q_t, keys, v_t = _nsa_prep(pbig, ptail, pos_col, off_q // NSA_WIDTH, S)
        kcmp = _compress(ptail, TAIL_CMP_BLOCK, cmp_pos_k[l], w_cmp_k1[l], w_cmp_k2[l], pos_cmp, B, S, True)
        vcmp = _compress(ptail, TAIL_CMP_BLOCK + NSA_GROUPS, cmp_pos_v[l], w_cmp_v1[l], w_cmp_v2[l].T,
                         pos_cmp, B, S, False)
        o_a = _nsa_attn(q_t, kcmp, vcmp, keys, v_t, ptail, pbig, off_z // NSA_WIDTH, B, S)

        cw = conv_w[l]
        dq = _dn_prep(pbig, cw, off_dq, "q", B, S)[0]
        dk, dkt = _dn_prep(pbig, cw, off_dq + DN_WIDTH, "k", B, S)
        dv = _dn_prep(pbig, cw, off_dq + 2 * DN_WIDTH, "v", B, S)[0]
        alog_row = _pad_cols(a_log[l][None, :].astype(F32), LANES)
        dtb_row = _pad_cols(dt_bias[l][None, :].astype(F32), LANES)
        dec, dect = _dn_gate(ptail, alog_row, dtb_row)
        o_b = _dn_scan(dq, dk, dv, dkt, dec, dect, pbig, off_dz // DN_WIDTH,
                       dn_norm_gain[l][None, :], B, S)

        mixed_in = _out1(o_a, o_b, w_proj_a[l], w_proj_b[l], pbig, D)
        mixed = _matmul(mixed_in, w_out[l], BF16, "out2")
        x2 = _final(mixed, x2, mod3, final_gain[None, :], S)
    return x2.reshape(B, S, D)
```

```python
import functools
import math

import numpy as np
import jax
import jax.numpy as jnp
from jax import lax
from jax.experimental import pallas as pl
from jax.experimental.pallas import tpu as pltpu

F32 = jnp.float32
BF16 = jnp.bfloat16

NSA_HEADS = 16
NSA_GROUPS = 2
NSA_HPG = NSA_HEADS // NSA_GROUPS
HEAD_DIM = 128
ROT_DIM = HEAD_DIM // 4
ROT_HALF = ROT_DIM // 2
ROPE_THETA = 500000.0
CMP_LEN = 32
CMP_STRIDE = 16
CMP_HIDDEN = 256
SLC_LEN = 64
SLC_TOPK = 16
WINDOW = 512
Q_BLOCK = 128
NSA_WIDTH = NSA_HEADS * HEAD_DIM
KV_WIDTH = NSA_GROUPS * HEAD_DIM
DN_HEADS = 16
DN_DK = 128
DN_DV = 128
DN_WIDTH = DN_HEADS * DN_DV
CONV_WIDTH = 4
EPS = 1e-6

LANES = 128
VMEM_LIMIT_BYTES = 56 * 1024 * 1024

DN_CHUNK = 128
DN_HEAD_GROUP = 16
SEL_TILE = 512
VT_ROWS = HEAD_DIM + 16
NEG = -1e30


def _cparams(sem):
    return pltpu.CompilerParams(dimension_semantics=sem, vmem_limit_bytes=VMEM_LIMIT_BYTES)


def _sigmoid(x):
    return 1.0 / (1.0 + jnp.exp(-x))


def _silu(x):
    return x * _sigmoid(x)


def _dot(a, b):
    return jnp.dot(a, b, preferred_element_type=F32)


def _dot_nt(a, b):
    return lax.dot_general(a, b, (((1,), (1,)), ((), ())), preferred_element_type=F32)


def _bmm(a, b):
    return lax.dot_general(a, b, (((2,), (1,)), ((0,), (0,))), preferred_element_type=F32)


def _dot_f32(a, b):
    return jnp.dot(a, b, preferred_element_type=F32, precision=lax.Precision.HIGHEST)


def _ada_kernel(c_ref, w_ref, b_ref, o_ref):
    c = c_ref[...]
    c_hi = c.astype(BF16)
    c_lo = (c - c_hi.astype(F32)).astype(BF16)
    w = w_ref[...]
    w_hi = w.astype(BF16)
    w_lo = (w - w_hi.astype(F32)).astype(BF16)
    acc = _dot(c_hi, w_hi) + _dot(c_lo, w_hi) + _dot(c_hi, w_lo)
    o_ref[...] = acc + b_ref[...]


def _ada(c8, w_ada, b_ada):
    D, N = w_ada.shape
    tn = min(512, N)
    return pl.pallas_call(
        _ada_kernel,
        grid=(N // tn,),
        in_specs=[pl.BlockSpec((8, D), lambda j: (0, 0)),
                  pl.BlockSpec((D, tn), lambda j: (0, j)),
                  pl.BlockSpec((1, tn), lambda j: (0, j))],
        out_specs=pl.BlockSpec((8, tn), lambda j: (0, j)),
        out_shape=jax.ShapeDtypeStruct((8, N), F32),
        compiler_params=_cparams(("arbitrary",)),
        name="ada",
    )(c8, w_ada, b_ada)


def _norm_kernel(x_ref, gain_ref, shift_ref, scale_ref, o_ref):
    x = x_ref[...]
    ms = jnp.mean(x * x, axis=-1, keepdims=True)
    y = x * lax.rsqrt(ms + EPS) * gain_ref[...]
    o_ref[...] = (y * (1.0 + scale_ref[0]) + shift_ref[0]).astype(o_ref.dtype)


def _norm(x2, gain, mod3, S):
    M, D = x2.shape
    tm = 256
    nb = S // tm
    return pl.pallas_call(
        _norm_kernel,
        grid=(M // tm,),
        in_specs=[pl.BlockSpec((tm, D), lambda i: (i, 0)),
                  pl.BlockSpec((1, D), lambda i: (0, 0)),
                  pl.BlockSpec((1, 1, D), lambda i: (i // nb, 0, 0)),
                  pl.BlockSpec((1, 1, D), lambda i: (i // nb, 0, 1))],
        out_specs=pl.BlockSpec((tm, D), lambda i: (i, 0)),
        out_shape=jax.ShapeDtypeStruct((M, D), BF16),
        compiler_params=_cparams(("arbitrary",)),
        name="norm",
    )(x2, gain, mod3, mod3)


MM_TM = 1024
MM_TN = 1024
W_ROW_ALIGN = 16
WT_CHUNK = 128


def _mm_kernel(offs_ref, a_ref, w_hbm, o_ref, wf32_ref, w16_ref, sem, *, w_is_nk, tn):
    j = pl.program_id(0)
    i = pl.program_id(1)

    def weight_copy(jj):
        if w_is_nk:
            src = w_hbm.at[pl.ds(pl.multiple_of(offs_ref[jj] * W_ROW_ALIGN, W_ROW_ALIGN), tn), :]
        else:
            src = w_hbm.at[:, pl.ds(pl.multiple_of(offs_ref[jj] * LANES, LANES), tn)]
        return pltpu.make_async_copy(src, wf32_ref, sem)

    @pl.when((j == 0) & (i == 0))
    def _():
        weight_copy(0).start()

    @pl.when(i == 0)
    def _():
        weight_copy(j).wait()
        if w_is_nk:
            for c in range(0, tn, WT_CHUNK):
                w16_ref[:, c:c + WT_CHUNK] = wf32_ref[c:c + WT_CHUNK, :].T.astype(BF16)
        else:
            w16_ref[...] = wf32_ref[...].astype(BF16)

        @pl.when(j + 1 < pl.num_programs(0))
        def _():
            weight_copy(j + 1).start()

    o_ref[...] = _dot(a_ref[...], w16_ref[...]).astype(o_ref.dtype)


def _mm_call(a, w, offs, w_is_nk, tm, tn, out_dtype, name):
    M, K = a.shape
    tm = min(tm, M)
    n_tiles = len(offs)
    grid_spec = pltpu.PrefetchScalarGridSpec(
        num_scalar_prefetch=1,
        grid=(n_tiles, M // tm),
        in_specs=[pl.BlockSpec((tm, K), lambda j, i, o: (i, 0)),
                  pl.BlockSpec(memory_space=pl.ANY)],
        out_specs=pl.BlockSpec((tm, tn), lambda j, i, o: (i, j)),
        scratch_shapes=[pltpu.VMEM((tn, K) if w_is_nk else (K, tn), F32),
                        pltpu.VMEM((K, tn), BF16),
                        pltpu.SemaphoreType.DMA(())],
    )
    return pl.pallas_call(
        functools.partial(_mm_kernel, w_is_nk=w_is_nk, tn=tn),
        grid_spec=grid_spec,
        out_shape=jax.ShapeDtypeStruct((M, n_tiles * tn), out_dtype),
        compiler_params=_cparams(("arbitrary", "arbitrary")),
        name=name,
    )(jnp.asarray(np.asarray(offs, np.int32)), a, w)


def _matmul(a, w, out_dtype, name):
    N = w.shape[1]
    tn = min(MM_TN, N)
    offs = [t * tn // LANES for t in range(N // tn)]
    return _mm_call(a, w, offs, False, MM_TM, tn, out_dtype, name)


def _proj(h, w_nk, row_offsets, tm, tn, out_dtype, name):
    assert all(int(o) % W_ROW_ALIGN == 0 and int(o) + tn <= w_nk.shape[0] for o in row_offsets)
    return _mm_call(h, w_nk, [int(o) // W_ROW_ALIGN for o in row_offsets], True, tm, tn, out_dtype, name)


def _rope_consts():
    inv = ROPE_THETA ** (-np.arange(ROT_HALF, dtype=np.float64) / ROT_HALF)
    invf = np.zeros((1, LANES), np.float32)
    invf[0, :ROT_HALF] = inv
    invf[0, ROT_HALF:ROT_DIM] = inv
    sgn = np.zeros((1, LANES), np.float32)
    sgn[0, :ROT_HALF] = -1.0
    sgn[0, ROT_HALF:ROT_DIM] = 1.0
    return jnp.asarray(invf), jnp.asarray(sgn)


def _rope_tables(pos_col, invf, sgn):
    ang = pos_col.astype(F32) * invf
    return jnp.cos(ang), jnp.sin(ang) * sgn


def _rope_apply(x, cos_t, sin_t):
    lane = lax.broadcasted_iota(jnp.int32, x.shape, 1)
    partner = jnp.where(lane < ROT_HALF,
                        pltpu.roll(x, LANES - ROT_HALF, 1),
                        pltpu.roll(x, ROT_HALF, 1))
    return x * cos_t + partner * sin_t


KEYS_WIDTH = 6 * HEAD_DIM


def _nsa_prep_kernel(q_ref, kv_ref, pos_ref, invf_ref, sgn_ref, qt_ref, keys_ref, vt_ref, *, seq_len):
    tm = q_ref.shape[0]
    cos_t, sin_t = _rope_tables(pos_ref[...], invf_ref[...], sgn_ref[...])
    qscale = (HEAD_DIM ** -0.5) * math.log2(math.e)
    for h in range(NSA_HEADS):
        sl = slice(h * HEAD_DIM, (h + 1) * HEAD_DIM)
        xq = _rope_apply(q_ref[:, sl].astype(F32), cos_t, sin_t) * qscale
        qt_ref[sl, :] = xq.T.astype(qt_ref.dtype)
    tok = (pl.program_id(0) * tm) % seq_len + lax.broadcasted_iota(jnp.int32, (tm, LANES), 0)
    lane = lax.broadcasted_iota(jnp.int32, (tm, LANES), 1)
    onehot = jnp.where(lane == (tok >> int(math.log2(SLC_LEN))), 1.0, 0.0).astype(keys_ref.dtype)
    blk = lambda i: kv_ref[:, i * HEAD_DIM:(i + 1) * HEAD_DIM]
    for g in range(NSA_GROUPS):
        ks = _rope_apply(blk(g), cos_t, sin_t)
        keys_ref[:, (2 * g) * HEAD_DIM:(2 * g + 1) * HEAD_DIM] = ks.astype(keys_ref.dtype)
        keys_ref[:, (2 * g + 1) * HEAD_DIM:(2 * g + 2) * HEAD_DIM] = onehot
        kw = _rope_apply(blk(4 + g), cos_t, sin_t)
        keys_ref[:, (4 + g) * HEAD_DIM:(5 + g) * HEAD_DIM] = kw.astype(keys_ref.dtype)
        ones = jnp.ones((VT_ROWS - HEAD_DIM, tm), vt_ref.dtype)
        for c, src in ((g, 2 + g), (2 + g, 6 + g)):
            vt_ref[c * VT_ROWS:c * VT_ROWS + HEAD_DIM, :] = blk(src).T.astype(vt_ref.dtype)
            vt_ref[c * VT_ROWS + HEAD_DIM:(c + 1) * VT_ROWS, :] = ones


def _nsa_prep(pbig, ptail, pos_col, q_col_block, S):
    M = pbig.shape[0]
    tm = 256
    invf, sgn = _rope_consts()
    return pl.pallas_call(
        functools.partial(_nsa_prep_kernel, seq_len=S),
        grid=(M // tm,),
        in_specs=[pl.BlockSpec((tm, NSA_WIDTH), lambda i: (i, q_col_block)),
                  pl.BlockSpec((tm, 1024), lambda i: (i, 0)),
                  pl.BlockSpec((tm, 1), lambda i: (i, 0)),
                  pl.BlockSpec((1, LANES), lambda i: (0, 0)),
                  pl.BlockSpec((1, LANES), lambda i: (0, 0))],
        out_specs=[pl.BlockSpec((NSA_WIDTH, tm), lambda i: (0, i)),
                   pl.BlockSpec((tm, KEYS_WIDTH), lambda i: (i, 0)),
                   pl.BlockSpec((4 * VT_ROWS, tm), lambda i: (0, i))],
        out_shape=[jax.ShapeDtypeStruct((NSA_WIDTH, M), BF16),
                   jax.ShapeDtypeStruct((M, KEYS_WIDTH), BF16),
                   jax.ShapeDtypeStruct((4 * VT_ROWS, M), BF16)],
        compiler_params=_cparams(("arbitrary",)),
        name="nsa_prep",
    )(pbig, ptail, pos_col, invf, sgn)


def _compress_kernel(x_ref, pe_ref, w1_ref, w2_ref, pos_ref, invf_ref, sgn_ref, o_ref, *, rope, n_rows):
    half = CMP_LEN // 2
    top = jnp.zeros((n_rows, CMP_HIDDEN), F32)
    bot = jnp.zeros((n_rows, CMP_HIDDEN), F32)
    for l in range(half):
        xl = x_ref[pl.ds(l, n_rows, stride=CMP_STRIDE), :]
        w_top = w1_ref[l * HEAD_DIM:(l + 1) * HEAD_DIM, :].astype(BF16)
        w_bot = w1_ref[(half + l) * HEAD_DIM:(half + l + 1) * HEAD_DIM, :].astype(BF16)
        top = top + _dot((xl + pe_ref[l:l + 1, :]).astype(BF16), w_top)
        bot = bot + _dot((xl + pe_ref[half + l:half + l + 1, :]).astype(BF16), w_bot)
    hid = top + pltpu.roll(bot, n_rows - 1, 0)
    act = _silu(hid).astype(BF16)
    if rope:
        out = _dot(act, w2_ref[...].astype(BF16))
        cos_t, sin_t = _rope_tables(pos_ref[0], invf_ref[...], sgn_ref[...])
        o_ref[0, 0] = _rope_apply(out, cos_t, sin_t).astype(o_ref.dtype)
    else:
        o_ref[0, 0, 0:HEAD_DIM, :] = _dot_nt(w2_ref[...].astype(BF16), act).astype(o_ref.dtype)
        o_ref[0, 0, HEAD_DIM:VT_ROWS, :] = jnp.ones((VT_ROWS - HEAD_DIM, n_rows), o_ref.dtype)


def _compress(ptail, col_block0, pe, w1, w2, pos_cmp, B, S, rope):
    n_rows = S // CMP_STRIDE
    invf, sgn = _rope_consts()
    kern = functools.partial(_compress_kernel, rope=rope, n_rows=n_rows)
    out_dims = (n_rows, HEAD_DIM) if rope else (VT_ROWS, n_rows)
    return pl.pallas_call(
        kern,
        grid=(B, NSA_GROUPS),
        in_specs=[pl.BlockSpec((S, HEAD_DIM), lambda b, g: (b, col_block0 + g)),
                  pl.BlockSpec((CMP_LEN, HEAD_DIM), lambda b, g: (0, 0)),
                  pl.BlockSpec((CMP_LEN * HEAD_DIM, CMP_HIDDEN), lambda b, g: (0, 0)),
                  pl.BlockSpec(w2.shape, lambda b, g: (0, 0)),
                  pl.BlockSpec((1, n_rows, 1), lambda b, g: (b, 0, 0)),
                  pl.BlockSpec((1, LANES), lambda b, g: (0, 0)),
                  pl.BlockSpec((1, LANES), lambda b, g: (0, 0))],
        out_specs=pl.BlockSpec((1, 1) + out_dims, lambda b, g: (b, g, 0, 0)),
        out_shape=jax.ShapeDtypeStruct((B, NSA_GROUPS) + out_dims, BF16),
        compiler_params=_cparams(("arbitrary", "arbitrary")),
        name="compress_k" if rope else "compress_v",
    )(ptail, pe, w1, w2, pos_cmp, invf, sgn)


def _tile_lanes(x, n):
    return jnp.concatenate([x] * n, axis=1)


def _nsa_attn_kernel(qt_ref, kc_ref, vct_ref, ks_ref, vst_ref, kw_ref, vwt_ref, g_ref, z_ref,
                     ovt_ref, o_ref, m_s, acc_s, qa_s, sa_s, sb_s, *, seq_len):
    T = Q_BLOCK
    P = NSA_HPG
    R = P * T
    n_cmp_pad = seq_len // CMP_STRIDE
    n_cmp = n_cmp_pad - 1
    nb = seq_len // SLC_LEN
    qi = pl.program_id(2)
    qs = qi * T
    d0 = pl.multiple_of(qs, T)
    wlen = WINDOW + T
    w0 = pl.multiple_of(jnp.maximum(qs - WINDOW, 0), T)

    qt = jnp.concatenate([qt_ref[p * HEAD_DIM:(p + 1) * HEAD_DIM, :] for p in range(P)], axis=1)

    s_c = _dot(kc_ref[0, 0], qt)
    s_w = _dot(kw_ref[pl.ds(w0, wlen), :], qt)
    s_d = _dot(ks_ref[pl.ds(d0, T), 0:HEAD_DIM], qt)

    tok = lambda n: qs + lax.broadcasted_iota(jnp.int32, (n, T), 1)
    row = lambda n: lax.broadcasted_iota(jnp.int32, (n, T), 0)
    ok_c = (row(n_cmp_pad) * CMP_STRIDE + (CMP_LEN - 1) <= tok(n_cmp_pad)) & (row(n_cmp_pad) < n_cmp)
    key_w = w0 + row(wlen)
    ok_w = (key_w <= tok(wlen)) & (tok(wlen) - key_w < WINDOW)
    ok_d = row(T) <= lax.broadcasted_iota(jnp.int32, (T, T), 1)
    addmask = lambda ok: _tile_lanes(jnp.where(ok, 0.0, NEG).astype(F32), P)
    s_c = s_c + addmask(ok_c)
    s_w = s_w + addmask(ok_w)
    s_d = s_d + addmask(ok_d)

    m_c = jnp.max(s_c, axis=0, keepdims=True)
    m_w = jnp.max(s_w, axis=0, keepdims=True)
    m_d = jnp.max(s_d, axis=0, keepdims=True)
    m_c = jnp.where(m_c > 0.5 * NEG, m_c, 0.0)
    p_c = jnp.exp2(s_c - m_c)
    p_w = jnp.exp2(s_w - m_w)
    p_d = jnp.exp2(s_d - m_d)
    o_c = _dot(vct_ref[0, 0], p_c.astype(BF16))
    o_w = _dot(vwt_ref[:, pl.ds(w0, wlen)], p_w.astype(BF16))
    inv_c = 1.0 / jnp.maximum(o_c[HEAD_DIM:HEAD_DIM + 1, :], 1e-30)
    inv_w = 1.0 / o_w[HEAD_DIM:HEAD_DIM + 1, :]
    m_s[...] = m_d
    acc_s[...] = _dot(vst_ref[:, pl.ds(d0, T)], p_d.astype(BF16))

    pn = p_c * inv_c
    p_sum = pn[:, 0:T]
    for p in range(1, P):
        p_sum = p_sum + pn[:, p * T:(p + 1) * T]
    imp = _dot_f32(ovt_ref[...], p_sum)[0:nb]
    jrow = row(nb)
    t_lane = tok(nb)
    tb = t_lane >> int(math.log2(SLC_LEN))
    visible = jrow * SLC_LEN <= t_lane
    forced = (jrow == 0) | (jrow == tb) | (jrow == tb - 1)
    score = jnp.where(forced, 1e9, jnp.where(visible, imp, -jnp.inf))
    cnt = jnp.zeros((nb, T), F32)
    for k in range(nb):
        rk = score[k:k + 1, :]
        cnt = cnt + jnp.where(jrow > k, jnp.where(rk >= score, 1.0, 0.0), jnp.where(rk > score, 1.0, 0.0))
    keep = (cnt < float(min(SLC_TOPK, nb))) & visible & (jrow < 2 * qi)
    bias = jnp.where(keep, 0.0, NEG).astype(F32)
    bias = jnp.concatenate([bias, jnp.full((LANES - nb, T), NEG, F32)], axis=0) if nb < LANES else bias
    qa_s[0:HEAD_DIM, :] = qt
    qa_s[HEAD_DIM:2 * HEAD_DIM, :] = _tile_lanes(bias.astype(BF16), P)

    n_t = (qs + SEL_TILE - 1) // SEL_TILE

    def scores_into(kt, dst):
        k0 = pl.multiple_of(kt * SEL_TILE, SEL_TILE)
        dst[...] = _dot(ks_ref[pl.ds(k0, SEL_TILE), :], qa_s[...])

    def consume(kt, src):
        k0 = pl.multiple_of(kt * SEL_TILE, SEL_TILE)
        s = src[...]
        m_prev = m_s[...]
        m_new = jnp.maximum(m_prev, jnp.max(s, axis=0, keepdims=True))
        alpha = jnp.exp2(m_prev - m_new)
        p = jnp.exp2(s - m_new)
        acc_s[...] = alpha * acc_s[...] + _dot(vst_ref[:, pl.ds(k0, SEL_TILE)], p.astype(BF16))
        m_s[...] = m_new

    @pl.when(n_t > 0)
    def _():
        scores_into(0, sa_s)

    def pair_body(i, carry):
        kt = 2 * i
        scores_into(kt + 1, sb_s)
        consume(kt, sa_s)
        scores_into(jnp.minimum(kt + 2, n_t - 1), sa_s)
        consume(kt + 1, sb_s)
        return carry

    lax.fori_loop(0, n_t // 2, pair_body, 0)

    @pl.when(n_t % 2 == 1)
    def _():
        consume(n_t - 1, sa_s)

    gates_t = _sigmoid(g_ref[...]).T
    first_group = pl.program_id(1) == 0

    def grow(r):
        rows = []
        for p in range(P):
            h0 = r * NSA_HEADS + p
            rows.append(jnp.where(first_group, gates_t[h0:h0 + 1, :], gates_t[h0 + P:h0 + P + 1, :]))
        return jnp.concatenate(rows, axis=1)

    o_s = acc_s[...]
    hd = slice(0, HEAD_DIM)
    o_t = (o_c[hd] * (grow(0) * inv_c) + o_s[hd] * (grow(1) / o_s[HEAD_DIM:HEAD_DIM + 1, :])
           + o_w[hd] * (grow(2) * inv_w))
    for p in range(P):
        sl = slice(p * HEAD_DIM, (p + 1) * HEAD_DIM)
        o_p = o_t[:, p * T:(p + 1) * T].T
        o_ref[:, sl] = (o_p * _silu(z_ref[:, sl].astype(F32))).astype(o_ref.dtype)


def _nsa_consts(S):
    n_cmp_pad = S // CMP_STRIDE
    nb = S // SLC_LEN
    cmp_start = np.arange(n_cmp_pad) * CMP_STRIDE
    slc_start = np.arange(LANES) * SLC_LEN
    ov = ((cmp_start[:, None] <= slc_start[None, :] + SLC_LEN - 1)
          & (cmp_start[:, None] + CMP_LEN - 1 >= slc_start[None, :])
          & (np.arange(LANES)[None, :] < nb)
          & (np.arange(n_cmp_pad)[:, None] < n_cmp_pad - 1)).astype(np.float32)
    return jnp.asarray(ov.T)


def _nsa_attn(q_t, kcmp, vcmp_t, keys, v_t, psmall, pbig, z_col_block, B, S):
    M = B * S
    nq = S // Q_BLOCK
    gw = NSA_HPG * HEAD_DIM
    ovt = _nsa_consts(S)
    n_cmp_pad = S // CMP_STRIDE
    R = NSA_HPG * Q_BLOCK
    kern = functools.partial(_nsa_attn_kernel, seq_len=S)
    return pl.pallas_call(
        kern,
        grid=(B, NSA_GROUPS, nq),
        in_specs=[pl.BlockSpec((gw, Q_BLOCK), lambda b, g, i: (g, b * nq + i)),
                  pl.BlockSpec((1, 1, n_cmp_pad, HEAD_DIM), lambda b, g, i: (b, g, 0, 0)),
                  pl.BlockSpec((1, 1, VT_ROWS, n_cmp_pad), lambda b, g, i: (b, g, 0, 0)),
                  pl.BlockSpec((S, 2 * HEAD_DIM), lambda b, g, i: (b, g)),
                  pl.BlockSpec((VT_ROWS, S), lambda b, g, i: (g, b)),
                  pl.BlockSpec((S, HEAD_DIM), lambda b, g, i: (b, 4 + g)),
                  pl.BlockSpec((VT_ROWS, S), lambda b, g, i: (2 + g, b)),
                  pl.BlockSpec((Q_BLOCK, LANES), lambda b, g, i: (b * nq + i, SMALL_GATE_BLOCK)),
                  pl.BlockSpec((Q_BLOCK, gw), lambda b, g, i: (b * nq + i, z_col_block * NSA_GROUPS + g)),
                  pl.BlockSpec((LANES, n_cmp_pad), lambda b, g, i: (0, 0))],
        out_specs=pl.BlockSpec((Q_BLOCK, gw), lambda b, g, i: (b * nq + i, g)),
        out_shape=jax.ShapeDtypeStruct((M, NSA_WIDTH), BF16),
        scratch_shapes=[pltpu.VMEM((1, R), F32),
                        pltpu.VMEM((VT_ROWS, R), F32),
                        pltpu.VMEM((2 * HEAD_DIM, R), BF16),
                        pltpu.VMEM((SEL_TILE, R), F32),
                        pltpu.VMEM((SEL_TILE, R), F32)],
        compiler_params=_cparams(("arbitrary", "arbitrary", "arbitrary")),
        name="nsa_attn",
    )(q_t, kcmp, vcmp_t, keys, v_t, keys, v_t, psmall, pbig, ovt)


def _dn_prep_kernel(x_ref, w_ref, o_ref, *rest, mode, seq_len):
    x = x_ref[...].astype(F32)
    w = w_ref[...]
    row = lax.broadcasted_iota(jnp.int32, x.shape, 0)
    y = x * w[CONV_WIDTH - 1:CONV_WIDTH, :]
    for k in range(1, CONV_WIDTH):
        xs = jnp.where(row >= k, pltpu.roll(x, k, 0), 0.0)
        y = y + xs * w[CONV_WIDTH - 1 - k:CONV_WIDTH - k, :]
    y = _silu(y)
    if mode in ("q", "k"):
        outs = []
        for h in range(x.shape[1] // DN_DK):
            yh = y[:, h * DN_DK:(h + 1) * DN_DK]
            ss = jnp.sum(yh * yh, axis=-1, keepdims=True)
            yh = yh * lax.rsqrt(ss + EPS)
            if mode == "q":
                yh = yh * (DN_DK ** -0.5)
            outs.append(yh)
        y = jnp.concatenate(outs, axis=1)
    o_ref[...] = y.astype(o_ref.dtype)
    if mode == "k":
        kt_ref = rest[0]
        kt_ref[...] = y.T.astype(kt_ref.dtype)


DN_PREP_COLS = 128


def _dn_prep(pbig, conv_w, col0, mode, B, S):
    M = B * S
    tw = DN_PREP_COLS
    col0_blocks = col0 // tw
    nj = DN_WIDTH // tw
    wcol0 = {"q": 0, "k": nj, "v": 2 * nj}[mode]
    kern = functools.partial(_dn_prep_kernel, mode=mode, seq_len=S)
    out_specs = [pl.BlockSpec((S, tw), lambda b, j: (b, j))]
    out_shape = [jax.ShapeDtypeStruct((M, DN_WIDTH), BF16)]
    if mode == "k":
        out_specs.append(pl.BlockSpec((tw, S), lambda b, j: (j, b)))
        out_shape.append(jax.ShapeDtypeStruct((DN_WIDTH, M), BF16))
    return pl.pallas_call(
        kern,
        grid=(B, nj),
        in_specs=[pl.BlockSpec((S, tw), lambda b, j: (b, col0_blocks + j)),
                  pl.BlockSpec((CONV_WIDTH, tw), lambda b, j: (0, wcol0 + j))],
        out_specs=out_specs,
        out_shape=out_shape,
        compiler_params=_cparams(("arbitrary", "arbitrary")),
        name="dn_prep_" + mode,
    )(pbig, conv_w)


def _dn_gate_kernel(ab_ref, alog_ref, dtb_ref, o_ref, ot_ref, *, tm):
    ab = ab_ref[...]
    x = ab + dtb_ref[...]
    softplus = jnp.maximum(x, 0.0) + jnp.log(1.0 + jnp.exp(-jnp.abs(x)))
    g = -jnp.exp(alog_ref[...]) * softplus
    beta = _sigmoid(ab)
    lane = lax.broadcasted_iota(jnp.int32, (DN_CHUNK, LANES), 1)
    r = lax.broadcasted_iota(jnp.int32, (DN_CHUNK, DN_CHUNK), 0)
    c = lax.broadcasted_iota(jnp.int32, (DN_CHUNK, DN_CHUNK), 1)
    tril = jnp.where(r >= c, 1.0, 0.0).astype(F32)
    for ci in range(tm // DN_CHUNK):
        rows = slice(ci * DN_CHUNK, (ci + 1) * DN_CHUNK)
        dec = _dot_f32(tril, g[rows])
        out = jnp.where(lane < DN_HEADS, dec, beta[rows])
        o_ref[rows, :] = out
        ot_ref[:, rows] = out.T


def _dn_gate(psmall, alog_row, dtb_row):
    M = psmall.shape[0]
    tm = 512
    kern = functools.partial(_dn_gate_kernel, tm=tm)
    return pl.pallas_call(
        kern,
        grid=(M // tm,),
        in_specs=[pl.BlockSpec((tm, LANES), lambda i: (i, SMALL_AB_BLOCK)),
                  pl.BlockSpec((1, LANES), lambda i: (0, 0)),
                  pl.BlockSpec((1, LANES), lambda i: (0, 0))],
        out_specs=[pl.BlockSpec((tm, LANES), lambda i: (i, 0)),
                   pl.BlockSpec((LANES, tm), lambda i: (0, i))],
        out_shape=[jax.ShapeDtypeStruct((M, LANES), F32),
                   jax.ShapeDtypeStruct((LANES, M), F32)],
        compiler_params=_cparams(("arbitrary",)),
        name="dn_gate",
    )(psmall, alog_row, dtb_row)


def _dn_scan_kernel(q_ref, k_ref, v_ref, kt_ref, dec_ref, dect_ref, z_ref, gain_ref, o_ref, st_ref):
    C = DN_CHUNK

    @pl.when(pl.program_id(1) == 0)
    def _():
        st_ref[...] = jnp.zeros(st_ref.shape, F32)

    dec = dec_ref[...]
    dect = dect_ref[...]
    r = lax.broadcasted_iota(jnp.int32, (C, C), 0)
    c = lax.broadcasted_iota(jnp.int32, (C, C), 1)
    tril = r >= c
    strict = r > c
    gain = gain_ref[...]
    eye = jnp.where(r == c, 1.0, 0.0).astype(F32)
    lvl_masks = []
    for lg in range(int(math.log2(C))):
        lvl_masks.append(((r >> (lg + 1)) == (c >> (lg + 1))) & ((r >> lg) != (c >> lg)))

    for h0 in range(0, DN_HEADS, DN_HEAD_GROUP):
        heads = range(h0, h0 + DN_HEAD_GROUP)
        cols = lambda ref: jnp.stack([ref[:, h * DN_DK:(h + 1) * DN_DK] for h in heads])
        qh = cols(q_ref).astype(F32)
        kh = cols(k_ref).astype(F32)
        vh = cols(v_ref).astype(F32)
        kth = jnp.stack([kt_ref[h * DN_DK:(h + 1) * DN_DK, :] for h in heads])
        dcol = jnp.stack([jnp.broadcast_to(dec[:, h:h + 1], (C, C)) for h in heads])
        bcol = jnp.stack([jnp.broadcast_to(dec[:, DN_HEADS + h:DN_HEADS + h + 1], (C, C)) for h in heads])
        drow = jnp.stack([jnp.broadcast_to(dect[h:h + 1, :], (C, C)) for h in heads])
        dlast = jnp.stack([jnp.broadcast_to(dect[h:h + 1, C - 1:C], (C, C)) for h in heads])
        lmat = jnp.exp(jnp.where(tril[None], dcol - drow, NEG))
        e_d = jnp.exp(dcol)
        kb = kh * bcol
        vb = vh * bcol
        a = jnp.where(strict[None], _bmm(kb.astype(BF16), kth) * lmat, 0.0)
        attn = jnp.where(tril[None], _bmm(qh.astype(BF16), kth) * lmat, 0.0)
        tinv = eye[None] - jnp.where(lvl_masks[0][None], a, 0.0)
        for lm in lvl_masks[1:]:
            t16 = tinv.astype(BF16)
            lo = jnp.where(lm[None], a, 0.0).astype(BF16)
            tinv = tinv - _bmm(t16, _bmm(lo, t16).astype(BF16))
        t16 = tinv.astype(BF16)
        u = _bmm(t16, vb.astype(BF16))
        w = _bmm(t16, (kb * e_d).astype(BF16))
        st = st_ref[h0:h0 + DN_HEAD_GROUP]
        st16 = st.astype(BF16)
        v_new = u - _bmm(w.astype(BF16), st16)
        v_new16 = v_new.astype(BF16)
        o = _bmm((qh * e_d).astype(BF16), st16) + _bmm(attn.astype(BF16), v_new16)
        kdt = (kth.astype(F32) * jnp.exp(dlast - drow)).astype(BF16)
        st_ref[h0:h0 + DN_HEAD_GROUP] = st * jnp.exp(dlast) + _bmm(kdt, v_new16)
        ms = jnp.mean(o * o, axis=-1, keepdims=True)
        y = o * lax.rsqrt(ms + EPS) * gain[None]
        for i, h in enumerate(heads):
            sl = slice(h * DN_DV, (h + 1) * DN_DV)
            o_ref[:, sl] = (y[i] * _silu(z_ref[:, sl].astype(F32))).astype(o_ref.dtype)


def _dn_scan(dq, dk, dv, dkt, dec, dect, pbig, z_col_block, gain_row, B, S):
    M = B * S
    C = DN_CHUNK
    nc = S // C
    tok = lambda b, n: (b * nc + n, 0)
    return pl.pallas_call(
        _dn_scan_kernel,
        grid=(B, nc),
        in_specs=[pl.BlockSpec((C, DN_WIDTH), tok),
                  pl.BlockSpec((C, DN_WIDTH), tok),
                  pl.BlockSpec((C, DN_WIDTH), tok),
                  pl.BlockSpec((DN_WIDTH, C), lambda b, n: (0, b * nc + n)),
                  pl.BlockSpec((C, LANES), tok),
                  pl.BlockSpec((LANES, C), lambda b, n: (0, b * nc + n)),
                  pl.BlockSpec((C, DN_WIDTH), lambda b, n: (b * nc + n, z_col_block)),
                  pl.BlockSpec((1, DN_DV), lambda b, n: (0, 0))],
        out_specs=pl.BlockSpec((C, DN_WIDTH), tok),
        out_shape=jax.ShapeDtypeStruct((M, DN_WIDTH), BF16),
        scratch_shapes=[pltpu.VMEM((DN_HEADS, DN_DK, DN_DV), F32)],
        compiler_params=_cparams(("arbitrary", "arbitrary")),
        name="dn_scan",
    )(dq, dk, dv, dkt, dec, dect, pbig, gain_row)


def _out1_kernel(oa_ref, ob_ref, wa_ref, wb_ref, ga_ref, gb_ref, o_ref, wa16_ref, wb16_ref):
    @pl.when(pl.program_id(1) == 0)
    def _():
        wa16_ref[...] = wa_ref[...].astype(BF16)
        wb16_ref[...] = wb_ref[...].astype(BF16)

    ya = _dot(oa_ref[...], wa16_ref[...])
    yb = _dot(ob_ref[...], wb16_ref[...])
    mix = _sigmoid(ga_ref[...].astype(F32)) * ya + _sigmoid(gb_ref[...].astype(F32)) * yb
    o_ref[...] = mix.astype(o_ref.dtype)


OUT1_TM = 512


def _out1(o_a, o_b, wa, wb, pbig, D):
    M, K = o_a.shape
    tm = min(OUT1_TM, M)
    tn = min(MM_TN, D)
    nbd = D // tn
    return pl.pallas_call(
        _out1_kernel,
        grid=(D // tn, M // tm),
        in_specs=[pl.BlockSpec((tm, K), lambda j, i: (i, 0)),
                  pl.BlockSpec((tm, K), lambda j, i: (i, 0)),
                  pl.BlockSpec((K, tn), lambda j, i: (0, j), pipeline_mode=pl.Buffered(1)),
                  pl.BlockSpec((K, tn), lambda j, i: (0, j), pipeline_mode=pl.Buffered(1)),
                  pl.BlockSpec((tm, tn), lambda j, i: (i, j)),
                  pl.BlockSpec((tm, tn), lambda j, i: (i, nbd + j))],
        out_specs=pl.BlockSpec((tm, tn), lambda j, i: (i, j)),
        out_shape=jax.ShapeDtypeStruct((M, D), BF16),
        scratch_shapes=[pltpu.VMEM((K, tn), BF16), pltpu.VMEM((K, tn), BF16)],
        compiler_params=_cparams(("arbitrary", "arbitrary")),
        name="out1",
    )(o_a, o_b, wa, wb, pbig, pbig)


def _final_kernel(mix_ref, x_ref, gate_ref, fg_ref, o_ref):
    xn = x_ref[...] + gate_ref[0] * mix_ref[...].astype(F32)
    ms = jnp.mean(xn * xn, axis=-1, keepdims=True)
    o_ref[...] = xn * lax.rsqrt(ms + EPS) * fg_ref[...]


def _final(mixed, x2, mod3, final_gain, S):
    M, D = x2.shape
    tm = 256
    nb = S // tm
    return pl.pallas_call(
        _final_kernel,
        grid=(M // tm,),
        in_specs=[pl.BlockSpec((tm, D), lambda i: (i, 0)),
                  pl.BlockSpec((tm, D), lambda i: (i, 0)),
                  pl.BlockSpec((1, 1, D), lambda i: (i // nb, 0, 2)),
                  pl.BlockSpec((1, D), lambda i: (0, 0))],
        out_specs=pl.BlockSpec((tm, D), lambda i: (i, 0)),
        out_shape=jax.ShapeDtypeStruct((M, D), F32),
        compiler_params=_cparams(("arbitrary",)),
        name="final",
    )(mixed, x2, mod3, final_gain)


def _pad_cols(w, width):
    return jnp.pad(w, ((0, 0), (0, width - w.shape[1])))


def _proj_row_offsets(D):
    widths = (NSA_WIDTH, 6 * KV_WIDTH, 3 * NSA_HEADS, NSA_WIDTH, 3 * DN_WIDTH, DN_HEADS, DN_HEADS, DN_WIDTH, 2 * D)
    offs = [int(o) for o in np.concatenate([[0], np.cumsum(widths)])]
    tiles = lambda seg: [offs[seg] + t * MM_TN for t in range(widths[seg] // MM_TN)]
    big = tiles(8) + tiles(4) + tiles(0) + tiles(3) + tiles(7)
    kv0 = offs[1]
    tail = [kv0 + 2 * KV_WIDTH, kv0 + 4 * KV_WIDTH, kv0]
    small = [offs[2], offs[5]]
    return big, tail, small


TAIL_TN = 512
TAIL_TM = 1024
TAIL_CMP_BLOCK = 2 * TAIL_TN // LANES
SMALL_GATE_BLOCK = 0
SMALL_AB_BLOCK = 1


def kernel(x, c, positions, w_ada, b_ada, norm_gain, w_in, cmp_pos_k, cmp_pos_v, w_cmp_k1, w_cmp_k2,
           w_cmp_v1, w_cmp_v2, conv_w, dt_bias, a_log, dn_norm_gain, w_proj_a, w_proj_b, w_out, final_gain):
    B, S, D = x.shape
    M = B * S
    depth = w_in.shape[0]
    assert S % SEL_TILE == 0 and S % DN_CHUNK == 0 and S >= WINDOW + Q_BLOCK and D % 512 == 0 and B <= 8
    assert S // SLC_LEN <= LANES and D % 1024 == 0 and DN_DK == DN_CHUNK
    assert depth == 1, "the final RMSNorm is fused into the last layer's output kernel"

    off_dq = 2 * D
    off_q = off_dq + 3 * DN_WIDTH
    off_z = off_q + NSA_WIDTH
    off_dz = off_z + NSA_WIDTH

    x2 = x.reshape(M, D)
    pos_col = positions.reshape(M, 1)
    cmp_end = np.arange(S // CMP_STRIDE - 1) * CMP_STRIDE + CMP_LEN - 1
    pos_cmp = jnp.pad(positions[:, cmp_end], ((0, 0), (0, 1)))[:, :, None]
    c8 = jnp.pad(c, ((0, 8 - B), (0, 0)))

    for l in range(depth):
        mod = _ada(c8, w_ada[l], b_ada[l][None, :])
        mod3 = mod[:B].reshape(B, 1, 3 * D)
        h = _norm(x2, norm_gain[l][None, :], mod3, S)

        w_nk = jnp.transpose(w_in[l])
        big_rows, tail_rows, small_rows = _proj_row_offsets(D)
        pbig = _proj(h, w_nk, big_rows, MM_TM, MM_TN, BF16, "proj_big")
        ptail = _proj(h, w_nk, tail_rows, TAIL_TM, TAIL_TN, F32, "proj_tail")
        psmall = _proj(h, w_nk, small_rows, TAIL_TM, LANES, F32, "proj_small")

        q_t, keys, v_t = _nsa_prep(pbig, ptail, pos_col, off_q // NSA_WIDTH, S)
        kcmp = _compress(ptail, TAIL_CMP_BLOCK, cmp_pos_k[l], w_cmp_k1[l], w_cmp_k2[l], pos_cmp, B, S, True)
        vcmp = _compress(ptail, TAIL_CMP_BLOCK + NSA_GROUPS, cmp_pos_v[l], w_cmp_v1[l], w_cmp_v2[l].T,
                         pos_cmp, B, S, False)
        o_a = _nsa_attn(q_t, kcmp, vcmp, keys, v_t, psmall, pbig, off_z // NSA_WIDTH, B, S)

        cw = conv_w[l]
        dq = _dn_prep(pbig, cw, off_dq, "q", B, S)[0]
        dk, dkt = _dn_prep(pbig, cw, off_dq + DN_WIDTH, "k", B, S)
        dv = _dn_prep(pbig, cw, off_dq + 2 * DN_WIDTH, "v", B, S)[0]
        alog_row = _pad_cols(a_log[l][None, :].astype(F32), LANES)
        dtb_row = _pad_cols(dt_bias[l][None, :].astype(F32), LANES)
        dec, dect = _dn_gate(psmall, alog_row, dtb_row)
        o_b = _dn_scan(dq, dk, dv, dkt, dec, dect, pbig, off_dz // DN_WIDTH,
                       dn_norm_gain[l][None, :], B, S)

        mixed_in = _out1(o_a, o_b, w_proj_a[l], w_proj_b[l], pbig, D)
        mixed = _matmul(mixed_in, w_out[l], BF16, "out2")
        x2 = _final(mixed, x2, mod3, final_gain[None, :], S)
    return x2.reshape(B, S, D)
```

```python
import functools
import math

import numpy as np
import jax
import jax.numpy as jnp
from jax import lax
from jax.experimental import pallas as pl
from jax.experimental.pallas import tpu as pltpu

F32 = jnp.float32
BF16 = jnp.bfloat16

NSA_HEADS = 16
NSA_GROUPS = 2
NSA_HPG = NSA_HEADS // NSA_GROUPS
HEAD_DIM = 128
ROT_DIM = HEAD_DIM // 4
ROT_HALF = ROT_DIM // 2
ROPE_THETA = 500000.0
CMP_LEN = 32
CMP_STRIDE = 16
CMP_HIDDEN = 256
SLC_LEN = 64
SLC_TOPK = 16
WINDOW = 512
Q_BLOCK = 128
NSA_WIDTH = NSA_HEADS * HEAD_DIM
KV_WIDTH = NSA_GROUPS * HEAD_DIM
DN_HEADS = 16
DN_DK = 128
DN_DV = 128
DN_WIDTH = DN_HEADS * DN_DV
CONV_WIDTH = 4
EPS = 1e-6

LANES = 128
VMEM_LIMIT_BYTES = 56 * 1024 * 1024

DN_CHUNK = 128
DN_HEAD_GROUP = 16
SEL_TILE = 512
VT_ROWS = HEAD_DIM + 16
NEG = -1e30


def _cparams(sem):
    return pltpu.CompilerParams(dimension_semantics=sem, vmem_limit_bytes=VMEM_LIMIT_BYTES)


def _sigmoid(x):
    return 1.0 / (1.0 + jnp.exp(-x))


def _silu(x):
    return x * _sigmoid(x)


def _dot(a, b):
    return jnp.dot(a, b, preferred_element_type=F32)


def _dot_nt(a, b):
    return lax.dot_general(a, b, (((1,), (1,)), ((), ())), preferred_element_type=F32)


def _bmm(a, b):
    return lax.dot_general(a, b, (((2,), (1,)), ((0,), (0,))), preferred_element_type=F32)


def _dot_f32(a, b):
    return jnp.dot(a, b, preferred_element_type=F32, precision=lax.Precision.HIGHEST)


def _ada_kernel(c_ref, w_ref, b_ref, o_ref):
    c = c_ref[...]
    c_hi = c.astype(BF16)
    c_lo = (c - c_hi.astype(F32)).astype(BF16)
    w = w_ref[...]
    w_hi = w.astype(BF16)
    w_lo = (w - w_hi.astype(F32)).astype(BF16)
    acc = _dot(c_hi, w_hi) + _dot(c_lo, w_hi) + _dot(c_hi, w_lo)
    o_ref[...] = acc + b_ref[...]


def _ada(c8, w_ada, b_ada):
    D, N = w_ada.shape
    tn = min(512, N)
    return pl.pallas_call(
        _ada_kernel,
        grid=(N // tn,),
        in_specs=[pl.BlockSpec((8, D), lambda j: (0, 0)),
                  pl.BlockSpec((D, tn), lambda j: (0, j)),
                  pl.BlockSpec((1, tn), lambda j: (0, j))],
        out_specs=pl.BlockSpec((8, tn), lambda j: (0, j)),
        out_shape=jax.ShapeDtypeStruct((8, N), F32),
        compiler_params=_cparams(("arbitrary",)),
        name="ada",
    )(c8, w_ada, b_ada)


def _norm_kernel(x_ref, gain_ref, shift_ref, scale_ref, o_ref):
    x = x_ref[...]
    ms = jnp.mean(x * x, axis=-1, keepdims=True)
    y = x * lax.rsqrt(ms + EPS) * gain_ref[...]
    o_ref[...] = (y * (1.0 + scale_ref[0]) + shift_ref[0]).astype(o_ref.dtype)


def _norm(x2, gain, mod3, S):
    M, D = x2.shape
    tm = 256
    nb = S // tm
    return pl.pallas_call(
        _norm_kernel,
        grid=(M // tm,),
        in_specs=[pl.BlockSpec((tm, D), lambda i: (i, 0)),
                  pl.BlockSpec((1, D), lambda i: (0, 0)),
                  pl.BlockSpec((1, 1, D), lambda i: (i // nb, 0, 0)),
                  pl.BlockSpec((1, 1, D), lambda i: (i // nb, 0, 1))],
        out_specs=pl.BlockSpec((tm, D), lambda i: (i, 0)),
        out_shape=jax.ShapeDtypeStruct((M, D), BF16),
        compiler_params=_cparams(("arbitrary",)),
        name="norm",
    )(x2, gain, mod3, mod3)


MM_TM = 1024
MM_TN = 1024
W_ROW_ALIGN = 16
WT_CHUNK = 128


def _mm_kernel(offs_ref, a_ref, w_hbm, o_ref, wf32_ref, w16_ref, sems, *, w_is_nk, tn, pieces):
    j = pl.program_id(0)
    i = pl.program_id(1)
    pw = tn // pieces

    def piece_copy(jj, p):
        off = offs_ref[jj * pieces + p]
        if w_is_nk:
            src = w_hbm.at[pl.ds(pl.multiple_of(off * W_ROW_ALIGN, W_ROW_ALIGN), pw), :]
            dst = wf32_ref.at[p * pw:(p + 1) * pw, :]
        else:
            src = w_hbm.at[:, pl.ds(pl.multiple_of(off * LANES, LANES), pw)]
            dst = wf32_ref.at[:, p * pw:(p + 1) * pw]
        return pltpu.make_async_copy(src, dst, sems.at[p])

    def start_tile(jj):
        for p in range(pieces):
            piece_copy(jj, p).start()

    @pl.when((j == 0) & (i == 0))
    def _():
        start_tile(0)

    @pl.when(i == 0)
    def _():
        for p in range(pieces):
            piece_copy(j, p).wait()
        if w_is_nk:
            for c in range(0, tn, WT_CHUNK):
                w16_ref[:, c:c + WT_CHUNK] = wf32_ref[c:c + WT_CHUNK, :].T.astype(BF16)
        else:
            w16_ref[...] = wf32_ref[...].astype(BF16)

        @pl.when(j + 1 < pl.num_programs(0))
        def _():
            start_tile(j + 1)

    o_ref[...] = _dot(a_ref[...], w16_ref[...]).astype(o_ref.dtype)


def _mm_call(a, w, offs, w_is_nk, tm, tn, out_dtype, name, pieces=1):
    M, K = a.shape
    tm = min(tm, M)
    n_tiles = len(offs) // pieces
    grid_spec = pltpu.PrefetchScalarGridSpec(
        num_scalar_prefetch=1,
        grid=(n_tiles, M // tm),
        in_specs=[pl.BlockSpec((tm, K), lambda j, i, o: (i, 0)),
                  pl.BlockSpec(memory_space=pl.ANY)],
        out_specs=pl.BlockSpec((tm, tn), lambda j, i, o: (i, j)),
        scratch_shapes=[pltpu.VMEM((tn, K) if w_is_nk else (K, tn), F32),
                        pltpu.VMEM((K, tn), BF16),
                        pltpu.SemaphoreType.DMA((pieces,))],
    )
    return pl.pallas_call(
        functools.partial(_mm_kernel, w_is_nk=w_is_nk, tn=tn, pieces=pieces),
        grid_spec=grid_spec,
        out_shape=jax.ShapeDtypeStruct((M, n_tiles * tn), out_dtype),
        compiler_params=_cparams(("arbitrary", "arbitrary")),
        name=name,
    )(jnp.asarray(np.asarray(offs, np.int32)), a, w)


def _matmul(a, w, out_dtype, name):
    N = w.shape[1]
    tn = min(MM_TN, N)
    offs = [t * tn // LANES for t in range(N // tn)]
    return _mm_call(a, w, offs, False, MM_TM, tn, out_dtype, name)


def _proj(h, w_nk, row_offsets, tm, tn, out_dtype, name, pieces=1):
    assert all(int(o) % W_ROW_ALIGN == 0 and int(o) + tn // pieces <= w_nk.shape[0] for o in row_offsets)
    offs = [int(o) // W_ROW_ALIGN for o in row_offsets]
    return _mm_call(h, w_nk, offs, True, tm, tn, out_dtype, name, pieces)


def _rope_consts():
    inv = ROPE_THETA ** (-np.arange(ROT_HALF, dtype=np.float64) / ROT_HALF)
    invf = np.zeros((1, LANES), np.float32)
    invf[0, :ROT_HALF] = inv
    invf[0, ROT_HALF:ROT_DIM] = inv
    sgn = np.zeros((1, LANES), np.float32)
    sgn[0, :ROT_HALF] = -1.0
    sgn[0, ROT_HALF:ROT_DIM] = 1.0
    return jnp.asarray(invf), jnp.asarray(sgn)


def _rope_tables(pos_col, invf, sgn):
    ang = pos_col.astype(F32) * invf
    return jnp.cos(ang), jnp.sin(ang) * sgn


def _rope_apply(x, cos_t, sin_t):
    lane = lax.broadcasted_iota(jnp.int32, x.shape, 1)
    partner = jnp.where(lane < ROT_HALF,
                        pltpu.roll(x, LANES - ROT_HALF, 1),
                        pltpu.roll(x, ROT_HALF, 1))
    return x * cos_t + partner * sin_t


KEYS_WIDTH = 6 * HEAD_DIM


def _nsa_prep_kernel(q_ref, kvs_ref, kvw_ref, pos_ref, invf_ref, sgn_ref, qt_ref, keys_ref, vt_ref, *, seq_len):
    tm = q_ref.shape[0]
    cos_t, sin_t = _rope_tables(pos_ref[...], invf_ref[...], sgn_ref[...])
    qscale = (HEAD_DIM ** -0.5) * math.log2(math.e)
    for h in range(NSA_HEADS):
        sl = slice(h * HEAD_DIM, (h + 1) * HEAD_DIM)
        xq = _rope_apply(q_ref[:, sl].astype(F32), cos_t, sin_t) * qscale
        qt_ref[sl, :] = xq.T.astype(qt_ref.dtype)
    tok = (pl.program_id(0) * tm) % seq_len + lax.broadcasted_iota(jnp.int32, (tm, LANES), 0)
    lane = lax.broadcasted_iota(jnp.int32, (tm, LANES), 1)
    onehot = jnp.where(lane == (tok >> int(math.log2(SLC_LEN))), 1.0, 0.0).astype(keys_ref.dtype)
    blk = lambda i: (kvs_ref if i < 4 else kvw_ref)[:, (i % 4) * HEAD_DIM:(i % 4 + 1) * HEAD_DIM]
    for g in range(NSA_GROUPS):
        ks = _rope_apply(blk(g), cos_t, sin_t)
        keys_ref[:, (2 * g) * HEAD_DIM:(2 * g + 1) * HEAD_DIM] = ks.astype(keys_ref.dtype)
        keys_ref[:, (2 * g + 1) * HEAD_DIM:(2 * g + 2) * HEAD_DIM] = onehot
        kw = _rope_apply(blk(4 + g), cos_t, sin_t)
        keys_ref[:, (4 + g) * HEAD_DIM:(5 + g) * HEAD_DIM] = kw.astype(keys_ref.dtype)
        ones = jnp.ones((VT_ROWS - HEAD_DIM, tm), vt_ref.dtype)
        for c, src in ((g, 2 + g), (2 + g, 6 + g)):
            vt_ref[c * VT_ROWS:c * VT_ROWS + HEAD_DIM, :] = blk(src).T.astype(vt_ref.dtype)
            vt_ref[c * VT_ROWS + HEAD_DIM:(c + 1) * VT_ROWS, :] = ones


def _nsa_prep(pbig, ptail, pos_col, q_col_block, S):
    M = pbig.shape[0]
    tm = 256
    invf, sgn = _rope_consts()
    return pl.pallas_call(
        functools.partial(_nsa_prep_kernel, seq_len=S),
        grid=(M // tm,),
        in_specs=[pl.BlockSpec((tm, NSA_WIDTH), lambda i: (i, q_col_block)),
                  pl.BlockSpec((tm, 4 * HEAD_DIM), lambda i: (i, 1)),
                  pl.BlockSpec((tm, 4 * HEAD_DIM), lambda i: (i, 2)),
                  pl.BlockSpec((tm, 1), lambda i: (i, 0)),
                  pl.BlockSpec((1, LANES), lambda i: (0, 0)),
                  pl.BlockSpec((1, LANES), lambda i: (0, 0))],
        out_specs=[pl.BlockSpec((NSA_WIDTH, tm), lambda i: (0, i)),
                   pl.BlockSpec((tm, KEYS_WIDTH), lambda i: (i, 0)),
                   pl.BlockSpec((4 * VT_ROWS, tm), lambda i: (0, i))],
        out_shape=[jax.ShapeDtypeStruct((NSA_WIDTH, M), BF16),
                   jax.ShapeDtypeStruct((M, KEYS_WIDTH), BF16),
                   jax.ShapeDtypeStruct((4 * VT_ROWS, M), BF16)],
        compiler_params=_cparams(("arbitrary",)),
        name="nsa_prep",
    )(pbig, ptail, ptail, pos_col, invf, sgn)


def _compress_kernel(x_ref, pe_ref, w1_ref, w2_ref, pos_ref, invf_ref, sgn_ref, o_ref, *, rope, n_rows):
    half = CMP_LEN // 2
    top = jnp.zeros((n_rows, CMP_HIDDEN), F32)
    bot = jnp.zeros((n_rows, CMP_HIDDEN), F32)
    for l in range(half):
        xl = x_ref[pl.ds(l, n_rows, stride=CMP_STRIDE), :]
        w_top = w1_ref[l * HEAD_DIM:(l + 1) * HEAD_DIM, :].astype(BF16)
        w_bot = w1_ref[(half + l) * HEAD_DIM:(half + l + 1) * HEAD_DIM, :].astype(BF16)
        top = top + _dot((xl + pe_ref[l:l + 1, :]).astype(BF16), w_top)
        bot = bot + _dot((xl + pe_ref[half + l:half + l + 1, :]).astype(BF16), w_bot)
    hid = top + pltpu.roll(bot, n_rows - 1, 0)
    act = _silu(hid).astype(BF16)
    if rope:
        out = _dot(act, w2_ref[...].astype(BF16))
        cos_t, sin_t = _rope_tables(pos_ref[0], invf_ref[...], sgn_ref[...])
        o_ref[0, 0] = _rope_apply(out, cos_t, sin_t).astype(o_ref.dtype)
    else:
        o_ref[0, 0, 0:HEAD_DIM, :] = _dot_nt(w2_ref[...].astype(BF16), act).astype(o_ref.dtype)
        o_ref[0, 0, HEAD_DIM:VT_ROWS, :] = jnp.ones((VT_ROWS - HEAD_DIM, n_rows), o_ref.dtype)


def _compress(ptail, col_block0, pe, w1, w2, pos_cmp, B, S, rope):
    n_rows = S // CMP_STRIDE
    invf, sgn = _rope_consts()
    kern = functools.partial(_compress_kernel, rope=rope, n_rows=n_rows)
    out_dims = (n_rows, HEAD_DIM) if rope else (VT_ROWS, n_rows)
    return pl.pallas_call(
        kern,
        grid=(B, NSA_GROUPS),
        in_specs=[pl.BlockSpec((S, HEAD_DIM), lambda b, g: (b, col_block0 + g)),
                  pl.BlockSpec((CMP_LEN, HEAD_DIM), lambda b, g: (0, 0)),
                  pl.BlockSpec((CMP_LEN * HEAD_DIM, CMP_HIDDEN), lambda b, g: (0, 0)),
                  pl.BlockSpec(w2.shape, lambda b, g: (0, 0)),
                  pl.BlockSpec((1, n_rows, 1), lambda b, g: (b, 0, 0)),
                  pl.BlockSpec((1, LANES), lambda b, g: (0, 0)),
                  pl.BlockSpec((1, LANES), lambda b, g: (0, 0))],
        out_specs=pl.BlockSpec((1, 1) + out_dims, lambda b, g: (b, g, 0, 0)),
        out_shape=jax.ShapeDtypeStruct((B, NSA_GROUPS) + out_dims, BF16),
        compiler_params=_cparams(("arbitrary", "arbitrary")),
        name="compress_k" if rope else "compress_v",
    )(ptail, pe, w1, w2, pos_cmp, invf, sgn)


def _tile_lanes(x, n):
    return jnp.concatenate([x] * n, axis=1)


def _nsa_attn_kernel(qt_ref, kc_ref, vct_ref, ks_ref, vst_ref, kw_ref, vwt_ref, g_ref, z_ref,
                     ovt_ref, o_ref, m_s, acc_s, qa_s, sa_s, sb_s, *, seq_len):
    T = Q_BLOCK
    P = NSA_HPG
    R = P * T
    n_cmp_pad = seq_len // CMP_STRIDE
    n_cmp = n_cmp_pad - 1
    nb = seq_len // SLC_LEN
    qi = pl.program_id(2)
    qs = qi * T
    d0 = pl.multiple_of(qs, T)
    wlen = WINDOW + T
    w0 = pl.multiple_of(jnp.maximum(qs - WINDOW, 0), T)

    qt = jnp.concatenate([qt_ref[p * HEAD_DIM:(p + 1) * HEAD_DIM, :] for p in range(P)], axis=1)

    s_c = _dot(kc_ref[0, 0], qt)
    s_w = _dot(kw_ref[pl.ds(w0, wlen), :], qt)
    s_d = _dot(ks_ref[pl.ds(d0, T), 0:HEAD_DIM], qt)

    tok = lambda n: qs + lax.broadcasted_iota(jnp.int32, (n, T), 1)
    row = lambda n: lax.broadcasted_iota(jnp.int32, (n, T), 0)
    ok_c = (row(n_cmp_pad) * CMP_STRIDE + (CMP_LEN - 1) <= tok(n_cmp_pad)) & (row(n_cmp_pad) < n_cmp)
    key_w = w0 + row(wlen)
    ok_w = (key_w <= tok(wlen)) & (tok(wlen) - key_w < WINDOW)
    ok_d = row(T) <= lax.broadcasted_iota(jnp.int32, (T, T), 1)
    addmask = lambda ok: _tile_lanes(jnp.where(ok, 0.0, NEG).astype(F32), P)
    s_c = s_c + addmask(ok_c)
    s_w = s_w + addmask(ok_w)
    s_d = s_d + addmask(ok_d)

    m_c = jnp.max(s_c, axis=0, keepdims=True)
    m_w = jnp.max(s_w, axis=0, keepdims=True)
    m_d = jnp.max(s_d, axis=0, keepdims=True)
    m_c = jnp.where(m_c > 0.5 * NEG, m_c, 0.0)
    p_c = jnp.exp2(s_c - m_c)
    p_w = jnp.exp2(s_w - m_w)
    p_d = jnp.exp2(s_d - m_d)
    o_c = _dot(vct_ref[0, 0], p_c.astype(BF16))
    o_w = _dot(vwt_ref[:, pl.ds(w0, wlen)], p_w.astype(BF16))
    inv_c = 1.0 / jnp.maximum(jnp.sum(p_c, axis=0, keepdims=True), 1e-30)
    inv_w = 1.0 / o_w[HEAD_DIM:HEAD_DIM + 1, :]
    m_s[...] = m_d
    acc_s[...] = _dot(vst_ref[:, pl.ds(d0, T)], p_d.astype(BF16))

    pn = p_c * inv_c
    p_sum = pn[:, 0:T]
    for p in range(1, P):
        p_sum = p_sum + pn[:, p * T:(p + 1) * T]
    ps_hi = p_sum.astype(BF16)
    ps_lo = (p_sum - ps_hi.astype(F32)).astype(BF16)
    imp = (_dot(ovt_ref[...], ps_hi) + _dot(ovt_ref[...], ps_lo))[0:nb]
    jrow = row(nb)
    t_lane = tok(nb)
    tb = t_lane >> int(math.log2(SLC_LEN))
    visible = jrow * SLC_LEN <= t_lane
    forced = (jrow == 0) | (jrow == tb) | (jrow == tb - 1)
    score = jnp.where(forced, 1e9, jnp.where(visible, imp, -jnp.inf))
    cnt = jnp.zeros((nb, T), F32)
    for k in range(nb):
        rk = score[k:k + 1, :]
        cnt = cnt + jnp.where(jrow > k, jnp.where(rk >= score, 1.0, 0.0), jnp.where(rk > score, 1.0, 0.0))
    keep = (cnt < float(min(SLC_TOPK, nb))) & visible & (jrow < 2 * qi)
    bias = jnp.where(keep, 0.0, NEG).astype(F32)
    bias = jnp.concatenate([bias, jnp.full((LANES - nb, T), NEG, F32)], axis=0) if nb < LANES else bias
    qa_s[0:HEAD_DIM, :] = qt
    qa_s[HEAD_DIM:2 * HEAD_DIM, :] = _tile_lanes(bias.astype(BF16), P)

    n_t = (qs + SEL_TILE - 1) // SEL_TILE

    def scores_into(kt, dst):
        k0 = pl.multiple_of(kt * SEL_TILE, SEL_TILE)
        dst[...] = _dot(ks_ref[pl.ds(k0, SEL_TILE), :], qa_s[...])

    def consume(kt, src):
        k0 = pl.multiple_of(kt * SEL_TILE, SEL_TILE)
        s = src[...]
        m_prev = m_s[...]
        m_new = jnp.maximum(m_prev, jnp.max(s, axis=0, keepdims=True))
        alpha = jnp.exp2(m_prev - m_new)
        p = jnp.exp2(s - m_new)
        acc_s[...] = alpha * acc_s[...] + _dot(vst_ref[:, pl.ds(k0, SEL_TILE)], p.astype(BF16))
        m_s[...] = m_new

    @pl.when(n_t > 0)
    def _():
        scores_into(0, sa_s)

    def pair_body(i, carry):
        kt = 2 * i
        scores_into(kt + 1, sb_s)
        consume(kt, sa_s)
        scores_into(jnp.minimum(kt + 2, n_t - 1), sa_s)
        consume(kt + 1, sb_s)
        return carry

    lax.fori_loop(0, n_t // 2, pair_body, 0)

    @pl.when(n_t % 2 == 1)
    def _():
        consume(n_t - 1, sa_s)

    gates_t = _sigmoid(g_ref[...]).T
    first_group = pl.program_id(1) == 0

    def grow(r):
        rows = []
        for p in range(P):
            h0 = r * NSA_HEADS + p
            rows.append(jnp.where(first_group, gates_t[h0:h0 + 1, :], gates_t[h0 + P:h0 + P + 1, :]))
        return jnp.concatenate(rows, axis=1)

    o_s = acc_s[...]
    hd = slice(0, HEAD_DIM)
    o_t = (o_c[hd] * (grow(0) * inv_c) + o_s[hd] * (grow(1) / o_s[HEAD_DIM:HEAD_DIM + 1, :])
           + o_w[hd] * (grow(2) * inv_w))
    for p in range(P):
        sl = slice(p * HEAD_DIM, (p + 1) * HEAD_DIM)
        o_p = o_t[:, p * T:(p + 1) * T].T
        o_ref[:, sl] = (o_p * _silu(z_ref[:, sl].astype(F32))).astype(o_ref.dtype)


def _nsa_consts(S):
    n_cmp_pad = S // CMP_STRIDE
    nb = S // SLC_LEN
    cmp_start = np.arange(n_cmp_pad) * CMP_STRIDE
    slc_start = np.arange(LANES) * SLC_LEN
    ov = ((cmp_start[:, None] <= slc_start[None, :] + SLC_LEN - 1)
          & (cmp_start[:, None] + CMP_LEN - 1 >= slc_start[None, :])
          & (np.arange(LANES)[None, :] < nb)
          & (np.arange(n_cmp_pad)[:, None] < n_cmp_pad - 1)).astype(np.float32)
    return jnp.asarray(ov.T, dtype=BF16)


def _nsa_attn(q_t, kcmp, vcmp_t, keys, v_t, psmall, pbig, z_col_block, B, S):
    M = B * S
    nq = S // Q_BLOCK
    gw = NSA_HPG * HEAD_DIM
    ovt = _nsa_consts(S)
    n_cmp_pad = S // CMP_STRIDE
    R = NSA_HPG * Q_BLOCK
    kern = functools.partial(_nsa_attn_kernel, seq_len=S)
    return pl.pallas_call(
        kern,
        grid=(B, NSA_GROUPS, nq),
        in_specs=[pl.BlockSpec((gw, Q_BLOCK), lambda b, g, i: (g, b * nq + i)),
                  pl.BlockSpec((1, 1, n_cmp_pad, HEAD_DIM), lambda b, g, i: (b, g, 0, 0)),
                  pl.BlockSpec((1, 1, VT_ROWS, n_cmp_pad), lambda b, g, i: (b, g, 0, 0)),
                  pl.BlockSpec((S, 2 * HEAD_DIM), lambda b, g, i: (b, g)),
                  pl.BlockSpec((VT_ROWS, S), lambda b, g, i: (g, b)),
                  pl.BlockSpec((S, HEAD_DIM), lambda b, g, i: (b, 4 + g)),
                  pl.BlockSpec((VT_ROWS, S), lambda b, g, i: (2 + g, b)),
                  pl.BlockSpec((Q_BLOCK, LANES), lambda b, g, i: (b * nq + i, SMALL_GATE_BLOCK)),
                  pl.BlockSpec((Q_BLOCK, gw), lambda b, g, i: (b * nq + i, z_col_block * NSA_GROUPS + g)),
                  pl.BlockSpec((LANES, n_cmp_pad), lambda b, g, i: (0, 0))],
        out_specs=pl.BlockSpec((Q_BLOCK, gw), lambda b, g, i: (b * nq + i, g)),
        out_shape=jax.ShapeDtypeStruct((M, NSA_WIDTH), BF16),
        scratch_shapes=[pltpu.VMEM((1, R), F32),
                        pltpu.VMEM((VT_ROWS, R), F32),
                        pltpu.VMEM((2 * HEAD_DIM, R), BF16),
                        pltpu.VMEM((SEL_TILE, R), F32),
                        pltpu.VMEM((SEL_TILE, R), F32)],
        compiler_params=_cparams(("arbitrary", "arbitrary", "arbitrary")),
        name="nsa_attn",
    )(q_t, kcmp, vcmp_t, keys, v_t, keys, v_t, psmall, pbig, ovt)


def _dn_prep_kernel(x_ref, w_ref, o_ref, *rest, mode, seq_len):
    xpad_ref = rest[-1]
    w = w_ref[...]
    xpad_ref[0:CONV_PAD, :] = jnp.zeros((CONV_PAD, xpad_ref.shape[1]), F32)
    xpad_ref[CONV_PAD:, :] = x_ref[...].astype(F32)
    y = xpad_ref[CONV_PAD:, :] * w[CONV_WIDTH - 1:CONV_WIDTH, :]
    for k in range(1, CONV_WIDTH):
        y = y + xpad_ref[pl.ds(CONV_PAD - k, seq_len), :] * w[CONV_WIDTH - 1 - k:CONV_WIDTH - k, :]
    y = _silu(y)
    if mode in ("q", "k"):
        outs = []
        for h in range(y.shape[1] // DN_DK):
            yh = y[:, h * DN_DK:(h + 1) * DN_DK]
            ss = jnp.sum(yh * yh, axis=-1, keepdims=True)
            yh = yh * lax.rsqrt(ss + EPS)
            if mode == "q":
                yh = yh * (DN_DK ** -0.5)
            outs.append(yh)
        y = jnp.concatenate(outs, axis=1)
    o_ref[...] = y.astype(o_ref.dtype)
    if mode == "k":
        kt_ref = rest[0]
        kt_ref[...] = y.T.astype(kt_ref.dtype)


DN_PREP_COLS = 128
CONV_PAD = 8


def _dn_prep(pbig, conv_w, col0, mode, B, S):
    M = B * S
    tw = DN_PREP_COLS
    col0_blocks = col0 // tw
    nj = DN_WIDTH // tw
    wcol0 = {"q": 0, "k": nj, "v": 2 * nj}[mode]
    kern = functools.partial(_dn_prep_kernel, mode=mode, seq_len=S)
    out_specs = [pl.BlockSpec((S, tw), lambda b, j: (b, j))]
    out_shape = [jax.ShapeDtypeStruct((M, DN_WIDTH), BF16)]
    if mode == "k":
        out_specs.append(pl.BlockSpec((tw, S), lambda b, j: (j, b)))
        out_shape.append(jax.ShapeDtypeStruct((DN_WIDTH, M), BF16))
    return pl.pallas_call(
        kern,
        grid=(B, nj),
        in_specs=[pl.BlockSpec((S, tw), lambda b, j: (b, col0_blocks + j)),
                  pl.BlockSpec((CONV_WIDTH, tw), lambda b, j: (0, wcol0 + j))],
        out_specs=out_specs,
        out_shape=out_shape,
        scratch_shapes=[pltpu.VMEM((CONV_PAD + S, tw), F32)],
        compiler_params=_cparams(("arbitrary", "arbitrary")),
        name="dn_prep_" + mode,
    )(pbig, conv_w)


def _dn_gate_kernel(ab_ref, alog_ref, dtb_ref, o_ref, ot_ref, *, tm):
    ab = ab_ref[...]
    x = ab + dtb_ref[...]
    softplus = jnp.maximum(x, 0.0) + jnp.log(1.0 + jnp.exp(-jnp.abs(x)))
    g = -jnp.exp(alog_ref[...]) * softplus
    beta = _sigmoid(ab)
    lane = lax.broadcasted_iota(jnp.int32, (DN_CHUNK, LANES), 1)
    r = lax.broadcasted_iota(jnp.int32, (DN_CHUNK, DN_CHUNK), 0)
    c = lax.broadcasted_iota(jnp.int32, (DN_CHUNK, DN_CHUNK), 1)
    tril = jnp.where(r >= c, 1.0, 0.0).astype(F32)
    for ci in range(tm // DN_CHUNK):
        rows = slice(ci * DN_CHUNK, (ci + 1) * DN_CHUNK)
        dec = _dot_f32(tril, g[rows])
        out = jnp.where(lane < DN_HEADS, dec, beta[rows])
        o_ref[rows, :] = out
        ot_ref[:, rows] = out.T


def _dn_gate(psmall, alog_row, dtb_row):
    M = psmall.shape[0]
    tm = 512
    kern = functools.partial(_dn_gate_kernel, tm=tm)
    return pl.pallas_call(
        kern,
        grid=(M // tm,),
        in_specs=[pl.BlockSpec((tm, LANES), lambda i: (i, SMALL_AB_BLOCK)),
                  pl.BlockSpec((1, LANES), lambda i: (0, 0)),
                  pl.BlockSpec((1, LANES), lambda i: (0, 0))],
        out_specs=[pl.BlockSpec((tm, LANES), lambda i: (i, 0)),
                   pl.BlockSpec((LANES, tm), lambda i: (0, i))],
        out_shape=[jax.ShapeDtypeStruct((M, LANES), F32),
                   jax.ShapeDtypeStruct((LANES, M), F32)],
        compiler_params=_cparams(("arbitrary",)),
        name="dn_gate",
    )(psmall, alog_row, dtb_row)


def _dn_scan_kernel(q_ref, k_ref, v_ref, kt_ref, dec_ref, dect_ref, z_ref, gain_ref, o_ref, st_ref):
    C = DN_CHUNK

    @pl.when(pl.program_id(1) == 0)
    def _():
        st_ref[...] = jnp.zeros(st_ref.shape, F32)

    dec = dec_ref[...]
    dect = dect_ref[...]
    r = lax.broadcasted_iota(jnp.int32, (C, C), 0)
    c = lax.broadcasted_iota(jnp.int32, (C, C), 1)
    tril = r >= c
    strict = r > c
    gain = gain_ref[...]
    eye = jnp.where(r == c, 1.0, 0.0).astype(F32)
    lvl_masks = []
    for lg in range(int(math.log2(C))):
        lvl_masks.append(((r >> (lg + 1)) == (c >> (lg + 1))) & ((r >> lg) != (c >> lg)))

    for h0 in range(0, DN_HEADS, DN_HEAD_GROUP):
        heads = range(h0, h0 + DN_HEAD_GROUP)
        cols = lambda ref: jnp.stack([ref[:, h * DN_DK:(h + 1) * DN_DK] for h in heads])
        qh = cols(q_ref).astype(F32)
        kh = cols(k_ref).astype(F32)
        vh = cols(v_ref).astype(F32)
        kth = jnp.stack([kt_ref[h * DN_DK:(h + 1) * DN_DK, :] for h in heads])
        dcol = jnp.stack([jnp.broadcast_to(dec[:, h:h + 1], (C, C)) for h in heads])
        bcol = jnp.stack([jnp.broadcast_to(dec[:, DN_HEADS + h:DN_HEADS + h + 1], (C, C)) for h in heads])
        drow = jnp.stack([jnp.broadcast_to(dect[h:h + 1, :], (C, C)) for h in heads])
        dlast = jnp.stack([jnp.broadcast_to(dect[h:h + 1, C - 1:C], (C, C)) for h in heads])
        lmat = jnp.exp(jnp.where(tril[None], dcol - drow, NEG))
        e_d = jnp.exp(dcol)
        kb = kh * bcol
        vb = vh * bcol
        a = jnp.where(strict[None], _bmm(kb.astype(BF16), kth) * lmat, 0.0)
        attn = jnp.where(tril[None], _bmm(qh.astype(BF16), kth) * lmat, 0.0)
        tinv = eye[None] - jnp.where(lvl_masks[0][None], a, 0.0)
        for lm in lvl_masks[1:]:
            t16 = tinv.astype(BF16)
            lo = jnp.where(lm[None], a, 0.0).astype(BF16)
            tinv = tinv - _bmm(t16, _bmm(lo, t16).astype(BF16))
        t16 = tinv.astype(BF16)
        u = _bmm(t16, vb.astype(BF16))
        w = _bmm(t16, (kb * e_d).astype(BF16))
        st = st_ref[h0:h0 + DN_HEAD_GROUP]
        st16 = st.astype(BF16)
        v_new = u - _bmm(w.astype(BF16), st16)
        v_new16 = v_new.astype(BF16)
        o = _bmm((qh * e_d).astype(BF16), st16) + _bmm(attn.astype(BF16), v_new16)
        kdt = (kth.astype(F32) * jnp.exp(dlast - drow)).astype(BF16)
        st_ref[h0:h0 + DN_HEAD_GROUP] = st * jnp.exp(dlast) + _bmm(kdt, v_new16)
        ms = jnp.mean(o * o, axis=-1, keepdims=True)
        y = o * lax.rsqrt(ms + EPS) * gain[None]
        for i, h in enumerate(heads):
            sl = slice(h * DN_DV, (h + 1) * DN_DV)
            o_ref[:, sl] = (y[i] * _silu(z_ref[:, sl].astype(F32))).astype(o_ref.dtype)


def _dn_scan(dq, dk, dv, dkt, dec, dect, pbig, z_col_block, gain_row, B, S):
    M = B * S
    C = DN_CHUNK
    nc = S // C
    tok = lambda b, n: (b * nc + n, 0)
    return pl.pallas_call(
        _dn_scan_kernel,
        grid=(B, nc),
        in_specs=[pl.BlockSpec((C, DN_WIDTH), tok),
                  pl.BlockSpec((C, DN_WIDTH), tok),
                  pl.BlockSpec((C, DN_WIDTH), tok),
                  pl.BlockSpec((DN_WIDTH, C), lambda b, n: (0, b * nc + n)),
                  pl.BlockSpec((C, LANES), tok),
                  pl.BlockSpec((LANES, C), lambda b, n: (0, b * nc + n)),
                  pl.BlockSpec((C, DN_WIDTH), lambda b, n: (b * nc + n, z_col_block)),
                  pl.BlockSpec((1, DN_DV), lambda b, n: (0, 0))],
        out_specs=pl.BlockSpec((C, DN_WIDTH), tok),
        out_shape=jax.ShapeDtypeStruct((M, DN_WIDTH), BF16),
        scratch_shapes=[pltpu.VMEM((DN_HEADS, DN_DK, DN_DV), F32)],
        compiler_params=_cparams(("arbitrary", "arbitrary")),
        name="dn_scan",
    )(dq, dk, dv, dkt, dec, dect, pbig, gain_row)


def _out1_kernel(oa_ref, ob_ref, wa_ref, wb_ref, ga_ref, gb_ref, o_ref, wa16_ref, wb16_ref):
    @pl.when(pl.program_id(1) == 0)
    def _():
        wa16_ref[...] = wa_ref[...].astype(BF16)
        wb16_ref[...] = wb_ref[...].astype(BF16)

    ya = _dot(oa_ref[...], wa16_ref[...])
    yb = _dot(ob_ref[...], wb16_ref[...])
    mix = _sigmoid(ga_ref[...].astype(F32)) * ya + _sigmoid(gb_ref[...].astype(F32)) * yb
    o_ref[...] = mix.astype(o_ref.dtype)


OUT1_TM = 512


def _out1(o_a, o_b, wa, wb, pbig, D):
    M, K = o_a.shape
    tm = min(OUT1_TM, M)
    tn = min(MM_TN, D)
    nbd = D // tn
    return pl.pallas_call(
        _out1_kernel,
        grid=(D // tn, M // tm),
        in_specs=[pl.BlockSpec((tm, K), lambda j, i: (i, 0)),
                  pl.BlockSpec((tm, K), lambda j, i: (i, 0)),
                  pl.BlockSpec((K, tn), lambda j, i: (0, j), pipeline_mode=pl.Buffered(1)),
                  pl.BlockSpec((K, tn), lambda j, i: (0, j), pipeline_mode=pl.Buffered(1)),
                  pl.BlockSpec((tm, tn), lambda j, i: (i, j)),
                  pl.BlockSpec((tm, tn), lambda j, i: (i, nbd + j))],
        out_specs=pl.BlockSpec((tm, tn), lambda j, i: (i, j)),
        out_shape=jax.ShapeDtypeStruct((M, D), BF16),
        scratch_shapes=[pltpu.VMEM((K, tn), BF16), pltpu.VMEM((K, tn), BF16)],
        compiler_params=_cparams(("arbitrary", "arbitrary")),
        name="out1",
    )(o_a, o_b, wa, wb, pbig, pbig)


def _final_kernel(mix_ref, x_ref, gate_ref, fg_ref, o_ref):
    xn = x_ref[...] + gate_ref[0] * mix_ref[...].astype(F32)
    ms = jnp.mean(xn * xn, axis=-1, keepdims=True)
    o_ref[...] = xn * lax.rsqrt(ms + EPS) * fg_ref[...]


def _final(mixed, x2, mod3, final_gain, S):
    M, D = x2.shape
    tm = 256
    nb = S // tm
    return pl.pallas_call(
        _final_kernel,
        grid=(M // tm,),
        in_specs=[pl.BlockSpec((tm, D), lambda i: (i, 0)),
                  pl.BlockSpec((tm, D), lambda i: (i, 0)),
                  pl.BlockSpec((1, 1, D), lambda i: (i // nb, 0, 2)),
                  pl.BlockSpec((1, D), lambda i: (0, 0))],
        out_specs=pl.BlockSpec((tm, D), lambda i: (i, 0)),
        out_shape=jax.ShapeDtypeStruct((M, D), F32),
        compiler_params=_cparams(("arbitrary",)),
        name="final",
    )(mixed, x2, mod3, final_gain)


def _pad_cols(w, width):
    return jnp.pad(w, ((0, 0), (0, width - w.shape[1])))


def _proj_row_offsets(D):
    widths = (NSA_WIDTH, 6 * KV_WIDTH, 3 * NSA_HEADS, NSA_WIDTH, 3 * DN_WIDTH, DN_HEADS, DN_HEADS, DN_WIDTH, 2 * D)
    offs = [int(o) for o in np.concatenate([[0], np.cumsum(widths)])]
    tiles = lambda seg: [offs[seg] + t * MM_TN for t in range(widths[seg] // MM_TN)]
    big = tiles(8) + tiles(4) + tiles(0) + tiles(3) + tiles(7)
    kv0 = offs[1]
    tail = [kv0, kv0 + TAIL_TN]
    small = [offs[2], offs[5]]
    return big, tail, small


TAIL_TN = 3 * KV_WIDTH
TAIL_TM = 1024
TAIL_CMP_BLOCK = 0
SMALL_GATE_BLOCK = 0
SMALL_AB_BLOCK = 1


def kernel(x, c, positions, w_ada, b_ada, norm_gain, w_in, cmp_pos_k, cmp_pos_v, w_cmp_k1, w_cmp_k2,
           w_cmp_v1, w_cmp_v2, conv_w, dt_bias, a_log, dn_norm_gain, w_proj_a, w_proj_b, w_out, final_gain):
    B, S, D = x.shape
    M = B * S
    depth = w_in.shape[0]
    assert S % SEL_TILE == 0 and S % DN_CHUNK == 0 and S >= WINDOW + Q_BLOCK and D % 512 == 0 and B <= 8
    assert S // SLC_LEN <= LANES and D % 1024 == 0 and DN_DK == DN_CHUNK
    assert depth == 1, "the final RMSNorm is fused into the last layer's output kernel"

    off_dq = 2 * D
    off_q = off_dq + 3 * DN_WIDTH
    off_z = off_q + NSA_WIDTH
    off_dz = off_z + NSA_WIDTH

    x2 = x.reshape(M, D)
    pos_col = positions.reshape(M, 1)
    cmp_end = np.arange(S // CMP_STRIDE - 1) * CMP_STRIDE + CMP_LEN - 1
    pos_cmp = jnp.pad(positions[:, cmp_end], ((0, 0), (0, 1)))[:, :, None]
    c8 = jnp.pad(c, ((0, 8 - B), (0, 0)))

    for l in range(depth):
        mod = _ada(c8, w_ada[l], b_ada[l][None, :])
        mod3 = mod[:B].reshape(B, 1, 3 * D)
        h = _norm(x2, norm_gain[l][None, :], mod3, S)

        w_nk = jnp.transpose(w_in[l])
        big_rows, tail_rows, small_rows = _proj_row_offsets(D)
        pbig = _proj(h, w_nk, big_rows, MM_TM, MM_TN, BF16, "proj_big")
        ptail = _proj(h, w_nk, tail_rows, TAIL_TM, TAIL_TN, F32, "proj_tail")
        psmall = _proj(h, w_nk, small_rows, TAIL_TM, 2 * LANES, F32, "proj_small", pieces=2)

        q_t, keys, v_t = _nsa_prep(pbig, ptail, pos_col, off_q // NSA_WIDTH, S)
        kcmp = _compress(ptail, TAIL_CMP_BLOCK, cmp_pos_k[l], w_cmp_k1[l], w_cmp_k2[l], pos_cmp, B, S, True)
        vcmp = _compress(ptail, TAIL_CMP_BLOCK + NSA_GROUPS, cmp_pos_v[l], w_cmp_v1[l], w_cmp_v2[l].T,
                         pos_cmp, B, S, False)
        o_a = _nsa_attn(q_t, kcmp, vcmp, keys, v_t, psmall, pbig, off_z // NSA_WIDTH, B, S)

        cw = conv_w[l]
        dq = _dn_prep(pbig, cw, off_dq, "q", B, S)[0]
        dk, dkt = _dn_prep(pbig, cw, off_dq + DN_WIDTH, "k", B, S)
        dv = _dn_prep(pbig, cw, off_dq + 2 * DN_WIDTH, "v", B, S)[0]
        alog_row = _pad_cols(a_log[l][None, :].astype(F32), LANES)
        dtb_row = _pad_cols(dt_bias[l][None, :].astype(F32), LANES)
        dec, dect = _dn_gate(psmall, alog_row, dtb_row)
        o_b = _dn_scan(dq, dk, dv, dkt, dec, dect, pbig, off_dz // DN_WIDTH,
                       dn_norm_gain[l][None, :], B, S)

        mixed_in = _out1(o_a, o_b, w_proj_a[l], w_proj_b[l], pbig, D)
        mixed = _matmul(mixed_in, w_out[l], BF16, "out2")
        x2 = _final(mixed, x2, mod3, final_gain[None, :], S)
    return x2.reshape(B, S, D)
```

```python
import functools
import math

import numpy as np
import jax
import jax.numpy as jnp
from jax import lax
from jax.experimental import pallas as pl
from jax.experimental.pallas import tpu as pltpu

F32 = jnp.float32
BF16 = jnp.bfloat16

NSA_HEADS = 16
NSA_GROUPS = 2
NSA_HPG = NSA_HEADS // NSA_GROUPS
HEAD_DIM = 128
ROT_DIM = HEAD_DIM // 4
ROT_HALF = ROT_DIM // 2
ROPE_THETA = 500000.0
CMP_LEN = 32
CMP_STRIDE = 16
CMP_HIDDEN = 256
SLC_LEN = 64
SLC_TOPK = 16
WINDOW = 512
Q_BLOCK = 128
NSA_WIDTH = NSA_HEADS * HEAD_DIM
KV_WIDTH = NSA_GROUPS * HEAD_DIM
DN_HEADS = 16
DN_DK = 128
DN_DV = 128
DN_WIDTH = DN_HEADS * DN_DV
CONV_WIDTH = 4
EPS = 1e-6

LANES = 128
VMEM_LIMIT_BYTES = 56 * 1024 * 1024

DN_CHUNK = 128
DN_HEAD_GROUP = 16
SEL_TILE = 512
VT_ROWS = HEAD_DIM + 16
NEG = -1e30


def _cparams(sem):
    return pltpu.CompilerParams(dimension_semantics=sem, vmem_limit_bytes=VMEM_LIMIT_BYTES)


def _sigmoid(x):
    return 1.0 / (1.0 + jnp.exp(-x))


def _silu(x):
    return x * _sigmoid(x)


def _dot(a, b):
    return jnp.dot(a, b, preferred_element_type=F32)


def _dot_nt(a, b):
    return lax.dot_general(a, b, (((1,), (1,)), ((), ())), preferred_element_type=F32)


def _bmm(a, b):
    return lax.dot_general(a, b, (((2,), (1,)), ((0,), (0,))), preferred_element_type=F32)


def _dot_f32(a, b):
    return jnp.dot(a, b, preferred_element_type=F32, precision=lax.Precision.HIGHEST)


def _ada_kernel(c_ref, w_ref, b_ref, o_ref):
    c = c_ref[...]
    c_hi = c.astype(BF16)
    c_lo = (c - c_hi.astype(F32)).astype(BF16)
    w = w_ref[...]
    w_hi = w.astype(BF16)
    w_lo = (w - w_hi.astype(F32)).astype(BF16)
    acc = _dot(c_hi, w_hi) + _dot(c_lo, w_hi) + _dot(c_hi, w_lo)
    o_ref[...] = acc + b_ref[...]


def _ada(c8, w_ada, b_ada):
    D, N = w_ada.shape
    tn = min(512, N)
    return pl.pallas_call(
        _ada_kernel,
        grid=(N // tn,),
        in_specs=[pl.BlockSpec((8, D), lambda j: (0, 0)),
                  pl.BlockSpec((D, tn), lambda j: (0, j)),
                  pl.BlockSpec((1, tn), lambda j: (0, j))],
        out_specs=pl.BlockSpec((8, tn), lambda j: (0, j)),
        out_shape=jax.ShapeDtypeStruct((8, N), F32),
        compiler_params=_cparams(("arbitrary",)),
        name="ada",
    )(c8, w_ada, b_ada)


def _norm_kernel(x_ref, gain_ref, shift_ref, scale_ref, o_ref):
    x = x_ref[...]
    ms = jnp.mean(x * x, axis=-1, keepdims=True)
    y = x * lax.rsqrt(ms + EPS) * gain_ref[...]
    o_ref[...] = (y * (1.0 + scale_ref[0]) + shift_ref[0]).astype(o_ref.dtype)


def _norm(x2, gain, mod3, S):
    M, D = x2.shape
    tm = 256
    nb = S // tm
    return pl.pallas_call(
        _norm_kernel,
        grid=(M // tm,),
        in_specs=[pl.BlockSpec((tm, D), lambda i: (i, 0)),
                  pl.BlockSpec((1, D), lambda i: (0, 0)),
                  pl.BlockSpec((1, 1, D), lambda i: (i // nb, 0, 0)),
                  pl.BlockSpec((1, 1, D), lambda i: (i // nb, 0, 1))],
        out_specs=pl.BlockSpec((tm, D), lambda i: (i, 0)),
        out_shape=jax.ShapeDtypeStruct((M, D), BF16),
        compiler_params=_cparams(("arbitrary",)),
        name="norm",
    )(x2, gain, mod3, mod3)


MM_TM = 1024
MM_TN = 1024
W_ROW_ALIGN = 16
WT_CHUNK = 128


def _mm_kernel(offs_ref, a_ref, w_hbm, o_ref, wf32_ref, w16_ref, sems, *, w_is_nk, tn, pieces):
    j = pl.program_id(0)
    i = pl.program_id(1)
    pw = tn // pieces

    def piece_copy(jj, p):
        off = offs_ref[jj * pieces + p]
        if w_is_nk:
            src = w_hbm.at[pl.ds(pl.multiple_of(off * W_ROW_ALIGN, W_ROW_ALIGN), pw), :]
            dst = wf32_ref.at[p * pw:(p + 1) * pw, :]
        else:
            src = w_hbm.at[:, pl.ds(pl.multiple_of(off * LANES, LANES), pw)]
            dst = wf32_ref.at[:, p * pw:(p + 1) * pw]
        return pltpu.make_async_copy(src, dst, sems.at[p])

    def start_tile(jj):
        for p in range(pieces):
            piece_copy(jj, p).start()

    @pl.when((j == 0) & (i == 0))
    def _():
        start_tile(0)

    @pl.when(i == 0)
    def _():
        for p in range(pieces):
            piece_copy(j, p).wait()
        if w_is_nk:
            for c in range(0, tn, WT_CHUNK):
                w16_ref[:, c:c + WT_CHUNK] = wf32_ref[c:c + WT_CHUNK, :].T.astype(BF16)
        else:
            w16_ref[...] = wf32_ref[...].astype(BF16)

        @pl.when(j + 1 < pl.num_programs(0))
        def _():
            start_tile(j + 1)

    o_ref[...] = _dot(a_ref[...], w16_ref[...]).astype(o_ref.dtype)


def _mm_call(a, w, offs, w_is_nk, tm, tn, out_dtype, name, pieces=1):
    M, K = a.shape
    tm = min(tm, M)
    n_tiles = len(offs) // pieces
    grid_spec = pltpu.PrefetchScalarGridSpec(
        num_scalar_prefetch=1,
        grid=(n_tiles, M // tm),
        in_specs=[pl.BlockSpec((tm, K), lambda j, i, o: (i, 0)),
                  pl.BlockSpec(memory_space=pl.ANY)],
        out_specs=pl.BlockSpec((tm, tn), lambda j, i, o: (i, j)),
        scratch_shapes=[pltpu.VMEM((tn, K) if w_is_nk else (K, tn), F32),
                        pltpu.VMEM((K, tn), BF16),
                        pltpu.SemaphoreType.DMA((pieces,))],
    )
    return pl.pallas_call(
        functools.partial(_mm_kernel, w_is_nk=w_is_nk, tn=tn, pieces=pieces),
        grid_spec=grid_spec,
        out_shape=jax.ShapeDtypeStruct((M, n_tiles * tn), out_dtype),
        compiler_params=_cparams(("arbitrary", "arbitrary")),
        name=name,
    )(jnp.asarray(np.asarray(offs, np.int32)), a, w)


def _matmul(a, w, out_dtype, name):
    N = w.shape[1]
    tn = min(MM_TN, N)
    offs = [t * tn // LANES for t in range(N // tn)]
    return _mm_call(a, w, offs, False, MM_TM, tn, out_dtype, name)


def _proj(h, w_nk, row_offsets, tm, tn, out_dtype, name, pieces=1):
    assert all(int(o) % W_ROW_ALIGN == 0 and int(o) + tn // pieces <= w_nk.shape[0] for o in row_offsets)
    offs = [int(o) // W_ROW_ALIGN for o in row_offsets]
    return _mm_call(h, w_nk, offs, True, tm, tn, out_dtype, name, pieces)


def _rope_consts():
    inv = ROPE_THETA ** (-np.arange(ROT_HALF, dtype=np.float64) / ROT_HALF)
    invf = np.zeros((1, LANES), np.float32)
    invf[0, :ROT_HALF] = inv
    invf[0, ROT_HALF:ROT_DIM] = inv
    sgn = np.zeros((1, LANES), np.float32)
    sgn[0, :ROT_HALF] = -1.0
    sgn[0, ROT_HALF:ROT_DIM] = 1.0
    return jnp.asarray(invf), jnp.asarray(sgn)


def _rope_tables(pos_col, invf, sgn):
    ang = pos_col.astype(F32) * invf
    return jnp.cos(ang), jnp.sin(ang) * sgn


def _rope_apply(x, cos_t, sin_t):
    lane = lax.broadcasted_iota(jnp.int32, x.shape, 1)
    partner = jnp.where(lane < ROT_HALF,
                        pltpu.roll(x, LANES - ROT_HALF, 1),
                        pltpu.roll(x, ROT_HALF, 1))
    return x * cos_t + partner * sin_t


KEYS_WIDTH = 6 * HEAD_DIM


def _nsa_prep_kernel(q_ref, kvs_ref, kvw_ref, pos_ref, invf_ref, sgn_ref, qt_ref, keys_ref, vt_ref, *, seq_len):
    tm = q_ref.shape[0]
    cos_t, sin_t = _rope_tables(pos_ref[...], invf_ref[...], sgn_ref[...])
    qscale = (HEAD_DIM ** -0.5) * math.log2(math.e)
    for h in range(NSA_HEADS):
        sl = slice(h * HEAD_DIM, (h + 1) * HEAD_DIM)
        xq = _rope_apply(q_ref[:, sl].astype(F32), cos_t, sin_t) * qscale
        qt_ref[sl, :] = xq.T.astype(qt_ref.dtype)
    tok = (pl.program_id(0) * tm) % seq_len + lax.broadcasted_iota(jnp.int32, (tm, LANES), 0)
    lane = lax.broadcasted_iota(jnp.int32, (tm, LANES), 1)
    onehot = jnp.where(lane == (tok >> int(math.log2(SLC_LEN))), 1.0, 0.0).astype(keys_ref.dtype)
    blk = lambda i: (kvs_ref if i < 4 else kvw_ref)[:, (i % 4) * HEAD_DIM:(i % 4 + 1) * HEAD_DIM]
    for g in range(NSA_GROUPS):
        ks = _rope_apply(blk(g), cos_t, sin_t)
        keys_ref[:, (2 * g) * HEAD_DIM:(2 * g + 1) * HEAD_DIM] = ks.astype(keys_ref.dtype)
        keys_ref[:, (2 * g + 1) * HEAD_DIM:(2 * g + 2) * HEAD_DIM] = onehot
        kw = _rope_apply(blk(4 + g), cos_t, sin_t)
        keys_ref[:, (4 + g) * HEAD_DIM:(5 + g) * HEAD_DIM] = kw.astype(keys_ref.dtype)
        ones = jnp.ones((VT_ROWS - HEAD_DIM, tm), vt_ref.dtype)
        for c, src in ((g, 2 + g), (2 + g, 6 + g)):
            vt_ref[c * VT_ROWS:c * VT_ROWS + HEAD_DIM, :] = blk(src).T.astype(vt_ref.dtype)
            vt_ref[c * VT_ROWS + HEAD_DIM:(c + 1) * VT_ROWS, :] = ones


def _nsa_prep(pbig, ptail, pos_col, q_col_block, S):
    M = pbig.shape[0]
    tm = 256
    invf, sgn = _rope_consts()
    return pl.pallas_call(
        functools.partial(_nsa_prep_kernel, seq_len=S),
        grid=(M // tm,),
        in_specs=[pl.BlockSpec((tm, NSA_WIDTH), lambda i: (i, q_col_block)),
                  pl.BlockSpec((tm, 4 * HEAD_DIM), lambda i: (i, 1)),
                  pl.BlockSpec((tm, 4 * HEAD_DIM), lambda i: (i, 2)),
                  pl.BlockSpec((tm, 1), lambda i: (i, 0)),
                  pl.BlockSpec((1, LANES), lambda i: (0, 0)),
                  pl.BlockSpec((1, LANES), lambda i: (0, 0))],
        out_specs=[pl.BlockSpec((NSA_WIDTH, tm), lambda i: (0, i)),
                   pl.BlockSpec((tm, KEYS_WIDTH), lambda i: (i, 0)),
                   pl.BlockSpec((4 * VT_ROWS, tm), lambda i: (0, i))],
        out_shape=[jax.ShapeDtypeStruct((NSA_WIDTH, M), BF16),
                   jax.ShapeDtypeStruct((M, KEYS_WIDTH), BF16),
                   jax.ShapeDtypeStruct((4 * VT_ROWS, M), BF16)],
        compiler_params=_cparams(("arbitrary",)),
        name="nsa_prep",
    )(pbig, ptail, ptail, pos_col, invf, sgn)


def _compress_kernel(x_ref, pe_ref, w1_ref, w2_ref, pos_ref, invf_ref, sgn_ref, o_ref, *, rope, n_rows):
    half = CMP_LEN // 2
    top = jnp.zeros((n_rows, CMP_HIDDEN), F32)
    bot = jnp.zeros((n_rows, CMP_HIDDEN), F32)
    for l in range(half):
        xl = x_ref[pl.ds(l, n_rows, stride=CMP_STRIDE), :]
        w_top = w1_ref[l * HEAD_DIM:(l + 1) * HEAD_DIM, :].astype(BF16)
        w_bot = w1_ref[(half + l) * HEAD_DIM:(half + l + 1) * HEAD_DIM, :].astype(BF16)
        top = top + _dot((xl + pe_ref[l:l + 1, :]).astype(BF16), w_top)
        bot = bot + _dot((xl + pe_ref[half + l:half + l + 1, :]).astype(BF16), w_bot)
    hid = top + pltpu.roll(bot, n_rows - 1, 0)
    act = _silu(hid).astype(BF16)
    if rope:
        out = _dot(act, w2_ref[...].astype(BF16))
        cos_t, sin_t = _rope_tables(pos_ref[0], invf_ref[...], sgn_ref[...])
        o_ref[0, 0] = _rope_apply(out, cos_t, sin_t).astype(o_ref.dtype)
    else:
        o_ref[0, 0, 0:HEAD_DIM, :] = _dot_nt(w2_ref[...].astype(BF16), act).astype(o_ref.dtype)
        o_ref[0, 0, HEAD_DIM:VT_ROWS, :] = jnp.ones((VT_ROWS - HEAD_DIM, n_rows), o_ref.dtype)


def _compress(ptail, col_block0, pe, w1, w2, pos_cmp, B, S, rope):
    n_rows = S // CMP_STRIDE
    invf, sgn = _rope_consts()
    kern = functools.partial(_compress_kernel, rope=rope, n_rows=n_rows)
    out_dims = (n_rows, HEAD_DIM) if rope else (VT_ROWS, n_rows)
    return pl.pallas_call(
        kern,
        grid=(B, NSA_GROUPS),
        in_specs=[pl.BlockSpec((S, HEAD_DIM), lambda b, g: (b, col_block0 + g)),
                  pl.BlockSpec((CMP_LEN, HEAD_DIM), lambda b, g: (0, 0)),
                  pl.BlockSpec((CMP_LEN * HEAD_DIM, CMP_HIDDEN), lambda b, g: (0, 0)),
                  pl.BlockSpec(w2.shape, lambda b, g: (0, 0)),
                  pl.BlockSpec((1, n_rows, 1), lambda b, g: (b, 0, 0)),
                  pl.BlockSpec((1, LANES), lambda b, g: (0, 0)),
                  pl.BlockSpec((1, LANES), lambda b, g: (0, 0))],
        out_specs=pl.BlockSpec((1, 1) + out_dims, lambda b, g: (b, g, 0, 0)),
        out_shape=jax.ShapeDtypeStruct((B, NSA_GROUPS) + out_dims, BF16),
        compiler_params=_cparams(("arbitrary", "arbitrary")),
        name="compress_k" if rope else "compress_v",
    )(ptail, pe, w1, w2, pos_cmp, invf, sgn)


def _tile_lanes(x, n):
    return jnp.concatenate([x] * n, axis=1)


def _nsa_attn_kernel(qt_ref, kc_ref, vct_ref, ks_ref, vst_ref, kw_ref, vwt_ref, g_ref, z_ref,
                     ovt_ref, o_ref, m_s, acc_s, qa_s, sa_s, sb_s, sc_s, *, seq_len):
    T = Q_BLOCK
    P = NSA_HPG
    R = P * T
    n_cmp_pad = seq_len // CMP_STRIDE
    n_cmp = n_cmp_pad - 1
    nb = seq_len // SLC_LEN
    qi = pl.program_id(2)
    qs = qi * T
    d0 = pl.multiple_of(qs, T)
    wlen = WINDOW + T
    w0 = pl.multiple_of(jnp.maximum(qs - WINDOW, 0), T)

    qt = jnp.concatenate([qt_ref[p * HEAD_DIM:(p + 1) * HEAD_DIM, :] for p in range(P)], axis=1)

    s_c = _dot(kc_ref[0, 0], qt)
    s_w = _dot(kw_ref[pl.ds(w0, wlen), :], qt)
    s_d = _dot(ks_ref[pl.ds(d0, T), 0:HEAD_DIM], qt)

    tok = lambda n: qs + lax.broadcasted_iota(jnp.int32, (n, T), 1)
    row = lambda n: lax.broadcasted_iota(jnp.int32, (n, T), 0)
    ok_c = (row(n_cmp_pad) * CMP_STRIDE + (CMP_LEN - 1) <= tok(n_cmp_pad)) & (row(n_cmp_pad) < n_cmp)
    key_w = w0 + row(wlen)
    ok_w = (key_w <= tok(wlen)) & (tok(wlen) - key_w < WINDOW)
    ok_d = row(T) <= lax.broadcasted_iota(jnp.int32, (T, T), 1)
    addmask = lambda ok: _tile_lanes(jnp.where(ok, 0.0, NEG).astype(F32), P)
    s_c = s_c + addmask(ok_c)
    s_w = s_w + addmask(ok_w)
    s_d = s_d + addmask(ok_d)

    m_c = jnp.max(s_c, axis=0, keepdims=True)
    m_w = jnp.max(s_w, axis=0, keepdims=True)
    m_d = jnp.max(s_d, axis=0, keepdims=True)
    m_c = jnp.where(m_c > 0.5 * NEG, m_c, 0.0)
    p_c = jnp.exp2(s_c - m_c)
    p_w = jnp.exp2(s_w - m_w)
    p_d = jnp.exp2(s_d - m_d)
    o_c = _dot(vct_ref[0, 0], p_c.astype(BF16))
    o_w = _dot(vwt_ref[:, pl.ds(w0, wlen)], p_w.astype(BF16))
    inv_c = 1.0 / jnp.maximum(jnp.sum(p_c, axis=0, keepdims=True), 1e-30)
    inv_w = 1.0 / o_w[HEAD_DIM:HEAD_DIM + 1, :]
    m_s[...] = m_d
    acc_s[...] = _dot(vst_ref[:, pl.ds(d0, T)], p_d.astype(BF16))

    pn = p_c * inv_c
    p_sum = pn[:, 0:T]
    for p in range(1, P):
        p_sum = p_sum + pn[:, p * T:(p + 1) * T]
    ps_hi = p_sum.astype(BF16)
    ps_lo = (p_sum - ps_hi.astype(F32)).astype(BF16)
    imp = (_dot(ovt_ref[...], ps_hi) + _dot(ovt_ref[...], ps_lo))[0:nb]
    jrow = row(nb)
    t_lane = tok(nb)
    tb = t_lane >> int(math.log2(SLC_LEN))
    visible = jrow * SLC_LEN <= t_lane
    forced = (jrow == 0) | (jrow == tb) | (jrow == tb - 1)
    score = jnp.where(forced, 1e9, jnp.where(visible, imp, -jnp.inf))
    cnt = jnp.zeros((nb, T), F32)
    for k in range(nb):
        rk = score[k:k + 1, :]
        cnt = cnt + jnp.where(jrow > k, jnp.where(rk >= score, 1.0, 0.0), jnp.where(rk > score, 1.0, 0.0))
    keep = (cnt < float(min(SLC_TOPK, nb))) & visible & (jrow < 2 * qi)
    bias = jnp.where(keep, 0.0, NEG).astype(F32)
    bias = jnp.concatenate([bias, jnp.full((LANES - nb, T), NEG, F32)], axis=0) if nb < LANES else bias
    qa_s[0:HEAD_DIM, :] = qt
    qa_s[HEAD_DIM:2 * HEAD_DIM, :] = _tile_lanes(bias.astype(BF16), P)

    n_t = (qs + SEL_TILE - 1) // SEL_TILE

    def scores_into(kt, dst):
        dst[...] = _dot(ks_ref[kt * SEL_TILE:(kt + 1) * SEL_TILE, :], qa_s[...])

    def consume(kt, src):
        k0 = kt * SEL_TILE
        s = src[...]
        m_prev = m_s[...]
        m_new = jnp.maximum(m_prev, jnp.max(s, axis=0, keepdims=True))
        alpha = jnp.exp2(m_prev - m_new)
        p = jnp.exp2(s - m_new)
        acc_s[...] = alpha * acc_s[...] + _dot(vst_ref[:, pl.ds(k0, SEL_TILE)], p.astype(BF16))
        m_s[...] = m_new

    bufs = (sa_s, sb_s, sc_s)
    for r in range(1, seq_len // SEL_TILE + 1):
        @pl.when(n_t == r)
        def _(r=r):
            scores_into(0, bufs[0])
            for t in range(r):
                if t + 1 < r:
                    scores_into(t + 1, bufs[(t + 1) % len(bufs)])
                consume(t, bufs[t % len(bufs)])

    gates_t = _sigmoid(g_ref[...]).T
    first_group = pl.program_id(1) == 0

    def grow(r):
        rows = []
        for p in range(P):
            h0 = r * NSA_HEADS + p
            rows.append(jnp.where(first_group, gates_t[h0:h0 + 1, :], gates_t[h0 + P:h0 + P + 1, :]))
        return jnp.concatenate(rows, axis=1)

    o_s = acc_s[...]
    hd = slice(0, HEAD_DIM)
    o_t = (o_c[hd] * (grow(0) * inv_c) + o_s[hd] * (grow(1) / o_s[HEAD_DIM:HEAD_DIM + 1, :])
           + o_w[hd] * (grow(2) * inv_w))
    for p in range(P):
        sl = slice(p * HEAD_DIM, (p + 1) * HEAD_DIM)
        o_p = o_t[:, p * T:(p + 1) * T].T
        o_ref[:, sl] = (o_p * _silu(z_ref[:, sl].astype(F32))).astype(o_ref.dtype)


def _nsa_consts(S):
    n_cmp_pad = S // CMP_STRIDE
    nb = S // SLC_LEN
    cmp_start = np.arange(n_cmp_pad) * CMP_STRIDE
    slc_start = np.arange(LANES) * SLC_LEN
    ov = ((cmp_start[:, None] <= slc_start[None, :] + SLC_LEN - 1)
          & (cmp_start[:, None] + CMP_LEN - 1 >= slc_start[None, :])
          & (np.arange(LANES)[None, :] < nb)
          & (np.arange(n_cmp_pad)[:, None] < n_cmp_pad - 1)).astype(np.float32)
    return jnp.asarray(ov.T, dtype=BF16)


def _nsa_attn(q_t, kcmp, vcmp_t, keys, v_t, psmall, pbig, z_col_block, B, S):
    M = B * S
    nq = S // Q_BLOCK
    gw = NSA_HPG * HEAD_DIM
    ovt = _nsa_consts(S)
    n_cmp_pad = S // CMP_STRIDE
    R = NSA_HPG * Q_BLOCK
    kern = functools.partial(_nsa_attn_kernel, seq_len=S)
    return pl.pallas_call(
        kern,
        grid=(B, NSA_GROUPS, nq),
        in_specs=[pl.BlockSpec((gw, Q_BLOCK), lambda b, g, i: (g, b * nq + i)),
                  pl.BlockSpec((1, 1, n_cmp_pad, HEAD_DIM), lambda b, g, i: (b, g, 0, 0)),
                  pl.BlockSpec((1, 1, VT_ROWS, n_cmp_pad), lambda b, g, i: (b, g, 0, 0)),
                  pl.BlockSpec((S, 2 * HEAD_DIM), lambda b, g, i: (b, g)),
                  pl.BlockSpec((VT_ROWS, S), lambda b, g, i: (g, b)),
                  pl.BlockSpec((S, HEAD_DIM), lambda b, g, i: (b, 4 + g)),
                  pl.BlockSpec((VT_ROWS, S), lambda b, g, i: (2 + g, b)),
                  pl.BlockSpec((Q_BLOCK, LANES), lambda b, g, i: (b * nq + i, SMALL_GATE_BLOCK)),
                  pl.BlockSpec((Q_BLOCK, gw), lambda b, g, i: (b * nq + i, z_col_block * NSA_GROUPS + g)),
                  pl.BlockSpec((LANES, n_cmp_pad), lambda b, g, i: (0, 0))],
        out_specs=pl.BlockSpec((Q_BLOCK, gw), lambda b, g, i: (b * nq + i, g)),
        out_shape=jax.ShapeDtypeStruct((M, NSA_WIDTH), BF16),
        scratch_shapes=[pltpu.VMEM((1, R), F32),
                        pltpu.VMEM((VT_ROWS, R), F32),
                        pltpu.VMEM((2 * HEAD_DIM, R), BF16),
                        pltpu.VMEM((SEL_TILE, R), F32),
                        pltpu.VMEM((SEL_TILE, R), F32),
                        pltpu.VMEM((SEL_TILE, R), F32)],
        compiler_params=_cparams(("arbitrary", "arbitrary", "arbitrary")),
        name="nsa_attn",
    )(q_t, kcmp, vcmp_t, keys, v_t, keys, v_t, psmall, pbig, ovt)


def _dn_prep_kernel(x_ref, w_ref, o_ref, *rest, mode, seq_len):
    xpad_ref = rest[-1]
    w = w_ref[...]
    xpad_ref[0:CONV_PAD, :] = jnp.zeros((CONV_PAD, xpad_ref.shape[1]), F32)
    xpad_ref[CONV_PAD:, :] = x_ref[...].astype(F32)
    y = xpad_ref[CONV_PAD:, :] * w[CONV_WIDTH - 1:CONV_WIDTH, :]
    for k in range(1, CONV_WIDTH):
        y = y + xpad_ref[pl.ds(CONV_PAD - k, seq_len), :] * w[CONV_WIDTH - 1 - k:CONV_WIDTH - k, :]
    y = _silu(y)
    if mode in ("q", "k"):
        outs = []
        for h in range(y.shape[1] // DN_DK):
            yh = y[:, h * DN_DK:(h + 1) * DN_DK]
            ss = jnp.sum(yh * yh, axis=-1, keepdims=True)
            yh = yh * lax.rsqrt(ss + EPS)
            if mode == "q":
                yh = yh * (DN_DK ** -0.5)
            outs.append(yh)
        y = jnp.concatenate(outs, axis=1)
    o_ref[...] = y.astype(o_ref.dtype)
    if mode == "k":
        kt_ref = rest[0]
        kt_ref[...] = y.T.astype(kt_ref.dtype)


DN_PREP_COLS = 128
CONV_PAD = 8


def _dn_prep(pbig, conv_w, col0, mode, B, S):
    M = B * S
    tw = DN_PREP_COLS
    col0_blocks = col0 // tw
    nj = DN_WIDTH // tw
    wcol0 = {"q": 0, "k": nj, "v": 2 * nj}[mode]
    kern = functools.partial(_dn_prep_kernel, mode=mode, seq_len=S)
    out_specs = [pl.BlockSpec((S, tw), lambda b, j: (b, j))]
    out_shape = [jax.ShapeDtypeStruct((M, DN_WIDTH), BF16)]
    if mode == "k":
        out_specs.append(pl.BlockSpec((tw, S), lambda b, j: (j, b)))
        out_shape.append(jax.ShapeDtypeStruct((DN_WIDTH, M), BF16))
    return pl.pallas_call(
        kern,
        grid=(B, nj),
        in_specs=[pl.BlockSpec((S, tw), lambda b, j: (b, col0_blocks + j)),
                  pl.BlockSpec((CONV_WIDTH, tw), lambda b, j: (0, wcol0 + j))],
        out_specs=out_specs,
        out_shape=out_shape,
        scratch_shapes=[pltpu.VMEM((CONV_PAD + S, tw), F32)],
        compiler_params=_cparams(("arbitrary", "arbitrary")),
        name="dn_prep_" + mode,
    )(pbig, conv_w)


def _dn_gate_kernel(ab_ref, alog_ref, dtb_ref, o_ref, ot_ref, *, tm):
    ab = ab_ref[...]
    x = ab + dtb_ref[...]
    softplus = jnp.maximum(x, 0.0) + jnp.log(1.0 + jnp.exp(-jnp.abs(x)))
    g = -jnp.exp(alog_ref[...]) * softplus
    beta = _sigmoid(ab)
    lane = lax.broadcasted_iota(jnp.int32, (DN_CHUNK, LANES), 1)
    r = lax.broadcasted_iota(jnp.int32, (DN_CHUNK, DN_CHUNK), 0)
    c = lax.broadcasted_iota(jnp.int32, (DN_CHUNK, DN_CHUNK), 1)
    tril = jnp.where(r >= c, 1.0, 0.0).astype(F32)
    for ci in range(tm // DN_CHUNK):
        rows = slice(ci * DN_CHUNK, (ci + 1) * DN_CHUNK)
        dec = _dot_f32(tril, g[rows])
        out = jnp.where(lane < DN_HEADS, dec, beta[rows])
        o_ref[rows, :] = out
        ot_ref[:, rows] = out.T


def _dn_gate(psmall, alog_row, dtb_row):
    M = psmall.shape[0]
    tm = 512
    kern = functools.partial(_dn_gate_kernel, tm=tm)
    return pl.pallas_call(
        kern,
        grid=(M // tm,),
        in_specs=[pl.BlockSpec((tm, LANES), lambda i: (i, SMALL_AB_BLOCK)),
                  pl.BlockSpec((1, LANES), lambda i: (0, 0)),
                  pl.BlockSpec((1, LANES), lambda i: (0, 0))],
        out_specs=[pl.BlockSpec((tm, LANES), lambda i: (i, 0)),
                   pl.BlockSpec((LANES, tm), lambda i: (0, i))],
        out_shape=[jax.ShapeDtypeStruct((M, LANES), F32),
                   jax.ShapeDtypeStruct((LANES, M), F32)],
        compiler_params=_cparams(("arbitrary",)),
        name="dn_gate",
    )(psmall, alog_row, dtb_row)


def _dn_scan_kernel(q_ref, k_ref, v_ref, kt_ref, dec_ref, dect_ref, z_ref, gain_ref, o_ref, st_ref):
    C = DN_CHUNK

    @pl.when(pl.program_id(1) == 0)
    def _():
        st_ref[...] = jnp.zeros(st_ref.shape, F32)

    dec = dec_ref[...]
    dect = dect_ref[...]
    r = lax.broadcasted_iota(jnp.int32, (C, C), 0)
    c = lax.broadcasted_iota(jnp.int32, (C, C), 1)
    tril = r >= c
    strict = r > c
    gain = gain_ref[...]
    eye = jnp.where(r == c, 1.0, 0.0).astype(F32)
    lvl_masks = []
    for lg in range(int(math.log2(C))):
        lvl_masks.append(((r >> (lg + 1)) == (c >> (lg + 1))) & ((r >> lg) != (c >> lg)))

    for h0 in range(0, DN_HEADS, DN_HEAD_GROUP):
        heads = range(h0, h0 + DN_HEAD_GROUP)
        cols = lambda ref: jnp.stack([ref[:, h * DN_DK:(h + 1) * DN_DK] for h in heads])
        qh = cols(q_ref).astype(F32)
        kh = cols(k_ref).astype(F32)
        vh = cols(v_ref).astype(F32)
        kth = jnp.stack([kt_ref[h * DN_DK:(h + 1) * DN_DK, :] for h in heads])
        dcol = jnp.stack([jnp.broadcast_to(dec[:, h:h + 1], (C, C)) for h in heads])
        bcol = jnp.stack([jnp.broadcast_to(dec[:, DN_HEADS + h:DN_HEADS + h + 1], (C, C)) for h in heads])
        drow = jnp.stack([jnp.broadcast_to(dect[h:h + 1, :], (C, C)) for h in heads])
        dlast = jnp.stack([jnp.broadcast_to(dect[h:h + 1, C - 1:C], (C, C)) for h in heads])
        lmat = jnp.exp(jnp.where(tril[None], dcol - drow, NEG))
        e_d = jnp.exp(dcol)
        kb = kh * bcol
        vb = vh * bcol
        a = jnp.where(strict[None], _bmm(kb.astype(BF16), kth) * lmat, 0.0)
        attn = jnp.where(tril[None], _bmm(qh.astype(BF16), kth) * lmat, 0.0)
        tinv = eye[None] - jnp.where(lvl_masks[0][None], a, 0.0)
        for lm in lvl_masks[1:]:
            t16 = tinv.astype(BF16)
            lo = jnp.where(lm[None], a, 0.0).astype(BF16)
            tinv = tinv - _bmm(t16, _bmm(lo, t16).astype(BF16))
        t16 = tinv.astype(BF16)
        u = _bmm(t16, vb.astype(BF16))
        w = _bmm(t16, (kb * e_d).astype(BF16))
        st = st_ref[h0:h0 + DN_HEAD_GROUP]
        st16 = st.astype(BF16)
        v_new = u - _bmm(w.astype(BF16), st16)
        v_new16 = v_new.astype(BF16)
        o = _bmm((qh * e_d).astype(BF16), st16) + _bmm(attn.astype(BF16), v_new16)
        kdt = (kth.astype(F32) * jnp.exp(dlast - drow)).astype(BF16)
        st_ref[h0:h0 + DN_HEAD_GROUP] = st * jnp.exp(dlast) + _bmm(kdt, v_new16)
        ms = jnp.mean(o * o, axis=-1, keepdims=True)
        y = o * lax.rsqrt(ms + EPS) * gain[None]
        for i, h in enumerate(heads):
            sl = slice(h * DN_DV, (h + 1) * DN_DV)
            o_ref[:, sl] = (y[i] * _silu(z_ref[:, sl].astype(F32))).astype(o_ref.dtype)


def _dn_scan(dq, dk, dv, dkt, dec, dect, pbig, z_col_block, gain_row, B, S):
    M = B * S
    C = DN_CHUNK
    nc = S // C
    tok = lambda b, n: (b * nc + n, 0)
    return pl.pallas_call(
        _dn_scan_kernel,
        grid=(B, nc),
        in_specs=[pl.BlockSpec((C, DN_WIDTH), tok),
                  pl.BlockSpec((C, DN_WIDTH), tok),
                  pl.BlockSpec((C, DN_WIDTH), tok),
                  pl.BlockSpec((DN_WIDTH, C), lambda b, n: (0, b * nc + n)),
                  pl.BlockSpec((C, LANES), tok),
                  pl.BlockSpec((LANES, C), lambda b, n: (0, b * nc + n)),
                  pl.BlockSpec((C, DN_WIDTH), lambda b, n: (b * nc + n, z_col_block)),
                  pl.BlockSpec((1, DN_DV), lambda b, n: (0, 0))],
        out_specs=pl.BlockSpec((C, DN_WIDTH), tok),
        out_shape=jax.ShapeDtypeStruct((M, DN_WIDTH), BF16),
        scratch_shapes=[pltpu.VMEM((DN_HEADS, DN_DK, DN_DV), F32)],
        compiler_params=_cparams(("arbitrary", "arbitrary")),
        name="dn_scan",
    )(dq, dk, dv, dkt, dec, dect, pbig, gain_row)


def _out1_kernel(oa_ref, ob_ref, wa_ref, wb_ref, ga_ref, gb_ref, o_ref, wa16_ref, wb16_ref):
    @pl.when(pl.program_id(1) == 0)
    def _():
        wa16_ref[...] = wa_ref[...].astype(BF16)
        wb16_ref[...] = wb_ref[...].astype(BF16)

    ya = _dot(oa_ref[...], wa16_ref[...])
    yb = _dot(ob_ref[...], wb16_ref[...])
    mix = _sigmoid(ga_ref[...].astype(F32)) * ya + _sigmoid(gb_ref[...].astype(F32)) * yb
    o_ref[...] = mix.astype(o_ref.dtype)


OUT1_TM = 512


def _out1(o_a, o_b, wa, wb, pbig, D):
    M, K = o_a.shape
    tm = min(OUT1_TM, M)
    tn = min(MM_TN, D)
    nbd = D // tn
    return pl.pallas_call(
        _out1_kernel,
        grid=(D // tn, M // tm),
        in_specs=[pl.BlockSpec((tm, K), lambda j, i: (i, 0)),
                  pl.BlockSpec((tm, K), lambda j, i: (i, 0)),
                  pl.BlockSpec((K, tn), lambda j, i: (0, j), pipeline_mode=pl.Buffered(1)),
                  pl.BlockSpec((K, tn), lambda j, i: (0, j), pipeline_mode=pl.Buffered(1)),
                  pl.BlockSpec((tm, tn), lambda j, i: (i, j)),
                  pl.BlockSpec((tm, tn), lambda j, i: (i, nbd + j))],
        out_specs=pl.BlockSpec((tm, tn), lambda j, i: (i, j)),
        out_shape=jax.ShapeDtypeStruct((M, D), BF16),
        scratch_shapes=[pltpu.VMEM((K, tn), BF16), pltpu.VMEM((K, tn), BF16)],
        compiler_params=_cparams(("arbitrary", "arbitrary")),
        name="out1",
    )(o_a, o_b, wa, wb, pbig, pbig)


def _final_kernel(mix_ref, x_ref, gate_ref, fg_ref, o_ref):
    xn = x_ref[...] + gate_ref[0] * mix_ref[...].astype(F32)
    ms = jnp.mean(xn * xn, axis=-1, keepdims=True)
    o_ref[...] = xn * lax.rsqrt(ms + EPS) * fg_ref[...]


def _final(mixed, x2, mod3, final_gain, S):
    M, D = x2.shape
    tm = 256
    nb = S // tm
    return pl.pallas_call(
        _final_kernel,
        grid=(M // tm,),
        in_specs=[pl.BlockSpec((tm, D), lambda i: (i, 0)),
                  pl.BlockSpec((tm, D), lambda i: (i, 0)),
                  pl.BlockSpec((1, 1, D), lambda i: (i // nb, 0, 2)),
                  pl.BlockSpec((1, D), lambda i: (0, 0))],
        out_specs=pl.BlockSpec((tm, D), lambda i: (i, 0)),
        out_shape=jax.ShapeDtypeStruct((M, D), F32),
        compiler_params=_cparams(("arbitrary",)),
        name="final",
    )(mixed, x2, mod3, final_gain)


def _pad_cols(w, width):
    return jnp.pad(w, ((0, 0), (0, width - w.shape[1])))


def _proj_row_offsets(D):
    widths = (NSA_WIDTH, 6 * KV_WIDTH, 3 * NSA_HEADS, NSA_WIDTH, 3 * DN_WIDTH, DN_HEADS, DN_HEADS, DN_WIDTH, 2 * D)
    offs = [int(o) for o in np.concatenate([[0], np.cumsum(widths)])]
    tiles = lambda seg: [offs[seg] + t * MM_TN for t in range(widths[seg] // MM_TN)]
    big = tiles(8) + tiles(4) + tiles(0) + tiles(3) + tiles(7)
    kv0 = offs[1]
    tail = [kv0, kv0 + TAIL_TN]
    small = [offs[2], offs[5]]
    return big, tail, small


TAIL_TN = 3 * KV_WIDTH
TAIL_TM = 1024
TAIL_CMP_BLOCK = 0
SMALL_GATE_BLOCK = 0
SMALL_AB_BLOCK = 1


def kernel(x, c, positions, w_ada, b_ada, norm_gain, w_in, cmp_pos_k, cmp_pos_v, w_cmp_k1, w_cmp_k2,
           w_cmp_v1, w_cmp_v2, conv_w, dt_bias, a_log, dn_norm_gain, w_proj_a, w_proj_b, w_out, final_gain):
    B, S, D = x.shape
    M = B * S
    depth = w_in.shape[0]
    assert S % SEL_TILE == 0 and S % DN_CHUNK == 0 and S >= WINDOW + Q_BLOCK and D % 512 == 0 and B <= 8
    assert S // SLC_LEN <= LANES and D % 1024 == 0 and DN_DK == DN_CHUNK
    assert depth == 1, "the final RMSNorm is fused into the last layer's output kernel"

    off_dq = 2 * D
    off_q = off_dq + 3 * DN_WIDTH
    off_z = off_q + NSA_WIDTH
    off_dz = off_z + NSA_WIDTH

    x2 = x.reshape(M, D)
    pos_col = positions.reshape(M, 1)
    cmp_end = np.arange(S // CMP_STRIDE - 1) * CMP_STRIDE + CMP_LEN - 1
    pos_cmp = jnp.pad(positions[:, cmp_end], ((0, 0), (0, 1)))[:, :, None]
    c8 = jnp.pad(c, ((0, 8 - B), (0, 0)))

    for l in range(depth):
        mod = _ada(c8, w_ada[l], b_ada[l][None, :])
        mod3 = mod[:B].reshape(B, 1, 3 * D)
        h = _norm(x2, norm_gain[l][None, :], mod3, S)

        w_nk = jnp.transpose(w_in[l])
        big_rows, tail_rows, small_rows = _proj_row_offsets(D)
        pbig = _proj(h, w_nk, big_rows, MM_TM, MM_TN, BF16, "proj_big")
        ptail = _proj(h, w_nk, tail_rows, TAIL_TM, TAIL_TN, F32, "proj_tail")
        psmall = _proj(h, w_nk, small_rows, TAIL_TM, 2 * LANES, F32, "proj_small", pieces=2)

        q_t, keys, v_t = _nsa_prep(pbig, ptail, pos_col, off_q // NSA_WIDTH, S)
        kcmp = _compress(ptail, TAIL_CMP_BLOCK, cmp_pos_k[l], w_cmp_k1[l], w_cmp_k2[l], pos_cmp, B, S, True)
        vcmp = _compress(ptail, TAIL_CMP_BLOCK + NSA_GROUPS, cmp_pos_v[l], w_cmp_v1[l], w_cmp_v2[l].T,
                         pos_cmp, B, S, False)
        o_a = _nsa_attn(q_t, kcmp, vcmp, keys, v_t, psmall, pbig, off_z // NSA_WIDTH, B, S)

        cw = conv_w[l]
        dq = _dn_prep(pbig, cw, off_dq, "q", B, S)[0]
        dk, dkt = _dn_prep(pbig, cw, off_dq + DN_WIDTH, "k", B, S)
        dv = _dn_prep(pbig, cw, off_dq + 2 * DN_WIDTH, "v", B, S)[0]
        alog_row = _pad_cols(a_log[l][None, :].astype(F32), LANES)
        dtb_row = _pad_cols(dt_bias[l][None, :].astype(F32), LANES)
        dec, dect = _dn_gate(psmall, alog_row, dtb_row)
        o_b = _dn_scan(dq, dk, dv, dkt, dec, dect, pbig, off_dz // DN_WIDTH,
                       dn_norm_gain[l][None, :], B, S)

        mixed_in = _out1(o_a, o_b, w_proj_a[l], w_proj_b[l], pbig, D)
        mixed = _matmul(mixed_in, w_out[l], BF16, "out2")
        x2 = _final(mixed, x2, mod3, final_gain[None, :], S)
    return x2.reshape(B, S, D)
```

```python
import functools
import math

import numpy as np
import jax
import jax.numpy as jnp
from jax import lax
from jax.experimental import pallas as pl
from jax.experimental.pallas import tpu as pltpu

F32 = jnp.float32
BF16 = jnp.bfloat16

NSA_HEADS = 16
NSA_GROUPS = 2
NSA_HPG = NSA_HEADS // NSA_GROUPS
HEAD_DIM = 128
ROT_DIM = HEAD_DIM // 4
ROT_HALF = ROT_DIM // 2
ROPE_THETA = 500000.0
CMP_LEN = 32
CMP_STRIDE = 16
CMP_HIDDEN = 256
SLC_LEN = 64
SLC_TOPK = 16
WINDOW = 512
Q_BLOCK = 128
NSA_WIDTH = NSA_HEADS * HEAD_DIM
KV_WIDTH = NSA_GROUPS * HEAD_DIM
DN_HEADS = 16
DN_DK = 128
DN_DV = 128
DN_WIDTH = DN_HEADS * DN_DV
CONV_WIDTH = 4
EPS = 1e-6

LANES = 128
VMEM_LIMIT_BYTES = 56 * 1024 * 1024

DN_CHUNK = 128
DN_HEAD_GROUP = 16
SEL_TILE = 512
VT_ROWS = HEAD_DIM + 16
NEG = -1e30


def _cparams(sem):
    return pltpu.CompilerParams(dimension_semantics=sem, vmem_limit_bytes=VMEM_LIMIT_BYTES)


def _sigmoid(x):
    return 1.0 / (1.0 + jnp.exp(-x))


def _silu(x):
    return x * _sigmoid(x)


def _dot(a, b):
    return jnp.dot(a, b, preferred_element_type=F32)


def _dot_nt(a, b):
    return lax.dot_general(a, b, (((1,), (1,)), ((), ())), preferred_element_type=F32)


def _bmm(a, b):
    return lax.dot_general(a, b, (((2,), (1,)), ((0,), (0,))), preferred_element_type=F32)


def _dot_f32(a, b):
    return jnp.dot(a, b, preferred_element_type=F32, precision=lax.Precision.HIGHEST)


def _ada_kernel(c_ref, w_ref, b_ref, o_ref):
    c = c_ref[...]
    c_hi = c.astype(BF16)
    c_lo = (c - c_hi.astype(F32)).astype(BF16)
    w = w_ref[...]
    w_hi = w.astype(BF16)
    w_lo = (w - w_hi.astype(F32)).astype(BF16)
    acc = _dot(c_hi, w_hi) + _dot(c_lo, w_hi) + _dot(c_hi, w_lo)
    o_ref[...] = acc + b_ref[...]


def _ada(c8, w_ada, b_ada):
    D, N = w_ada.shape
    tn = min(512, N)
    return pl.pallas_call(
        _ada_kernel,
        grid=(N // tn,),
        in_specs=[pl.BlockSpec((8, D), lambda j: (0, 0)),
                  pl.BlockSpec((D, tn), lambda j: (0, j)),
                  pl.BlockSpec((1, tn), lambda j: (0, j))],
        out_specs=pl.BlockSpec((8, tn), lambda j: (0, j)),
        out_shape=jax.ShapeDtypeStruct((8, N), F32),
        compiler_params=_cparams(("arbitrary",)),
        name="ada",
    )(c8, w_ada, b_ada)


def _norm_kernel(x_ref, gain_ref, shift_ref, scale_ref, o_ref):
    x = x_ref[...]
    ms = jnp.mean(x * x, axis=-1, keepdims=True)
    y = x * lax.rsqrt(ms + EPS) * gain_ref[...]
    o_ref[...] = (y * (1.0 + scale_ref[0]) + shift_ref[0]).astype(o_ref.dtype)


def _norm(x2, gain, mod3, S):
    M, D = x2.shape
    tm = 256
    nb = S // tm
    return pl.pallas_call(
        _norm_kernel,
        grid=(M // tm,),
        in_specs=[pl.BlockSpec((tm, D), lambda i: (i, 0)),
                  pl.BlockSpec((1, D), lambda i: (0, 0)),
                  pl.BlockSpec((1, 1, D), lambda i: (i // nb, 0, 0)),
                  pl.BlockSpec((1, 1, D), lambda i: (i // nb, 0, 1))],
        out_specs=pl.BlockSpec((tm, D), lambda i: (i, 0)),
        out_shape=jax.ShapeDtypeStruct((M, D), BF16),
        compiler_params=_cparams(("arbitrary",)),
        name="norm",
    )(x2, gain, mod3, mod3)


MM_TM = 1024
MM_TN = 1024
W_ROW_ALIGN = 16
WT_CHUNK = 128
MM_CAST_COLS = 256


def _mm_kernel(offs_ref, a_ref, w_hbm, o_ref, wf32_ref, w16_ref, sems, *, w_is_nk, tn, pieces):
    j = pl.program_id(0)
    i = pl.program_id(1)
    pw = tn // pieces

    def piece_copy(jj, p):
        off = offs_ref[jj * pieces + p]
        if w_is_nk:
            src = w_hbm.at[pl.ds(pl.multiple_of(off * W_ROW_ALIGN, W_ROW_ALIGN), pw), :]
            dst = wf32_ref.at[p * pw:(p + 1) * pw, :]
        else:
            src = w_hbm.at[:, pl.ds(pl.multiple_of(off * LANES, LANES), pw)]
            dst = wf32_ref.at[:, p * pw:(p + 1) * pw]
        return pltpu.make_async_copy(src, dst, sems.at[p])

    def start_tile(jj):
        for p in range(pieces):
            piece_copy(jj, p).start()

    @pl.when((j == 0) & (i == 0))
    def _():
        start_tile(0)

    @pl.when(i == 0)
    def _():
        for p in range(pieces):
            piece_copy(j, p).wait()
        cc = min(MM_CAST_COLS, tn)
        for c0 in range(0, tn, cc):
            if w_is_nk:
                for c in range(c0, c0 + cc, WT_CHUNK):
                    w16_ref[:, c:c + WT_CHUNK] = wf32_ref[c:c + WT_CHUNK, :].T.astype(BF16)
            else:
                w16_ref[:, c0:c0 + cc] = wf32_ref[:, c0:c0 + cc].astype(BF16)
            o_ref[:, c0:c0 + cc] = _dot(a_ref[...], w16_ref[:, c0:c0 + cc]).astype(o_ref.dtype)

        @pl.when(j + 1 < pl.num_programs(0))
        def _():
            start_tile(j + 1)

    @pl.when(i > 0)
    def _():
        o_ref[...] = _dot(a_ref[...], w16_ref[...]).astype(o_ref.dtype)


def _mm_call(a, w, offs, w_is_nk, tm, tn, out_dtype, name, pieces=1):
    M, K = a.shape
    tm = min(tm, M)
    n_tiles = len(offs) // pieces
    grid_spec = pltpu.PrefetchScalarGridSpec(
        num_scalar_prefetch=1,
        grid=(n_tiles, M // tm),
        in_specs=[pl.BlockSpec((tm, K), lambda j, i, o: (i, 0)),
                  pl.BlockSpec(memory_space=pl.ANY)],
        out_specs=pl.BlockSpec((tm, tn), lambda j, i, o: (i, j)),
        scratch_shapes=[pltpu.VMEM((tn, K) if w_is_nk else (K, tn), F32),
                        pltpu.VMEM((K, tn), BF16),
                        pltpu.SemaphoreType.DMA((pieces,))],
    )
    return pl.pallas_call(
        functools.partial(_mm_kernel, w_is_nk=w_is_nk, tn=tn, pieces=pieces),
        grid_spec=grid_spec,
        out_shape=jax.ShapeDtypeStruct((M, n_tiles * tn), out_dtype),
        compiler_params=_cparams(("arbitrary", "arbitrary")),
        name=name,
    )(jnp.asarray(np.asarray(offs, np.int32)), a, w)


def _matmul(a, w, out_dtype, name):
    N = w.shape[1]
    tn = min(MM_TN, N)
    offs = [t * tn // LANES for t in range(N // tn)]
    return _mm_call(a, w, offs, False, MM_TM, tn, out_dtype, name)


def _proj(h, w_nk, row_offsets, tm, tn, out_dtype, name, pieces=1):
    assert all(int(o) % W_ROW_ALIGN == 0 and int(o) + tn // pieces <= w_nk.shape[0] for o in row_offsets)
    offs = [int(o) // W_ROW_ALIGN for o in row_offsets]
    return _mm_call(h, w_nk, offs, True, tm, tn, out_dtype, name, pieces)


def _rope_consts():
    inv = ROPE_THETA ** (-np.arange(ROT_HALF, dtype=np.float64) / ROT_HALF)
    invf = np.zeros((1, LANES), np.float32)
    invf[0, :ROT_HALF] = inv
    invf[0, ROT_HALF:ROT_DIM] = inv
    sgn = np.zeros((1, LANES), np.float32)
    sgn[0, :ROT_HALF] = -1.0
    sgn[0, ROT_HALF:ROT_DIM] = 1.0
    return jnp.asarray(invf), jnp.asarray(sgn)


def _rope_tables(pos_col, invf, sgn):
    ang = pos_col.astype(F32) * invf
    return jnp.cos(ang), jnp.sin(ang) * sgn


def _rope_apply(x, cos_t, sin_t):
    lane = lax.broadcasted_iota(jnp.int32, x.shape, 1)
    partner = jnp.where(lane < ROT_HALF,
                        pltpu.roll(x, LANES - ROT_HALF, 1),
                        pltpu.roll(x, ROT_HALF, 1))
    return x * cos_t + partner * sin_t


KEYS_WIDTH = 6 * HEAD_DIM


def _nsa_prep_kernel(q_ref, kvs_ref, kvw_ref, pos_ref, invf_ref, sgn_ref, qt_ref, keys_ref, vt_ref, *, seq_len):
    tm = q_ref.shape[0]
    cos_t, sin_t = _rope_tables(pos_ref[...], invf_ref[...], sgn_ref[...])
    qscale = (HEAD_DIM ** -0.5) * math.log2(math.e)
    for h in range(NSA_HEADS):
        sl = slice(h * HEAD_DIM, (h + 1) * HEAD_DIM)
        xq = _rope_apply(q_ref[:, sl].astype(F32), cos_t, sin_t) * qscale
        qt_ref[sl, :] = xq.T.astype(qt_ref.dtype)
    tok = (pl.program_id(0) * tm) % seq_len + lax.broadcasted_iota(jnp.int32, (tm, LANES), 0)
    lane = lax.broadcasted_iota(jnp.int32, (tm, LANES), 1)
    onehot = jnp.where(lane == (tok >> int(math.log2(SLC_LEN))), 1.0, 0.0).astype(keys_ref.dtype)
    blk = lambda i: (kvs_ref if i < 4 else kvw_ref)[:, (i % 4) * HEAD_DIM:(i % 4 + 1) * HEAD_DIM]
    for g in range(NSA_GROUPS):
        ks = _rope_apply(blk(g), cos_t, sin_t)
        keys_ref[:, (2 * g) * HEAD_DIM:(2 * g + 1) * HEAD_DIM] = ks.astype(keys_ref.dtype)
        keys_ref[:, (2 * g + 1) * HEAD_DIM:(2 * g + 2) * HEAD_DIM] = onehot
        kw = _rope_apply(blk(4 + g), cos_t, sin_t)
        keys_ref[:, (4 + g) * HEAD_DIM:(5 + g) * HEAD_DIM] = kw.astype(keys_ref.dtype)
        ones = jnp.ones((VT_ROWS - HEAD_DIM, tm), vt_ref.dtype)
        for c, src in ((g, 2 + g), (2 + g, 6 + g)):
            vt_ref[c * VT_ROWS:c * VT_ROWS + HEAD_DIM, :] = blk(src).T.astype(vt_ref.dtype)
            vt_ref[c * VT_ROWS + HEAD_DIM:(c + 1) * VT_ROWS, :] = ones


def _nsa_prep(pbig, ptail, pos_col, q_col_block, S):
    M = pbig.shape[0]
    tm = 256
    invf, sgn = _rope_consts()
    return pl.pallas_call(
        functools.partial(_nsa_prep_kernel, seq_len=S),
        grid=(M // tm,),
        in_specs=[pl.BlockSpec((tm, NSA_WIDTH), lambda i: (i, q_col_block)),
                  pl.BlockSpec((tm, 4 * HEAD_DIM), lambda i: (i, 1)),
                  pl.BlockSpec((tm, 4 * HEAD_DIM), lambda i: (i, 2)),
                  pl.BlockSpec((tm, 1), lambda i: (i, 0)),
                  pl.BlockSpec((1, LANES), lambda i: (0, 0)),
                  pl.BlockSpec((1, LANES), lambda i: (0, 0))],
        out_specs=[pl.BlockSpec((NSA_WIDTH, tm), lambda i: (0, i)),
                   pl.BlockSpec((tm, KEYS_WIDTH), lambda i: (i, 0)),
                   pl.BlockSpec((4 * VT_ROWS, tm), lambda i: (0, i))],
        out_shape=[jax.ShapeDtypeStruct((NSA_WIDTH, M), BF16),
                   jax.ShapeDtypeStruct((M, KEYS_WIDTH), BF16),
                   jax.ShapeDtypeStruct((4 * VT_ROWS, M), BF16)],
        compiler_params=_cparams(("arbitrary",)),
        name="nsa_prep",
    )(pbig, ptail, ptail, pos_col, invf, sgn)


def _compress_kernel(x_ref, pe_ref, w1_ref, w2_ref, pos_ref, invf_ref, sgn_ref, o_ref, *, rope, n_rows):
    half = CMP_LEN // 2
    top = jnp.zeros((n_rows, CMP_HIDDEN), F32)
    bot = jnp.zeros((n_rows, CMP_HIDDEN), F32)
    for l in range(half):
        xl = x_ref[pl.ds(l, n_rows, stride=CMP_STRIDE), :]
        w_top = w1_ref[l * HEAD_DIM:(l + 1) * HEAD_DIM, :].astype(BF16)
        w_bot = w1_ref[(half + l) * HEAD_DIM:(half + l + 1) * HEAD_DIM, :].astype(BF16)
        top = top + _dot((xl + pe_ref[l:l + 1, :]).astype(BF16), w_top)
        bot = bot + _dot((xl + pe_ref[half + l:half + l + 1, :]).astype(BF16), w_bot)
    hid = top + pltpu.roll(bot, n_rows - 1, 0)
    act = _silu(hid).astype(BF16)
    if rope:
        out = _dot(act, w2_ref[...].astype(BF16))
        cos_t, sin_t = _rope_tables(pos_ref[0], invf_ref[...], sgn_ref[...])
        o_ref[0, 0] = _rope_apply(out, cos_t, sin_t).astype(o_ref.dtype)
    else:
        o_ref[0, 0, 0:HEAD_DIM, :] = _dot_nt(w2_ref[...].astype(BF16), act).astype(o_ref.dtype)
        o_ref[0, 0, HEAD_DIM:VT_ROWS, :] = jnp.ones((VT_ROWS - HEAD_DIM, n_rows), o_ref.dtype)


def _compress(ptail, col_block0, pe, w1, w2, pos_cmp, B, S, rope):
    n_rows = S // CMP_STRIDE
    invf, sgn = _rope_consts()
    kern = functools.partial(_compress_kernel, rope=rope, n_rows=n_rows)
    out_dims = (n_rows, HEAD_DIM) if rope else (VT_ROWS, n_rows)
    return pl.pallas_call(
        kern,
        grid=(B, NSA_GROUPS),
        in_specs=[pl.BlockSpec((S, HEAD_DIM), lambda b, g: (b, col_block0 + g)),
                  pl.BlockSpec((CMP_LEN, HEAD_DIM), lambda b, g: (0, 0)),
                  pl.BlockSpec((CMP_LEN * HEAD_DIM, CMP_HIDDEN), lambda b, g: (0, 0)),
                  pl.BlockSpec(w2.shape, lambda b, g: (0, 0)),
                  pl.BlockSpec((1, n_rows, 1), lambda b, g: (b, 0, 0)),
                  pl.BlockSpec((1, LANES), lambda b, g: (0, 0)),
                  pl.BlockSpec((1, LANES), lambda b, g: (0, 0))],
        out_specs=pl.BlockSpec((1, 1) + out_dims, lambda b, g: (b, g, 0, 0)),
        out_shape=jax.ShapeDtypeStruct((B, NSA_GROUPS) + out_dims, BF16),
        compiler_params=_cparams(("arbitrary", "arbitrary")),
        name="compress_k" if rope else "compress_v",
    )(ptail, pe, w1, w2, pos_cmp, invf, sgn)


def _tile_lanes(x, n):
    return jnp.concatenate([x] * n, axis=1)


def _nsa_attn_kernel(qt_ref, kc_ref, vct_ref, ks_ref, vst_ref, kw_ref, vwt_ref, g_ref, z_ref,
                     ovt_ref, o_ref, m_s, acc_s, qa_s, sa_s, sb_s, sc_s, *, seq_len):
    T = Q_BLOCK
    P = NSA_HPG
    R = P * T
    n_cmp_pad = seq_len // CMP_STRIDE
    n_cmp = n_cmp_pad - 1
    nb = seq_len // SLC_LEN
    qi = pl.program_id(2)
    qs = qi * T
    d0 = pl.multiple_of(qs, T)
    wlen = WINDOW + T
    w0 = pl.multiple_of(jnp.maximum(qs - WINDOW, 0), T)

    qt = jnp.concatenate([qt_ref[p * HEAD_DIM:(p + 1) * HEAD_DIM, :] for p in range(P)], axis=1)

    s_c = _dot(kc_ref[0, 0], qt)
    s_w = _dot(kw_ref[pl.ds(w0, wlen), :], qt)
    s_d = _dot(ks_ref[pl.ds(d0, T), 0:HEAD_DIM], qt)

    tok = lambda n: qs + lax.broadcasted_iota(jnp.int32, (n, T), 1)
    row = lambda n: lax.broadcasted_iota(jnp.int32, (n, T), 0)
    ok_c = (row(n_cmp_pad) * CMP_STRIDE + (CMP_LEN - 1) <= tok(n_cmp_pad)) & (row(n_cmp_pad) < n_cmp)
    key_w = w0 + row(wlen)
    ok_w = (key_w <= tok(wlen)) & (tok(wlen) - key_w < WINDOW)
    ok_d = row(T) <= lax.broadcasted_iota(jnp.int32, (T, T), 1)
    addmask = lambda ok: _tile_lanes(jnp.where(ok, 0.0, NEG).astype(F32), P)
    s_c = s_c + addmask(ok_c)
    s_w = s_w + addmask(ok_w)
    s_d = s_d + addmask(ok_d)

    m_c = jnp.max(s_c, axis=0, keepdims=True)
    m_w = jnp.max(s_w, axis=0, keepdims=True)
    m_d = jnp.max(s_d, axis=0, keepdims=True)
    m_c = jnp.where(m_c > 0.5 * NEG, m_c, 0.0)
    p_c = jnp.exp2(s_c - m_c)
    p_w = jnp.exp2(s_w - m_w)
    p_d = jnp.exp2(s_d - m_d)
    o_c = _dot(vct_ref[0, 0], p_c.astype(BF16))
    o_w = _dot(vwt_ref[:, pl.ds(w0, wlen)], p_w.astype(BF16))
    inv_c = 1.0 / jnp.maximum(jnp.sum(p_c, axis=0, keepdims=True), 1e-30)
    inv_w = 1.0 / o_w[HEAD_DIM:HEAD_DIM + 1, :]
    m_s[...] = m_d
    acc_s[...] = _dot(vst_ref[:, pl.ds(d0, T)], p_d.astype(BF16))

    pn = p_c * inv_c
    p_sum = pn[:, 0:T]
    for p in range(1, P):
        p_sum = p_sum + pn[:, p * T:(p + 1) * T]
    ps_hi = p_sum.astype(BF16)
    ps_lo = (p_sum - ps_hi.astype(F32)).astype(BF16)
    imp = (_dot(ovt_ref[...], ps_hi) + _dot(ovt_ref[...], ps_lo))[0:nb]
    jrow = row(nb)
    t_lane = tok(nb)
    tb = t_lane >> int(math.log2(SLC_LEN))
    visible = jrow * SLC_LEN <= t_lane
    forced = (jrow == 0) | (jrow == tb) | (jrow == tb - 1)
    score = jnp.where(forced, 1e9, jnp.where(visible, imp, -jnp.inf))
    cnt = jnp.zeros((nb, T), F32)
    for k in range(nb):
        rk = score[k:k + 1, :]
        cnt = cnt + jnp.where(jrow > k, jnp.where(rk >= score, 1.0, 0.0), jnp.where(rk > score, 1.0, 0.0))
    keep = (cnt < float(min(SLC_TOPK, nb))) & visible & (jrow < 2 * qi)
    bias = jnp.where(keep, 0.0, NEG).astype(F32)
    bias = jnp.concatenate([bias, jnp.full((LANES - nb, T), NEG, F32)], axis=0) if nb < LANES else bias
    qa_s[0:HEAD_DIM, :] = qt
    qa_s[HEAD_DIM:2 * HEAD_DIM, :] = _tile_lanes(bias.astype(BF16), P)

    n_t = (qs + SEL_TILE - 1) // SEL_TILE

    def scores_into(kt, dst):
        dst[...] = _dot(ks_ref[kt * SEL_TILE:(kt + 1) * SEL_TILE, :], qa_s[...])

    def consume(kt, src):
        k0 = kt * SEL_TILE
        s = src[...]
        m_prev = m_s[...]
        m_new = jnp.maximum(m_prev, jnp.max(s, axis=0, keepdims=True))
        alpha = jnp.exp2(m_prev - m_new)
        p = jnp.exp2(s - m_new)
        acc_s[...] = alpha * acc_s[...] + _dot(vst_ref[:, pl.ds(k0, SEL_TILE)], p.astype(BF16))
        m_s[...] = m_new

    bufs = (sa_s, sb_s, sc_s)
    for r in range(1, seq_len // SEL_TILE + 1):
        @pl.when(n_t == r)
        def _(r=r):
            scores_into(0, bufs[0])
            for t in range(r):
                if t + 1 < r:
                    scores_into(t + 1, bufs[(t + 1) % len(bufs)])
                consume(t, bufs[t % len(bufs)])

    gates_t = _sigmoid(g_ref[...]).T
    first_group = pl.program_id(1) == 0

    def grow(r):
        rows = []
        for p in range(P):
            h0 = r * NSA_HEADS + p
            rows.append(jnp.where(first_group, gates_t[h0:h0 + 1, :], gates_t[h0 + P:h0 + P + 1, :]))
        return jnp.concatenate(rows, axis=1)

    o_s = acc_s[...]
    hd = slice(0, HEAD_DIM)
    o_t = (o_c[hd] * (grow(0) * inv_c) + o_s[hd] * (grow(1) / o_s[HEAD_DIM:HEAD_DIM + 1, :])
           + o_w[hd] * (grow(2) * inv_w))
    for p in range(P):
        sl = slice(p * HEAD_DIM, (p + 1) * HEAD_DIM)
        o_p = o_t[:, p * T:(p + 1) * T].T
        o_ref[:, sl] = (o_p * _silu(z_ref[:, sl].astype(F32))).astype(o_ref.dtype)


def _nsa_consts(S):
    n_cmp_pad = S // CMP_STRIDE
    nb = S // SLC_LEN
    cmp_start = np.arange(n_cmp_pad) * CMP_STRIDE
    slc_start = np.arange(LANES) * SLC_LEN
    ov = ((cmp_start[:, None] <= slc_start[None, :] + SLC_LEN - 1)
          & (cmp_start[:, None] + CMP_LEN - 1 >= slc_start[None, :])
          & (np.arange(LANES)[None, :] < nb)
          & (np.arange(n_cmp_pad)[:, None] < n_cmp_pad - 1)).astype(np.float32)
    return jnp.asarray(ov.T, dtype=BF16)


def _nsa_attn(q_t, kcmp, vcmp_t, keys, v_t, psmall, pbig, z_col_block, B, S):
    M = B * S
    nq = S // Q_BLOCK
    gw = NSA_HPG * HEAD_DIM
    ovt = _nsa_consts(S)
    n_cmp_pad = S // CMP_STRIDE
    R = NSA_HPG * Q_BLOCK
    kern = functools.partial(_nsa_attn_kernel, seq_len=S)
    return pl.pallas_call(
        kern,
        grid=(B, NSA_GROUPS, nq),
        in_specs=[pl.BlockSpec((gw, Q_BLOCK), lambda b, g, i: (g, b * nq + i)),
                  pl.BlockSpec((1, 1, n_cmp_pad, HEAD_DIM), lambda b, g, i: (b, g, 0, 0)),
                  pl.BlockSpec((1, 1, VT_ROWS, n_cmp_pad), lambda b, g, i: (b, g, 0, 0)),
                  pl.BlockSpec((S, 2 * HEAD_DIM), lambda b, g, i: (b, g)),
                  pl.BlockSpec((VT_ROWS, S), lambda b, g, i: (g, b)),
                  pl.BlockSpec((S, HEAD_DIM), lambda b, g, i: (b, 4 + g)),
                  pl.BlockSpec((VT_ROWS, S), lambda b, g, i: (2 + g, b)),
                  pl.BlockSpec((Q_BLOCK, LANES), lambda b, g, i: (b * nq + i, SMALL_GATE_BLOCK)),
                  pl.BlockSpec((Q_BLOCK, gw), lambda b, g, i: (b * nq + i, z_col_block * NSA_GROUPS + g)),
                  pl.BlockSpec((LANES, n_cmp_pad), lambda b, g, i: (0, 0))],
        out_specs=pl.BlockSpec((Q_BLOCK, gw), lambda b, g, i: (b * nq + i, g)),
        out_shape=jax.ShapeDtypeStruct((M, NSA_WIDTH), BF16),
        scratch_shapes=[pltpu.VMEM((1, R), F32),
                        pltpu.VMEM((VT_ROWS, R), F32),
                        pltpu.VMEM((2 * HEAD_DIM, R), BF16),
                        pltpu.VMEM((SEL_TILE, R), F32),
                        pltpu.VMEM((SEL_TILE, R), F32),
                        pltpu.VMEM((SEL_TILE, R), F32)],
        compiler_params=_cparams(("arbitrary", "arbitrary", "arbitrary")),
        name="nsa_attn",
    )(q_t, kcmp, vcmp_t, keys, v_t, keys, v_t, psmall, pbig, ovt)


def _dn_prep_kernel(x_ref, w_ref, o_ref, *rest, mode, seq_len):
    xpad_ref = rest[-1]
    w = w_ref[...]
    xpad_ref[0:CONV_PAD, :] = jnp.zeros((CONV_PAD, xpad_ref.shape[1]), F32)
    xpad_ref[CONV_PAD:, :] = x_ref[...].astype(F32)
    y = xpad_ref[CONV_PAD:, :] * w[CONV_WIDTH - 1:CONV_WIDTH, :]
    for k in range(1, CONV_WIDTH):
        y = y + xpad_ref[pl.ds(CONV_PAD - k, seq_len), :] * w[CONV_WIDTH - 1 - k:CONV_WIDTH - k, :]
    y = _silu(y)
    if mode in ("q", "k"):
        outs = []
        for h in range(y.shape[1] // DN_DK):
            yh = y[:, h * DN_DK:(h + 1) * DN_DK]
            ss = jnp.sum(yh * yh, axis=-1, keepdims=True)
            yh = yh * lax.rsqrt(ss + EPS)
            if mode == "q":
                yh = yh * (DN_DK ** -0.5)
            outs.append(yh)
        y = jnp.concatenate(outs, axis=1)
    o_ref[...] = y.astype(o_ref.dtype)
    if mode == "k":
        kt_ref = rest[0]
        kt_ref[...] = y.T.astype(kt_ref.dtype)


DN_PREP_COLS = 128
CONV_PAD = 8


def _dn_prep(pbig, conv_w, col0, mode, B, S):
    M = B * S
    tw = DN_PREP_COLS
    col0_blocks = col0 // tw
    nj = DN_WIDTH // tw
    wcol0 = {"q": 0, "k": nj, "v": 2 * nj}[mode]
    kern = functools.partial(_dn_prep_kernel, mode=mode, seq_len=S)
    out_specs = [pl.BlockSpec((S, tw), lambda b, j: (b, j))]
    out_shape = [jax.ShapeDtypeStruct((M, DN_WIDTH), BF16)]
    if mode == "k":
        out_specs.append(pl.BlockSpec((tw, S), lambda b, j: (j, b)))
        out_shape.append(jax.ShapeDtypeStruct((DN_WIDTH, M), BF16))
    return pl.pallas_call(
        kern,
        grid=(B, nj),
        in_specs=[pl.BlockSpec((S, tw), lambda b, j: (b, col0_blocks + j)),
                  pl.BlockSpec((CONV_WIDTH, tw), lambda b, j: (0, wcol0 + j))],
        out_specs=out_specs,
        out_shape=out_shape,
        scratch_shapes=[pltpu.VMEM((CONV_PAD + S, tw), F32)],
        compiler_params=_cparams(("arbitrary", "arbitrary")),
        name="dn_prep_" + mode,
    )(pbig, conv_w)


def _dn_gate_kernel(ab_ref, alog_ref, dtb_ref, o_ref, ot_ref, *, tm):
    ab = ab_ref[...]
    x = ab + dtb_ref[...]
    softplus = jnp.maximum(x, 0.0) + jnp.log(1.0 + jnp.exp(-jnp.abs(x)))
    g = -jnp.exp(alog_ref[...]) * softplus
    beta = _sigmoid(ab)
    lane = lax.broadcasted_iota(jnp.int32, (DN_CHUNK, LANES), 1)
    r = lax.broadcasted_iota(jnp.int32, (DN_CHUNK, DN_CHUNK), 0)
    c = lax.broadcasted_iota(jnp.int32, (DN_CHUNK, DN_CHUNK), 1)
    tril = jnp.where(r >= c, 1.0, 0.0).astype(F32)
    for ci in range(tm // DN_CHUNK):
        rows = slice(ci * DN_CHUNK, (ci + 1) * DN_CHUNK)
        dec = _dot_f32(tril, g[rows])
        out = jnp.where(lane < DN_HEADS, dec, beta[rows])
        o_ref[rows, :] = out
        ot_ref[:, rows] = out.T


def _dn_gate(psmall, alog_row, dtb_row):
    M = psmall.shape[0]
    tm = 512
    kern = functools.partial(_dn_gate_kernel, tm=tm)
    return pl.pallas_call(
        kern,
        grid=(M // tm,),
        in_specs=[pl.BlockSpec((tm, LANES), lambda i: (i, SMALL_AB_BLOCK)),
                  pl.BlockSpec((1, LANES), lambda i: (0, 0)),
                  pl.BlockSpec((1, LANES), lambda i: (0, 0))],
        out_specs=[pl.BlockSpec((tm, LANES), lambda i: (i, 0)),
                   pl.BlockSpec((LANES, tm), lambda i: (0, i))],
        out_shape=[jax.ShapeDtypeStruct((M, LANES), F32),
                   jax.ShapeDtypeStruct((LANES, M), F32)],
        compiler_params=_cparams(("arbitrary",)),
        name="dn_gate",
    )(psmall, alog_row, dtb_row)


def _dn_scan_kernel(q_ref, k_ref, v_ref, kt_ref, dec_ref, dect_ref, z_ref, gain_ref, o_ref, st_ref):
    C = DN_CHUNK

    @pl.when(pl.program_id(1) == 0)
    def _():
        st_ref[...] = jnp.zeros(st_ref.shape, F32)

    dec = dec_ref[...]
    dect = dect_ref[...]
    r = lax.broadcasted_iota(jnp.int32, (C, C), 0)
    c = lax.broadcasted_iota(jnp.int32, (C, C), 1)
    tril = r >= c
    strict = r > c
    gain = gain_ref[...]
    eye = jnp.where(r == c, 1.0, 0.0).astype(F32)
    lvl_masks = []
    for lg in range(int(math.log2(C))):
        lvl_masks.append(((r >> (lg + 1)) == (c >> (lg + 1))) & ((r >> lg) != (c >> lg)))

    for h0 in range(0, DN_HEADS, DN_HEAD_GROUP):
        heads = range(h0, h0 + DN_HEAD_GROUP)
        cols = lambda ref: jnp.stack([ref[:, h * DN_DK:(h + 1) * DN_DK] for h in heads])
        qh = cols(q_ref).astype(F32)
        kh = cols(k_ref).astype(F32)
        vh = cols(v_ref).astype(F32)
        kth = jnp.stack([kt_ref[h * DN_DK:(h + 1) * DN_DK, :] for h in heads])
        dcol = jnp.stack([jnp.broadcast_to(dec[:, h:h + 1], (C, C)) for h in heads])
        bcol = jnp.stack([jnp.broadcast_to(dec[:, DN_HEADS + h:DN_HEADS + h + 1], (C, C)) for h in heads])
        drow = jnp.stack([jnp.broadcast_to(dect[h:h + 1, :], (C, C)) for h in heads])
        dlast = jnp.stack([jnp.broadcast_to(dect[h:h + 1, C - 1:C], (C, C)) for h in heads])
        lmat = jnp.exp(jnp.where(tril[None], dcol - drow, NEG))
        e_d = jnp.exp(dcol)
        kb = kh * bcol
        vb = vh * bcol
        a = jnp.where(strict[None], _bmm(kb.astype(BF16), kth) * lmat, 0.0)
        attn = jnp.where(tril[None], _bmm(qh.astype(BF16), kth) * lmat, 0.0)
        tinv = eye[None] - jnp.where(lvl_masks[0][None], a, 0.0)
        for lm in lvl_masks[1:]:
            t16 = tinv.astype(BF16)
            lo = jnp.where(lm[None], a, 0.0).astype(BF16)
            tinv = tinv - _bmm(t16, _bmm(lo, t16).astype(BF16))
        t16 = tinv.astype(BF16)
        u = _bmm(t16, vb.astype(BF16))
        w = _bmm(t16, (kb * e_d).astype(BF16))
        st = st_ref[h0:h0 + DN_HEAD_GROUP]
        st16 = st.astype(BF16)
        v_new = u - _bmm(w.astype(BF16), st16)
        v_new16 = v_new.astype(BF16)
        o = _bmm((qh * e_d).astype(BF16), st16) + _bmm(attn.astype(BF16), v_new16)
        kdt = (kth.astype(F32) * jnp.exp(dlast - drow)).astype(BF16)
        st_ref[h0:h0 + DN_HEAD_GROUP] = st * jnp.exp(dlast) + _bmm(kdt, v_new16)
        ms = jnp.mean(o * o, axis=-1, keepdims=True)
        y = o * lax.rsqrt(ms + EPS) * gain[None]
        for i, h in enumerate(heads):
            sl = slice(h * DN_DV, (h + 1) * DN_DV)
            o_ref[:, sl] = (y[i] * _silu(z_ref[:, sl].astype(F32))).astype(o_ref.dtype)


def _dn_scan(dq, dk, dv, dkt, dec, dect, pbig, z_col_block, gain_row, B, S):
    M = B * S
    C = DN_CHUNK
    nc = S // C
    tok = lambda b, n: (b * nc + n, 0)
    return pl.pallas_call(
        _dn_scan_kernel,
        grid=(B, nc),
        in_specs=[pl.BlockSpec((C, DN_WIDTH), tok),
                  pl.BlockSpec((C, DN_WIDTH), tok),
                  pl.BlockSpec((C, DN_WIDTH), tok),
                  pl.BlockSpec((DN_WIDTH, C), lambda b, n: (0, b * nc + n)),
                  pl.BlockSpec((C, LANES), tok),
                  pl.BlockSpec((LANES, C), lambda b, n: (0, b * nc + n)),
                  pl.BlockSpec((C, DN_WIDTH), lambda b, n: (b * nc + n, z_col_block)),
                  pl.BlockSpec((1, DN_DV), lambda b, n: (0, 0))],
        out_specs=pl.BlockSpec((C, DN_WIDTH), tok),
        out_shape=jax.ShapeDtypeStruct((M, DN_WIDTH), BF16),
        scratch_shapes=[pltpu.VMEM((DN_HEADS, DN_DK, DN_DV), F32)],
        compiler_params=_cparams(("arbitrary", "arbitrary")),
        name="dn_scan",
    )(dq, dk, dv, dkt, dec, dect, pbig, gain_row)


def _out1_kernel(oa_ref, ob_ref, wa_ref, wb_ref, ga_ref, gb_ref, o_ref, wa16_ref, wb16_ref):
    @pl.when(pl.program_id(1) == 0)
    def _():
        wa16_ref[...] = wa_ref[...].astype(BF16)
        wb16_ref[...] = wb_ref[...].astype(BF16)

    ya = _dot(oa_ref[...], wa16_ref[...])
    yb = _dot(ob_ref[...], wb16_ref[...])
    mix = _sigmoid(ga_ref[...].astype(F32)) * ya + _sigmoid(gb_ref[...].astype(F32)) * yb
    o_ref[...] = mix.astype(o_ref.dtype)


OUT1_TM = 512


def _out1(o_a, o_b, wa, wb, pbig, D):
    M, K = o_a.shape
    tm = min(OUT1_TM, M)
    tn = min(MM_TN, D)
    nbd = D // tn
    return pl.pallas_call(
        _out1_kernel,
        grid=(D // tn, M // tm),
        in_specs=[pl.BlockSpec((tm, K), lambda j, i: (i, 0)),
                  pl.BlockSpec((tm, K), lambda j, i: (i, 0)),
                  pl.BlockSpec((K, tn), lambda j, i: (0, j), pipeline_mode=pl.Buffered(1)),
                  pl.BlockSpec((K, tn), lambda j, i: (0, j), pipeline_mode=pl.Buffered(1)),
                  pl.BlockSpec((tm, tn), lambda j, i: (i, j)),
                  pl.BlockSpec((tm, tn), lambda j, i: (i, nbd + j))],
        out_specs=pl.BlockSpec((tm, tn), lambda j, i: (i, j)),
        out_shape=jax.ShapeDtypeStruct((M, D), BF16),
        scratch_shapes=[pltpu.VMEM((K, tn), BF16), pltpu.VMEM((K, tn), BF16)],
        compiler_params=_cparams(("arbitrary", "arbitrary")),
        name="out1",
    )(o_a, o_b, wa, wb, pbig, pbig)


def _final_kernel(mix_ref, x_ref, gate_ref, fg_ref, o_ref):
    xn = x_ref[...] + gate_ref[0] * mix_ref[...].astype(F32)
    ms = jnp.mean(xn * xn, axis=-1, keepdims=True)
    o_ref[...] = xn * lax.rsqrt(ms + EPS) * fg_ref[...]


def _final(mixed, x2, mod3, final_gain, S):
    M, D = x2.shape
    tm = 256
    nb = S // tm
    return pl.pallas_call(
        _final_kernel,
        grid=(M // tm,),
        in_specs=[pl.BlockSpec((tm, D), lambda i: (i, 0)),
                  pl.BlockSpec((tm, D), lambda i: (i, 0)),
                  pl.BlockSpec((1, 1, D), lambda i: (i // nb, 0, 2)),
                  pl.BlockSpec((1, D), lambda i: (0, 0))],
        out_specs=pl.BlockSpec((tm, D), lambda i: (i, 0)),
        out_shape=jax.ShapeDtypeStruct((M, D), F32),
        compiler_params=_cparams(("arbitrary",)),
        name="final",
    )(mixed, x2, mod3, final_gain)


def _pad_cols(w, width):
    return jnp.pad(w, ((0, 0), (0, width - w.shape[1])))


def _proj_row_offsets(D):
    widths = (NSA_WIDTH, 6 * KV_WIDTH, 3 * NSA_HEADS, NSA_WIDTH, 3 * DN_WIDTH, DN_HEADS, DN_HEADS, DN_WIDTH, 2 * D)
    offs = [int(o) for o in np.concatenate([[0], np.cumsum(widths)])]
    tiles = lambda seg: [offs[seg] + t * MM_TN for t in range(widths[seg] // MM_TN)]
    big = tiles(8) + tiles(4) + tiles(0) + tiles(3) + tiles(7)
    kv0 = offs[1]
    tail = [kv0, kv0 + TAIL_TN]
    small = [offs[2], offs[5]]
    return big, tail, small


TAIL_TN = 3 * KV_WIDTH
TAIL_TM = 1024
TAIL_CMP_BLOCK = 0
SMALL_GATE_BLOCK = 0
SMALL_AB_BLOCK = 1


def kernel(x, c, positions, w_ada, b_ada, norm_gain, w_in, cmp_pos_k, cmp_pos_v, w_cmp_k1, w_cmp_k2,
           w_cmp_v1, w_cmp_v2, conv_w, dt_bias, a_log, dn_norm_gain, w_proj_a, w_proj_b, w_out, final_gain):
    B, S, D = x.shape
    M = B * S
    depth = w_in.shape[0]
    assert S % SEL_TILE == 0 and S % DN_CHUNK == 0 and S >= WINDOW + Q_BLOCK and D % 512 == 0 and B <= 8
    assert S // SLC_LEN <= LANES and D % 1024 == 0 and DN_DK == DN_CHUNK
    assert depth == 1, "the final RMSNorm is fused into the last layer's output kernel"

    off_dq = 2 * D
    off_q = off_dq + 3 * DN_WIDTH
    off_z = off_q + NSA_WIDTH
    off_dz = off_z + NSA_WIDTH

    x2 = x.reshape(M, D)
    pos_col = positions.reshape(M, 1)
    cmp_end = np.arange(S // CMP_STRIDE - 1) * CMP_STRIDE + CMP_LEN - 1
    pos_cmp = jnp.pad(positions[:, cmp_end], ((0, 0), (0, 1)))[:, :, None]
    c8 = jnp.pad(c, ((0, 8 - B), (0, 0)))

    for l in range(depth):
        mod = _ada(c8, w_ada[l], b_ada[l][None, :])
        mod3 = mod[:B].reshape(B, 1, 3 * D)
        h = _norm(x2, norm_gain[l][None, :], mod3, S)

        w_nk = jnp.transpose(w_in[l])
        big_rows, tail_rows, small_rows = _proj_row_offsets(D)
        pbig = _proj(h, w_nk, big_rows, MM_TM, MM_TN, BF16, "proj_big")
        ptail = _proj(h, w_nk, tail_rows, TAIL_TM, TAIL_TN, F32, "proj_tail")
        psmall = _proj(h, w_nk, small_rows, TAIL_TM, 2 * LANES, F32, "proj_small", pieces=2)

        q_t, keys, v_t = _nsa_prep(pbig, ptail, pos_col, off_q // NSA_WIDTH, S)
        kcmp = _compress(ptail, TAIL_CMP_BLOCK, cmp_pos_k[l], w_cmp_k1[l], w_cmp_k2[l], pos_cmp, B, S, True)
        vcmp = _compress(ptail, TAIL_CMP_BLOCK + NSA_GROUPS, cmp_pos_v[l], w_cmp_v1[l], w_cmp_v2[l].T,
                         pos_cmp, B, S, False)
        o_a = _nsa_attn(q_t, kcmp, vcmp, keys, v_t, psmall, pbig, off_z // NSA_WIDTH, B, S)

        cw = conv_w[l]
        dq = _dn_prep(pbig, cw, off_dq, "q", B, S)[0]
        dk, dkt = _dn_prep(pbig, cw, off_dq + DN_WIDTH, "k", B, S)
        dv = _dn_prep(pbig, cw, off_dq + 2 * DN_WIDTH, "v", B, S)[0]
        alog_row = _pad_cols(a_log[l][None, :].astype(F32), LANES)
        dtb_row = _pad_cols(dt_bias[l][None, :].astype(F32), LANES)
        dec, dect = _dn_gate(psmall, alog_row, dtb_row)
        o_b = _dn_scan(dq, dk, dv, dkt, dec, dect, pbig, off_dz // DN_WIDTH,
                       dn_norm_gain[l][None, :], B, S)

        mixed_in = _out1(o_a, o_b, w_proj_a[l], w_proj_b[l], pbig, D)
        mixed = _matmul(mixed_in, w_out[l], BF16, "out2")
        x2 = _final(mixed, x2, mod3, final_gain[None, :], S)
    return x2.reshape(B, S, D)
```

```python
import functools
import math

import numpy as np
import jax
import jax.numpy as jnp
from jax import lax
from jax.experimental import pallas as pl
from jax.experimental.pallas import tpu as pltpu

F32 = jnp.float32
BF16 = jnp.bfloat16

NSA_HEADS = 16
NSA_GROUPS = 2
NSA_HPG = NSA_HEADS // NSA_GROUPS
HEAD_DIM = 128
ROT_DIM = HEAD_DIM // 4
ROT_HALF = ROT_DIM // 2
ROPE_THETA = 500000.0
CMP_LEN = 32
CMP_STRIDE = 16
CMP_HIDDEN = 256
SLC_LEN = 64
SLC_TOPK = 16
WINDOW = 512
Q_BLOCK = 128
NSA_WIDTH = NSA_HEADS * HEAD_DIM
KV_WIDTH = NSA_GROUPS * HEAD_DIM
DN_HEADS = 16
DN_DK = 128
DN_DV = 128
DN_WIDTH = DN_HEADS * DN_DV
CONV_WIDTH = 4
EPS = 1e-6

LANES = 128
VMEM_LIMIT_BYTES = 56 * 1024 * 1024

DN_CHUNK = 128
DN_HEAD_GROUP = 16
SEL_TILE = 512
VT_ROWS = HEAD_DIM + 16
NEG = -1e30


def _cparams(sem):
    return pltpu.CompilerParams(dimension_semantics=sem, vmem_limit_bytes=VMEM_LIMIT_BYTES)


def _sigmoid(x):
    return 1.0 / (1.0 + jnp.exp(-x))


def _silu(x):
    return x * _sigmoid(x)


def _dot(a, b):
    return jnp.dot(a, b, preferred_element_type=F32)


def _dot_nt(a, b):
    return lax.dot_general(a, b, (((1,), (1,)), ((), ())), preferred_element_type=F32)


def _bmm(a, b):
    return lax.dot_general(a, b, (((2,), (1,)), ((0,), (0,))), preferred_element_type=F32)


def _dot_f32(a, b):
    return jnp.dot(a, b, preferred_element_type=F32, precision=lax.Precision.HIGHEST)


def _ada_kernel(c_ref, w_ref, b_ref, o_ref):
    c = c_ref[...]
    c_hi = c.astype(BF16)
    c_lo = (c - c_hi.astype(F32)).astype(BF16)
    w = w_ref[...]
    w_hi = w.astype(BF16)
    w_lo = (w - w_hi.astype(F32)).astype(BF16)
    acc = _dot(c_hi, w_hi) + _dot(c_lo, w_hi) + _dot(c_hi, w_lo)
    o_ref[...] = acc + b_ref[...]


def _ada(c8, w_ada, b_ada):
    D, N = w_ada.shape
    tn = min(512, N)
    return pl.pallas_call(
        _ada_kernel,
        grid=(N // tn,),
        in_specs=[pl.BlockSpec((8, D), lambda j: (0, 0)),
                  pl.BlockSpec((D, tn), lambda j: (0, j)),
                  pl.BlockSpec((1, tn), lambda j: (0, j))],
        out_specs=pl.BlockSpec((8, tn), lambda j: (0, j)),
        out_shape=jax.ShapeDtypeStruct((8, N), F32),
        compiler_params=_cparams(("arbitrary",)),
        name="ada",
    )(c8, w_ada, b_ada)


def _norm_kernel(x_ref, gain_ref, shift_ref, scale_ref, o_ref):
    x = x_ref[...]
    ms = jnp.mean(x * x, axis=-1, keepdims=True)
    y = x * lax.rsqrt(ms + EPS) * gain_ref[...]
    o_ref[...] = (y * (1.0 + scale_ref[0]) + shift_ref[0]).astype(o_ref.dtype)


def _norm(x2, gain, mod3, S):
    M, D = x2.shape
    tm = 256
    nb = S // tm
    return pl.pallas_call(
        _norm_kernel,
        grid=(M // tm,),
        in_specs=[pl.BlockSpec((tm, D), lambda i: (i, 0)),
                  pl.BlockSpec((1, D), lambda i: (0, 0)),
                  pl.BlockSpec((1, 1, D), lambda i: (i // nb, 0, 0)),
                  pl.BlockSpec((1, 1, D), lambda i: (i // nb, 0, 1))],
        out_specs=pl.BlockSpec((tm, D), lambda i: (i, 0)),
        out_shape=jax.ShapeDtypeStruct((M, D), BF16),
        compiler_params=_cparams(("arbitrary",)),
        name="norm",
    )(x2, gain, mod3, mod3)


MM_TM = 1024
MM_TN = 1024
W_ROW_ALIGN = 16
WT_CHUNK = 128
MM_CAST_COLS = 256


def _mm_kernel(offs_ref, a_ref, w_hbm, o_ref, wf32_ref, w16_ref, sems, *, w_is_nk, tn, pieces):
    j = pl.program_id(0)
    i = pl.program_id(1)
    pw = tn // pieces

    def piece_copy(jj, p):
        off = offs_ref[jj * pieces + p]
        if w_is_nk:
            src = w_hbm.at[pl.ds(pl.multiple_of(off * W_ROW_ALIGN, W_ROW_ALIGN), pw), :]
            dst = wf32_ref.at[p * pw:(p + 1) * pw, :]
        else:
            src = w_hbm.at[:, pl.ds(pl.multiple_of(off * LANES, LANES), pw)]
            dst = wf32_ref.at[:, p * pw:(p + 1) * pw]
        return pltpu.make_async_copy(src, dst, sems.at[p])

    def start_tile(jj):
        for p in range(pieces):
            piece_copy(jj, p).start()

    @pl.when((j == 0) & (i == 0))
    def _():
        start_tile(0)

    @pl.when(i == 0)
    def _():
        for p in range(pieces):
            piece_copy(j, p).wait()
        cc = min(MM_CAST_COLS, tn)
        for c0 in range(0, tn, cc):
            if w_is_nk:
                for c in range(c0, c0 + cc, WT_CHUNK):
                    w16_ref[:, c:c + WT_CHUNK] = wf32_ref[c:c + WT_CHUNK, :].T.astype(BF16)
            else:
                w16_ref[:, c0:c0 + cc] = wf32_ref[:, c0:c0 + cc].astype(BF16)
            o_ref[:, c0:c0 + cc] = _dot(a_ref[...], w16_ref[:, c0:c0 + cc]).astype(o_ref.dtype)

        @pl.when(j + 1 < pl.num_programs(0))
        def _():
            start_tile(j + 1)

    @pl.when(i > 0)
    def _():
        o_ref[...] = _dot(a_ref[...], w16_ref[...]).astype(o_ref.dtype)


def _mm_call(a, w, offs, w_is_nk, tm, tn, out_dtype, name, pieces=1):
    M, K = a.shape
    tm = min(tm, M)
    n_tiles = len(offs) // pieces
    grid_spec = pltpu.PrefetchScalarGridSpec(
        num_scalar_prefetch=1,
        grid=(n_tiles, M // tm),
        in_specs=[pl.BlockSpec((tm, K), lambda j, i, o: (i, 0)),
                  pl.BlockSpec(memory_space=pl.ANY)],
        out_specs=pl.BlockSpec((tm, tn), lambda j, i, o: (i, j)),
        scratch_shapes=[pltpu.VMEM((tn, K) if w_is_nk else (K, tn), F32),
                        pltpu.VMEM((K, tn), BF16),
                        pltpu.SemaphoreType.DMA((pieces,))],
    )
    return pl.pallas_call(
        functools.partial(_mm_kernel, w_is_nk=w_is_nk, tn=tn, pieces=pieces),
        grid_spec=grid_spec,
        out_shape=jax.ShapeDtypeStruct((M, n_tiles * tn), out_dtype),
        compiler_params=_cparams(("arbitrary", "arbitrary")),
        name=name,
    )(jnp.asarray(np.asarray(offs, np.int32)), a, w)


def _matmul(a, w, out_dtype, name):
    N = w.shape[1]
    tn = min(MM_TN, N)
    offs = [t * tn // LANES for t in range(N // tn)]
    return _mm_call(a, w, offs, False, MM_TM, tn, out_dtype, name)


def _proj(h, w_nk, row_offsets, tm, tn, out_dtype, name, pieces=1):
    assert all(int(o) % W_ROW_ALIGN == 0 and int(o) + tn // pieces <= w_nk.shape[0] for o in row_offsets)
    offs = [int(o) // W_ROW_ALIGN for o in row_offsets]
    return _mm_call(h, w_nk, offs, True, tm, tn, out_dtype, name, pieces)


def _rope_consts():
    inv = ROPE_THETA ** (-np.arange(ROT_HALF, dtype=np.float64) / ROT_HALF)
    invf = np.zeros((1, LANES), np.float32)
    invf[0, :ROT_HALF] = inv
    invf[0, ROT_HALF:ROT_DIM] = inv
    sgn = np.zeros((1, LANES), np.float32)
    sgn[0, :ROT_HALF] = -1.0
    sgn[0, ROT_HALF:ROT_DIM] = 1.0
    return jnp.asarray(invf), jnp.asarray(sgn)


def _rope_tables(pos_col, invf, sgn):
    ang = pos_col.astype(F32) * invf
    return jnp.cos(ang), jnp.sin(ang) * sgn


def _rope_apply(x, cos_t, sin_t):
    lane = lax.broadcasted_iota(jnp.int32, x.shape, 1)
    partner = jnp.where(lane < ROT_HALF,
                        pltpu.roll(x, LANES - ROT_HALF, 1),
                        pltpu.roll(x, ROT_HALF, 1))
    return x * cos_t + partner * sin_t


KEYS_WIDTH = 6 * HEAD_DIM


def _nsa_prep_kernel(q_ref, kvs_ref, kvw_ref, pos_ref, invf_ref, sgn_ref, qt_ref, keys_ref, vt_ref, *, seq_len):
    tm = q_ref.shape[0]
    cos_t, sin_t = _rope_tables(pos_ref[...], invf_ref[...], sgn_ref[...])
    qscale = (HEAD_DIM ** -0.5) * math.log2(math.e)
    for h in range(NSA_HEADS):
        sl = slice(h * HEAD_DIM, (h + 1) * HEAD_DIM)
        xq = _rope_apply(q_ref[:, sl].astype(F32), cos_t, sin_t) * qscale
        qt_ref[sl, :] = xq.T.astype(qt_ref.dtype)
    tok = (pl.program_id(0) * tm) % seq_len + lax.broadcasted_iota(jnp.int32, (tm, LANES), 0)
    lane = lax.broadcasted_iota(jnp.int32, (tm, LANES), 1)
    onehot = jnp.where(lane == (tok >> int(math.log2(SLC_LEN))), 1.0, 0.0).astype(keys_ref.dtype)
    blk = lambda i: (kvs_ref if i < 4 else kvw_ref)[:, (i % 4) * HEAD_DIM:(i % 4 + 1) * HEAD_DIM]
    for g in range(NSA_GROUPS):
        ks = _rope_apply(blk(g), cos_t, sin_t)
        keys_ref[:, (2 * g) * HEAD_DIM:(2 * g + 1) * HEAD_DIM] = ks.astype(keys_ref.dtype)
        keys_ref[:, (2 * g + 1) * HEAD_DIM:(2 * g + 2) * HEAD_DIM] = onehot
        kw = _rope_apply(blk(4 + g), cos_t, sin_t)
        keys_ref[:, (4 + g) * HEAD_DIM:(5 + g) * HEAD_DIM] = kw.astype(keys_ref.dtype)
        ones = jnp.ones((VT_ROWS - HEAD_DIM, tm), vt_ref.dtype)
        for c, src in ((g, 2 + g), (2 + g, 6 + g)):
            vt_ref[c * VT_ROWS:c * VT_ROWS + HEAD_DIM, :] = blk(src).T.astype(vt_ref.dtype)
            vt_ref[c * VT_ROWS + HEAD_DIM:(c + 1) * VT_ROWS, :] = ones


def _nsa_prep(pbig, ptail, pos_col, q_col_block, S):
    M = pbig.shape[0]
    tm = 256
    invf, sgn = _rope_consts()
    return pl.pallas_call(
        functools.partial(_nsa_prep_kernel, seq_len=S),
        grid=(M // tm,),
        in_specs=[pl.BlockSpec((tm, NSA_WIDTH), lambda i: (i, q_col_block)),
                  pl.BlockSpec((tm, 4 * HEAD_DIM), lambda i: (i, 1)),
                  pl.BlockSpec((tm, 4 * HEAD_DIM), lambda i: (i, 2)),
                  pl.BlockSpec((tm, 1), lambda i: (i, 0)),
                  pl.BlockSpec((1, LANES), lambda i: (0, 0)),
                  pl.BlockSpec((1, LANES), lambda i: (0, 0))],
        out_specs=[pl.BlockSpec((NSA_WIDTH, tm), lambda i: (0, i)),
                   pl.BlockSpec((tm, KEYS_WIDTH), lambda i: (i, 0)),
                   pl.BlockSpec((4 * VT_ROWS, tm), lambda i: (0, i))],
        out_shape=[jax.ShapeDtypeStruct((NSA_WIDTH, M), BF16),
                   jax.ShapeDtypeStruct((M, KEYS_WIDTH), BF16),
                   jax.ShapeDtypeStruct((4 * VT_ROWS, M), BF16)],
        compiler_params=_cparams(("arbitrary",)),
        name="nsa_prep",
    )(pbig, ptail, ptail, pos_col, invf, sgn)


def _compress_kernel(x_ref, pe_ref, w1_ref, w2_ref, pos_ref, invf_ref, sgn_ref, o_ref, *, rope, n_rows):
    half = CMP_LEN // 2
    top = jnp.zeros((n_rows, CMP_HIDDEN), F32)
    bot = jnp.zeros((n_rows, CMP_HIDDEN), F32)
    for l in range(half):
        xl = x_ref[pl.ds(l, n_rows, stride=CMP_STRIDE), :]
        w_top = w1_ref[l * HEAD_DIM:(l + 1) * HEAD_DIM, :].astype(BF16)
        w_bot = w1_ref[(half + l) * HEAD_DIM:(half + l + 1) * HEAD_DIM, :].astype(BF16)
        top = top + _dot((xl + pe_ref[l:l + 1, :]).astype(BF16), w_top)
        bot = bot + _dot((xl + pe_ref[half + l:half + l + 1, :]).astype(BF16), w_bot)
    hid = top + pltpu.roll(bot, n_rows - 1, 0)
    act = _silu(hid).astype(BF16)
    if rope:
        out = _dot(act, w2_ref[...].astype(BF16))
        cos_t, sin_t = _rope_tables(pos_ref[0], invf_ref[...], sgn_ref[...])
        o_ref[0, 0] = _rope_apply(out, cos_t, sin_t).astype(o_ref.dtype)
    else:
        o_ref[0, 0, 0:HEAD_DIM, :] = _dot_nt(w2_ref[...].astype(BF16), act).astype(o_ref.dtype)
        o_ref[0, 0, HEAD_DIM:VT_ROWS, :] = jnp.ones((VT_ROWS - HEAD_DIM, n_rows), o_ref.dtype)


def _compress(ptail, col_block0, pe, w1, w2, pos_cmp, B, S, rope):
    n_rows = S // CMP_STRIDE
    invf, sgn = _rope_consts()
    kern = functools.partial(_compress_kernel, rope=rope, n_rows=n_rows)
    out_dims = (n_rows, HEAD_DIM) if rope else (VT_ROWS, n_rows)
    return pl.pallas_call(
        kern,
        grid=(B, NSA_GROUPS),
        in_specs=[pl.BlockSpec((S, HEAD_DIM), lambda b, g: (b, col_block0 + g)),
                  pl.BlockSpec((CMP_LEN, HEAD_DIM), lambda b, g: (0, 0)),
                  pl.BlockSpec((CMP_LEN * HEAD_DIM, CMP_HIDDEN), lambda b, g: (0, 0)),
                  pl.BlockSpec(w2.shape, lambda b, g: (0, 0)),
                  pl.BlockSpec((1, n_rows, 1), lambda b, g: (b, 0, 0)),
                  pl.BlockSpec((1, LANES), lambda b, g: (0, 0)),
                  pl.BlockSpec((1, LANES), lambda b, g: (0, 0))],
        out_specs=pl.BlockSpec((1, 1) + out_dims, lambda b, g: (b, g, 0, 0)),
        out_shape=jax.ShapeDtypeStruct((B, NSA_GROUPS) + out_dims, BF16),
        compiler_params=_cparams(("arbitrary", "arbitrary")),
        name="compress_k" if rope else "compress_v",
    )(ptail, pe, w1, w2, pos_cmp, invf, sgn)


def _tile_lanes(x, n):
    return jnp.concatenate([x] * n, axis=1)


def _nsa_attn_kernel(qt_ref, kc_ref, vct_ref, ks_ref, vst_ref, kw_ref, vwt_ref, g_ref, z_ref,
                     ovt_ref, o_ref, m_s, acc_s, qa_s, sa_s, sb_s, sc_s, *, seq_len):
    T = Q_BLOCK
    P = NSA_HPG
    R = P * T
    n_cmp_pad = seq_len // CMP_STRIDE
    n_cmp = n_cmp_pad - 1
    nb = seq_len // SLC_LEN
    qi = pl.program_id(2)
    qs = qi * T
    d0 = pl.multiple_of(qs, T)
    wlen = WINDOW + T
    w0 = pl.multiple_of(jnp.maximum(qs - WINDOW, 0), T)

    qt = jnp.concatenate([qt_ref[p * HEAD_DIM:(p + 1) * HEAD_DIM, :] for p in range(P)], axis=1)

    s_c = _dot(kc_ref[0, 0], qt)
    s_w = _dot(kw_ref[pl.ds(w0, wlen), :], qt)
    s_d = _dot(ks_ref[pl.ds(d0, T), 0:HEAD_DIM], qt)

    tok = lambda n: qs + lax.broadcasted_iota(jnp.int32, (n, T), 1)
    row = lambda n: lax.broadcasted_iota(jnp.int32, (n, T), 0)
    ok_c = (row(n_cmp_pad) * CMP_STRIDE + (CMP_LEN - 1) <= tok(n_cmp_pad)) & (row(n_cmp_pad) < n_cmp)
    key_w = w0 + row(wlen)
    ok_w = (key_w <= tok(wlen)) & (tok(wlen) - key_w < WINDOW)
    ok_d = row(T) <= lax.broadcasted_iota(jnp.int32, (T, T), 1)
    addmask = lambda ok: _tile_lanes(jnp.where(ok, 0.0, NEG).astype(F32), P)
    s_c = s_c + addmask(ok_c)
    s_w = s_w + addmask(ok_w)
    s_d = s_d + addmask(ok_d)

    m_c = jnp.max(s_c, axis=0, keepdims=True)
    m_w = jnp.max(s_w, axis=0, keepdims=True)
    m_d = jnp.max(s_d, axis=0, keepdims=True)
    m_c = jnp.where(m_c > 0.5 * NEG, m_c, 0.0)
    p_c = jnp.exp2(s_c - m_c)
    p_w = jnp.exp2(s_w - m_w)
    p_d = jnp.exp2(s_d - m_d)
    o_c = _dot(vct_ref[0, 0], p_c.astype(BF16))
    o_w = _dot(vwt_ref[:, pl.ds(w0, wlen)], p_w.astype(BF16))
    inv_c = 1.0 / jnp.maximum(jnp.sum(p_c, axis=0, keepdims=True), 1e-30)
    inv_w = 1.0 / o_w[HEAD_DIM:HEAD_DIM + 1, :]
    m_s[...] = m_d
    acc_s[...] = _dot(vst_ref[:, pl.ds(d0, T)], p_d.astype(BF16))

    pn = p_c * inv_c
    p_sum = pn[:, 0:T]
    for p in range(1, P):
        p_sum = p_sum + pn[:, p * T:(p + 1) * T]
    ps_hi = p_sum.astype(BF16)
    ps_lo = (p_sum - ps_hi.astype(F32)).astype(BF16)
    imp = (_dot(ovt_ref[...], ps_hi) + _dot(ovt_ref[...], ps_lo))[0:nb]
    jrow = row(nb)
    t_lane = tok(nb)
    tb = t_lane >> int(math.log2(SLC_LEN))
    visible = jrow * SLC_LEN <= t_lane
    forced = (jrow == 0) | (jrow == tb) | (jrow == tb - 1)
    score = jnp.where(forced, 1e9, jnp.where(visible, imp, -jnp.inf))
    cnt = jnp.zeros((nb, T), F32)
    for k in range(nb):
        rk = score[k:k + 1, :]
        cnt = cnt + jnp.where(jrow > k, jnp.where(rk >= score, 1.0, 0.0), jnp.where(rk > score, 1.0, 0.0))
    keep = (cnt < float(min(SLC_TOPK, nb))) & visible & (jrow < 2 * qi)
    bias = jnp.where(keep, 0.0, NEG).astype(F32)
    bias = jnp.concatenate([bias, jnp.full((LANES - nb, T), NEG, F32)], axis=0) if nb < LANES else bias
    qa_s[0:HEAD_DIM, :] = qt
    qa_s[HEAD_DIM:2 * HEAD_DIM, :] = _tile_lanes(bias.astype(BF16), P)

    n_t = (qs + SEL_TILE - 1) // SEL_TILE

    def scores_into(kt, dst):
        dst[...] = _dot(ks_ref[kt * SEL_TILE:(kt + 1) * SEL_TILE, :], qa_s[...])

    def consume(kt, src):
        k0 = kt * SEL_TILE
        s = src[...]
        m_prev = m_s[...]
        m_new = jnp.maximum(m_prev, jnp.max(s, axis=0, keepdims=True))
        alpha = jnp.exp2(m_prev - m_new)
        p = jnp.exp2(s - m_new)
        acc_s[...] = alpha * acc_s[...] + _dot(vst_ref[:, pl.ds(k0, SEL_TILE)], p.astype(BF16))
        m_s[...] = m_new

    bufs = (sa_s, sb_s, sc_s)
    for r in range(1, seq_len // SEL_TILE + 1):
        @pl.when(n_t == r)
        def _(r=r):
            scores_into(0, bufs[0])
            for t in range(r):
                if t + 1 < r:
                    scores_into(t + 1, bufs[(t + 1) % len(bufs)])
                consume(t, bufs[t % len(bufs)])

    gates_t = _sigmoid(g_ref[...]).T
    first_group = pl.program_id(1) == 0

    def grow(r):
        rows = []
        for p in range(P):
            h0 = r * NSA_HEADS + p
            rows.append(jnp.where(first_group, gates_t[h0:h0 + 1, :], gates_t[h0 + P:h0 + P + 1, :]))
        return jnp.concatenate(rows, axis=1)

    o_s = acc_s[...]
    hd = slice(0, HEAD_DIM)
    o_t = (o_c[hd] * (grow(0) * inv_c) + o_s[hd] * (grow(1) / o_s[HEAD_DIM:HEAD_DIM + 1, :])
           + o_w[hd] * (grow(2) * inv_w))
    for p in range(P):
        sl = slice(p * HEAD_DIM, (p + 1) * HEAD_DIM)
        o_p = o_t[:, p * T:(p + 1) * T].T
        o_ref[:, sl] = (o_p * _silu(z_ref[:, sl].astype(F32))).astype(o_ref.dtype)


def _nsa_consts(S):
    n_cmp_pad = S // CMP_STRIDE
    nb = S // SLC_LEN
    cmp_start = np.arange(n_cmp_pad) * CMP_STRIDE
    slc_start = np.arange(LANES) * SLC_LEN
    ov = ((cmp_start[:, None] <= slc_start[None, :] + SLC_LEN - 1)
          & (cmp_start[:, None] + CMP_LEN - 1 >= slc_start[None, :])
          & (np.arange(LANES)[None, :] < nb)
          & (np.arange(n_cmp_pad)[:, None] < n_cmp_pad - 1)).astype(np.float32)
    return jnp.asarray(ov.T, dtype=BF16)


def _nsa_attn(q_t, kcmp, vcmp_t, keys, v_t, psmall, pbig, z_col_block, B, S):
    M = B * S
    nq = S // Q_BLOCK
    gw = NSA_HPG * HEAD_DIM
    ovt = _nsa_consts(S)
    n_cmp_pad = S // CMP_STRIDE
    R = NSA_HPG * Q_BLOCK
    kern = functools.partial(_nsa_attn_kernel, seq_len=S)
    return pl.pallas_call(
        kern,
        grid=(B, NSA_GROUPS, nq),
        in_specs=[pl.BlockSpec((gw, Q_BLOCK), lambda b, g, i: (g, b * nq + i)),
                  pl.BlockSpec((1, 1, n_cmp_pad, HEAD_DIM), lambda b, g, i: (b, g, 0, 0)),
                  pl.BlockSpec((1, 1, VT_ROWS, n_cmp_pad), lambda b, g, i: (b, g, 0, 0)),
                  pl.BlockSpec((S, 2 * HEAD_DIM), lambda b, g, i: (b, g)),
                  pl.BlockSpec((VT_ROWS, S), lambda b, g, i: (g, b)),
                  pl.BlockSpec((S, HEAD_DIM), lambda b, g, i: (b, 4 + g)),
                  pl.BlockSpec((VT_ROWS, S), lambda b, g, i: (2 + g, b)),
                  pl.BlockSpec((Q_BLOCK, LANES), lambda b, g, i: (b * nq + i, SMALL_GATE_BLOCK)),
                  pl.BlockSpec((Q_BLOCK, gw), lambda b, g, i: (b * nq + i, z_col_block * NSA_GROUPS + g)),
                  pl.BlockSpec((LANES, n_cmp_pad), lambda b, g, i: (0, 0))],
        out_specs=pl.BlockSpec((Q_BLOCK, gw), lambda b, g, i: (b * nq + i, g)),
        out_shape=jax.ShapeDtypeStruct((M, NSA_WIDTH), BF16),
        scratch_shapes=[pltpu.VMEM((1, R), F32),
                        pltpu.VMEM((VT_ROWS, R), F32),
                        pltpu.VMEM((2 * HEAD_DIM, R), BF16),
                        pltpu.VMEM((SEL_TILE, R), F32),
                        pltpu.VMEM((SEL_TILE, R), F32),
                        pltpu.VMEM((SEL_TILE, R), F32)],
        compiler_params=_cparams(("arbitrary", "arbitrary", "arbitrary")),
        name="nsa_attn",
    )(q_t, kcmp, vcmp_t, keys, v_t, keys, v_t, psmall, pbig, ovt)


def _dn_prep_kernel(x_ref, w_ref, o_ref, *rest, mode, seq_len):
    xpad_ref = rest[-1]
    w = w_ref[...]
    xpad_ref[0:CONV_PAD, :] = jnp.zeros((CONV_PAD, xpad_ref.shape[1]), F32)
    xpad_ref[CONV_PAD:, :] = x_ref[...].astype(F32)
    y = xpad_ref[CONV_PAD:, :] * w[CONV_WIDTH - 1:CONV_WIDTH, :]
    for k in range(1, CONV_WIDTH):
        y = y + xpad_ref[pl.ds(CONV_PAD - k, seq_len), :] * w[CONV_WIDTH - 1 - k:CONV_WIDTH - k, :]
    y = _silu(y)
    if mode in ("q", "k"):
        outs = []
        for h in range(y.shape[1] // DN_DK):
            yh = y[:, h * DN_DK:(h + 1) * DN_DK]
            ss = jnp.sum(yh * yh, axis=-1, keepdims=True)
            yh = yh * lax.rsqrt(ss + EPS)
            if mode == "q":
                yh = yh * (DN_DK ** -0.5)
            outs.append(yh)
        y = jnp.concatenate(outs, axis=1)
    o_ref[...] = y.astype(o_ref.dtype)
    if mode == "k":
        kt_ref = rest[0]
        kt_ref[...] = y.T.astype(kt_ref.dtype)


DN_PREP_COLS = 128
CONV_PAD = 8


def _dn_prep(pbig, conv_w, col0, mode, B, S):
    M = B * S
    tw = DN_PREP_COLS
    col0_blocks = col0 // tw
    nj = DN_WIDTH // tw
    wcol0 = {"q": 0, "k": nj, "v": 2 * nj}[mode]
    kern = functools.partial(_dn_prep_kernel, mode=mode, seq_len=S)
    out_specs = [pl.BlockSpec((S, tw), lambda b, j: (b, j))]
    out_shape = [jax.ShapeDtypeStruct((M, DN_WIDTH), BF16)]
    if mode == "k":
        out_specs.append(pl.BlockSpec((tw, S), lambda b, j: (j, b)))
        out_shape.append(jax.ShapeDtypeStruct((DN_WIDTH, M), BF16))
    return pl.pallas_call(
        kern,
        grid=(B, nj),
        in_specs=[pl.BlockSpec((S, tw), lambda b, j: (b, col0_blocks + j)),
                  pl.BlockSpec((CONV_WIDTH, tw), lambda b, j: (0, wcol0 + j))],
        out_specs=out_specs,
        out_shape=out_shape,
        scratch_shapes=[pltpu.VMEM((CONV_PAD + S, tw), F32)],
        compiler_params=_cparams(("arbitrary", "arbitrary")),
        name="dn_prep_" + mode,
    )(pbig, conv_w)


def _dn_gate_kernel(ab_ref, alog_ref, dtb_ref, o_ref, ot_ref, *, tm):
    ab = ab_ref[...]
    x = ab + dtb_ref[...]
    softplus = jnp.maximum(x, 0.0) + jnp.log(1.0 + jnp.exp(-jnp.abs(x)))
    g = -jnp.exp(alog_ref[...]) * softplus
    beta = _sigmoid(ab)
    lane = lax.broadcasted_iota(jnp.int32, (DN_CHUNK, LANES), 1)
    r = lax.broadcasted_iota(jnp.int32, (DN_CHUNK, DN_CHUNK), 0)
    c = lax.broadcasted_iota(jnp.int32, (DN_CHUNK, DN_CHUNK), 1)
    tril = jnp.where(r >= c, 1.0, 0.0).astype(F32)
    for ci in range(tm // DN_CHUNK):
        rows = slice(ci * DN_CHUNK, (ci + 1) * DN_CHUNK)
        dec = _dot_f32(tril, g[rows])
        out = jnp.where(lane < DN_HEADS, dec, beta[rows])
        o_ref[rows, :] = out
        ot_ref[:, rows] = out.T


def _dn_gate(psmall, alog_row, dtb_row):
    M = psmall.shape[0]
    tm = 512
    kern = functools.partial(_dn_gate_kernel, tm=tm)
    return pl.pallas_call(
        kern,
        grid=(M // tm,),
        in_specs=[pl.BlockSpec((tm, LANES), lambda i: (i, SMALL_AB_BLOCK)),
                  pl.BlockSpec((1, LANES), lambda i: (0, 0)),
                  pl.BlockSpec((1, LANES), lambda i: (0, 0))],
        out_specs=[pl.BlockSpec((tm, LANES), lambda i: (i, 0)),
                   pl.BlockSpec((LANES, tm), lambda i: (0, i))],
        out_shape=[jax.ShapeDtypeStruct((M, LANES), F32),
                   jax.ShapeDtypeStruct((LANES, M), F32)],
        compiler_params=_cparams(("arbitrary",)),
        name="dn_gate",
    )(psmall, alog_row, dtb_row)


def _dn_scan_kernel(q_ref, k_ref, v_ref, kt_ref, dec_ref, dect_ref, z_ref, gain_ref, o_ref, st_ref):
    C = DN_CHUNK

    @pl.when(pl.program_id(1) == 0)
    def _():
        st_ref[...] = jnp.zeros(st_ref.shape, F32)

    dec = dec_ref[...]
    dect = dect_ref[...]
    r = lax.broadcasted_iota(jnp.int32, (C, C), 0)
    c = lax.broadcasted_iota(jnp.int32, (C, C), 1)
    tril = r >= c
    strict = r > c
    gain = gain_ref[...]
    eye = jnp.where(r == c, 1.0, 0.0).astype(F32)
    lvl_masks = []
    for lg in range(int(math.log2(C))):
        lvl_masks.append(((r >> (lg + 1)) == (c >> (lg + 1))) & ((r >> lg) != (c >> lg)))

    for h0 in range(0, DN_HEADS, DN_HEAD_GROUP):
        heads = range(h0, h0 + DN_HEAD_GROUP)
        cols = lambda ref: jnp.stack([ref[:, h * DN_DK:(h + 1) * DN_DK] for h in heads])
        qh = cols(q_ref).astype(F32)
        kh = cols(k_ref).astype(F32)
        vh = cols(v_ref).astype(F32)
        kth = jnp.stack([kt_ref[h * DN_DK:(h + 1) * DN_DK, :] for h in heads])
        dcol = jnp.stack([jnp.broadcast_to(dec[:, h:h + 1], (C, C)) for h in heads])
        bcol = jnp.stack([jnp.broadcast_to(dec[:, DN_HEADS + h:DN_HEADS + h + 1], (C, C)) for h in heads])
        drow = jnp.stack([jnp.broadcast_to(dect[h:h + 1, :], (C, C)) for h in heads])
        dlast = jnp.stack([jnp.broadcast_to(dect[h:h + 1, C - 1:C], (C, C)) for h in heads])
        lmat = jnp.exp(jnp.where(tril[None], dcol - drow, NEG))
        e_d = jnp.exp(dcol)
        kb = kh * bcol
        vb = vh * bcol
        a = jnp.where(strict[None], _bmm(kb.astype(BF16), kth) * lmat, 0.0)
        attn = jnp.where(tril[None], _bmm(qh.astype(BF16), kth) * lmat, 0.0)
        tinv = eye[None] - jnp.where(lvl_masks[0][None], a, 0.0)
        for lm in lvl_masks[1:]:
            t16 = tinv.astype(BF16)
            lo = jnp.where(lm[None], a, 0.0).astype(BF16)
            tinv = tinv - _bmm(t16, _bmm(lo, t16).astype(BF16))
        t16 = tinv.astype(BF16)
        u = _bmm(t16, vb.astype(BF16))
        w = _bmm(t16, (kb * e_d).astype(BF16))
        st = st_ref[h0:h0 + DN_HEAD_GROUP]
        st16 = st.astype(BF16)
        v_new = u - _bmm(w.astype(BF16), st16)
        v_new16 = v_new.astype(BF16)
        o = _bmm((qh * e_d).astype(BF16), st16) + _bmm(attn.astype(BF16), v_new16)
        kdt = (kth.astype(F32) * jnp.exp(dlast - drow)).astype(BF16)
        st_ref[h0:h0 + DN_HEAD_GROUP] = st * jnp.exp(dlast) + _bmm(kdt, v_new16)
        ms = jnp.mean(o * o, axis=-1, keepdims=True)
        y = o * lax.rsqrt(ms + EPS) * gain[None]
        for i, h in enumerate(heads):
            sl = slice(h * DN_DV, (h + 1) * DN_DV)
            o_ref[:, sl] = (y[i] * _silu(z_ref[:, sl].astype(F32))).astype(o_ref.dtype)


def _dn_scan(dq, dk, dv, dkt, dec, dect, pbig, z_col_block, gain_row, B, S):
    M = B * S
    C = DN_CHUNK
    nc = S // C
    tok = lambda b, n: (b * nc + n, 0)
    return pl.pallas_call(
        _dn_scan_kernel,
        grid=(B, nc),
        in_specs=[pl.BlockSpec((C, DN_WIDTH), tok),
                  pl.BlockSpec((C, DN_WIDTH), tok),
                  pl.BlockSpec((C, DN_WIDTH), tok),
                  pl.BlockSpec((DN_WIDTH, C), lambda b, n: (0, b * nc + n)),
                  pl.BlockSpec((C, LANES), tok),
                  pl.BlockSpec((LANES, C), lambda b, n: (0, b * nc + n)),
                  pl.BlockSpec((C, DN_WIDTH), lambda b, n: (b * nc + n, z_col_block)),
                  pl.BlockSpec((1, DN_DV), lambda b, n: (0, 0))],
        out_specs=pl.BlockSpec((C, DN_WIDTH), tok),
        out_shape=jax.ShapeDtypeStruct((M, DN_WIDTH), BF16),
        scratch_shapes=[pltpu.VMEM((DN_HEADS, DN_DK, DN_DV), F32)],
        compiler_params=_cparams(("arbitrary", "arbitrary")),
        name="dn_scan",
    )(dq, dk, dv, dkt, dec, dect, pbig, gain_row)


def _out1_kernel(oa_ref, ob_ref, wa_hbm, wb_hbm, ga_ref, gb_ref, o_ref, wf32_ref, w16_ref, sems, *, tn):
    j = pl.program_id(0)
    i = pl.program_id(1)

    def weight_copy(jj, b):
        src = (wa_hbm, wb_hbm)[b].at[:, pl.ds(pl.multiple_of(jj * tn, tn), tn)]
        return pltpu.make_async_copy(src, wf32_ref.at[b], sems.at[b])

    def merged(cols):
        ya = _dot(oa_ref[...], w16_ref[0, :, cols])
        yb = _dot(ob_ref[...], w16_ref[1, :, cols])
        mix = _sigmoid(ga_ref[:, cols].astype(F32)) * ya + _sigmoid(gb_ref[:, cols].astype(F32)) * yb
        o_ref[:, cols] = mix.astype(o_ref.dtype)

    @pl.when((j == 0) & (i == 0))
    def _():
        weight_copy(0, 0).start()
        weight_copy(0, 1).start()

    @pl.when(i == 0)
    def _():
        weight_copy(j, 0).wait()
        weight_copy(j, 1).wait()
        cc = min(MM_CAST_COLS, tn)
        for c0 in range(0, tn, cc):
            cols = slice(c0, c0 + cc)
            w16_ref[0, :, cols] = wf32_ref[0, :, cols].astype(BF16)
            w16_ref[1, :, cols] = wf32_ref[1, :, cols].astype(BF16)
            merged(cols)

        @pl.when(j + 1 < pl.num_programs(0))
        def _():
            weight_copy(j + 1, 0).start()
            weight_copy(j + 1, 1).start()

    @pl.when(i > 0)
    def _():
        merged(slice(0, tn))


OUT1_TM = 512


def _out1(o_a, o_b, wa, wb, pbig, D):
    M, K = o_a.shape
    tm = min(OUT1_TM, M)
    tn = min(MM_TN, D)
    nbd = D // tn
    return pl.pallas_call(
        functools.partial(_out1_kernel, tn=tn),
        grid=(D // tn, M // tm),
        in_specs=[pl.BlockSpec((tm, K), lambda j, i: (i, 0)),
                  pl.BlockSpec((tm, K), lambda j, i: (i, 0)),
                  pl.BlockSpec(memory_space=pl.ANY),
                  pl.BlockSpec(memory_space=pl.ANY),
                  pl.BlockSpec((tm, tn), lambda j, i: (i, j)),
                  pl.BlockSpec((tm, tn), lambda j, i: (i, nbd + j))],
        out_specs=pl.BlockSpec((tm, tn), lambda j, i: (i, j)),
        out_shape=jax.ShapeDtypeStruct((M, D), BF16),
        scratch_shapes=[pltpu.VMEM((2, K, tn), F32), pltpu.VMEM((2, K, tn), BF16),
                        pltpu.SemaphoreType.DMA((2,))],
        compiler_params=_cparams(("arbitrary", "arbitrary")),
        name="out1",
    )(o_a, o_b, wa, wb, pbig, pbig)


def _final_kernel(mix_ref, x_ref, gate_ref, fg_ref, o_ref):
    xn = x_ref[...] + gate_ref[0] * mix_ref[...].astype(F32)
    ms = jnp.mean(xn * xn, axis=-1, keepdims=True)
    o_ref[...] = xn * lax.rsqrt(ms + EPS) * fg_ref[...]


def _final(mixed, x2, mod3, final_gain, S):
    M, D = x2.shape
    tm = 256
    nb = S // tm
    return pl.pallas_call(
        _final_kernel,
        grid=(M // tm,),
        in_specs=[pl.BlockSpec((tm, D), lambda i: (i, 0)),
                  pl.BlockSpec((tm, D), lambda i: (i, 0)),
                  pl.BlockSpec((1, 1, D), lambda i: (i // nb, 0, 2)),
                  pl.BlockSpec((1, D), lambda i: (0, 0))],
        out_specs=pl.BlockSpec((tm, D), lambda i: (i, 0)),
        out_shape=jax.ShapeDtypeStruct((M, D), F32),
        compiler_params=_cparams(("arbitrary",)),
        name="final",
    )(mixed, x2, mod3, final_gain)


def _pad_cols(w, width):
    return jnp.pad(w, ((0, 0), (0, width - w.shape[1])))


def _proj_row_offsets(D):
    widths = (NSA_WIDTH, 6 * KV_WIDTH, 3 * NSA_HEADS, NSA_WIDTH, 3 * DN_WIDTH, DN_HEADS, DN_HEADS, DN_WIDTH, 2 * D)
    offs = [int(o) for o in np.concatenate([[0], np.cumsum(widths)])]
    tiles = lambda seg: [offs[seg] + t * MM_TN for t in range(widths[seg] // MM_TN)]
    big = tiles(8) + tiles(4) + tiles(0) + tiles(3) + tiles(7)
    kv0 = offs[1]
    tail = [kv0, kv0 + TAIL_TN]
    small = [offs[2], offs[5]]
    return big, tail, small


TAIL_TN = 3 * KV_WIDTH
TAIL_TM = 1024
TAIL_CMP_BLOCK = 0
SMALL_GATE_BLOCK = 0
SMALL_AB_BLOCK = 1


def kernel(x, c, positions, w_ada, b_ada, norm_gain, w_in, cmp_pos_k, cmp_pos_v, w_cmp_k1, w_cmp_k2,
           w_cmp_v1, w_cmp_v2, conv_w, dt_bias, a_log, dn_norm_gain, w_proj_a, w_proj_b, w_out, final_gain):
    B, S, D = x.shape
    M = B * S
    depth = w_in.shape[0]
    assert S % SEL_TILE == 0 and S % DN_CHUNK == 0 and S >= WINDOW + Q_BLOCK and D % 512 == 0 and B <= 8
    assert S // SLC_LEN <= LANES and D % 1024 == 0 and DN_DK == DN_CHUNK
    assert depth == 1, "the final RMSNorm is fused into the last layer's output kernel"

    off_dq = 2 * D
    off_q = off_dq + 3 * DN_WIDTH
    off_z = off_q + NSA_WIDTH
    off_dz = off_z + NSA_WIDTH

    x2 = x.reshape(M, D)
    pos_col = positions.reshape(M, 1)
    cmp_end = np.arange(S // CMP_STRIDE - 1) * CMP_STRIDE + CMP_LEN - 1
    pos_cmp = jnp.pad(positions[:, cmp_end], ((0, 0), (0, 1)))[:, :, None]
    c8 = jnp.pad(c, ((0, 8 - B), (0, 0)))

    for l in range(depth):
        mod = _ada(c8, w_ada[l], b_ada[l][None, :])
        mod3 = mod[:B].reshape(B, 1, 3 * D)
        h = _norm(x2, norm_gain[l][None, :], mod3, S)

        w_nk = jnp.transpose(w_in[l])
        big_rows, tail_rows, small_rows = _proj_row_offsets(D)
        pbig = _proj(h, w_nk, big_rows, MM_TM, MM_TN, BF16, "proj_big")
        ptail = _proj(h, w_nk, tail_rows, TAIL_TM, TAIL_TN, F32, "proj_tail")
        psmall = _proj(h, w_nk, small_rows, TAIL_TM, 2 * LANES, F32, "proj_small", pieces=2)

        q_t, keys, v_t = _nsa_prep(pbig, ptail, pos_col, off_q // NSA_WIDTH, S)
        kcmp = _compress(ptail, TAIL_CMP_BLOCK, cmp_pos_k[l], w_cmp_k1[l], w_cmp_k2[l], pos_cmp, B, S, True)
        vcmp = _compress(ptail, TAIL_CMP_BLOCK + NSA_GROUPS, cmp_pos_v[l], w_cmp_v1[l], w_cmp_v2[l].T,
                         pos_cmp, B, S, False)
        o_a = _nsa_attn(q_t, kcmp, vcmp, keys, v_t, psmall, pbig, off_z // NSA_WIDTH, B, S)

        cw = conv_w[l]
        dq = _dn_prep(pbig, cw, off_dq, "q", B, S)[0]
        dk, dkt = _dn_prep(pbig, cw, off_dq + DN_WIDTH, "k", B, S)
        dv = _dn_prep(pbig, cw, off_dq + 2 * DN_WIDTH, "v", B, S)[0]
        alog_row = _pad_cols(a_log[l][None, :].astype(F32), LANES)
        dtb_row = _pad_cols(dt_bias[l][None, :].astype(F32), LANES)
        dec, dect = _dn_gate(psmall, alog_row, dtb_row)
        o_b = _dn_scan(dq, dk, dv, dkt, dec, dect, pbig, off_dz // DN_WIDTH,
                       dn_norm_gain[l][None, :], B, S)

        mixed_in = _out1(o_a, o_b, w_proj_a[l], w_proj_b[l], pbig, D)
        mixed = _matmul(mixed_in, w_out[l], BF16, "out2")
        x2 = _final(mixed, x2, mod3, final_gain[None, :], S)
    return x2.reshape(B, S, D)
```

```python
import functools
import math

import numpy as np
import jax
import jax.numpy as jnp
from jax import lax
from jax.experimental import pallas as pl
from jax.experimental.pallas import tpu as pltpu

F32 = jnp.float32
BF16 = jnp.bfloat16

NSA_HEADS = 16
NSA_GROUPS = 2
NSA_HPG = NSA_HEADS // NSA_GROUPS
HEAD_DIM = 128
ROT_DIM = HEAD_DIM // 4
ROT_HALF = ROT_DIM // 2
ROPE_THETA = 500000.0
CMP_LEN = 32
CMP_STRIDE = 16
CMP_HIDDEN = 256
SLC_LEN = 64
SLC_TOPK = 16
WINDOW = 512
Q_BLOCK = 128
NSA_WIDTH = NSA_HEADS * HEAD_DIM
KV_WIDTH = NSA_GROUPS * HEAD_DIM
DN_HEADS = 16
DN_DK = 128
DN_DV = 128
DN_WIDTH = DN_HEADS * DN_DV
CONV_WIDTH = 4
EPS = 1e-6

LANES = 128
VMEM_LIMIT_BYTES = 56 * 1024 * 1024
ROW_TILE = 512

DN_CHUNK = 128
DN_HEAD_GROUP = 16
SEL_TILE = 512
VT_ROWS = HEAD_DIM + 16
NEG = -1e30


def _cparams(sem):
    return pltpu.CompilerParams(dimension_semantics=sem, vmem_limit_bytes=VMEM_LIMIT_BYTES)


def _sigmoid(x):
    return 1.0 / (1.0 + jnp.exp(-x))


def _silu(x):
    return x * _sigmoid(x)


def _dot(a, b):
    return jnp.dot(a, b, preferred_element_type=F32)


def _dot_nt(a, b):
    return lax.dot_general(a, b, (((1,), (1,)), ((), ())), preferred_element_type=F32)


def _bmm(a, b):
    return lax.dot_general(a, b, (((2,), (1,)), ((0,), (0,))), preferred_element_type=F32)


def _dot_f32(a, b):
    return jnp.dot(a, b, preferred_element_type=F32, precision=lax.Precision.HIGHEST)


def _ada_kernel(c_ref, w_ref, b_ref, o_ref):
    c = c_ref[...]
    c_hi = c.astype(BF16)
    c_lo = (c - c_hi.astype(F32)).astype(BF16)
    w = w_ref[...]
    w_hi = w.astype(BF16)
    w_lo = (w - w_hi.astype(F32)).astype(BF16)
    acc = _dot(c_hi, w_hi) + _dot(c_lo, w_hi) + _dot(c_hi, w_lo)
    o_ref[...] = acc + b_ref[...]


def _ada(c8, w_ada, b_ada):
    D, N = w_ada.shape
    tn = min(512, N)
    return pl.pallas_call(
        _ada_kernel,
        grid=(N // tn,),
        in_specs=[pl.BlockSpec((8, D), lambda j: (0, 0)),
                  pl.BlockSpec((D, tn), lambda j: (0, j)),
                  pl.BlockSpec((1, tn), lambda j: (0, j))],
        out_specs=pl.BlockSpec((8, tn), lambda j: (0, j)),
        out_shape=jax.ShapeDtypeStruct((8, N), F32),
        compiler_params=_cparams(("arbitrary",)),
        name="ada",
    )(c8, w_ada, b_ada)


def _norm_kernel(x_ref, gain_ref, shift_ref, scale_ref, o_ref):
    x = x_ref[...]
    ms = jnp.mean(x * x, axis=-1, keepdims=True)
    y = x * lax.rsqrt(ms + EPS) * gain_ref[...]
    o_ref[...] = (y * (1.0 + scale_ref[0]) + shift_ref[0]).astype(o_ref.dtype)


def _norm(x2, gain, mod3, S):
    M, D = x2.shape
    tm = ROW_TILE
    nb = S // tm
    return pl.pallas_call(
        _norm_kernel,
        grid=(M // tm,),
        in_specs=[pl.BlockSpec((tm, D), lambda i: (i, 0)),
                  pl.BlockSpec((1, D), lambda i: (0, 0)),
                  pl.BlockSpec((1, 1, D), lambda i: (i // nb, 0, 0)),
                  pl.BlockSpec((1, 1, D), lambda i: (i // nb, 0, 1))],
        out_specs=pl.BlockSpec((tm, D), lambda i: (i, 0)),
        out_shape=jax.ShapeDtypeStruct((M, D), BF16),
        compiler_params=_cparams(("arbitrary",)),
        name="norm",
    )(x2, gain, mod3, mod3)


MM_TM = 1024
MM_TN = 1024
W_ROW_ALIGN = 16
WT_CHUNK = 128
MM_CAST_COLS = 256


def _mm_kernel(offs_ref, a_ref, w_hbm, o_ref, wf32_ref, w16_ref, sems, *, w_is_nk, tn, pieces):
    j = pl.program_id(0)
    i = pl.program_id(1)
    pw = tn // pieces

    def piece_copy(jj, p):
        off = offs_ref[jj * pieces + p]
        if w_is_nk:
            src = w_hbm.at[pl.ds(pl.multiple_of(off * W_ROW_ALIGN, W_ROW_ALIGN), pw), :]
            dst = wf32_ref.at[p * pw:(p + 1) * pw, :]
        else:
            src = w_hbm.at[:, pl.ds(pl.multiple_of(off * LANES, LANES), pw)]
            dst = wf32_ref.at[:, p * pw:(p + 1) * pw]
        return pltpu.make_async_copy(src, dst, sems.at[p])

    def start_tile(jj):
        for p in range(pieces):
            piece_copy(jj, p).start()

    @pl.when((j == 0) & (i == 0))
    def _():
        start_tile(0)

    @pl.when(i == 0)
    def _():
        for p in range(pieces):
            piece_copy(j, p).wait()
        cc = min(MM_CAST_COLS, tn)
        for c0 in range(0, tn, cc):
            if w_is_nk:
                for c in range(c0, c0 + cc, WT_CHUNK):
                    w16_ref[:, c:c + WT_CHUNK] = wf32_ref[c:c + WT_CHUNK, :].T.astype(BF16)
            else:
                w16_ref[:, c0:c0 + cc] = wf32_ref[:, c0:c0 + cc].astype(BF16)
            o_ref[:, c0:c0 + cc] = _dot(a_ref[...], w16_ref[:, c0:c0 + cc]).astype(o_ref.dtype)

        @pl.when(j + 1 < pl.num_programs(0))
        def _():
            start_tile(j + 1)

    @pl.when(i > 0)
    def _():
        o_ref[...] = _dot(a_ref[...], w16_ref[...]).astype(o_ref.dtype)


def _mm_call(a, w, offs, w_is_nk, tm, tn, out_dtype, name, pieces=1):
    M, K = a.shape
    tm = min(tm, M)
    n_tiles = len(offs) // pieces
    grid_spec = pltpu.PrefetchScalarGridSpec(
        num_scalar_prefetch=1,
        grid=(n_tiles, M // tm),
        in_specs=[pl.BlockSpec((tm, K), lambda j, i, o: (i, 0)),
                  pl.BlockSpec(memory_space=pl.ANY)],
        out_specs=pl.BlockSpec((tm, tn), lambda j, i, o: (i, j)),
        scratch_shapes=[pltpu.VMEM((tn, K) if w_is_nk else (K, tn), F32),
                        pltpu.VMEM((K, tn), BF16),
                        pltpu.SemaphoreType.DMA((pieces,))],
    )
    return pl.pallas_call(
        functools.partial(_mm_kernel, w_is_nk=w_is_nk, tn=tn, pieces=pieces),
        grid_spec=grid_spec,
        out_shape=jax.ShapeDtypeStruct((M, n_tiles * tn), out_dtype),
        compiler_params=_cparams(("arbitrary", "arbitrary")),
        name=name,
    )(jnp.asarray(np.asarray(offs, np.int32)), a, w)


def _matmul(a, w, out_dtype, name):
    N = w.shape[1]
    tn = min(MM_TN, N)
    offs = [t * tn // LANES for t in range(N // tn)]
    return _mm_call(a, w, offs, False, MM_TM, tn, out_dtype, name)


def _proj(h, w_nk, row_offsets, tm, tn, out_dtype, name, pieces=1):
    assert all(int(o) % W_ROW_ALIGN == 0 and int(o) + tn // pieces <= w_nk.shape[0] for o in row_offsets)
    offs = [int(o) // W_ROW_ALIGN for o in row_offsets]
    return _mm_call(h, w_nk, offs, True, tm, tn, out_dtype, name, pieces)


def _rope_consts():
    inv = ROPE_THETA ** (-np.arange(ROT_HALF, dtype=np.float64) / ROT_HALF)
    invf = np.zeros((1, LANES), np.float32)
    invf[0, :ROT_HALF] = inv
    invf[0, ROT_HALF:ROT_DIM] = inv
    sgn = np.zeros((1, LANES), np.float32)
    sgn[0, :ROT_HALF] = -1.0
    sgn[0, ROT_HALF:ROT_DIM] = 1.0
    return jnp.asarray(invf), jnp.asarray(sgn)


def _rope_tables(pos_col, invf, sgn):
    ang = pos_col.astype(F32) * invf
    return jnp.cos(ang), jnp.sin(ang) * sgn


def _rope_apply(x, cos_t, sin_t):
    lane = lax.broadcasted_iota(jnp.int32, x.shape, 1)
    partner = jnp.where(lane < ROT_HALF,
                        pltpu.roll(x, LANES - ROT_HALF, 1),
                        pltpu.roll(x, ROT_HALF, 1))
    return x * cos_t + partner * sin_t


KEYS_WIDTH = 6 * HEAD_DIM


def _nsa_prep_kernel(q_ref, kvs_ref, kvw_ref, pos_ref, invf_ref, sgn_ref, qt_ref, keys_ref, vt_ref, *, seq_len):
    tm = q_ref.shape[0]
    cos_t, sin_t = _rope_tables(pos_ref[...], invf_ref[...], sgn_ref[...])
    qscale = (HEAD_DIM ** -0.5) * math.log2(math.e)
    for h in range(NSA_HEADS):
        sl = slice(h * HEAD_DIM, (h + 1) * HEAD_DIM)
        xq = _rope_apply(q_ref[:, sl].astype(F32), cos_t, sin_t) * qscale
        qt_ref[sl, :] = xq.T.astype(qt_ref.dtype)
    tok = (pl.program_id(0) * tm) % seq_len + lax.broadcasted_iota(jnp.int32, (tm, LANES), 0)
    lane = lax.broadcasted_iota(jnp.int32, (tm, LANES), 1)
    onehot = jnp.where(lane == (tok >> int(math.log2(SLC_LEN))), 1.0, 0.0).astype(keys_ref.dtype)
    blk = lambda i: (kvs_ref if i < 4 else kvw_ref)[:, (i % 4) * HEAD_DIM:(i % 4 + 1) * HEAD_DIM]
    for g in range(NSA_GROUPS):
        ks = _rope_apply(blk(g), cos_t, sin_t)
        keys_ref[:, (2 * g) * HEAD_DIM:(2 * g + 1) * HEAD_DIM] = ks.astype(keys_ref.dtype)
        keys_ref[:, (2 * g + 1) * HEAD_DIM:(2 * g + 2) * HEAD_DIM] = onehot
        kw = _rope_apply(blk(4 + g), cos_t, sin_t)
        keys_ref[:, (4 + g) * HEAD_DIM:(5 + g) * HEAD_DIM] = kw.astype(keys_ref.dtype)
        ones = jnp.ones((VT_ROWS - HEAD_DIM, tm), vt_ref.dtype)
        for c, src in ((g, 2 + g), (2 + g, 6 + g)):
            vt_ref[c * VT_ROWS:c * VT_ROWS + HEAD_DIM, :] = blk(src).T.astype(vt_ref.dtype)
            vt_ref[c * VT_ROWS + HEAD_DIM:(c + 1) * VT_ROWS, :] = ones


def _nsa_prep(pbig, ptail, pos_col, q_col_block, S):
    M = pbig.shape[0]
    tm = ROW_TILE
    invf, sgn = _rope_consts()
    return pl.pallas_call(
        functools.partial(_nsa_prep_kernel, seq_len=S),
        grid=(M // tm,),
        in_specs=[pl.BlockSpec((tm, NSA_WIDTH), lambda i: (i, q_col_block)),
                  pl.BlockSpec((tm, 4 * HEAD_DIM), lambda i: (i, 1)),
                  pl.BlockSpec((tm, 4 * HEAD_DIM), lambda i: (i, 2)),
                  pl.BlockSpec((tm, 1), lambda i: (i, 0)),
                  pl.BlockSpec((1, LANES), lambda i: (0, 0)),
                  pl.BlockSpec((1, LANES), lambda i: (0, 0))],
        out_specs=[pl.BlockSpec((NSA_WIDTH, tm), lambda i: (0, i)),
                   pl.BlockSpec((tm, KEYS_WIDTH), lambda i: (i, 0)),
                   pl.BlockSpec((4 * VT_ROWS, tm), lambda i: (0, i))],
        out_shape=[jax.ShapeDtypeStruct((NSA_WIDTH, M), BF16),
                   jax.ShapeDtypeStruct((M, KEYS_WIDTH), BF16),
                   jax.ShapeDtypeStruct((4 * VT_ROWS, M), BF16)],
        compiler_params=_cparams(("arbitrary",)),
        name="nsa_prep",
    )(pbig, ptail, ptail, pos_col, invf, sgn)


def _compress_kernel(x_ref, pe_ref, w1_ref, w2_ref, pos_ref, invf_ref, sgn_ref, o_ref, *, rope, n_rows):
    half = CMP_LEN // 2
    top = jnp.zeros((n_rows, CMP_HIDDEN), F32)
    bot = jnp.zeros((n_rows, CMP_HIDDEN), F32)
    for l in range(half):
        xl = x_ref[pl.ds(l, n_rows, stride=CMP_STRIDE), :]
        w_top = w1_ref[l * HEAD_DIM:(l + 1) * HEAD_DIM, :].astype(BF16)
        w_bot = w1_ref[(half + l) * HEAD_DIM:(half + l + 1) * HEAD_DIM, :].astype(BF16)
        top = top + _dot((xl + pe_ref[l:l + 1, :]).astype(BF16), w_top)
        bot = bot + _dot((xl + pe_ref[half + l:half + l + 1, :]).astype(BF16), w_bot)
    hid = top + pltpu.roll(bot, n_rows - 1, 0)
    act = _silu(hid).astype(BF16)
    if rope:
        out = _dot(act, w2_ref[...].astype(BF16))
        cos_t, sin_t = _rope_tables(pos_ref[0], invf_ref[...], sgn_ref[...])
        o_ref[0, 0] = _rope_apply(out, cos_t, sin_t).astype(o_ref.dtype)
    else:
        o_ref[0, 0, 0:HEAD_DIM, :] = _dot_nt(w2_ref[...].astype(BF16), act).astype(o_ref.dtype)
        o_ref[0, 0, HEAD_DIM:VT_ROWS, :] = jnp.ones((VT_ROWS - HEAD_DIM, n_rows), o_ref.dtype)


def _compress(ptail, col_block0, pe, w1, w2, pos_cmp, B, S, rope):
    n_rows = S // CMP_STRIDE
    invf, sgn = _rope_consts()
    kern = functools.partial(_compress_kernel, rope=rope, n_rows=n_rows)
    out_dims = (n_rows, HEAD_DIM) if rope else (VT_ROWS, n_rows)
    return pl.pallas_call(
        kern,
        grid=(B, NSA_GROUPS),
        in_specs=[pl.BlockSpec((S, HEAD_DIM), lambda b, g: (b, col_block0 + g)),
                  pl.BlockSpec((CMP_LEN, HEAD_DIM), lambda b, g: (0, 0)),
                  pl.BlockSpec((CMP_LEN * HEAD_DIM, CMP_HIDDEN), lambda b, g: (0, 0)),
                  pl.BlockSpec(w2.shape, lambda b, g: (0, 0)),
                  pl.BlockSpec((1, n_rows, 1), lambda b, g: (b, 0, 0)),
                  pl.BlockSpec((1, LANES), lambda b, g: (0, 0)),
                  pl.BlockSpec((1, LANES), lambda b, g: (0, 0))],
        out_specs=pl.BlockSpec((1, 1) + out_dims, lambda b, g: (b, g, 0, 0)),
        out_shape=jax.ShapeDtypeStruct((B, NSA_GROUPS) + out_dims, BF16),
        compiler_params=_cparams(("arbitrary", "arbitrary")),
        name="compress_k" if rope else "compress_v",
    )(ptail, pe, w1, w2, pos_cmp, invf, sgn)


def _tile_lanes(x, n):
    return jnp.concatenate([x] * n, axis=1)


def _nsa_attn_kernel(qt_ref, kc_ref, vct_ref, ks_ref, vst_ref, kw_ref, vwt_ref, g_ref, z_ref,
                     ovt_ref, o_ref, m_s, acc_s, qa_s, sa_s, sb_s, sc_s, *, seq_len):
    T = Q_BLOCK
    P = NSA_HPG
    R = P * T
    n_cmp_pad = seq_len // CMP_STRIDE
    n_cmp = n_cmp_pad - 1
    nb = seq_len // SLC_LEN
    qi = pl.program_id(2)
    qs = qi * T
    d0 = pl.multiple_of(qs, T)
    wlen = WINDOW + T
    w0 = pl.multiple_of(jnp.maximum(qs - WINDOW, 0), T)

    qt = jnp.concatenate([qt_ref[p * HEAD_DIM:(p + 1) * HEAD_DIM, :] for p in range(P)], axis=1)

    s_c = _dot(kc_ref[0, 0], qt)
    s_w = _dot(kw_ref[pl.ds(w0, wlen), :], qt)
    s_d = _dot(ks_ref[pl.ds(d0, T), 0:HEAD_DIM], qt)

    tok = lambda n: qs + lax.broadcasted_iota(jnp.int32, (n, T), 1)
    row = lambda n: lax.broadcasted_iota(jnp.int32, (n, T), 0)
    ok_c = (row(n_cmp_pad) * CMP_STRIDE + (CMP_LEN - 1) <= tok(n_cmp_pad)) & (row(n_cmp_pad) < n_cmp)
    key_w = w0 + row(wlen)
    ok_w = (key_w <= tok(wlen)) & (tok(wlen) - key_w < WINDOW)
    ok_d = row(T) <= lax.broadcasted_iota(jnp.int32, (T, T), 1)
    addmask = lambda ok: _tile_lanes(jnp.where(ok, 0.0, NEG).astype(F32), P)
    s_c = s_c + addmask(ok_c)
    s_w = s_w + addmask(ok_w)
    s_d = s_d + addmask(ok_d)

    m_c = jnp.max(s_c, axis=0, keepdims=True)
    m_w = jnp.max(s_w, axis=0, keepdims=True)
    m_d = jnp.max(s_d, axis=0, keepdims=True)
    m_c = jnp.where(m_c > 0.5 * NEG, m_c, 0.0)
    p_c = jnp.exp2(s_c - m_c)
    p_w = jnp.exp2(s_w - m_w)
    p_d = jnp.exp2(s_d - m_d)
    o_c = _dot(vct_ref[0, 0], p_c.astype(BF16))
    o_w = _dot(vwt_ref[:, pl.ds(w0, wlen)], p_w.astype(BF16))
    inv_c = 1.0 / jnp.maximum(jnp.sum(p_c, axis=0, keepdims=True), 1e-30)
    inv_w = 1.0 / o_w[HEAD_DIM:HEAD_DIM + 1, :]
    m_s[...] = m_d
    acc_s[...] = _dot(vst_ref[:, pl.ds(d0, T)], p_d.astype(BF16))

    pn = p_c * inv_c
    p_sum = pn[:, 0:T]
    for p in range(1, P):
        p_sum = p_sum + pn[:, p * T:(p + 1) * T]
    ps_hi = p_sum.astype(BF16)
    ps_lo = (p_sum - ps_hi.astype(F32)).astype(BF16)
    imp = (_dot(ovt_ref[...], ps_hi) + _dot(ovt_ref[...], ps_lo))[0:nb]
    jrow = row(nb)
    t_lane = tok(nb)
    tb = t_lane >> int(math.log2(SLC_LEN))
    visible = jrow * SLC_LEN <= t_lane
    forced = (jrow == 0) | (jrow == tb) | (jrow == tb - 1)
    score = jnp.where(forced, 1e9, jnp.where(visible, imp, -jnp.inf))
    cnt = jnp.zeros((nb, T), F32)
    for k in range(nb):
        rk = score[k:k + 1, :]
        cnt = cnt + jnp.where(jrow > k, jnp.where(rk >= score, 1.0, 0.0), jnp.where(rk > score, 1.0, 0.0))
    keep = (cnt < float(min(SLC_TOPK, nb))) & visible & (jrow < 2 * qi)
    bias = jnp.where(keep, 0.0, NEG).astype(F32)
    bias = jnp.concatenate([bias, jnp.full((LANES - nb, T), NEG, F32)], axis=0) if nb < LANES else bias
    qa_s[0:HEAD_DIM, :] = qt
    qa_s[HEAD_DIM:2 * HEAD_DIM, :] = _tile_lanes(bias.astype(BF16), P)

    n_t = (qs + SEL_TILE - 1) // SEL_TILE

    def scores_into(kt, dst):
        dst[...] = _dot(ks_ref[kt * SEL_TILE:(kt + 1) * SEL_TILE, :], qa_s[...])

    def consume(kt, src):
        k0 = kt * SEL_TILE
        s = src[...]
        m_prev = m_s[...]
        m_new = jnp.maximum(m_prev, jnp.max(s, axis=0, keepdims=True))
        alpha = jnp.exp2(m_prev - m_new)
        p = jnp.exp2(s - m_new)
        acc_s[...] = alpha * acc_s[...] + _dot(vst_ref[:, pl.ds(k0, SEL_TILE)], p.astype(BF16))
        m_s[...] = m_new

    bufs = (sa_s, sb_s, sc_s)
    for r in range(1, seq_len // SEL_TILE + 1):
        @pl.when(n_t == r)
        def _(r=r):
            scores_into(0, bufs[0])
            for t in range(r):
                if t + 1 < r:
                    scores_into(t + 1, bufs[(t + 1) % len(bufs)])
                consume(t, bufs[t % len(bufs)])

    gates_t = _sigmoid(g_ref[...]).T
    first_group = pl.program_id(1) == 0

    def grow(r):
        rows = []
        for p in range(P):
            h0 = r * NSA_HEADS + p
            rows.append(jnp.where(first_group, gates_t[h0:h0 + 1, :], gates_t[h0 + P:h0 + P + 1, :]))
        return jnp.concatenate(rows, axis=1)

    o_s = acc_s[...]
    hd = slice(0, HEAD_DIM)
    o_t = (o_c[hd] * (grow(0) * inv_c) + o_s[hd] * (grow(1) / o_s[HEAD_DIM:HEAD_DIM + 1, :])
           + o_w[hd] * (grow(2) * inv_w))
    for p in range(P):
        sl = slice(p * HEAD_DIM, (p + 1) * HEAD_DIM)
        o_p = o_t[:, p * T:(p + 1) * T].T
        o_ref[:, sl] = (o_p * _silu(z_ref[:, sl].astype(F32))).astype(o_ref.dtype)


def _nsa_consts(S):
    n_cmp_pad = S // CMP_STRIDE
    nb = S // SLC_LEN
    cmp_start = np.arange(n_cmp_pad) * CMP_STRIDE
    slc_start = np.arange(LANES) * SLC_LEN
    ov = ((cmp_start[:, None] <= slc_start[None, :] + SLC_LEN - 1)
          & (cmp_start[:, None] + CMP_LEN - 1 >= slc_start[None, :])
          & (np.arange(LANES)[None, :] < nb)
          & (np.arange(n_cmp_pad)[:, None] < n_cmp_pad - 1)).astype(np.float32)
    return jnp.asarray(ov.T, dtype=BF16)


def _nsa_attn(q_t, kcmp, vcmp_t, keys, v_t, psmall, pbig, z_col_block, B, S):
    M = B * S
    nq = S // Q_BLOCK
    gw = NSA_HPG * HEAD_DIM
    ovt = _nsa_consts(S)
    n_cmp_pad = S // CMP_STRIDE
    R = NSA_HPG * Q_BLOCK
    kern = functools.partial(_nsa_attn_kernel, seq_len=S)
    return pl.pallas_call(
        kern,
        grid=(B, NSA_GROUPS, nq),
        in_specs=[pl.BlockSpec((gw, Q_BLOCK), lambda b, g, i: (g, b * nq + i)),
                  pl.BlockSpec((1, 1, n_cmp_pad, HEAD_DIM), lambda b, g, i: (b, g, 0, 0)),
                  pl.BlockSpec((1, 1, VT_ROWS, n_cmp_pad), lambda b, g, i: (b, g, 0, 0)),
                  pl.BlockSpec((S, 2 * HEAD_DIM), lambda b, g, i: (b, g)),
                  pl.BlockSpec((VT_ROWS, S), lambda b, g, i: (g, b)),
                  pl.BlockSpec((S, HEAD_DIM), lambda b, g, i: (b, 4 + g)),
                  pl.BlockSpec((VT_ROWS, S), lambda b, g, i: (2 + g, b)),
                  pl.BlockSpec((Q_BLOCK, LANES), lambda b, g, i: (b * nq + i, SMALL_GATE_BLOCK)),
                  pl.BlockSpec((Q_BLOCK, gw), lambda b, g, i: (b * nq + i, z_col_block * NSA_GROUPS + g)),
                  pl.BlockSpec((LANES, n_cmp_pad), lambda b, g, i: (0, 0))],
        out_specs=pl.BlockSpec((Q_BLOCK, gw), lambda b, g, i: (b * nq + i, g)),
        out_shape=jax.ShapeDtypeStruct((M, NSA_WIDTH), BF16),
        scratch_shapes=[pltpu.VMEM((1, R), F32),
                        pltpu.VMEM((VT_ROWS, R), F32),
                        pltpu.VMEM((2 * HEAD_DIM, R), BF16),
                        pltpu.VMEM((SEL_TILE, R), F32),
                        pltpu.VMEM((SEL_TILE, R), F32),
                        pltpu.VMEM((SEL_TILE, R), F32)],
        compiler_params=_cparams(("arbitrary", "arbitrary", "arbitrary")),
        name="nsa_attn",
    )(q_t, kcmp, vcmp_t, keys, v_t, keys, v_t, psmall, pbig, ovt)


def _dn_prep_kernel(x_ref, w_ref, o_ref, *rest, mode, seq_len):
    xpad_ref = rest[-1]
    w = w_ref[...]
    xpad_ref[0:CONV_PAD, :] = jnp.zeros((CONV_PAD, xpad_ref.shape[1]), F32)
    xpad_ref[CONV_PAD:, :] = x_ref[...].astype(F32)
    y = xpad_ref[CONV_PAD:, :] * w[CONV_WIDTH - 1:CONV_WIDTH, :]
    for k in range(1, CONV_WIDTH):
        y = y + xpad_ref[pl.ds(CONV_PAD - k, seq_len), :] * w[CONV_WIDTH - 1 - k:CONV_WIDTH - k, :]
    y = _silu(y)
    if mode in ("q", "k"):
        outs = []
        for h in range(y.shape[1] // DN_DK):
            yh = y[:, h * DN_DK:(h + 1) * DN_DK]
            ss = jnp.sum(yh * yh, axis=-1, keepdims=True)
            yh = yh * lax.rsqrt(ss + EPS)
            if mode == "q":
                yh = yh * (DN_DK ** -0.5)
            outs.append(yh)
        y = jnp.concatenate(outs, axis=1)
    o_ref[...] = y.astype(o_ref.dtype)
    if mode == "k":
        kt_ref = rest[0]
        kt_ref[...] = y.T.astype(kt_ref.dtype)


DN_PREP_COLS = 256
CONV_PAD = 8


def _dn_prep(pbig, conv_w, col0, mode, B, S):
    M = B * S
    tw = DN_PREP_COLS
    col0_blocks = col0 // tw
    nj = DN_WIDTH // tw
    wcol0 = {"q": 0, "k": nj, "v": 2 * nj}[mode]
    kern = functools.partial(_dn_prep_kernel, mode=mode, seq_len=S)
    out_specs = [pl.BlockSpec((S, tw), lambda b, j: (b, j))]
    out_shape = [jax.ShapeDtypeStruct((M, DN_WIDTH), BF16)]
    if mode == "k":
        out_specs.append(pl.BlockSpec((tw, S), lambda b, j: (j, b)))
        out_shape.append(jax.ShapeDtypeStruct((DN_WIDTH, M), BF16))
    return pl.pallas_call(
        kern,
        grid=(B, nj),
        in_specs=[pl.BlockSpec((S, tw), lambda b, j: (b, col0_blocks + j)),
                  pl.BlockSpec((CONV_WIDTH, tw), lambda b, j: (0, wcol0 + j))],
        out_specs=out_specs,
        out_shape=out_shape,
        scratch_shapes=[pltpu.VMEM((CONV_PAD + S, tw), F32)],
        compiler_params=_cparams(("arbitrary", "arbitrary")),
        name="dn_prep_" + mode,
    )(pbig, conv_w)


def _dn_gate_kernel(ab_ref, alog_ref, dtb_ref, o_ref, ot_ref, *, tm):
    ab = ab_ref[...]
    x = ab + dtb_ref[...]
    softplus = jnp.maximum(x, 0.0) + jnp.log(1.0 + jnp.exp(-jnp.abs(x)))
    g = -jnp.exp(alog_ref[...]) * softplus
    beta = _sigmoid(ab)
    lane = lax.broadcasted_iota(jnp.int32, (DN_CHUNK, LANES), 1)
    r = lax.broadcasted_iota(jnp.int32, (DN_CHUNK, DN_CHUNK), 0)
    c = lax.broadcasted_iota(jnp.int32, (DN_CHUNK, DN_CHUNK), 1)
    tril = jnp.where(r >= c, 1.0, 0.0).astype(F32)
    for ci in range(tm // DN_CHUNK):
        rows = slice(ci * DN_CHUNK, (ci + 1) * DN_CHUNK)
        dec = _dot_f32(tril, g[rows])
        out = jnp.where(lane < DN_HEADS, dec, beta[rows])
        o_ref[rows, :] = out
        ot_ref[:, rows] = out.T


def _dn_gate(psmall, alog_row, dtb_row):
    M = psmall.shape[0]
    tm = 512
    kern = functools.partial(_dn_gate_kernel, tm=tm)
    return pl.pallas_call(
        kern,
        grid=(M // tm,),
        in_specs=[pl.BlockSpec((tm, LANES), lambda i: (i, SMALL_AB_BLOCK)),
                  pl.BlockSpec((1, LANES), lambda i: (0, 0)),
                  pl.BlockSpec((1, LANES), lambda i: (0, 0))],
        out_specs=[pl.BlockSpec((tm, LANES), lambda i: (i, 0)),
                   pl.BlockSpec((LANES, tm), lambda i: (0, i))],
        out_shape=[jax.ShapeDtypeStruct((M, LANES), F32),
                   jax.ShapeDtypeStruct((LANES, M), F32)],
        compiler_params=_cparams(("arbitrary",)),
        name="dn_gate",
    )(psmall, alog_row, dtb_row)


def _dn_scan_kernel(q_ref, k_ref, v_ref, kt_ref, dec_ref, dect_ref, z_ref, gain_ref, o_ref, st_ref):
    C = DN_CHUNK

    @pl.when(pl.program_id(1) == 0)
    def _():
        st_ref[...] = jnp.zeros(st_ref.shape, F32)

    dec = dec_ref[...]
    dect = dect_ref[...]
    r = lax.broadcasted_iota(jnp.int32, (C, C), 0)
    c = lax.broadcasted_iota(jnp.int32, (C, C), 1)
    tril = r >= c
    strict = r > c
    gain = gain_ref[...]
    eye = jnp.where(r == c, 1.0, 0.0).astype(F32)
    lvl_masks = []
    for lg in range(int(math.log2(C))):
        lvl_masks.append(((r >> (lg + 1)) == (c >> (lg + 1))) & ((r >> lg) != (c >> lg)))

    for h0 in range(0, DN_HEADS, DN_HEAD_GROUP):
        heads = range(h0, h0 + DN_HEAD_GROUP)
        cols = lambda ref: jnp.stack([ref[:, h * DN_DK:(h + 1) * DN_DK] for h in heads])
        qh = cols(q_ref).astype(F32)
        kh = cols(k_ref).astype(F32)
        vh = cols(v_ref).astype(F32)
        kth = jnp.stack([kt_ref[h * DN_DK:(h + 1) * DN_DK, :] for h in heads])
        dcol = jnp.stack([jnp.broadcast_to(dec[:, h:h + 1], (C, C)) for h in heads])
        bcol = jnp.stack([jnp.broadcast_to(dec[:, DN_HEADS + h:DN_HEADS + h + 1], (C, C)) for h in heads])
        drow = jnp.stack([jnp.broadcast_to(dect[h:h + 1, :], (C, C)) for h in heads])
        dlast = jnp.stack([jnp.broadcast_to(dect[h:h + 1, C - 1:C], (C, C)) for h in heads])
        lmat = jnp.exp(jnp.where(tril[None], dcol - drow, NEG))
        e_d = jnp.exp(dcol)
        kb = kh * bcol
        vb = vh * bcol
        a = jnp.where(strict[None], _bmm(kb.astype(BF16), kth) * lmat, 0.0)
        attn = jnp.where(tril[None], _bmm(qh.astype(BF16), kth) * lmat, 0.0)
        tinv = eye[None] - jnp.where(lvl_masks[0][None], a, 0.0)
        for lm in lvl_masks[1:]:
            t16 = tinv.astype(BF16)
            lo = jnp.where(lm[None], a, 0.0).astype(BF16)
            tinv = tinv - _bmm(t16, _bmm(lo, t16).astype(BF16))
        t16 = tinv.astype(BF16)
        u = _bmm(t16, vb.astype(BF16))
        w = _bmm(t16, (kb * e_d).astype(BF16))
        st = st_ref[h0:h0 + DN_HEAD_GROUP]
        st16 = st.astype(BF16)
        v_new = u - _bmm(w.astype(BF16), st16)
        v_new16 = v_new.astype(BF16)
        o = _bmm((qh * e_d).astype(BF16), st16) + _bmm(attn.astype(BF16), v_new16)
        kdt = (kth.astype(F32) * jnp.exp(dlast - drow)).astype(BF16)
        st_ref[h0:h0 + DN_HEAD_GROUP] = st * jnp.exp(dlast) + _bmm(kdt, v_new16)
        ms = jnp.mean(o * o, axis=-1, keepdims=True)
        y = o * lax.rsqrt(ms + EPS) * gain[None]
        for i, h in enumerate(heads):
            sl = slice(h * DN_DV, (h + 1) * DN_DV)
            o_ref[:, sl] = (y[i] * _silu(z_ref[:, sl].astype(F32))).astype(o_ref.dtype)


def _dn_scan(dq, dk, dv, dkt, dec, dect, pbig, z_col_block, gain_row, B, S):
    M = B * S
    C = DN_CHUNK
    nc = S // C
    tok = lambda b, n: (b * nc + n, 0)
    return pl.pallas_call(
        _dn_scan_kernel,
        grid=(B, nc),
        in_specs=[pl.BlockSpec((C, DN_WIDTH), tok),
                  pl.BlockSpec((C, DN_WIDTH), tok),
                  pl.BlockSpec((C, DN_WIDTH), tok),
                  pl.BlockSpec((DN_WIDTH, C), lambda b, n: (0, b * nc + n)),
                  pl.BlockSpec((C, LANES), tok),
                  pl.BlockSpec((LANES, C), lambda b, n: (0, b * nc + n)),
                  pl.BlockSpec((C, DN_WIDTH), lambda b, n: (b * nc + n, z_col_block)),
                  pl.BlockSpec((1, DN_DV), lambda b, n: (0, 0))],
        out_specs=pl.BlockSpec((C, DN_WIDTH), tok),
        out_shape=jax.ShapeDtypeStruct((M, DN_WIDTH), BF16),
        scratch_shapes=[pltpu.VMEM((DN_HEADS, DN_DK, DN_DV), F32)],
        compiler_params=_cparams(("arbitrary", "arbitrary")),
        name="dn_scan",
    )(dq, dk, dv, dkt, dec, dect, pbig, gain_row)


def _out1_kernel(oa_ref, ob_ref, wa_hbm, wb_hbm, ga_ref, gb_ref, o_ref, wf32_ref, w16_ref, sems, *, tn):
    j = pl.program_id(0)
    i = pl.program_id(1)

    def weight_copy(jj, b):
        src = (wa_hbm, wb_hbm)[b].at[:, pl.ds(pl.multiple_of(jj * tn, tn), tn)]
        return pltpu.make_async_copy(src, wf32_ref.at[b], sems.at[b])

    def merged(cols):
        ya = _dot(oa_ref[...], w16_ref[0, :, cols])
        yb = _dot(ob_ref[...], w16_ref[1, :, cols])
        mix = _sigmoid(ga_ref[:, cols].astype(F32)) * ya + _sigmoid(gb_ref[:, cols].astype(F32)) * yb
        o_ref[:, cols] = mix.astype(o_ref.dtype)

    @pl.when((j == 0) & (i == 0))
    def _():
        weight_copy(0, 0).start()
        weight_copy(0, 1).start()

    @pl.when(i == 0)
    def _():
        weight_copy(j, 0).wait()
        weight_copy(j, 1).wait()
        cc = min(MM_CAST_COLS, tn)
        for c0 in range(0, tn, cc):
            cols = slice(c0, c0 + cc)
            w16_ref[0, :, cols] = wf32_ref[0, :, cols].astype(BF16)
            w16_ref[1, :, cols] = wf32_ref[1, :, cols].astype(BF16)
            merged(cols)

        @pl.when(j + 1 < pl.num_programs(0))
        def _():
            weight_copy(j + 1, 0).start()
            weight_copy(j + 1, 1).start()

    @pl.when(i > 0)
    def _():
        merged(slice(0, tn))


OUT1_TM = 512


def _out1(o_a, o_b, wa, wb, pbig, D):
    M, K = o_a.shape
    tm = min(OUT1_TM, M)
    tn = min(MM_TN, D)
    nbd = D // tn
    return pl.pallas_call(
        functools.partial(_out1_kernel, tn=tn),
        grid=(D // tn, M // tm),
        in_specs=[pl.BlockSpec((tm, K), lambda j, i: (i, 0)),
                  pl.BlockSpec((tm, K), lambda j, i: (i, 0)),
                  pl.BlockSpec(memory_space=pl.ANY),
                  pl.BlockSpec(memory_space=pl.ANY),
                  pl.BlockSpec((tm, tn), lambda j, i: (i, j)),
                  pl.BlockSpec((tm, tn), lambda j, i: (i, nbd + j))],
        out_specs=pl.BlockSpec((tm, tn), lambda j, i: (i, j)),
        out_shape=jax.ShapeDtypeStruct((M, D), BF16),
        scratch_shapes=[pltpu.VMEM((2, K, tn), F32), pltpu.VMEM((2, K, tn), BF16),
                        pltpu.SemaphoreType.DMA((2,))],
        compiler_params=_cparams(("arbitrary", "arbitrary")),
        name="out1",
    )(o_a, o_b, wa, wb, pbig, pbig)


def _final_kernel(mix_ref, x_ref, gate_ref, fg_ref, o_ref):
    xn = x_ref[...] + gate_ref[0] * mix_ref[...].astype(F32)
    ms = jnp.mean(xn * xn, axis=-1, keepdims=True)
    o_ref[...] = xn * lax.rsqrt(ms + EPS) * fg_ref[...]


def _final(mixed, x2, mod3, final_gain, S):
    M, D = x2.shape
    tm = ROW_TILE
    nb = S // tm
    return pl.pallas_call(
        _final_kernel,
        grid=(M // tm,),
        in_specs=[pl.BlockSpec((tm, D), lambda i: (i, 0)),
                  pl.BlockSpec((tm, D), lambda i: (i, 0)),
                  pl.BlockSpec((1, 1, D), lambda i: (i // nb, 0, 2)),
                  pl.BlockSpec((1, D), lambda i: (0, 0))],
        out_specs=pl.BlockSpec((tm, D), lambda i: (i, 0)),
        out_shape=jax.ShapeDtypeStruct((M, D), F32),
        compiler_params=_cparams(("arbitrary",)),
        name="final",
    )(mixed, x2, mod3, final_gain)


def _pad_cols(w, width):
    return jnp.pad(w, ((0, 0), (0, width - w.shape[1])))


def _proj_row_offsets(D):
    widths = (NSA_WIDTH, 6 * KV_WIDTH, 3 * NSA_HEADS, NSA_WIDTH, 3 * DN_WIDTH, DN_HEADS, DN_HEADS, DN_WIDTH, 2 * D)
    offs = [int(o) for o in np.concatenate([[0], np.cumsum(widths)])]
    tiles = lambda seg: [offs[seg] + t * MM_TN for t in range(widths[seg] // MM_TN)]
    big = tiles(8) + tiles(4) + tiles(0) + tiles(3) + tiles(7)
    kv0 = offs[1]
    tail = [kv0, kv0 + TAIL_TN]
    small = [offs[2], offs[5]]
    return big, tail, small


TAIL_TN = 3 * KV_WIDTH
TAIL_TM = 1024
TAIL_CMP_BLOCK = 0
SMALL_GATE_BLOCK = 0
SMALL_AB_BLOCK = 1


def kernel(x, c, positions, w_ada, b_ada, norm_gain, w_in, cmp_pos_k, cmp_pos_v, w_cmp_k1, w_cmp_k2,
           w_cmp_v1, w_cmp_v2, conv_w, dt_bias, a_log, dn_norm_gain, w_proj_a, w_proj_b, w_out, final_gain):
    B, S, D = x.shape
    M = B * S
    depth = w_in.shape[0]
    assert S % SEL_TILE == 0 and S % DN_CHUNK == 0 and S >= WINDOW + Q_BLOCK and D % 512 == 0 and B <= 8
    assert S // SLC_LEN <= LANES and D % 1024 == 0 and DN_DK == DN_CHUNK
    assert depth == 1, "the final RMSNorm is fused into the last layer's output kernel"

    off_dq = 2 * D
    off_q = off_dq + 3 * DN_WIDTH
    off_z = off_q + NSA_WIDTH
    off_dz = off_z + NSA_WIDTH

    x2 = x.reshape(M, D)
    pos_col = positions.reshape(M, 1)
    cmp_end = np.arange(S // CMP_STRIDE - 1) * CMP_STRIDE + CMP_LEN - 1
    pos_cmp = jnp.pad(positions[:, cmp_end], ((0, 0), (0, 1)))[:, :, None]
    c8 = jnp.pad(c, ((0, 8 - B), (0, 0)))

    for l in range(depth):
        mod = _ada(c8, w_ada[l], b_ada[l][None, :])
        mod3 = mod[:B].reshape(B, 1, 3 * D)
        h = _norm(x2, norm_gain[l][None, :], mod3, S)

        w_nk = jnp.transpose(w_in[l])
        big_rows, tail_rows, small_rows = _proj_row_offsets(D)
        pbig = _proj(h, w_nk, big_rows, MM_TM, MM_TN, BF16, "proj_big")
        ptail = _proj(h, w_nk, tail_rows, TAIL_TM, TAIL_TN, F32, "proj_tail")
        psmall = _proj(h, w_nk, small_rows, TAIL_TM, 2 * LANES, F32, "proj_small", pieces=2)

        q_t, keys, v_t = _nsa_prep(pbig, ptail, pos_col, off_q // NSA_WIDTH, S)
        kcmp = _compress(ptail, TAIL_CMP_BLOCK, cmp_pos_k[l], w_cmp_k1[l], w_cmp_k2[l], pos_cmp, B, S, True)
        vcmp = _compress(ptail, TAIL_CMP_BLOCK + NSA_GROUPS, cmp_pos_v[l], w_cmp_v1[l], w_cmp_v2[l].T,
                         pos_cmp, B, S, False)
        o_a = _nsa_attn(q_t, kcmp, vcmp, keys, v_t, psmall, pbig, off_z // NSA_WIDTH, B, S)

        cw = conv_w[l]
        dq = _dn_prep(pbig, cw, off_dq, "q", B, S)[0]
        dk, dkt = _dn_prep(pbig, cw, off_dq + DN_WIDTH, "k", B, S)
        dv = _dn_prep(pbig, cw, off_dq + 2 * DN_WIDTH, "v", B, S)[0]
        alog_row = _pad_cols(a_log[l][None, :].astype(F32), LANES)
        dtb_row = _pad_cols(dt_bias[l][None, :].astype(F32), LANES)
        dec, dect = _dn_gate(psmall, alog_row, dtb_row)
        o_b = _dn_scan(dq, dk, dv, dkt, dec, dect, pbig, off_dz // DN_WIDTH,
                       dn_norm_gain[l][None, :], B, S)

        mixed_in = _out1(o_a, o_b, w_proj_a[l], w_proj_b[l], pbig, D)
        mixed = _matmul(mixed_in, w_out[l], BF16, "out2")
        x2 = _final(mixed, x2, mod3, final_gain[None, :], S)
    return x2.reshape(B, S, D)
```

```python
import functools
import math

import numpy as np
import jax
import jax.numpy as jnp
from jax import lax
from jax.experimental import pallas as pl
from jax.experimental.pallas import tpu as pltpu

F32 = jnp.float32
BF16 = jnp.bfloat16

NSA_HEADS = 16
NSA_GROUPS = 2
NSA_HPG = NSA_HEADS // NSA_GROUPS
HEAD_DIM = 128
ROT_DIM = HEAD_DIM // 4
ROT_HALF = ROT_DIM // 2
ROPE_THETA = 500000.0
CMP_LEN = 32
CMP_STRIDE = 16
CMP_HIDDEN = 256
SLC_LEN = 64
SLC_TOPK = 16
WINDOW = 512
Q_BLOCK = 128
NSA_WIDTH = NSA_HEADS * HEAD_DIM
KV_WIDTH = NSA_GROUPS * HEAD_DIM
DN_HEADS = 16
DN_DK = 128
DN_DV = 128
DN_WIDTH = DN_HEADS * DN_DV
CONV_WIDTH = 4
EPS = 1e-6

LANES = 128
VMEM_LIMIT_BYTES = 56 * 1024 * 1024
ROW_TILE = 512

DN_CHUNK = 128
DN_HEAD_GROUP = 16
SEL_TILE = 512
VT_ROWS = HEAD_DIM + 16
NEG = -1e30


def _cparams(sem):
    return pltpu.CompilerParams(dimension_semantics=sem, vmem_limit_bytes=VMEM_LIMIT_BYTES)


def _sigmoid(x):
    return 1.0 / (1.0 + jnp.exp(-x))


def _silu(x):
    return x * _sigmoid(x)


def _dot(a, b):
    return jnp.dot(a, b, preferred_element_type=F32)


def _dot_nt(a, b):
    return lax.dot_general(a, b, (((1,), (1,)), ((), ())), preferred_element_type=F32)


def _bmm(a, b):
    return lax.dot_general(a, b, (((2,), (1,)), ((0,), (0,))), preferred_element_type=F32)


def _dot_f32(a, b):
    return jnp.dot(a, b, preferred_element_type=F32, precision=lax.Precision.HIGHEST)


def _ada_kernel(c_ref, w_ref, b_ref, o_ref):
    c = c_ref[...]
    c_hi = c.astype(BF16)
    c_lo = (c - c_hi.astype(F32)).astype(BF16)
    w = w_ref[...]
    w_hi = w.astype(BF16)
    w_lo = (w - w_hi.astype(F32)).astype(BF16)
    acc = _dot(c_hi, w_hi) + _dot(c_lo, w_hi) + _dot(c_hi, w_lo)
    o_ref[...] = acc + b_ref[...]


def _ada(c8, w_ada, b_ada):
    D, N = w_ada.shape
    tn = min(512, N)
    return pl.pallas_call(
        _ada_kernel,
        grid=(N // tn,),
        in_specs=[pl.BlockSpec((8, D), lambda j: (0, 0)),
                  pl.BlockSpec((D, tn), lambda j: (0, j)),
                  pl.BlockSpec((1, tn), lambda j: (0, j))],
        out_specs=pl.BlockSpec((8, tn), lambda j: (0, j)),
        out_shape=jax.ShapeDtypeStruct((8, N), F32),
        compiler_params=_cparams(("arbitrary",)),
        name="ada",
    )(c8, w_ada, b_ada)


def _norm_kernel(x_ref, gain_ref, shift_ref, scale_ref, o_ref):
    x = x_ref[...]
    ms = jnp.mean(x * x, axis=-1, keepdims=True)
    y = x * lax.rsqrt(ms + EPS) * gain_ref[...]
    o_ref[...] = (y * (1.0 + scale_ref[0]) + shift_ref[0]).astype(o_ref.dtype)


def _norm(x2, gain, mod3, S):
    M, D = x2.shape
    tm = ROW_TILE
    nb = S // tm
    return pl.pallas_call(
        _norm_kernel,
        grid=(M // tm,),
        in_specs=[pl.BlockSpec((tm, D), lambda i: (i, 0)),
                  pl.BlockSpec((1, D), lambda i: (0, 0)),
                  pl.BlockSpec((1, 1, D), lambda i: (i // nb, 0, 0)),
                  pl.BlockSpec((1, 1, D), lambda i: (i // nb, 0, 1))],
        out_specs=pl.BlockSpec((tm, D), lambda i: (i, 0)),
        out_shape=jax.ShapeDtypeStruct((M, D), BF16),
        compiler_params=_cparams(("arbitrary",)),
        name="norm",
    )(x2, gain, mod3, mod3)


MM_TM = 1024
MM_TN = 1024
W_ROW_ALIGN = 16
WT_CHUNK = 128
MM_CAST_COLS = 256


def _mm_kernel(offs_ref, a_ref, w_hbm, o_ref, wf32_ref, w16_ref, sems, *, w_is_nk, tn, pieces):
    j = pl.program_id(0)
    i = pl.program_id(1)
    pw = tn // pieces

    def piece_copy(jj, p):
        off = offs_ref[jj * pieces + p]
        if w_is_nk:
            src = w_hbm.at[pl.ds(pl.multiple_of(off * W_ROW_ALIGN, W_ROW_ALIGN), pw), :]
            dst = wf32_ref.at[p * pw:(p + 1) * pw, :]
        else:
            src = w_hbm.at[:, pl.ds(pl.multiple_of(off * LANES, LANES), pw)]
            dst = wf32_ref.at[:, p * pw:(p + 1) * pw]
        return pltpu.make_async_copy(src, dst, sems.at[p])

    def start_tile(jj):
        for p in range(pieces):
            piece_copy(jj, p).start()

    @pl.when((j == 0) & (i == 0))
    def _():
        start_tile(0)

    @pl.when(i == 0)
    def _():
        for p in range(pieces):
            piece_copy(j, p).wait()
        cc = min(MM_CAST_COLS, tn)
        for c0 in range(0, tn, cc):
            if w_is_nk:
                for c in range(c0, c0 + cc, WT_CHUNK):
                    w16_ref[:, c:c + WT_CHUNK] = wf32_ref[c:c + WT_CHUNK, :].T.astype(BF16)
            else:
                w16_ref[:, c0:c0 + cc] = wf32_ref[:, c0:c0 + cc].astype(BF16)
            o_ref[:, c0:c0 + cc] = _dot(a_ref[...], w16_ref[:, c0:c0 + cc]).astype(o_ref.dtype)

        @pl.when(j + 1 < pl.num_programs(0))
        def _():
            start_tile(j + 1)

    @pl.when(i > 0)
    def _():
        o_ref[...] = _dot(a_ref[...], w16_ref[...]).astype(o_ref.dtype)


def _mm_call(a, w, offs, w_is_nk, tm, tn, out_dtype, name, pieces=1):
    M, K = a.shape
    tm = min(tm, M)
    n_tiles = len(offs) // pieces
    grid_spec = pltpu.PrefetchScalarGridSpec(
        num_scalar_prefetch=1,
        grid=(n_tiles, M // tm),
        in_specs=[pl.BlockSpec((tm, K), lambda j, i, o: (i, 0)),
                  pl.BlockSpec(memory_space=pl.ANY)],
        out_specs=pl.BlockSpec((tm, tn), lambda j, i, o: (i, j)),
        scratch_shapes=[pltpu.VMEM((tn, K) if w_is_nk else (K, tn), F32),
                        pltpu.VMEM((K, tn), BF16),
                        pltpu.SemaphoreType.DMA((pieces,))],
    )
    return pl.pallas_call(
        functools.partial(_mm_kernel, w_is_nk=w_is_nk, tn=tn, pieces=pieces),
        grid_spec=grid_spec,
        out_shape=jax.ShapeDtypeStruct((M, n_tiles * tn), out_dtype),
        compiler_params=_cparams(("arbitrary", "arbitrary")),
        name=name,
    )(jnp.asarray(np.asarray(offs, np.int32)), a, w)


def _matmul(a, w, out_dtype, name):
    N = w.shape[1]
    tn = min(MM_TN, N)
    offs = [t * tn // LANES for t in range(N // tn)]
    return _mm_call(a, w, offs, False, MM_TM, tn, out_dtype, name)


def _proj(h, w_nk, row_offsets, tm, tn, out_dtype, name, pieces=1):
    assert all(int(o) % W_ROW_ALIGN == 0 and int(o) + tn // pieces <= w_nk.shape[0] for o in row_offsets)
    offs = [int(o) // W_ROW_ALIGN for o in row_offsets]
    return _mm_call(h, w_nk, offs, True, tm, tn, out_dtype, name, pieces)


def _rope_consts():
    inv = ROPE_THETA ** (-np.arange(ROT_HALF, dtype=np.float64) / ROT_HALF)
    invf = np.zeros((1, LANES), np.float32)
    invf[0, :ROT_HALF] = inv
    invf[0, ROT_HALF:ROT_DIM] = inv
    sgn = np.zeros((1, LANES), np.float32)
    sgn[0, :ROT_HALF] = -1.0
    sgn[0, ROT_HALF:ROT_DIM] = 1.0
    return jnp.asarray(invf), jnp.asarray(sgn)


def _rope_tables(pos_col, invf, sgn):
    ang = pos_col.astype(F32) * invf
    return jnp.cos(ang), jnp.sin(ang) * sgn


def _rope_apply(x, cos_t, sin_t):
    lane = lax.broadcasted_iota(jnp.int32, x.shape, 1)
    partner = jnp.where(lane < ROT_HALF,
                        pltpu.roll(x, LANES - ROT_HALF, 1),
                        pltpu.roll(x, ROT_HALF, 1))
    return x * cos_t + partner * sin_t


KEYS_WIDTH = 6 * HEAD_DIM


def _nsa_prep_kernel(q_ref, kvs_ref, kvw_ref, pos_ref, invf_ref, sgn_ref, qt_ref, keys_ref, vt_ref, *, seq_len):
    tm = q_ref.shape[0]
    cos_t, sin_t = _rope_tables(pos_ref[...], invf_ref[...], sgn_ref[...])
    qscale = (HEAD_DIM ** -0.5) * math.log2(math.e)
    for h in range(NSA_HEADS):
        sl = slice(h * HEAD_DIM, (h + 1) * HEAD_DIM)
        xq = _rope_apply(q_ref[:, sl].astype(F32), cos_t, sin_t) * qscale
        qt_ref[sl, :] = xq.T.astype(qt_ref.dtype)
    tok = (pl.program_id(0) * tm) % seq_len + lax.broadcasted_iota(jnp.int32, (tm, LANES), 0)
    lane = lax.broadcasted_iota(jnp.int32, (tm, LANES), 1)
    onehot = jnp.where(lane == (tok >> int(math.log2(SLC_LEN))), 1.0, 0.0).astype(keys_ref.dtype)
    blk = lambda i: (kvs_ref if i < 4 else kvw_ref)[:, (i % 4) * HEAD_DIM:(i % 4 + 1) * HEAD_DIM]
    for g in range(NSA_GROUPS):
        ks = _rope_apply(blk(g), cos_t, sin_t)
        keys_ref[:, (2 * g) * HEAD_DIM:(2 * g + 1) * HEAD_DIM] = ks.astype(keys_ref.dtype)
        keys_ref[:, (2 * g + 1) * HEAD_DIM:(2 * g + 2) * HEAD_DIM] = onehot
        kw = _rope_apply(blk(4 + g), cos_t, sin_t)
        keys_ref[:, (4 + g) * HEAD_DIM:(5 + g) * HEAD_DIM] = kw.astype(keys_ref.dtype)
        ones = jnp.ones((VT_ROWS - HEAD_DIM, tm), vt_ref.dtype)
        for c, src in ((g, 2 + g), (2 + g, 6 + g)):
            vt_ref[c * VT_ROWS:c * VT_ROWS + HEAD_DIM, :] = blk(src).T.astype(vt_ref.dtype)
            vt_ref[c * VT_ROWS + HEAD_DIM:(c + 1) * VT_ROWS, :] = ones


def _nsa_prep(pbig, ptail, pos_col, q_col_block, S):
    M = pbig.shape[0]
    tm = ROW_TILE
    invf, sgn = _rope_consts()
    return pl.pallas_call(
        functools.partial(_nsa_prep_kernel, seq_len=S),
        grid=(M // tm,),
        in_specs=[pl.BlockSpec((tm, NSA_WIDTH), lambda i: (i, q_col_block)),
                  pl.BlockSpec((tm, 4 * HEAD_DIM), lambda i: (i, 1)),
                  pl.BlockSpec((tm, 4 * HEAD_DIM), lambda i: (i, 2)),
                  pl.BlockSpec((tm, 1), lambda i: (i, 0)),
                  pl.BlockSpec((1, LANES), lambda i: (0, 0)),
                  pl.BlockSpec((1, LANES), lambda i: (0, 0))],
        out_specs=[pl.BlockSpec((NSA_WIDTH, tm), lambda i: (0, i)),
                   pl.BlockSpec((tm, KEYS_WIDTH), lambda i: (i, 0)),
                   pl.BlockSpec((4 * VT_ROWS, tm), lambda i: (0, i))],
        out_shape=[jax.ShapeDtypeStruct((NSA_WIDTH, M), BF16),
                   jax.ShapeDtypeStruct((M, KEYS_WIDTH), BF16),
                   jax.ShapeDtypeStruct((4 * VT_ROWS, M), BF16)],
        compiler_params=_cparams(("arbitrary",)),
        name="nsa_prep",
    )(pbig, ptail, ptail, pos_col, invf, sgn)


def _compress_kernel(x_ref, pe_ref, w1_ref, w2_ref, pos_ref, invf_ref, sgn_ref, o_ref, *, rope, n_rows):
    half = CMP_LEN // 2
    top = jnp.zeros((n_rows, CMP_HIDDEN), F32)
    bot = jnp.zeros((n_rows, CMP_HIDDEN), F32)
    for l in range(half):
        xl = x_ref[pl.ds(l, n_rows, stride=CMP_STRIDE), :]
        w_top = w1_ref[l * HEAD_DIM:(l + 1) * HEAD_DIM, :].astype(BF16)
        w_bot = w1_ref[(half + l) * HEAD_DIM:(half + l + 1) * HEAD_DIM, :].astype(BF16)
        top = top + _dot((xl + pe_ref[l:l + 1, :]).astype(BF16), w_top)
        bot = bot + _dot((xl + pe_ref[half + l:half + l + 1, :]).astype(BF16), w_bot)
    hid = top + pltpu.roll(bot, n_rows - 1, 0)
    act = _silu(hid).astype(BF16)
    if rope:
        out = _dot(act, w2_ref[...].astype(BF16))
        cos_t, sin_t = _rope_tables(pos_ref[0], invf_ref[...], sgn_ref[...])
        o_ref[0, 0] = _rope_apply(out, cos_t, sin_t).astype(o_ref.dtype)
    else:
        o_ref[0, 0, 0:HEAD_DIM, :] = _dot_nt(w2_ref[...].astype(BF16), act).astype(o_ref.dtype)
        o_ref[0, 0, HEAD_DIM:VT_ROWS, :] = jnp.ones((VT_ROWS - HEAD_DIM, n_rows), o_ref.dtype)


def _compress(ptail, col_block0, pe, w1, w2, pos_cmp, B, S, rope):
    n_rows = S // CMP_STRIDE
    invf, sgn = _rope_consts()
    kern = functools.partial(_compress_kernel, rope=rope, n_rows=n_rows)
    out_dims = (n_rows, HEAD_DIM) if rope else (VT_ROWS, n_rows)
    return pl.pallas_call(
        kern,
        grid=(B, NSA_GROUPS),
        in_specs=[pl.BlockSpec((S, HEAD_DIM), lambda b, g: (b, col_block0 + g)),
                  pl.BlockSpec((CMP_LEN, HEAD_DIM), lambda b, g: (0, 0)),
                  pl.BlockSpec((CMP_LEN * HEAD_DIM, CMP_HIDDEN), lambda b, g: (0, 0)),
                  pl.BlockSpec(w2.shape, lambda b, g: (0, 0)),
                  pl.BlockSpec((1, n_rows, 1), lambda b, g: (b, 0, 0)),
                  pl.BlockSpec((1, LANES), lambda b, g: (0, 0)),
                  pl.BlockSpec((1, LANES), lambda b, g: (0, 0))],
        out_specs=pl.BlockSpec((1, 1) + out_dims, lambda b, g: (b, g, 0, 0)),
        out_shape=jax.ShapeDtypeStruct((B, NSA_GROUPS) + out_dims, BF16),
        compiler_params=_cparams(("arbitrary", "arbitrary")),
        name="compress_k" if rope else "compress_v",
    )(ptail, pe, w1, w2, pos_cmp, invf, sgn)


def _tile_lanes(x, n):
    return jnp.concatenate([x] * n, axis=1)


def _nsa_attn_kernel(qt_ref, kc_ref, vct_ref, ks_ref, vst_ref, kw_ref, vwt_ref, g_ref, z_ref,
                     ovt_ref, o_ref, m_s, acc_s, qa_s, sa_s, sb_s, sc_s, *, seq_len):
    T = Q_BLOCK
    P = NSA_HPG
    R = P * T
    n_cmp_pad = seq_len // CMP_STRIDE
    n_cmp = n_cmp_pad - 1
    nb = seq_len // SLC_LEN
    qi = pl.program_id(2)
    qs = qi * T
    d0 = pl.multiple_of(qs, T)
    wlen = WINDOW + T
    w0 = pl.multiple_of(jnp.maximum(qs - WINDOW, 0), T)

    qt = jnp.concatenate([qt_ref[p * HEAD_DIM:(p + 1) * HEAD_DIM, :] for p in range(P)], axis=1)

    s_c = _dot(kc_ref[0, 0], qt)
    s_w = _dot(kw_ref[pl.ds(w0, wlen), :], qt)
    s_d = _dot(ks_ref[pl.ds(d0, T), 0:HEAD_DIM], qt)

    tok = lambda n: qs + lax.broadcasted_iota(jnp.int32, (n, T), 1)
    row = lambda n: lax.broadcasted_iota(jnp.int32, (n, T), 0)
    ok_c = (row(n_cmp_pad) * CMP_STRIDE + (CMP_LEN - 1) <= tok(n_cmp_pad)) & (row(n_cmp_pad) < n_cmp)
    key_w = w0 + row(wlen)
    ok_w = (key_w <= tok(wlen)) & (tok(wlen) - key_w < WINDOW)
    ok_d = row(T) <= lax.broadcasted_iota(jnp.int32, (T, T), 1)
    addmask = lambda ok: _tile_lanes(jnp.where(ok, 0.0, NEG).astype(F32), P)
    s_c = s_c + addmask(ok_c)
    s_w = s_w + addmask(ok_w)
    s_d = s_d + addmask(ok_d)

    m_c = jnp.max(s_c, axis=0, keepdims=True)
    m_w = jnp.max(s_w, axis=0, keepdims=True)
    m_d = jnp.max(s_d, axis=0, keepdims=True)
    m_c = jnp.where(m_c > 0.5 * NEG, m_c, 0.0)
    p_c = jnp.exp2(s_c - m_c)
    p_w = jnp.exp2(s_w - m_w)
    p_d = jnp.exp2(s_d - m_d)
    o_c = _dot(vct_ref[0, 0], p_c.astype(BF16))
    o_w = _dot(vwt_ref[:, pl.ds(w0, wlen)], p_w.astype(BF16))
    inv_c = 1.0 / jnp.maximum(jnp.sum(p_c, axis=0, keepdims=True), 1e-30)
    inv_w = 1.0 / o_w[HEAD_DIM:HEAD_DIM + 1, :]
    m_s[...] = m_d
    acc_s[...] = _dot(vst_ref[:, pl.ds(d0, T)], p_d.astype(BF16))

    pn = p_c * inv_c
    p_sum = pn[:, 0:T]
    for p in range(1, P):
        p_sum = p_sum + pn[:, p * T:(p + 1) * T]
    ps_hi = p_sum.astype(BF16)
    ps_lo = (p_sum - ps_hi.astype(F32)).astype(BF16)
    imp = (_dot(ovt_ref[...], ps_hi) + _dot(ovt_ref[...], ps_lo))[0:nb]
    jrow = row(nb)
    t_lane = tok(nb)
    tb = t_lane >> int(math.log2(SLC_LEN))
    visible = jrow * SLC_LEN <= t_lane
    forced = (jrow == 0) | (jrow == tb) | (jrow == tb - 1)
    score = jnp.where(forced, 1e9, jnp.where(visible, imp, -jnp.inf))
    cnt = jnp.zeros((nb, T), F32)
    for k in range(nb):
        rk = score[k:k + 1, :]
        cnt = cnt + jnp.where(jrow > k, jnp.where(rk >= score, 1.0, 0.0), jnp.where(rk > score, 1.0, 0.0))
    keep = (cnt < float(min(SLC_TOPK, nb))) & visible & (jrow < 2 * qi)
    bias = jnp.where(keep, 0.0, NEG).astype(F32)
    bias = jnp.concatenate([bias, jnp.full((LANES - nb, T), NEG, F32)], axis=0) if nb < LANES else bias
    qa_s[0:HEAD_DIM, :] = qt
    qa_s[HEAD_DIM:2 * HEAD_DIM, :] = _tile_lanes(bias.astype(BF16), P)

    n_t = (qs + SEL_TILE - 1) // SEL_TILE

    def scores_into(kt, dst):
        dst[...] = _dot(ks_ref[kt * SEL_TILE:(kt + 1) * SEL_TILE, :], qa_s[...])

    def consume(kt, src):
        k0 = kt * SEL_TILE
        s = src[...]
        m_prev = m_s[...]
        m_new = jnp.maximum(m_prev, jnp.max(s, axis=0, keepdims=True))
        alpha = jnp.exp2(m_prev - m_new)
        p = jnp.exp2(s - m_new)
        acc_s[...] = alpha * acc_s[...] + _dot(vst_ref[:, pl.ds(k0, SEL_TILE)], p.astype(BF16))
        m_s[...] = m_new

    bufs = (sa_s, sb_s, sc_s)

    def finish():
        gates_t = _sigmoid(g_ref[...]).T
        first_group = pl.program_id(1) == 0

        def grow(r):
            rows = []
            for p in range(P):
                h0 = r * NSA_HEADS + p
                rows.append(jnp.where(first_group, gates_t[h0:h0 + 1, :], gates_t[h0 + P:h0 + P + 1, :]))
            return jnp.concatenate(rows, axis=1)

        o_s = acc_s[...]
        hd = slice(0, HEAD_DIM)
        o_t = (o_c[hd] * (grow(0) * inv_c) + o_s[hd] * (grow(1) / o_s[HEAD_DIM:HEAD_DIM + 1, :])
               + o_w[hd] * (grow(2) * inv_w))
        for p in range(P):
            sl = slice(p * HEAD_DIM, (p + 1) * HEAD_DIM)
            o_p = o_t[:, p * T:(p + 1) * T].T
            o_ref[:, sl] = (o_p * _silu(z_ref[:, sl].astype(F32))).astype(o_ref.dtype)

    for r in range(0, seq_len // SEL_TILE + 1):
        @pl.when(n_t == r)
        def _(r=r):
            if r > 0:
                scores_into(0, bufs[0])
            for t in range(r):
                if t + 1 < r:
                    scores_into(t + 1, bufs[(t + 1) % len(bufs)])
                consume(t, bufs[t % len(bufs)])
            finish()


def _nsa_consts(S):
    n_cmp_pad = S // CMP_STRIDE
    nb = S // SLC_LEN
    cmp_start = np.arange(n_cmp_pad) * CMP_STRIDE
    slc_start = np.arange(LANES) * SLC_LEN
    ov = ((cmp_start[:, None] <= slc_start[None, :] + SLC_LEN - 1)
          & (cmp_start[:, None] + CMP_LEN - 1 >= slc_start[None, :])
          & (np.arange(LANES)[None, :] < nb)
          & (np.arange(n_cmp_pad)[:, None] < n_cmp_pad - 1)).astype(np.float32)
    return jnp.asarray(ov.T, dtype=BF16)


def _nsa_attn(q_t, kcmp, vcmp_t, keys, v_t, psmall, pbig, z_col_block, B, S):
    M = B * S
    nq = S // Q_BLOCK
    gw = NSA_HPG * HEAD_DIM
    ovt = _nsa_consts(S)
    n_cmp_pad = S // CMP_STRIDE
    R = NSA_HPG * Q_BLOCK
    kern = functools.partial(_nsa_attn_kernel, seq_len=S)
    return pl.pallas_call(
        kern,
        grid=(B, NSA_GROUPS, nq),
        in_specs=[pl.BlockSpec((gw, Q_BLOCK), lambda b, g, i: (g, b * nq + i)),
                  pl.BlockSpec((1, 1, n_cmp_pad, HEAD_DIM), lambda b, g, i: (b, g, 0, 0)),
                  pl.BlockSpec((1, 1, VT_ROWS, n_cmp_pad), lambda b, g, i: (b, g, 0, 0)),
                  pl.BlockSpec((S, 2 * HEAD_DIM), lambda b, g, i: (b, g)),
                  pl.BlockSpec((VT_ROWS, S), lambda b, g, i: (g, b)),
                  pl.BlockSpec((S, HEAD_DIM), lambda b, g, i: (b, 4 + g)),
                  pl.BlockSpec((VT_ROWS, S), lambda b, g, i: (2 + g, b)),
                  pl.BlockSpec((Q_BLOCK, LANES), lambda b, g, i: (b * nq + i, SMALL_GATE_BLOCK)),
                  pl.BlockSpec((Q_BLOCK, gw), lambda b, g, i: (b * nq + i, z_col_block * NSA_GROUPS + g)),
                  pl.BlockSpec((LANES, n_cmp_pad), lambda b, g, i: (0, 0))],
        out_specs=pl.BlockSpec((Q_BLOCK, gw), lambda b, g, i: (b * nq + i, g)),
        out_shape=jax.ShapeDtypeStruct((M, NSA_WIDTH), BF16),
        scratch_shapes=[pltpu.VMEM((1, R), F32),
                        pltpu.VMEM((VT_ROWS, R), F32),
                        pltpu.VMEM((2 * HEAD_DIM, R), BF16),
                        pltpu.VMEM((SEL_TILE, R), F32),
                        pltpu.VMEM((SEL_TILE, R), F32),
                        pltpu.VMEM((SEL_TILE, R), F32)],
        compiler_params=_cparams(("arbitrary", "arbitrary", "arbitrary")),
        name="nsa_attn",
    )(q_t, kcmp, vcmp_t, keys, v_t, keys, v_t, psmall, pbig, ovt)


def _dn_prep_kernel(x_ref, w_ref, o_ref, *rest, mode, seq_len):
    xpad_ref = rest[-1]
    w = w_ref[...]
    xpad_ref[0:CONV_PAD, :] = jnp.zeros((CONV_PAD, xpad_ref.shape[1]), F32)
    xpad_ref[CONV_PAD:, :] = x_ref[...].astype(F32)
    y = xpad_ref[CONV_PAD:, :] * w[CONV_WIDTH - 1:CONV_WIDTH, :]
    for k in range(1, CONV_WIDTH):
        y = y + xpad_ref[pl.ds(CONV_PAD - k, seq_len), :] * w[CONV_WIDTH - 1 - k:CONV_WIDTH - k, :]
    y = _silu(y)
    if mode in ("q", "k"):
        outs = []
        for h in range(y.shape[1] // DN_DK):
            yh = y[:, h * DN_DK:(h + 1) * DN_DK]
            ss = jnp.sum(yh * yh, axis=-1, keepdims=True)
            yh = yh * lax.rsqrt(ss + EPS)
            if mode == "q":
                yh = yh * (DN_DK ** -0.5)
            outs.append(yh)
        y = jnp.concatenate(outs, axis=1)
    o_ref[...] = y.astype(o_ref.dtype)
    if mode == "k":
        kt_ref = rest[0]
        kt_ref[...] = y.T.astype(kt_ref.dtype)


DN_PREP_COLS = 128
CONV_PAD = 8


def _dn_prep(pbig, conv_w, col0, mode, B, S):
    M = B * S
    tw = DN_PREP_COLS
    col0_blocks = col0 // tw
    nj = DN_WIDTH // tw
    wcol0 = {"q": 0, "k": nj, "v": 2 * nj}[mode]
    kern = functools.partial(_dn_prep_kernel, mode=mode, seq_len=S)
    out_specs = [pl.BlockSpec((S, tw), lambda b, j: (b, j))]
    out_shape = [jax.ShapeDtypeStruct((M, DN_WIDTH), BF16)]
    if mode == "k":
        out_specs.append(pl.BlockSpec((tw, S), lambda b, j: (j, b)))
        out_shape.append(jax.ShapeDtypeStruct((DN_WIDTH, M), BF16))
    return pl.pallas_call(
        kern,
        grid=(B, nj),
        in_specs=[pl.BlockSpec((S, tw), lambda b, j: (b, col0_blocks + j)),
                  pl.BlockSpec((CONV_WIDTH, tw), lambda b, j: (0, wcol0 + j))],
        out_specs=out_specs,
        out_shape=out_shape,
        scratch_shapes=[pltpu.VMEM((CONV_PAD + S, tw), F32)],
        compiler_params=_cparams(("arbitrary", "arbitrary")),
        name="dn_prep_" + mode,
    )(pbig, conv_w)


def _dn_gate_kernel(ab_ref, alog_ref, dtb_ref, o_ref, ot_ref, *, tm):
    ab = ab_ref[...]
    x = ab + dtb_ref[...]
    softplus = jnp.maximum(x, 0.0) + jnp.log(1.0 + jnp.exp(-jnp.abs(x)))
    g = -jnp.exp(alog_ref[...]) * softplus
    beta = _sigmoid(ab)
    lane = lax.broadcasted_iota(jnp.int32, (DN_CHUNK, LANES), 1)
    r = lax.broadcasted_iota(jnp.int32, (DN_CHUNK, DN_CHUNK), 0)
    c = lax.broadcasted_iota(jnp.int32, (DN_CHUNK, DN_CHUNK), 1)
    tril = jnp.where(r >= c, 1.0, 0.0).astype(F32)
    for ci in range(tm // DN_CHUNK):
        rows = slice(ci * DN_CHUNK, (ci + 1) * DN_CHUNK)
        dec = _dot_f32(tril, g[rows])
        out = jnp.where(lane < DN_HEADS, dec, beta[rows])
        o_ref[rows, :] = out
        ot_ref[:, rows] = out.T


def _dn_gate(psmall, alog_row, dtb_row):
    M = psmall.shape[0]
    tm = 512
    kern = functools.partial(_dn_gate_kernel, tm=tm)
    return pl.pallas_call(
        kern,
        grid=(M // tm,),
        in_specs=[pl.BlockSpec((tm, LANES), lambda i: (i, SMALL_AB_BLOCK)),
                  pl.BlockSpec((1, LANES), lambda i: (0, 0)),
                  pl.BlockSpec((1, LANES), lambda i: (0, 0))],
        out_specs=[pl.BlockSpec((tm, LANES), lambda i: (i, 0)),
                   pl.BlockSpec((LANES, tm), lambda i: (0, i))],
        out_shape=[jax.ShapeDtypeStruct((M, LANES), F32),
                   jax.ShapeDtypeStruct((LANES, M), F32)],
        compiler_params=_cparams(("arbitrary",)),
        name="dn_gate",
    )(psmall, alog_row, dtb_row)


def _dn_scan_kernel(q_ref, k_ref, v_ref, kt_ref, dec_ref, dect_ref, z_ref, gain_ref, o_ref, st_ref):
    C = DN_CHUNK

    @pl.when(pl.program_id(1) == 0)
    def _():
        st_ref[...] = jnp.zeros(st_ref.shape, F32)

    dec = dec_ref[...]
    dect = dect_ref[...]
    r = lax.broadcasted_iota(jnp.int32, (C, C), 0)
    c = lax.broadcasted_iota(jnp.int32, (C, C), 1)
    tril = r >= c
    strict = r > c
    gain = gain_ref[...]
    eye = jnp.where(r == c, 1.0, 0.0).astype(F32)
    lvl_masks = []
    for lg in range(int(math.log2(C))):
        lvl_masks.append(((r >> (lg + 1)) == (c >> (lg + 1))) & ((r >> lg) != (c >> lg)))

    for h0 in range(0, DN_HEADS, DN_HEAD_GROUP):
        heads = range(h0, h0 + DN_HEAD_GROUP)
        cols = lambda ref: jnp.stack([ref[:, h * DN_DK:(h + 1) * DN_DK] for h in heads])
        qh = cols(q_ref).astype(F32)
        kh = cols(k_ref).astype(F32)
        vh = cols(v_ref).astype(F32)
        kth = jnp.stack([kt_ref[h * DN_DK:(h + 1) * DN_DK, :] for h in heads])
        dcol = jnp.stack([jnp.broadcast_to(dec[:, h:h + 1], (C, C)) for h in heads])
        bcol = jnp.stack([jnp.broadcast_to(dec[:, DN_HEADS + h:DN_HEADS + h + 1], (C, C)) for h in heads])
        drow = jnp.stack([jnp.broadcast_to(dect[h:h + 1, :], (C, C)) for h in heads])
        dlast = jnp.stack([jnp.broadcast_to(dect[h:h + 1, C - 1:C], (C, C)) for h in heads])
        lmat = jnp.exp(jnp.where(tril[None], dcol - drow, NEG))
        e_d = jnp.exp(dcol)
        kb = kh * bcol
        vb = vh * bcol
        a = jnp.where(strict[None], _bmm(kb.astype(BF16), kth) * lmat, 0.0)
        attn = jnp.where(tril[None], _bmm(qh.astype(BF16), kth) * lmat, 0.0)
        tinv = eye[None] - jnp.where(lvl_masks[0][None], a, 0.0)
        for lm in lvl_masks[1:]:
            t16 = tinv.astype(BF16)
            lo = jnp.where(lm[None], a, 0.0).astype(BF16)
            tinv = tinv - _bmm(t16, _bmm(lo, t16).astype(BF16))
        t16 = tinv.astype(BF16)
        u = _bmm(t16, vb.astype(BF16))
        w = _bmm(t16, (kb * e_d).astype(BF16))
        st = st_ref[h0:h0 + DN_HEAD_GROUP]
        st16 = st.astype(BF16)
        v_new = u - _bmm(w.astype(BF16), st16)
        v_new16 = v_new.astype(BF16)
        o = _bmm((qh * e_d).astype(BF16), st16) + _bmm(attn.astype(BF16), v_new16)
        kdt = (kth.astype(F32) * jnp.exp(dlast - drow)).astype(BF16)
        st_ref[h0:h0 + DN_HEAD_GROUP] = st * jnp.exp(dlast) + _bmm(kdt, v_new16)
        ms = jnp.mean(o * o, axis=-1, keepdims=True)
        y = o * lax.rsqrt(ms + EPS) * gain[None]
        for i, h in enumerate(heads):
            sl = slice(h * DN_DV, (h + 1) * DN_DV)
            o_ref[:, sl] = (y[i] * _silu(z_ref[:, sl].astype(F32))).astype(o_ref.dtype)


def _dn_scan(dq, dk, dv, dkt, dec, dect, pbig, z_col_block, gain_row, B, S):
    M = B * S
    C = DN_CHUNK
    nc = S // C
    tok = lambda b, n: (b * nc + n, 0)
    return pl.pallas_call(
        _dn_scan_kernel,
        grid=(B, nc),
        in_specs=[pl.BlockSpec((C, DN_WIDTH), tok),
                  pl.BlockSpec((C, DN_WIDTH), tok),
                  pl.BlockSpec((C, DN_WIDTH), tok),
                  pl.BlockSpec((DN_WIDTH, C), lambda b, n: (0, b * nc + n)),
                  pl.BlockSpec((C, LANES), tok),
                  pl.BlockSpec((LANES, C), lambda b, n: (0, b * nc + n)),
                  pl.BlockSpec((C, DN_WIDTH), lambda b, n: (b * nc + n, z_col_block)),
                  pl.BlockSpec((1, DN_DV), lambda b, n: (0, 0))],
        out_specs=pl.BlockSpec((C, DN_WIDTH), tok),
        out_shape=jax.ShapeDtypeStruct((M, DN_WIDTH), BF16),
        scratch_shapes=[pltpu.VMEM((DN_HEADS, DN_DK, DN_DV), F32)],
        compiler_params=_cparams(("arbitrary", "arbitrary")),
        name="dn_scan",
    )(dq, dk, dv, dkt, dec, dect, pbig, gain_row)


def _out1_kernel(oa_ref, ob_ref, wa_hbm, wb_hbm, ga_ref, gb_ref, o_ref, wf32_ref, w16_ref, sems, *, tn):
    j = pl.program_id(0)
    i = pl.program_id(1)

    def weight_copy(jj, b):
        src = (wa_hbm, wb_hbm)[b].at[:, pl.ds(pl.multiple_of(jj * tn, tn), tn)]
        return pltpu.make_async_copy(src, wf32_ref.at[b], sems.at[b])

    def merged(cols):
        ya = _dot(oa_ref[...], w16_ref[0, :, cols])
        yb = _dot(ob_ref[...], w16_ref[1, :, cols])
        mix = _sigmoid(ga_ref[:, cols].astype(F32)) * ya + _sigmoid(gb_ref[:, cols].astype(F32)) * yb
        o_ref[:, cols] = mix.astype(o_ref.dtype)

    @pl.when((j == 0) & (i == 0))
    def _():
        weight_copy(0, 0).start()
        weight_copy(0, 1).start()

    @pl.when(i == 0)
    def _():
        weight_copy(j, 0).wait()
        weight_copy(j, 1).wait()
        cc = min(MM_CAST_COLS, tn)
        for c0 in range(0, tn, cc):
            cols = slice(c0, c0 + cc)
            w16_ref[0, :, cols] = wf32_ref[0, :, cols].astype(BF16)
            w16_ref[1, :, cols] = wf32_ref[1, :, cols].astype(BF16)
            merged(cols)

        @pl.when(j + 1 < pl.num_programs(0))
        def _():
            weight_copy(j + 1, 0).start()
            weight_copy(j + 1, 1).start()

    @pl.when(i > 0)
    def _():
        merged(slice(0, tn))


OUT1_TM = 512


def _out1(o_a, o_b, wa, wb, pbig, D):
    M, K = o_a.shape
    tm = min(OUT1_TM, M)
    tn = min(MM_TN, D)
    nbd = D // tn
    return pl.pallas_call(
        functools.partial(_out1_kernel, tn=tn),
        grid=(D // tn, M // tm),
        in_specs=[pl.BlockSpec((tm, K), lambda j, i: (i, 0)),
                  pl.BlockSpec((tm, K), lambda j, i: (i, 0)),
                  pl.BlockSpec(memory_space=pl.ANY),
                  pl.BlockSpec(memory_space=pl.ANY),
                  pl.BlockSpec((tm, tn), lambda j, i: (i, j)),
                  pl.BlockSpec((tm, tn), lambda j, i: (i, nbd + j))],
        out_specs=pl.BlockSpec((tm, tn), lambda j, i: (i, j)),
        out_shape=jax.ShapeDtypeStruct((M, D), BF16),
        scratch_shapes=[pltpu.VMEM((2, K, tn), F32), pltpu.VMEM((2, K, tn), BF16),
                        pltpu.SemaphoreType.DMA((2,))],
        compiler_params=_cparams(("arbitrary", "arbitrary")),
        name="out1",
    )(o_a, o_b, wa, wb, pbig, pbig)


def _final_kernel(mix_ref, x_ref, gate_ref, fg_ref, o_ref):
    xn = x_ref[...] + gate_ref[0] * mix_ref[...].astype(F32)
    ms = jnp.mean(xn * xn, axis=-1, keepdims=True)
    o_ref[...] = xn * lax.rsqrt(ms + EPS) * fg_ref[...]


def _final(mixed, x2, mod3, final_gain, S):
    M, D = x2.shape
    tm = ROW_TILE
    nb = S // tm
    return pl.pallas_call(
        _final_kernel,
        grid=(M // tm,),
        in_specs=[pl.BlockSpec((tm, D), lambda i: (i, 0)),
                  pl.BlockSpec((tm, D), lambda i: (i, 0)),
                  pl.BlockSpec((1, 1, D), lambda i: (i // nb, 0, 2)),
                  pl.BlockSpec((1, D), lambda i: (0, 0))],
        out_specs=pl.BlockSpec((tm, D), lambda i: (i, 0)),
        out_shape=jax.ShapeDtypeStruct((M, D), F32),
        compiler_params=_cparams(("arbitrary",)),
        name="final",
    )(mixed, x2, mod3, final_gain)


def _pad_cols(w, width):
    return jnp.pad(w, ((0, 0), (0, width - w.shape[1])))


def _proj_row_offsets(D):
    widths = (NSA_WIDTH, 6 * KV_WIDTH, 3 * NSA_HEADS, NSA_WIDTH, 3 * DN_WIDTH, DN_HEADS, DN_HEADS, DN_WIDTH, 2 * D)
    offs = [int(o) for o in np.concatenate([[0], np.cumsum(widths)])]
    tiles = lambda seg: [offs[seg] + t * MM_TN for t in range(widths[seg] // MM_TN)]
    big = tiles(8) + tiles(4) + tiles(0) + tiles(3) + tiles(7)
    kv0 = offs[1]
    tail = [kv0, kv0 + TAIL_TN]
    small = [offs[2], offs[5]]
    return big, tail, small


TAIL_TN = 3 * KV_WIDTH
TAIL_TM = 1024
TAIL_CMP_BLOCK = 0
SMALL_GATE_BLOCK = 0
SMALL_AB_BLOCK = 1


def kernel(x, c, positions, w_ada, b_ada, norm_gain, w_in, cmp_pos_k, cmp_pos_v, w_cmp_k1, w_cmp_k2,
           w_cmp_v1, w_cmp_v2, conv_w, dt_bias, a_log, dn_norm_gain, w_proj_a, w_proj_b, w_out, final_gain):
    B, S, D = x.shape
    M = B * S
    depth = w_in.shape[0]
    assert S % SEL_TILE == 0 and S % DN_CHUNK == 0 and S >= WINDOW + Q_BLOCK and D % 512 == 0 and B <= 8
    assert S // SLC_LEN <= LANES and D % 1024 == 0 and DN_DK == DN_CHUNK
    assert depth == 1, "the final RMSNorm is fused into the last layer's output kernel"

    off_dq = 2 * D
    off_q = off_dq + 3 * DN_WIDTH
    off_z = off_q + NSA_WIDTH
    off_dz = off_z + NSA_WIDTH

    x2 = x.reshape(M, D)
    pos_col = positions.reshape(M, 1)
    cmp_end = np.arange(S // CMP_STRIDE - 1) * CMP_STRIDE + CMP_LEN - 1
    pos_cmp = jnp.pad(positions[:, cmp_end], ((0, 0), (0, 1)))[:, :, None]
    c8 = jnp.pad(c, ((0, 8 - B), (0, 0)))

    for l in range(depth):
        mod = _ada(c8, w_ada[l], b_ada[l][None, :])
        mod3 = mod[:B].reshape(B, 1, 3 * D)
        h = _norm(x2, norm_gain[l][None, :], mod3, S)

        w_nk = jnp.transpose(w_in[l])
        big_rows, tail_rows, small_rows = _proj_row_offsets(D)
        pbig = _proj(h, w_nk, big_rows, MM_TM, MM_TN, BF16, "proj_big")
        ptail = _proj(h, w_nk, tail_rows, TAIL_TM, TAIL_TN, F32, "proj_tail")
        psmall = _proj(h, w_nk, small_rows, TAIL_TM, 2 * LANES, F32, "proj_small", pieces=2)

        q_t, keys, v_t = _nsa_prep(pbig, ptail, pos_col, off_q // NSA_WIDTH, S)
        kcmp = _compress(ptail, TAIL_CMP_BLOCK, cmp_pos_k[l], w_cmp_k1[l], w_cmp_k2[l], pos_cmp, B, S, True)
        vcmp = _compress(ptail, TAIL_CMP_BLOCK + NSA_GROUPS, cmp_pos_v[l], w_cmp_v1[l], w_cmp_v2[l].T,
                         pos_cmp, B, S, False)
        o_a = _nsa_attn(q_t, kcmp, vcmp, keys, v_t, psmall, pbig, off_z // NSA_WIDTH, B, S)

        cw = conv_w[l]
        dq = _dn_prep(pbig, cw, off_dq, "q", B, S)[0]
        dk, dkt = _dn_prep(pbig, cw, off_dq + DN_WIDTH, "k", B, S)
        dv = _dn_prep(pbig, cw, off_dq + 2 * DN_WIDTH, "v", B, S)[0]
        alog_row = _pad_cols(a_log[l][None, :].astype(F32), LANES)
        dtb_row = _pad_cols(dt_bias[l][None, :].astype(F32), LANES)
        dec, dect = _dn_gate(psmall, alog_row, dtb_row)
        o_b = _dn_scan(dq, dk, dv, dkt, dec, dect, pbig, off_dz // DN_WIDTH,
                       dn_norm_gain[l][None, :], B, S)

        mixed_in = _out1(o_a, o_b, w_proj_a[l], w_proj_b[l], pbig, D)
        mixed = _matmul(mixed_in, w_out[l], BF16, "out2")
        x2 = _final(mixed, x2, mod3, final_gain[None, :], S)
    return x2.reshape(B, S, D)
```

```python
import functools
import math

import numpy as np
import jax
import jax.numpy as jnp
from jax import lax
from jax.experimental import pallas as pl
from jax.experimental.pallas import tpu as pltpu

F32 = jnp.float32
BF16 = jnp.bfloat16

NSA_HEADS = 16
NSA_GROUPS = 2
NSA_HPG = NSA_HEADS // NSA_GROUPS
HEAD_DIM = 128
ROT_DIM = HEAD_DIM // 4
ROT_HALF = ROT_DIM // 2
ROPE_THETA = 500000.0
CMP_LEN = 32
CMP_STRIDE = 16
CMP_HIDDEN = 256
SLC_LEN = 64
SLC_TOPK = 16
WINDOW = 512
Q_BLOCK = 128
NSA_WIDTH = NSA_HEADS * HEAD_DIM
KV_WIDTH = NSA_GROUPS * HEAD_DIM
DN_HEADS = 16
DN_DK = 128
DN_DV = 128
DN_WIDTH = DN_HEADS * DN_DV
CONV_WIDTH = 4
EPS = 1e-6

LANES = 128
VMEM_LIMIT_BYTES = 56 * 1024 * 1024
ROW_TILE = 512

DN_CHUNK = 128
DN_HEAD_GROUP = 16
SEL_TILE = 512
VT_ROWS = HEAD_DIM + 16
NEG = -1e30


def _cparams(sem):
    return pltpu.CompilerParams(dimension_semantics=sem, vmem_limit_bytes=VMEM_LIMIT_BYTES)


def _sigmoid(x):
    return 1.0 / (1.0 + jnp.exp(-x))


def _silu(x):
    return x * _sigmoid(x)


def _dot(a, b):
    return jnp.dot(a, b, preferred_element_type=F32)


def _dot_nt(a, b):
    return lax.dot_general(a, b, (((1,), (1,)), ((), ())), preferred_element_type=F32)


def _bmm(a, b):
    return lax.dot_general(a, b, (((2,), (1,)), ((0,), (0,))), preferred_element_type=F32)


def _dot_f32(a, b):
    return jnp.dot(a, b, preferred_element_type=F32, precision=lax.Precision.HIGHEST)


def _ada_kernel(c_ref, w_ref, b_ref, o_ref):
    c = c_ref[...]
    c_hi = c.astype(BF16)
    c_lo = (c - c_hi.astype(F32)).astype(BF16)
    w = w_ref[...]
    w_hi = w.astype(BF16)
    w_lo = (w - w_hi.astype(F32)).astype(BF16)
    acc = _dot(c_hi, w_hi) + _dot(c_lo, w_hi) + _dot(c_hi, w_lo)
    o_ref[...] = acc + b_ref[...]


def _ada(c8, w_ada, b_ada):
    D, N = w_ada.shape
    tn = min(512, N)
    return pl.pallas_call(
        _ada_kernel,
        grid=(N // tn,),
        in_specs=[pl.BlockSpec((8, D), lambda j: (0, 0)),
                  pl.BlockSpec((D, tn), lambda j: (0, j)),
                  pl.BlockSpec((1, tn), lambda j: (0, j))],
        out_specs=pl.BlockSpec((8, tn), lambda j: (0, j)),
        out_shape=jax.ShapeDtypeStruct((8, N), F32),
        compiler_params=_cparams(("arbitrary",)),
        name="ada",
    )(c8, w_ada, b_ada)


def _norm_kernel(x_ref, gain_ref, shift_ref, scale_ref, o_ref):
    x = x_ref[...]
    ms = jnp.mean(x * x, axis=-1, keepdims=True)
    y = x * lax.rsqrt(ms + EPS) * gain_ref[...]
    o_ref[...] = (y * (1.0 + scale_ref[0]) + shift_ref[0]).astype(o_ref.dtype)


def _norm(x2, gain, mod3, S):
    M, D = x2.shape
    tm = ROW_TILE
    nb = S // tm
    return pl.pallas_call(
        _norm_kernel,
        grid=(M // tm,),
        in_specs=[pl.BlockSpec((tm, D), lambda i: (i, 0)),
                  pl.BlockSpec((1, D), lambda i: (0, 0)),
                  pl.BlockSpec((1, 1, D), lambda i: (i // nb, 0, 0)),
                  pl.BlockSpec((1, 1, D), lambda i: (i // nb, 0, 1))],
        out_specs=pl.BlockSpec((tm, D), lambda i: (i, 0)),
        out_shape=jax.ShapeDtypeStruct((M, D), BF16),
        compiler_params=_cparams(("arbitrary",)),
        name="norm",
    )(x2, gain, mod3, mod3)


MM_TM = 1024
MM_TN = 1024
W_ROW_ALIGN = 16
WT_CHUNK = 128
MM_CAST_COLS = 256


def _mm_kernel(offs_ref, a_ref, w_hbm, o_ref, wf32_ref, w16_ref, sems, *, w_is_nk, tn, pieces):
    j = pl.program_id(0)
    i = pl.program_id(1)
    pw = tn // pieces

    def piece_copy(jj, p):
        off = offs_ref[jj * pieces + p]
        if w_is_nk:
            src = w_hbm.at[pl.ds(pl.multiple_of(off * W_ROW_ALIGN, W_ROW_ALIGN), pw), :]
            dst = wf32_ref.at[p * pw:(p + 1) * pw, :]
        else:
            src = w_hbm.at[:, pl.ds(pl.multiple_of(off * LANES, LANES), pw)]
            dst = wf32_ref.at[:, p * pw:(p + 1) * pw]
        return pltpu.make_async_copy(src, dst, sems.at[p])

    def start_tile(jj):
        for p in range(pieces):
            piece_copy(jj, p).start()

    @pl.when((j == 0) & (i == 0))
    def _():
        start_tile(0)

    @pl.when(i == 0)
    def _():
        for p in range(pieces):
            piece_copy(j, p).wait()
        cc = min(MM_CAST_COLS, tn)
        for c0 in range(0, tn, cc):
            if w_is_nk:
                for c in range(c0, c0 + cc, WT_CHUNK):
                    w16_ref[:, c:c + WT_CHUNK] = wf32_ref[c:c + WT_CHUNK, :].T.astype(BF16)
            else:
                w16_ref[:, c0:c0 + cc] = wf32_ref[:, c0:c0 + cc].astype(BF16)
            o_ref[:, c0:c0 + cc] = _dot(a_ref[...], w16_ref[:, c0:c0 + cc]).astype(o_ref.dtype)

        @pl.when(j + 1 < pl.num_programs(0))
        def _():
            start_tile(j + 1)

    @pl.when(i > 0)
    def _():
        o_ref[...] = _dot(a_ref[...], w16_ref[...]).astype(o_ref.dtype)


def _mm_call(a, w, offs, w_is_nk, tm, tn, out_dtype, name, pieces=1):
    M, K = a.shape
    tm = min(tm, M)
    n_tiles = len(offs) // pieces
    grid_spec = pltpu.PrefetchScalarGridSpec(
        num_scalar_prefetch=1,
        grid=(n_tiles, M // tm),
        in_specs=[pl.BlockSpec((tm, K), lambda j, i, o: (i, 0)),
                  pl.BlockSpec(memory_space=pl.ANY)],
        out_specs=pl.BlockSpec((tm, tn), lambda j, i, o: (i, j)),
        scratch_shapes=[pltpu.VMEM((tn, K) if w_is_nk else (K, tn), F32),
                        pltpu.VMEM((K, tn), BF16),
                        pltpu.SemaphoreType.DMA((pieces,))],
    )
    return pl.pallas_call(
        functools.partial(_mm_kernel, w_is_nk=w_is_nk, tn=tn, pieces=pieces),
        grid_spec=grid_spec,
        out_shape=jax.ShapeDtypeStruct((M, n_tiles * tn), out_dtype),
        compiler_params=_cparams(("arbitrary", "arbitrary")),
        name=name,
    )(jnp.asarray(np.asarray(offs, np.int32)), a, w)


def _matmul(a, w, out_dtype, name):
    N = w.shape[1]
    tn = min(MM_TN, N)
    offs = [t * tn // LANES for t in range(N // tn)]
    return _mm_call(a, w, offs, False, MM_TM, tn, out_dtype, name)


def _proj(h, w_nk, row_offsets, tm, tn, out_dtype, name, pieces=1):
    assert all(int(o) % W_ROW_ALIGN == 0 and int(o) + tn // pieces <= w_nk.shape[0] for o in row_offsets)
    offs = [int(o) // W_ROW_ALIGN for o in row_offsets]
    return _mm_call(h, w_nk, offs, True, tm, tn, out_dtype, name, pieces)


def _rope_consts():
    inv = ROPE_THETA ** (-np.arange(ROT_HALF, dtype=np.float64) / ROT_HALF)
    invf = np.zeros((1, LANES), np.float32)
    invf[0, :ROT_HALF] = inv
    invf[0, ROT_HALF:ROT_DIM] = inv
    sgn = np.zeros((1, LANES), np.float32)
    sgn[0, :ROT_HALF] = -1.0
    sgn[0, ROT_HALF:ROT_DIM] = 1.0
    return jnp.asarray(invf), jnp.asarray(sgn)


def _rope_tables(pos_col, invf, sgn):
    ang = pos_col.astype(F32) * invf
    return jnp.cos(ang), jnp.sin(ang) * sgn


def _rope_apply(x, cos_t, sin_t):
    lane = lax.broadcasted_iota(jnp.int32, x.shape, 1)
    partner = jnp.where(lane < ROT_HALF,
                        pltpu.roll(x, LANES - ROT_HALF, 1),
                        pltpu.roll(x, ROT_HALF, 1))
    return x * cos_t + partner * sin_t


KEYS_WIDTH = 6 * HEAD_DIM


def _nsa_prep_kernel(q_ref, kvs_ref, kvw_ref, pos_ref, invf_ref, sgn_ref, qt_ref, keys_ref, vt_ref, *, seq_len):
    tm = q_ref.shape[0]
    cos_t, sin_t = _rope_tables(pos_ref[...], invf_ref[...], sgn_ref[...])
    qscale = (HEAD_DIM ** -0.5) * math.log2(math.e)
    for h in range(NSA_HEADS):
        sl = slice(h * HEAD_DIM, (h + 1) * HEAD_DIM)
        xq = _rope_apply(q_ref[:, sl].astype(F32), cos_t, sin_t) * qscale
        qt_ref[sl, :] = xq.T.astype(qt_ref.dtype)
    tok = (pl.program_id(0) * tm) % seq_len + lax.broadcasted_iota(jnp.int32, (tm, LANES), 0)
    lane = lax.broadcasted_iota(jnp.int32, (tm, LANES), 1)
    onehot = jnp.where(lane == (tok >> int(math.log2(SLC_LEN))), 1.0, 0.0).astype(keys_ref.dtype)
    blk = lambda i: (kvs_ref if i < 4 else kvw_ref)[:, (i % 4) * HEAD_DIM:(i % 4 + 1) * HEAD_DIM]
    for g in range(NSA_GROUPS):
        ks = _rope_apply(blk(g), cos_t, sin_t)
        keys_ref[:, (2 * g) * HEAD_DIM:(2 * g + 1) * HEAD_DIM] = ks.astype(keys_ref.dtype)
        keys_ref[:, (2 * g + 1) * HEAD_DIM:(2 * g + 2) * HEAD_DIM] = onehot
        kw = _rope_apply(blk(4 + g), cos_t, sin_t)
        keys_ref[:, (4 + g) * HEAD_DIM:(5 + g) * HEAD_DIM] = kw.astype(keys_ref.dtype)
        ones = jnp.ones((VT_ROWS - HEAD_DIM, tm), vt_ref.dtype)
        for c, src in ((g, 2 + g), (2 + g, 6 + g)):
            vt_ref[c * VT_ROWS:c * VT_ROWS + HEAD_DIM, :] = blk(src).T.astype(vt_ref.dtype)
            vt_ref[c * VT_ROWS + HEAD_DIM:(c + 1) * VT_ROWS, :] = ones


def _nsa_prep(pbig, ptail, pos_col, q_col_block, S):
    M = pbig.shape[0]
    tm = ROW_TILE
    invf, sgn = _rope_consts()
    return pl.pallas_call(
        functools.partial(_nsa_prep_kernel, seq_len=S),
        grid=(M // tm,),
        in_specs=[pl.BlockSpec((tm, NSA_WIDTH), lambda i: (i, q_col_block)),
                  pl.BlockSpec((tm, 4 * HEAD_DIM), lambda i: (i, 1)),
                  pl.BlockSpec((tm, 4 * HEAD_DIM), lambda i: (i, 2)),
                  pl.BlockSpec((tm, 1), lambda i: (i, 0)),
                  pl.BlockSpec((1, LANES), lambda i: (0, 0)),
                  pl.BlockSpec((1, LANES), lambda i: (0, 0))],
        out_specs=[pl.BlockSpec((NSA_WIDTH, tm), lambda i: (0, i)),
                   pl.BlockSpec((tm, KEYS_WIDTH), lambda i: (i, 0)),
                   pl.BlockSpec((4 * VT_ROWS, tm), lambda i: (0, i))],
        out_shape=[jax.ShapeDtypeStruct((NSA_WIDTH, M), BF16),
                   jax.ShapeDtypeStruct((M, KEYS_WIDTH), BF16),
                   jax.ShapeDtypeStruct((4 * VT_ROWS, M), BF16)],
        compiler_params=_cparams(("arbitrary",)),
        name="nsa_prep",
    )(pbig, ptail, ptail, pos_col, invf, sgn)


def _compress_kernel(x_ref, pe_ref, w1_ref, w2_ref, pos_ref, invf_ref, sgn_ref, o_ref, *, rope, n_rows):
    half = CMP_LEN // 2
    top = jnp.zeros((n_rows, CMP_HIDDEN), F32)
    bot = jnp.zeros((n_rows, CMP_HIDDEN), F32)
    for l in range(half):
        xl = x_ref[pl.ds(l, n_rows, stride=CMP_STRIDE), :]
        w_top = w1_ref[l * HEAD_DIM:(l + 1) * HEAD_DIM, :].astype(BF16)
        w_bot = w1_ref[(half + l) * HEAD_DIM:(half + l + 1) * HEAD_DIM, :].astype(BF16)
        top = top + _dot((xl + pe_ref[l:l + 1, :]).astype(BF16), w_top)
        bot = bot + _dot((xl + pe_ref[half + l:half + l + 1, :]).astype(BF16), w_bot)
    hid = top + pltpu.roll(bot, n_rows - 1, 0)
    act = _silu(hid).astype(BF16)
    if rope:
        out = _dot(act, w2_ref[...].astype(BF16))
        cos_t, sin_t = _rope_tables(pos_ref[0], invf_ref[...], sgn_ref[...])
        o_ref[0, 0] = _rope_apply(out, cos_t, sin_t).astype(o_ref.dtype)
    else:
        o_ref[0, 0, 0:HEAD_DIM, :] = _dot_nt(w2_ref[...].astype(BF16), act).astype(o_ref.dtype)
        o_ref[0, 0, HEAD_DIM:VT_ROWS, :] = jnp.ones((VT_ROWS - HEAD_DIM, n_rows), o_ref.dtype)


def _compress(ptail, col_block0, pe, w1, w2, pos_cmp, B, S, rope):
    n_rows = S // CMP_STRIDE
    invf, sgn = _rope_consts()
    kern = functools.partial(_compress_kernel, rope=rope, n_rows=n_rows)
    out_dims = (n_rows, HEAD_DIM) if rope else (VT_ROWS, n_rows)
    return pl.pallas_call(
        kern,
        grid=(B, NSA_GROUPS),
        in_specs=[pl.BlockSpec((S, HEAD_DIM), lambda b, g: (b, col_block0 + g)),
                  pl.BlockSpec((CMP_LEN, HEAD_DIM), lambda b, g: (0, 0)),
                  pl.BlockSpec((CMP_LEN * HEAD_DIM, CMP_HIDDEN), lambda b, g: (0, 0)),
                  pl.BlockSpec(w2.shape, lambda b, g: (0, 0)),
                  pl.BlockSpec((1, n_rows, 1), lambda b, g: (b, 0, 0)),
                  pl.BlockSpec((1, LANES), lambda b, g: (0, 0)),
                  pl.BlockSpec((1, LANES), lambda b, g: (0, 0))],
        out_specs=pl.BlockSpec((1, 1) + out_dims, lambda b, g: (b, g, 0, 0)),
        out_shape=jax.ShapeDtypeStruct((B, NSA_GROUPS) + out_dims, BF16),
        compiler_params=_cparams(("arbitrary", "arbitrary")),
        name="compress_k" if rope else "compress_v",
    )(ptail, pe, w1, w2, pos_cmp, invf, sgn)


def _tile_lanes(x, n):
    return jnp.concatenate([x] * n, axis=1)


def _nsa_attn_kernel(qt_ref, kc_ref, vct_ref, ks_ref, vst_ref, kw_ref, vwt_ref, g_ref, z_ref,
                     ovt_ref, o_ref, m_s, acc_s, qa_s, sa_s, sb_s, sc_s, *, seq_len):
    T = Q_BLOCK
    P = NSA_HPG
    R = P * T
    n_cmp_pad = seq_len // CMP_STRIDE
    n_cmp = n_cmp_pad - 1
    nb = seq_len // SLC_LEN
    qi = pl.program_id(2)
    qs = qi * T
    d0 = pl.multiple_of(qs, T)
    wlen = WINDOW + T
    w0 = pl.multiple_of(jnp.maximum(qs - WINDOW, 0), T)

    def first_stage():
        qt = jnp.concatenate([qt_ref[p * HEAD_DIM:(p + 1) * HEAD_DIM, :] for p in range(P)], axis=1)

        s_c = _dot(kc_ref[0, 0], qt)
        s_w = _dot(kw_ref[pl.ds(w0, wlen), :], qt)
        s_d = _dot(ks_ref[pl.ds(d0, T), 0:HEAD_DIM], qt)

        tok = lambda n: qs + lax.broadcasted_iota(jnp.int32, (n, T), 1)
        row = lambda n: lax.broadcasted_iota(jnp.int32, (n, T), 0)
        ok_c = (row(n_cmp_pad) * CMP_STRIDE + (CMP_LEN - 1) <= tok(n_cmp_pad)) & (row(n_cmp_pad) < n_cmp)
        key_w = w0 + row(wlen)
        ok_w = (key_w <= tok(wlen)) & (tok(wlen) - key_w < WINDOW)
        ok_d = row(T) <= lax.broadcasted_iota(jnp.int32, (T, T), 1)
        addmask = lambda ok: _tile_lanes(jnp.where(ok, 0.0, NEG).astype(F32), P)
        s_c = s_c + addmask(ok_c)
        s_w = s_w + addmask(ok_w)
        s_d = s_d + addmask(ok_d)

        m_c = jnp.max(s_c, axis=0, keepdims=True)
        m_w = jnp.max(s_w, axis=0, keepdims=True)
        m_d = jnp.max(s_d, axis=0, keepdims=True)
        m_c = jnp.where(m_c > 0.5 * NEG, m_c, 0.0)
        p_c = jnp.exp2(s_c - m_c)
        p_w = jnp.exp2(s_w - m_w)
        p_d = jnp.exp2(s_d - m_d)
        o_c = _dot(vct_ref[0, 0], p_c.astype(BF16))
        o_w = _dot(vwt_ref[:, pl.ds(w0, wlen)], p_w.astype(BF16))
        inv_c = 1.0 / jnp.maximum(jnp.sum(p_c, axis=0, keepdims=True), 1e-30)
        inv_w = 1.0 / o_w[HEAD_DIM:HEAD_DIM + 1, :]
        m_s[...] = m_d
        acc_s[...] = _dot(vst_ref[:, pl.ds(d0, T)], p_d.astype(BF16))

        pn = p_c * inv_c
        p_sum = pn[:, 0:T]
        for p in range(1, P):
            p_sum = p_sum + pn[:, p * T:(p + 1) * T]
        ps_hi = p_sum.astype(BF16)
        ps_lo = (p_sum - ps_hi.astype(F32)).astype(BF16)
        imp = (_dot(ovt_ref[...], ps_hi) + _dot(ovt_ref[...], ps_lo))[0:nb]
        jrow = row(nb)
        t_lane = tok(nb)
        tb = t_lane >> int(math.log2(SLC_LEN))
        visible = jrow * SLC_LEN <= t_lane
        forced = (jrow == 0) | (jrow == tb) | (jrow == tb - 1)
        score = jnp.where(forced, 1e9, jnp.where(visible, imp, -jnp.inf))
        cnt = jnp.zeros((nb, T), F32)
        for k in range(nb):
            rk = score[k:k + 1, :]
            cnt = cnt + jnp.where(jrow > k, jnp.where(rk >= score, 1.0, 0.0), jnp.where(rk > score, 1.0, 0.0))
        keep = (cnt < float(min(SLC_TOPK, nb))) & visible & (jrow < 2 * qi)
        bias = jnp.where(keep, 0.0, NEG).astype(F32)
        bias = jnp.concatenate([bias, jnp.full((LANES - nb, T), NEG, F32)], axis=0) if nb < LANES else bias
        qa_s[0:HEAD_DIM, :] = qt
        qa_s[HEAD_DIM:2 * HEAD_DIM, :] = _tile_lanes(bias.astype(BF16), P)
        return o_c, o_w, inv_c, inv_w

    n_t = (qs + SEL_TILE - 1) // SEL_TILE

    def scores_into(kt, dst):
        dst[...] = _dot(ks_ref[kt * SEL_TILE:(kt + 1) * SEL_TILE, :], qa_s[...])

    def consume(kt, src):
        k0 = kt * SEL_TILE
        s = src[...]
        m_prev = m_s[...]
        m_new = jnp.maximum(m_prev, jnp.max(s, axis=0, keepdims=True))
        alpha = jnp.exp2(m_prev - m_new)
        p = jnp.exp2(s - m_new)
        acc_s[...] = alpha * acc_s[...] + _dot(vst_ref[:, pl.ds(k0, SEL_TILE)], p.astype(BF16))
        m_s[...] = m_new

    bufs = (sa_s, sb_s, sc_s)

    def finish(o_c, o_w, inv_c, inv_w):
        gates_t = _sigmoid(g_ref[...]).T
        first_group = pl.program_id(1) == 0

        def grow(r):
            rows = []
            for p in range(P):
                h0 = r * NSA_HEADS + p
                rows.append(jnp.where(first_group, gates_t[h0:h0 + 1, :], gates_t[h0 + P:h0 + P + 1, :]))
            return jnp.concatenate(rows, axis=1)

        o_s = acc_s[...]
        hd = slice(0, HEAD_DIM)
        o_t = (o_c[hd] * (grow(0) * inv_c) + o_s[hd] * (grow(1) / o_s[HEAD_DIM:HEAD_DIM + 1, :])
               + o_w[hd] * (grow(2) * inv_w))
        for p in range(P):
            sl = slice(p * HEAD_DIM, (p + 1) * HEAD_DIM)
            o_p = o_t[:, p * T:(p + 1) * T].T
            o_ref[:, sl] = (o_p * _silu(z_ref[:, sl].astype(F32))).astype(o_ref.dtype)

    for r in range(0, seq_len // SEL_TILE + 1):
        @pl.when(n_t == r)
        def _(r=r):
            first = first_stage()
            if r > 0:
                scores_into(0, bufs[0])
            for t in range(r):
                if t + 1 < r:
                    scores_into(t + 1, bufs[(t + 1) % len(bufs)])
                consume(t, bufs[t % len(bufs)])
            finish(*first)


def _nsa_consts(S):
    n_cmp_pad = S // CMP_STRIDE
    nb = S // SLC_LEN
    cmp_start = np.arange(n_cmp_pad) * CMP_STRIDE
    slc_start = np.arange(LANES) * SLC_LEN
    ov = ((cmp_start[:, None] <= slc_start[None, :] + SLC_LEN - 1)
          & (cmp_start[:, None] + CMP_LEN - 1 >= slc_start[None, :])
          & (np.arange(LANES)[None, :] < nb)
          & (np.arange(n_cmp_pad)[:, None] < n_cmp_pad - 1)).astype(np.float32)
    return jnp.asarray(ov.T, dtype=BF16)


def _nsa_attn(q_t, kcmp, vcmp_t, keys, v_t, psmall, pbig, z_col_block, B, S):
    M = B * S
    nq = S // Q_BLOCK
    gw = NSA_HPG * HEAD_DIM
    ovt = _nsa_consts(S)
    n_cmp_pad = S // CMP_STRIDE
    R = NSA_HPG * Q_BLOCK
    kern = functools.partial(_nsa_attn_kernel, seq_len=S)
    return pl.pallas_call(
        kern,
        grid=(B, NSA_GROUPS, nq),
        in_specs=[pl.BlockSpec((gw, Q_BLOCK), lambda b, g, i: (g, b * nq + i)),
                  pl.BlockSpec((1, 1, n_cmp_pad, HEAD_DIM), lambda b, g, i: (b, g, 0, 0)),
                  pl.BlockSpec((1, 1, VT_ROWS, n_cmp_pad), lambda b, g, i: (b, g, 0, 0)),
                  pl.BlockSpec((S, 2 * HEAD_DIM), lambda b, g, i: (b, g)),
                  pl.BlockSpec((VT_ROWS, S), lambda b, g, i: (g, b)),
                  pl.BlockSpec((S, HEAD_DIM), lambda b, g, i: (b, 4 + g)),
                  pl.BlockSpec((VT_ROWS, S), lambda b, g, i: (2 + g, b)),
                  pl.BlockSpec((Q_BLOCK, LANES), lambda b, g, i: (b * nq + i, SMALL_GATE_BLOCK)),
                  pl.BlockSpec((Q_BLOCK, gw), lambda b, g, i: (b * nq + i, z_col_block * NSA_GROUPS + g)),
                  pl.BlockSpec((LANES, n_cmp_pad), lambda b, g, i: (0, 0))],
        out_specs=pl.BlockSpec((Q_BLOCK, gw), lambda b, g, i: (b * nq + i, g)),
        out_shape=jax.ShapeDtypeStruct((M, NSA_WIDTH), BF16),
        scratch_shapes=[pltpu.VMEM((1, R), F32),
                        pltpu.VMEM((VT_ROWS, R), F32),
                        pltpu.VMEM((2 * HEAD_DIM, R), BF16),
                        pltpu.VMEM((SEL_TILE, R), F32),
                        pltpu.VMEM((SEL_TILE, R), F32),
                        pltpu.VMEM((SEL_TILE, R), F32)],
        compiler_params=_cparams(("arbitrary", "arbitrary", "arbitrary")),
        name="nsa_attn",
    )(q_t, kcmp, vcmp_t, keys, v_t, keys, v_t, psmall, pbig, ovt)


def _dn_prep_kernel(x_ref, w_ref, o_ref, *rest, mode, seq_len):
    xpad_ref = rest[-1]
    w = w_ref[...]
    xpad_ref[0:CONV_PAD, :] = jnp.zeros((CONV_PAD, xpad_ref.shape[1]), F32)
    xpad_ref[CONV_PAD:, :] = x_ref[...].astype(F32)
    y = xpad_ref[CONV_PAD:, :] * w[CONV_WIDTH - 1:CONV_WIDTH, :]
    for k in range(1, CONV_WIDTH):
        y = y + xpad_ref[pl.ds(CONV_PAD - k, seq_len), :] * w[CONV_WIDTH - 1 - k:CONV_WIDTH - k, :]
    y = _silu(y)
    if mode in ("q", "k"):
        outs = []
        for h in range(y.shape[1] // DN_DK):
            yh = y[:, h * DN_DK:(h + 1) * DN_DK]
            ss = jnp.sum(yh * yh, axis=-1, keepdims=True)
            yh = yh * lax.rsqrt(ss + EPS)
            if mode == "q":
                yh = yh * (DN_DK ** -0.5)
            outs.append(yh)
        y = jnp.concatenate(outs, axis=1)
    o_ref[...] = y.astype(o_ref.dtype)
    if mode == "k":
        kt_ref = rest[0]
        kt_ref[...] = y.T.astype(kt_ref.dtype)


DN_PREP_COLS = 128
CONV_PAD = 8


def _dn_prep(pbig, conv_w, col0, mode, B, S):
    M = B * S
    tw = DN_PREP_COLS
    col0_blocks = col0 // tw
    nj = DN_WIDTH // tw
    wcol0 = {"q": 0, "k": nj, "v": 2 * nj}[mode]
    kern = functools.partial(_dn_prep_kernel, mode=mode, seq_len=S)
    out_specs = [pl.BlockSpec((S, tw), lambda b, j: (b, j))]
    out_shape = [jax.ShapeDtypeStruct((M, DN_WIDTH), BF16)]
    if mode == "k":
        out_specs.append(pl.BlockSpec((tw, S), lambda b, j: (j, b)))
        out_shape.append(jax.ShapeDtypeStruct((DN_WIDTH, M), BF16))
    return pl.pallas_call(
        kern,
        grid=(B, nj),
        in_specs=[pl.BlockSpec((S, tw), lambda b, j: (b, col0_blocks + j)),
                  pl.BlockSpec((CONV_WIDTH, tw), lambda b, j: (0, wcol0 + j))],
        out_specs=out_specs,
        out_shape=out_shape,
        scratch_shapes=[pltpu.VMEM((CONV_PAD + S, tw), F32)],
        compiler_params=_cparams(("arbitrary", "arbitrary")),
        name="dn_prep_" + mode,
    )(pbig, conv_w)


def _dn_gate_kernel(ab_ref, alog_ref, dtb_ref, o_ref, ot_ref, *, tm):
    ab = ab_ref[...]
    x = ab + dtb_ref[...]
    softplus = jnp.maximum(x, 0.0) + jnp.log(1.0 + jnp.exp(-jnp.abs(x)))
    g = -jnp.exp(alog_ref[...]) * softplus
    beta = _sigmoid(ab)
    lane = lax.broadcasted_iota(jnp.int32, (DN_CHUNK, LANES), 1)
    r = lax.broadcasted_iota(jnp.int32, (DN_CHUNK, DN_CHUNK), 0)
    c = lax.broadcasted_iota(jnp.int32, (DN_CHUNK, DN_CHUNK), 1)
    tril = jnp.where(r >= c, 1.0, 0.0).astype(F32)
    for ci in range(tm // DN_CHUNK):
        rows = slice(ci * DN_CHUNK, (ci + 1) * DN_CHUNK)
        dec = _dot_f32(tril, g[rows])
        out = jnp.where(lane < DN_HEADS, dec, beta[rows])
        o_ref[rows, :] = out
        ot_ref[:, rows] = out.T


def _dn_gate(psmall, alog_row, dtb_row):
    M = psmall.shape[0]
    tm = 512
    kern = functools.partial(_dn_gate_kernel, tm=tm)
    return pl.pallas_call(
        kern,
        grid=(M // tm,),
        in_specs=[pl.BlockSpec((tm, LANES), lambda i: (i, SMALL_AB_BLOCK)),
                  pl.BlockSpec((1, LANES), lambda i: (0, 0)),
                  pl.BlockSpec((1, LANES), lambda i: (0, 0))],
        out_specs=[pl.BlockSpec((tm, LANES), lambda i: (i, 0)),
                   pl.BlockSpec((LANES, tm), lambda i: (0, i))],
        out_shape=[jax.ShapeDtypeStruct((M, LANES), F32),
                   jax.ShapeDtypeStruct((LANES, M), F32)],
        compiler_params=_cparams(("arbitrary",)),
        name="dn_gate",
    )(psmall, alog_row, dtb_row)


def _dn_scan_kernel(q_ref, k_ref, v_ref, kt_ref, dec_ref, dect_ref, z_ref, gain_ref, o_ref, st_ref):
    C = DN_CHUNK

    @pl.when(pl.program_id(1) == 0)
    def _():
        st_ref[...] = jnp.zeros(st_ref.shape, F32)

    dec = dec_ref[...]
    dect = dect_ref[...]
    r = lax.broadcasted_iota(jnp.int32, (C, C), 0)
    c = lax.broadcasted_iota(jnp.int32, (C, C), 1)
    tril = r >= c
    strict = r > c
    gain = gain_ref[...]
    eye = jnp.where(r == c, 1.0, 0.0).astype(F32)
    lvl_masks = []
    for lg in range(int(math.log2(C))):
        lvl_masks.append(((r >> (lg + 1)) == (c >> (lg + 1))) & ((r >> lg) != (c >> lg)))

    for h0 in range(0, DN_HEADS, DN_HEAD_GROUP):
        heads = range(h0, h0 + DN_HEAD_GROUP)
        cols = lambda ref: jnp.stack([ref[:, h * DN_DK:(h + 1) * DN_DK] for h in heads])
        qh = cols(q_ref).astype(F32)
        kh = cols(k_ref).astype(F32)
        vh = cols(v_ref).astype(F32)
        kth = jnp.stack([kt_ref[h * DN_DK:(h + 1) * DN_DK, :] for h in heads])
        dcol = jnp.stack([jnp.broadcast_to(dec[:, h:h + 1], (C, C)) for h in heads])
        bcol = jnp.stack([jnp.broadcast_to(dec[:, DN_HEADS + h:DN_HEADS + h + 1], (C, C)) for h in heads])
        drow = jnp.stack([jnp.broadcast_to(dect[h:h + 1, :], (C, C)) for h in heads])
        dlast = jnp.stack([jnp.broadcast_to(dect[h:h + 1, C - 1:C], (C, C)) for h in heads])
        lmat = jnp.exp(jnp.where(tril[None], dcol - drow, NEG))
        e_d = jnp.exp(dcol)
        kb = kh * bcol
        vb = vh * bcol
        a = jnp.where(strict[None], _bmm(kb.astype(BF16), kth) * lmat, 0.0)
        attn = jnp.where(tril[None], _bmm(qh.astype(BF16), kth) * lmat, 0.0)
        tinv = eye[None] - jnp.where(lvl_masks[0][None], a, 0.0)
        for lm in lvl_masks[1:]:
            t16 = tinv.astype(BF16)
            lo = jnp.where(lm[None], a, 0.0).astype(BF16)
            tinv = tinv - _bmm(t16, _bmm(lo, t16).astype(BF16))
        t16 = tinv.astype(BF16)
        u = _bmm(t16, vb.astype(BF16))
        w = _bmm(t16, (kb * e_d).astype(BF16))
        st = st_ref[h0:h0 + DN_HEAD_GROUP]
        st16 = st.astype(BF16)
        v_new = u - _bmm(w.astype(BF16), st16)
        v_new16 = v_new.astype(BF16)
        o = _bmm((qh * e_d).astype(BF16), st16) + _bmm(attn.astype(BF16), v_new16)
        kdt = (kth.astype(F32) * jnp.exp(dlast - drow)).astype(BF16)
        st_ref[h0:h0 + DN_HEAD_GROUP] = st * jnp.exp(dlast) + _bmm(kdt, v_new16)
        ms = jnp.mean(o * o, axis=-1, keepdims=True)
        y = o * lax.rsqrt(ms + EPS) * gain[None]
        for i, h in enumerate(heads):
            sl = slice(h * DN_DV, (h + 1) * DN_DV)
            o_ref[:, sl] = (y[i] * _silu(z_ref[:, sl].astype(F32))).astype(o_ref.dtype)


def _dn_scan(dq, dk, dv, dkt, dec, dect, pbig, z_col_block, gain_row, B, S):
    M = B * S
    C = DN_CHUNK
    nc = S // C
    tok = lambda b, n: (b * nc + n, 0)
    return pl.pallas_call(
        _dn_scan_kernel,
        grid=(B, nc),
        in_specs=[pl.BlockSpec((C, DN_WIDTH), tok),
                  pl.BlockSpec((C, DN_WIDTH), tok),
                  pl.BlockSpec((C, DN_WIDTH), tok),
                  pl.BlockSpec((DN_WIDTH, C), lambda b, n: (0, b * nc + n)),
                  pl.BlockSpec((C, LANES), tok),
                  pl.BlockSpec((LANES, C), lambda b, n: (0, b * nc + n)),
                  pl.BlockSpec((C, DN_WIDTH), lambda b, n: (b * nc + n, z_col_block)),
                  pl.BlockSpec((1, DN_DV), lambda b, n: (0, 0))],
        out_specs=pl.BlockSpec((C, DN_WIDTH), tok),
        out_shape=jax.ShapeDtypeStruct((M, DN_WIDTH), BF16),
        scratch_shapes=[pltpu.VMEM((DN_HEADS, DN_DK, DN_DV), F32)],
        compiler_params=_cparams(("arbitrary", "arbitrary")),
        name="dn_scan",
    )(dq, dk, dv, dkt, dec, dect, pbig, gain_row)


def _out1_kernel(oa_ref, ob_ref, wa_hbm, wb_hbm, ga_ref, gb_ref, o_ref, wf32_ref, w16_ref, sems, *, tn):
    j = pl.program_id(0)
    i = pl.program_id(1)

    def weight_copy(jj, b):
        src = (wa_hbm, wb_hbm)[b].at[:, pl.ds(pl.multiple_of(jj * tn, tn), tn)]
        return pltpu.make_async_copy(src, wf32_ref.at[b], sems.at[b])

    def merged(cols):
        ya = _dot(oa_ref[...], w16_ref[0, :, cols])
        yb = _dot(ob_ref[...], w16_ref[1, :, cols])
        mix = _sigmoid(ga_ref[:, cols].astype(F32)) * ya + _sigmoid(gb_ref[:, cols].astype(F32)) * yb
        o_ref[:, cols] = mix.astype(o_ref.dtype)

    @pl.when((j == 0) & (i == 0))
    def _():
        weight_copy(0, 0).start()
        weight_copy(0, 1).start()

    @pl.when(i == 0)
    def _():
        weight_copy(j, 0).wait()
        weight_copy(j, 1).wait()
        cc = min(MM_CAST_COLS, tn)
        for c0 in range(0, tn, cc):
            cols = slice(c0, c0 + cc)
            w16_ref[0, :, cols] = wf32_ref[0, :, cols].astype(BF16)
            w16_ref[1, :, cols] = wf32_ref[1, :, cols].astype(BF16)
            merged(cols)

        @pl.when(j + 1 < pl.num_programs(0))
        def _():
            weight_copy(j + 1, 0).start()
            weight_copy(j + 1, 1).start()

    @pl.when(i > 0)
    def _():
        merged(slice(0, tn))


OUT1_TM = 512


def _out1(o_a, o_b, wa, wb, pbig, D):
    M, K = o_a.shape
    tm = min(OUT1_TM, M)
    tn = min(MM_TN, D)
    nbd = D // tn
    return pl.pallas_call(
        functools.partial(_out1_kernel, tn=tn),
        grid=(D // tn, M // tm),
        in_specs=[pl.BlockSpec((tm, K), lambda j, i: (i, 0)),
                  pl.BlockSpec((tm, K), lambda j, i: (i, 0)),
                  pl.BlockSpec(memory_space=pl.ANY),
                  pl.BlockSpec(memory_space=pl.ANY),
                  pl.BlockSpec((tm, tn), lambda j, i: (i, j)),
                  pl.BlockSpec((tm, tn), lambda j, i: (i, nbd + j))],
        out_specs=pl.BlockSpec((tm, tn), lambda j, i: (i, j)),
        out_shape=jax.ShapeDtypeStruct((M, D), BF16),
        scratch_shapes=[pltpu.VMEM((2, K, tn), F32), pltpu.VMEM((2, K, tn), BF16),
                        pltpu.SemaphoreType.DMA((2,))],
        compiler_params=_cparams(("arbitrary", "arbitrary")),
        name="out1",
    )(o_a, o_b, wa, wb, pbig, pbig)


def _final_kernel(mix_ref, x_ref, gate_ref, fg_ref, o_ref):
    xn = x_ref[...] + gate_ref[0] * mix_ref[...].astype(F32)
    ms = jnp.mean(xn * xn, axis=-1, keepdims=True)
    o_ref[...] = xn * lax.rsqrt(ms + EPS) * fg_ref[...]


def _final(mixed, x2, mod3, final_gain, S):
    M, D = x2.shape
    tm = ROW_TILE
    nb = S // tm
    return pl.pallas_call(
        _final_kernel,
        grid=(M // tm,),
        in_specs=[pl.BlockSpec((tm, D), lambda i: (i, 0)),
                  pl.BlockSpec((tm, D), lambda i: (i, 0)),
                  pl.BlockSpec((1, 1, D), lambda i: (i // nb, 0, 2)),
                  pl.BlockSpec((1, D), lambda i: (0, 0))],
        out_specs=pl.BlockSpec((tm, D), lambda i: (i, 0)),
        out_shape=jax.ShapeDtypeStruct((M, D), F32),
        compiler_params=_cparams(("arbitrary",)),
        name="final",
    )(mixed, x2, mod3, final_gain)


def _pad_cols(w, width):
    return jnp.pad(w, ((0, 0), (0, width - w.shape[1])))


def _proj_row_offsets(D):
    widths = (NSA_WIDTH, 6 * KV_WIDTH, 3 * NSA_HEADS, NSA_WIDTH, 3 * DN_WIDTH, DN_HEADS, DN_HEADS, DN_WIDTH, 2 * D)
    offs = [int(o) for o in np.concatenate([[0], np.cumsum(widths)])]
    tiles = lambda seg: [offs[seg] + t * MM_TN for t in range(widths[seg] // MM_TN)]
    big = tiles(8) + tiles(4) + tiles(0) + tiles(3) + tiles(7)
    kv0 = offs[1]
    tail = [kv0, kv0 + TAIL_TN]
    small = [offs[2], offs[5]]
    return big, tail, small


TAIL_TN = 3 * KV_WIDTH
TAIL_TM = 1024
TAIL_CMP_BLOCK = 0
SMALL_GATE_BLOCK = 0
SMALL_AB_BLOCK = 1


def kernel(x, c, positions, w_ada, b_ada, norm_gain, w_in, cmp_pos_k, cmp_pos_v, w_cmp_k1, w_cmp_k2,
           w_cmp_v1, w_cmp_v2, conv_w, dt_bias, a_log, dn_norm_gain, w_proj_a, w_proj_b, w_out, final_gain):
    B, S, D = x.shape
    M = B * S
    depth = w_in.shape[0]
    assert S % SEL_TILE == 0 and S % DN_CHUNK == 0 and S >= WINDOW + Q_BLOCK and D % 512 == 0 and B <= 8
    assert S // SLC_LEN <= LANES and D % 1024 == 0 and DN_DK == DN_CHUNK
    assert depth == 1, "the final RMSNorm is fused into the last layer's output kernel"

    off_dq = 2 * D
    off_q = off_dq + 3 * DN_WIDTH
    off_z = off_q + NSA_WIDTH
    off_dz = off_z + NSA_WIDTH

    x2 = x.reshape(M, D)
    pos_col = positions.reshape(M, 1)
    cmp_end = np.arange(S // CMP_STRIDE - 1) * CMP_STRIDE + CMP_LEN - 1
    pos_cmp = jnp.pad(positions[:, cmp_end], ((0, 0), (0, 1)))[:, :, None]
    c8 = jnp.pad(c, ((0, 8 - B), (0, 0)))

    for l in range(depth):
        mod = _ada(c8, w_ada[l], b_ada[l][None, :])
        mod3 = mod[:B].reshape(B, 1, 3 * D)
        h = _norm(x2, norm_gain[l][None, :], mod3, S)

        w_nk = jnp.transpose(w_in[l])
        big_rows, tail_rows, small_rows = _proj_row_offsets(D)
        pbig = _proj(h, w_nk, big_rows, MM_TM, MM_TN, BF16, "proj_big")
        ptail = _proj(h, w_nk, tail_rows, TAIL_TM, TAIL_TN, F32, "proj_tail")
        psmall = _proj(h, w_nk, small_rows, TAIL_TM, 2 * LANES, F32, "proj_small", pieces=2)

        q_t, keys, v_t = _nsa_prep(pbig, ptail, pos_col, off_q // NSA_WIDTH, S)
        kcmp = _compress(ptail, TAIL_CMP_BLOCK, cmp_pos_k[l], w_cmp_k1[l], w_cmp_k2[l], pos_cmp, B, S, True)
        vcmp = _compress(ptail, TAIL_CMP_BLOCK + NSA_GROUPS, cmp_pos_v[l], w_cmp_v1[l], w_cmp_v2[l].T,
                         pos_cmp, B, S, False)
        o_a = _nsa_attn(q_t, kcmp, vcmp, keys, v_t, psmall, pbig, off_z // NSA_WIDTH, B, S)

        cw = conv_w[l]
        dq = _dn_prep(pbig, cw, off_dq, "q", B, S)[0]
        dk, dkt = _dn_prep(pbig, cw, off_dq + DN_WIDTH, "k", B, S)
        dv = _dn_prep(pbig, cw, off_dq + 2 * DN_WIDTH, "v", B, S)[0]
        alog_row = _pad_cols(a_log[l][None, :].astype(F32), LANES)
        dtb_row = _pad_cols(dt_bias[l][None, :].astype(F32), LANES)
        dec, dect = _dn_gate(psmall, alog_row, dtb_row)
        o_b = _dn_scan(dq, dk, dv, dkt, dec, dect, pbig, off_dz // DN_WIDTH,
                       dn_norm_gain[l][None, :], B, S)

        mixed_in = _out1(o_a, o_b, w_proj_a[l], w_proj_b[l], pbig, D)
        mixed = _matmul(mixed_in, w_out[l], BF16, "out2")
        x2 = _final(mixed, x2, mod3, final_gain[None, :], S)
    return x2.reshape(B, S, D)
```

```python
import functools
import math

import numpy as np
import jax
import jax.numpy as jnp
from jax import lax
from jax.experimental import pallas as pl
from jax.experimental.pallas import tpu as pltpu

F32 = jnp.float32
BF16 = jnp.bfloat16

NSA_HEADS = 16
NSA_GROUPS = 2
NSA_HPG = NSA_HEADS // NSA_GROUPS
HEAD_DIM = 128
ROT_DIM = HEAD_DIM // 4
ROT_HALF = ROT_DIM // 2
ROPE_THETA = 500000.0
CMP_LEN = 32
CMP_STRIDE = 16
CMP_HIDDEN = 256
SLC_LEN = 64
SLC_TOPK = 16
WINDOW = 512
Q_BLOCK = 128
NSA_WIDTH = NSA_HEADS * HEAD_DIM
KV_WIDTH = NSA_GROUPS * HEAD_DIM
DN_HEADS = 16
DN_DK = 128
DN_DV = 128
DN_WIDTH = DN_HEADS * DN_DV
CONV_WIDTH = 4
EPS = 1e-6

LANES = 128
VMEM_LIMIT_BYTES = 56 * 1024 * 1024
ROW_TILE = 512

DN_CHUNK = 128
DN_HEAD_GROUP = 16
SEL_TILE = 512
VT_ROWS = HEAD_DIM + 16
NEG = -1e30


def _cparams(sem):
    return pltpu.CompilerParams(dimension_semantics=sem, vmem_limit_bytes=VMEM_LIMIT_BYTES)


def _sigmoid(x):
    return 1.0 / (1.0 + jnp.exp(-x))


def _silu(x):
    return x * _sigmoid(x)


def _dot(a, b):
    return jnp.dot(a, b, preferred_element_type=F32)


def _dot_nt(a, b):
    return lax.dot_general(a, b, (((1,), (1,)), ((), ())), preferred_element_type=F32)


def _bmm(a, b):
    return lax.dot_general(a, b, (((2,), (1,)), ((0,), (0,))), preferred_element_type=F32)


def _dot_f32(a, b):
    return jnp.dot(a, b, preferred_element_type=F32, precision=lax.Precision.HIGHEST)


def _ada_kernel(c_ref, w_ref, b_ref, o_ref):
    c = c_ref[...]
    c_hi = c.astype(BF16)
    c_lo = (c - c_hi.astype(F32)).astype(BF16)
    w = w_ref[...]
    w_hi = w.astype(BF16)
    w_lo = (w - w_hi.astype(F32)).astype(BF16)
    acc = _dot(c_hi, w_hi) + _dot(c_lo, w_hi) + _dot(c_hi, w_lo)
    o_ref[...] = acc + b_ref[...]


def _ada(c8, w_ada, b_ada):
    D, N = w_ada.shape
    tn = min(512, N)
    return pl.pallas_call(
        _ada_kernel,
        grid=(N // tn,),
        in_specs=[pl.BlockSpec((8, D), lambda j: (0, 0)),
                  pl.BlockSpec((D, tn), lambda j: (0, j)),
                  pl.BlockSpec((1, tn), lambda j: (0, j))],
        out_specs=pl.BlockSpec((8, tn), lambda j: (0, j)),
        out_shape=jax.ShapeDtypeStruct((8, N), F32),
        compiler_params=_cparams(("arbitrary",)),
        name="ada",
    )(c8, w_ada, b_ada)


def _norm_kernel(x_ref, gain_ref, shift_ref, scale_ref, o_ref):
    x = x_ref[...]
    ms = jnp.mean(x * x, axis=-1, keepdims=True)
    y = x * lax.rsqrt(ms + EPS) * gain_ref[...]
    o_ref[...] = (y * (1.0 + scale_ref[0]) + shift_ref[0]).astype(o_ref.dtype)


def _norm(x2, gain, mod3, S):
    M, D = x2.shape
    tm = ROW_TILE
    nb = S // tm
    return pl.pallas_call(
        _norm_kernel,
        grid=(M // tm,),
        in_specs=[pl.BlockSpec((tm, D), lambda i: (i, 0)),
                  pl.BlockSpec((1, D), lambda i: (0, 0)),
                  pl.BlockSpec((1, 1, D), lambda i: (i // nb, 0, 0)),
                  pl.BlockSpec((1, 1, D), lambda i: (i // nb, 0, 1))],
        out_specs=pl.BlockSpec((tm, D), lambda i: (i, 0)),
        out_shape=jax.ShapeDtypeStruct((M, D), BF16),
        compiler_params=_cparams(("arbitrary",)),
        name="norm",
    )(x2, gain, mod3, mod3)


MM_TM = 1024
MM_TN = 1024
W_ROW_ALIGN = 16
WT_CHUNK = 128
MM_CAST_COLS = 256


def _mm_kernel(offs_ref, a_ref, w_hbm, o_ref, wf32_ref, w16_ref, sems, *, w_is_nk, tn, pieces):
    j = pl.program_id(0)
    i = pl.program_id(1)
    pw = tn // pieces

    def piece_copy(jj, p):
        off = offs_ref[jj * pieces + p]
        if w_is_nk:
            src = w_hbm.at[pl.ds(pl.multiple_of(off * W_ROW_ALIGN, W_ROW_ALIGN), pw), :]
            dst = wf32_ref.at[p * pw:(p + 1) * pw, :]
        else:
            src = w_hbm.at[:, pl.ds(pl.multiple_of(off * LANES, LANES), pw)]
            dst = wf32_ref.at[:, p * pw:(p + 1) * pw]
        return pltpu.make_async_copy(src, dst, sems.at[p])

    def start_tile(jj):
        for p in range(pieces):
            piece_copy(jj, p).start()

    @pl.when((j == 0) & (i == 0))
    def _():
        start_tile(0)

    @pl.when(i == 0)
    def _():
        for p in range(pieces):
            piece_copy(j, p).wait()
        cc = min(MM_CAST_COLS, tn)
        for c0 in range(0, tn, cc):
            if w_is_nk:
                for c in range(c0, c0 + cc, WT_CHUNK):
                    w16_ref[:, c:c + WT_CHUNK] = wf32_ref[c:c + WT_CHUNK, :].T.astype(BF16)
            else:
                w16_ref[:, c0:c0 + cc] = wf32_ref[:, c0:c0 + cc].astype(BF16)
            o_ref[:, c0:c0 + cc] = _dot(a_ref[...], w16_ref[:, c0:c0 + cc]).astype(o_ref.dtype)

        @pl.when(j + 1 < pl.num_programs(0))
        def _():
            start_tile(j + 1)

    @pl.when(i > 0)
    def _():
        o_ref[...] = _dot(a_ref[...], w16_ref[...]).astype(o_ref.dtype)


def _mm_call(a, w, offs, w_is_nk, tm, tn, out_dtype, name, pieces=1):
    M, K = a.shape
    tm = min(tm, M)
    n_tiles = len(offs) // pieces
    grid_spec = pltpu.PrefetchScalarGridSpec(
        num_scalar_prefetch=1,
        grid=(n_tiles, M // tm),
        in_specs=[pl.BlockSpec((tm, K), lambda j, i, o: (i, 0)),
                  pl.BlockSpec(memory_space=pl.ANY)],
        out_specs=pl.BlockSpec((tm, tn), lambda j, i, o: (i, j)),
        scratch_shapes=[pltpu.VMEM((tn, K) if w_is_nk else (K, tn), F32),
                        pltpu.VMEM((K, tn), BF16),
                        pltpu.SemaphoreType.DMA((pieces,))],
    )
    return pl.pallas_call(
        functools.partial(_mm_kernel, w_is_nk=w_is_nk, tn=tn, pieces=pieces),
        grid_spec=grid_spec,
        out_shape=jax.ShapeDtypeStruct((M, n_tiles * tn), out_dtype),
        compiler_params=_cparams(("arbitrary", "arbitrary")),
        name=name,
    )(jnp.asarray(np.asarray(offs, np.int32)), a, w)


def _matmul(a, w, out_dtype, name):
    N = w.shape[1]
    tn = min(MM_TN, N)
    offs = [t * tn // LANES for t in range(N // tn)]
    return _mm_call(a, w, offs, False, MM_TM, tn, out_dtype, name)


def _proj(h, w_nk, row_offsets, tm, tn, out_dtype, name, pieces=1):
    assert all(int(o) % W_ROW_ALIGN == 0 and int(o) + tn // pieces <= w_nk.shape[0] for o in row_offsets)
    offs = [int(o) // W_ROW_ALIGN for o in row_offsets]
    return _mm_call(h, w_nk, offs, True, tm, tn, out_dtype, name, pieces)


def _rope_consts():
    inv = ROPE_THETA ** (-np.arange(ROT_HALF, dtype=np.float64) / ROT_HALF)
    invf = np.zeros((1, LANES), np.float32)
    invf[0, :ROT_HALF] = inv
    invf[0, ROT_HALF:ROT_DIM] = inv
    sgn = np.zeros((1, LANES), np.float32)
    sgn[0, :ROT_HALF] = -1.0
    sgn[0, ROT_HALF:ROT_DIM] = 1.0
    return jnp.asarray(invf), jnp.asarray(sgn)


def _rope_tables(pos_col, invf, sgn):
    ang = pos_col.astype(F32) * invf
    return jnp.cos(ang), jnp.sin(ang) * sgn


def _rope_apply(x, cos_t, sin_t):
    lane = lax.broadcasted_iota(jnp.int32, x.shape, 1)
    partner = jnp.where(lane < ROT_HALF,
                        pltpu.roll(x, LANES - ROT_HALF, 1),
                        pltpu.roll(x, ROT_HALF, 1))
    return x * cos_t + partner * sin_t


KEYS_WIDTH = 6 * HEAD_DIM


def _nsa_prep_kernel(q_ref, kvs_ref, kvw_ref, pos_ref, invf_ref, sgn_ref, qt_ref, keys_ref, vt_ref, *, seq_len):
    tm = q_ref.shape[0]
    cos_t, sin_t = _rope_tables(pos_ref[...], invf_ref[...], sgn_ref[...])
    qscale = (HEAD_DIM ** -0.5) * math.log2(math.e)
    for h in range(NSA_HEADS):
        sl = slice(h * HEAD_DIM, (h + 1) * HEAD_DIM)
        xq = _rope_apply(q_ref[:, sl].astype(F32), cos_t, sin_t) * qscale
        qt_ref[sl, :] = xq.T.astype(qt_ref.dtype)
    tok = (pl.program_id(0) * tm) % seq_len + lax.broadcasted_iota(jnp.int32, (tm, LANES), 0)
    lane = lax.broadcasted_iota(jnp.int32, (tm, LANES), 1)
    onehot = jnp.where(lane == (tok >> int(math.log2(SLC_LEN))), 1.0, 0.0).astype(keys_ref.dtype)
    blk = lambda i: (kvs_ref if i < 4 else kvw_ref)[:, (i % 4) * HEAD_DIM:(i % 4 + 1) * HEAD_DIM]
    for g in range(NSA_GROUPS):
        ks = _rope_apply(blk(g), cos_t, sin_t)
        keys_ref[:, (2 * g) * HEAD_DIM:(2 * g + 1) * HEAD_DIM] = ks.astype(keys_ref.dtype)
        keys_ref[:, (2 * g + 1) * HEAD_DIM:(2 * g + 2) * HEAD_DIM] = onehot
        kw = _rope_apply(blk(4 + g), cos_t, sin_t)
        keys_ref[:, (4 + g) * HEAD_DIM:(5 + g) * HEAD_DIM] = kw.astype(keys_ref.dtype)
        ones = jnp.ones((VT_ROWS - HEAD_DIM, tm), vt_ref.dtype)
        for c, src in ((g, 2 + g), (2 + g, 6 + g)):
            vt_ref[c * VT_ROWS:c * VT_ROWS + HEAD_DIM, :] = blk(src).T.astype(vt_ref.dtype)
            vt_ref[c * VT_ROWS + HEAD_DIM:(c + 1) * VT_ROWS, :] = ones


def _nsa_prep(pbig, ptail, pos_col, q_col_block, S):
    M = pbig.shape[0]
    tm = ROW_TILE
    invf, sgn = _rope_consts()
    return pl.pallas_call(
        functools.partial(_nsa_prep_kernel, seq_len=S),
        grid=(M // tm,),
        in_specs=[pl.BlockSpec((tm, NSA_WIDTH), lambda i: (i, q_col_block)),
                  pl.BlockSpec((tm, 4 * HEAD_DIM), lambda i: (i, 1)),
                  pl.BlockSpec((tm, 4 * HEAD_DIM), lambda i: (i, 2)),
                  pl.BlockSpec((tm, 1), lambda i: (i, 0)),
                  pl.BlockSpec((1, LANES), lambda i: (0, 0)),
                  pl.BlockSpec((1, LANES), lambda i: (0, 0))],
        out_specs=[pl.BlockSpec((NSA_WIDTH, tm), lambda i: (0, i)),
                   pl.BlockSpec((tm, KEYS_WIDTH), lambda i: (i, 0)),
                   pl.BlockSpec((4 * VT_ROWS, tm), lambda i: (0, i))],
        out_shape=[jax.ShapeDtypeStruct((NSA_WIDTH, M), BF16),
                   jax.ShapeDtypeStruct((M, KEYS_WIDTH), BF16),
                   jax.ShapeDtypeStruct((4 * VT_ROWS, M), BF16)],
        compiler_params=_cparams(("arbitrary",)),
        name="nsa_prep",
    )(pbig, ptail, ptail, pos_col, invf, sgn)


def _compress_kernel(x_ref, pe_ref, w1_ref, w2_ref, pos_ref, invf_ref, sgn_ref, o_ref, *, rope, n_rows):
    half = CMP_LEN // 2
    top = jnp.zeros((n_rows, CMP_HIDDEN), F32)
    bot = jnp.zeros((n_rows, CMP_HIDDEN), F32)
    for l in range(half):
        xl = x_ref[pl.ds(l, n_rows, stride=CMP_STRIDE), :]
        w_top = w1_ref[l * HEAD_DIM:(l + 1) * HEAD_DIM, :].astype(BF16)
        w_bot = w1_ref[(half + l) * HEAD_DIM:(half + l + 1) * HEAD_DIM, :].astype(BF16)
        top = top + _dot((xl + pe_ref[l:l + 1, :]).astype(BF16), w_top)
        bot = bot + _dot((xl + pe_ref[half + l:half + l + 1, :]).astype(BF16), w_bot)
    hid = top + pltpu.roll(bot, n_rows - 1, 0)
    act = _silu(hid).astype(BF16)
    if rope:
        out = _dot(act, w2_ref[...].astype(BF16))
        cos_t, sin_t = _rope_tables(pos_ref[0], invf_ref[...], sgn_ref[...])
        o_ref[0, 0] = _rope_apply(out, cos_t, sin_t).astype(o_ref.dtype)
    else:
        o_ref[0, 0, 0:HEAD_DIM, :] = _dot_nt(w2_ref[...].astype(BF16), act).astype(o_ref.dtype)
        o_ref[0, 0, HEAD_DIM:VT_ROWS, :] = jnp.ones((VT_ROWS - HEAD_DIM, n_rows), o_ref.dtype)


def _compress(ptail, col_block0, pe, w1, w2, pos_cmp, B, S, rope):
    n_rows = S // CMP_STRIDE
    invf, sgn = _rope_consts()
    kern = functools.partial(_compress_kernel, rope=rope, n_rows=n_rows)
    out_dims = (n_rows, HEAD_DIM) if rope else (VT_ROWS, n_rows)
    return pl.pallas_call(
        kern,
        grid=(B, NSA_GROUPS),
        in_specs=[pl.BlockSpec((S, HEAD_DIM), lambda b, g: (b, col_block0 + g)),
                  pl.BlockSpec((CMP_LEN, HEAD_DIM), lambda b, g: (0, 0)),
                  pl.BlockSpec((CMP_LEN * HEAD_DIM, CMP_HIDDEN), lambda b, g: (0, 0)),
                  pl.BlockSpec(w2.shape, lambda b, g: (0, 0)),
                  pl.BlockSpec((1, n_rows, 1), lambda b, g: (b, 0, 0)),
                  pl.BlockSpec((1, LANES), lambda b, g: (0, 0)),
                  pl.BlockSpec((1, LANES), lambda b, g: (0, 0))],
        out_specs=pl.BlockSpec((1, 1) + out_dims, lambda b, g: (b, g, 0, 0)),
        out_shape=jax.ShapeDtypeStruct((B, NSA_GROUPS) + out_dims, BF16),
        compiler_params=_cparams(("arbitrary", "arbitrary")),
        name="compress_k" if rope else "compress_v",
    )(ptail, pe, w1, w2, pos_cmp, invf, sgn)


def _tile_lanes(x, n):
    return jnp.concatenate([x] * n, axis=1)


def _nsa_attn_kernel(qt_ref, kc_ref, vct_ref, ks_ref, vst_ref, kw_ref, vwt_ref, g_ref, z_ref,
                     ovt_ref, o_ref, m_s, acc_s, qa_s, sa_s, sb_s, sc_s, *, seq_len):
    T = Q_BLOCK
    P = NSA_HPG
    R = P * T
    n_cmp_pad = seq_len // CMP_STRIDE
    n_cmp = n_cmp_pad - 1
    nb = seq_len // SLC_LEN
    qi = pl.program_id(2)
    qs = qi * T
    d0 = pl.multiple_of(qs, T)
    wlen = WINDOW + T
    w0 = pl.multiple_of(jnp.maximum(qs - WINDOW, 0), T)

    qt = jnp.concatenate([qt_ref[p * HEAD_DIM:(p + 1) * HEAD_DIM, :] for p in range(P)], axis=1)

    s_c = _dot(kc_ref[0, 0], qt)
    s_w = _dot(kw_ref[pl.ds(w0, wlen), :], qt)
    s_d = _dot(ks_ref[pl.ds(d0, T), 0:HEAD_DIM], qt)

    tok = lambda n: qs + lax.broadcasted_iota(jnp.int32, (n, T), 1)
    row = lambda n: lax.broadcasted_iota(jnp.int32, (n, T), 0)
    ok_c = (row(n_cmp_pad) * CMP_STRIDE + (CMP_LEN - 1) <= tok(n_cmp_pad)) & (row(n_cmp_pad) < n_cmp)
    key_w = w0 + row(wlen)
    ok_w = (key_w <= tok(wlen)) & (tok(wlen) - key_w < WINDOW)
    ok_d = row(T) <= lax.broadcasted_iota(jnp.int32, (T, T), 1)
    addmask = lambda ok: _tile_lanes(jnp.where(ok, 0.0, NEG).astype(F32), P)
    s_c = s_c + addmask(ok_c)
    s_w = s_w + addmask(ok_w)
    s_d = s_d + addmask(ok_d)

    m_c = jnp.max(s_c, axis=0, keepdims=True)
    m_w = jnp.max(s_w, axis=0, keepdims=True)
    m_d = jnp.max(s_d, axis=0, keepdims=True)
    m_c = jnp.where(m_c > 0.5 * NEG, m_c, 0.0)
    p_c = jnp.exp2(s_c - m_c)
    p_w = jnp.exp2(s_w - m_w)
    p_d = jnp.exp2(s_d - m_d)
    o_c = _dot(vct_ref[0, 0], p_c.astype(BF16))
    o_w = _dot(vwt_ref[:, pl.ds(w0, wlen)], p_w.astype(BF16))
    inv_c = 1.0 / jnp.maximum(jnp.sum(p_c, axis=0, keepdims=True), 1e-30)
    inv_w = 1.0 / o_w[HEAD_DIM:HEAD_DIM + 1, :]
    m_s[...] = m_d
    acc_s[...] = _dot(vst_ref[:, pl.ds(d0, T)], p_d.astype(BF16))

    pn = p_c * inv_c
    p_sum = pn[:, 0:T]
    for p in range(1, P):
        p_sum = p_sum + pn[:, p * T:(p + 1) * T]
    ps_hi = p_sum.astype(BF16)
    ps_lo = (p_sum - ps_hi.astype(F32)).astype(BF16)
    imp = (_dot(ovt_ref[...], ps_hi) + _dot(ovt_ref[...], ps_lo))[0:nb]
    jrow = row(nb)
    t_lane = tok(nb)
    tb = t_lane >> int(math.log2(SLC_LEN))
    visible = jrow * SLC_LEN <= t_lane
    forced = (jrow == 0) | (jrow == tb) | (jrow == tb - 1)
    score = jnp.where(forced, 1e9, jnp.where(visible, imp, -jnp.inf))
    cnt = jnp.zeros((nb, T), F32)
    for k in range(nb):
        rk = score[k:k + 1, :]
        cnt = cnt + jnp.where(jrow > k, jnp.where(rk >= score, 1.0, 0.0), jnp.where(rk > score, 1.0, 0.0))
    keep = (cnt < float(min(SLC_TOPK, nb))) & visible & (jrow < 2 * qi)
    bias = jnp.where(keep, 0.0, NEG).astype(F32)
    bias = jnp.concatenate([bias, jnp.full((LANES - nb, T), NEG, F32)], axis=0) if nb < LANES else bias
    qa_s[0:HEAD_DIM, :] = qt
    qa_s[HEAD_DIM:2 * HEAD_DIM, :] = _tile_lanes(bias.astype(BF16), P)

    n_t = (qs + SEL_TILE - 1) // SEL_TILE

    def scores_into(kt, dst):
        dst[...] = _dot(ks_ref[kt * SEL_TILE:(kt + 1) * SEL_TILE, :], qa_s[...])

    def consume(kt, src):
        k0 = kt * SEL_TILE
        s = src[...]
        m_prev = m_s[...]
        m_new = jnp.maximum(m_prev, jnp.max(s, axis=0, keepdims=True))
        alpha = jnp.exp2(m_prev - m_new)
        p = jnp.exp2(s - m_new)
        acc_s[...] = alpha * acc_s[...] + _dot(vst_ref[:, pl.ds(k0, SEL_TILE)], p.astype(BF16))
        m_s[...] = m_new

    bufs = (sa_s, sb_s, sc_s)

    def finish():
        gates_t = _sigmoid(g_ref[...]).T
        first_group = pl.program_id(1) == 0

        def grow(r):
            rows = []
            for p in range(P):
                h0 = r * NSA_HEADS + p
                rows.append(jnp.where(first_group, gates_t[h0:h0 + 1, :], gates_t[h0 + P:h0 + P + 1, :]))
            return jnp.concatenate(rows, axis=1)

        o_s = acc_s[...]
        hd = slice(0, HEAD_DIM)
        o_t = (o_c[hd] * (grow(0) * inv_c) + o_s[hd] * (grow(1) / o_s[HEAD_DIM:HEAD_DIM + 1, :])
               + o_w[hd] * (grow(2) * inv_w))
        for p in range(P):
            sl = slice(p * HEAD_DIM, (p + 1) * HEAD_DIM)
            o_p = o_t[:, p * T:(p + 1) * T].T
            o_ref[:, sl] = (o_p * _silu(z_ref[:, sl].astype(F32))).astype(o_ref.dtype)

    for r in range(0, seq_len // SEL_TILE + 1):
        @pl.when(n_t == r)
        def _(r=r):
            if r > 0:
                scores_into(0, bufs[0])
            for t in range(r):
                if t + 1 < r:
                    scores_into(t + 1, bufs[(t + 1) % len(bufs)])
                consume(t, bufs[t % len(bufs)])
            finish()


def _nsa_consts(S):
    n_cmp_pad = S // CMP_STRIDE
    nb = S // SLC_LEN
    cmp_start = np.arange(n_cmp_pad) * CMP_STRIDE
    slc_start = np.arange(LANES) * SLC_LEN
    ov = ((cmp_start[:, None] <= slc_start[None, :] + SLC_LEN - 1)
          & (cmp_start[:, None] + CMP_LEN - 1 >= slc_start[None, :])
          & (np.arange(LANES)[None, :] < nb)
          & (np.arange(n_cmp_pad)[:, None] < n_cmp_pad - 1)).astype(np.float32)
    return jnp.asarray(ov.T, dtype=BF16)


def _nsa_attn(q_t, kcmp, vcmp_t, keys, v_t, psmall, pbig, z_col_block, B, S):
    M = B * S
    nq = S // Q_BLOCK
    gw = NSA_HPG * HEAD_DIM
    ovt = _nsa_consts(S)
    n_cmp_pad = S // CMP_STRIDE
    R = NSA_HPG * Q_BLOCK
    kern = functools.partial(_nsa_attn_kernel, seq_len=S)
    return pl.pallas_call(
        kern,
        grid=(B, NSA_GROUPS, nq),
        in_specs=[pl.BlockSpec((gw, Q_BLOCK), lambda b, g, i: (g, b * nq + i)),
                  pl.BlockSpec((1, 1, n_cmp_pad, HEAD_DIM), lambda b, g, i: (b, g, 0, 0)),
                  pl.BlockSpec((1, 1, VT_ROWS, n_cmp_pad), lambda b, g, i: (b, g, 0, 0)),
                  pl.BlockSpec((S, 2 * HEAD_DIM), lambda b, g, i: (b, g)),
                  pl.BlockSpec((VT_ROWS, S), lambda b, g, i: (g, b)),
                  pl.BlockSpec((S, HEAD_DIM), lambda b, g, i: (b, 4 + g)),
                  pl.BlockSpec((VT_ROWS, S), lambda b, g, i: (2 + g, b)),
                  pl.BlockSpec((Q_BLOCK, LANES), lambda b, g, i: (b * nq + i, SMALL_GATE_BLOCK)),
                  pl.BlockSpec((Q_BLOCK, gw), lambda b, g, i: (b * nq + i, z_col_block * NSA_GROUPS + g)),
                  pl.BlockSpec((LANES, n_cmp_pad), lambda b, g, i: (0, 0))],
        out_specs=pl.BlockSpec((Q_BLOCK, gw), lambda b, g, i: (b * nq + i, g)),
        out_shape=jax.ShapeDtypeStruct((M, NSA_WIDTH), BF16),
        scratch_shapes=[pltpu.VMEM((1, R), F32),
                        pltpu.VMEM((VT_ROWS, R), F32),
                        pltpu.VMEM((2 * HEAD_DIM, R), BF16),
                        pltpu.VMEM((SEL_TILE, R), F32),
                        pltpu.VMEM((SEL_TILE, R), F32),
                        pltpu.VMEM((SEL_TILE, R), F32)],
        compiler_params=_cparams(("arbitrary", "arbitrary", "arbitrary")),
        name="nsa_attn",
    )(q_t, kcmp, vcmp_t, keys, v_t, keys, v_t, psmall, pbig, ovt)


def _dn_prep_kernel(x_ref, w_ref, o_ref, *rest, mode, seq_len):
    xpad_ref = rest[-1]
    w = w_ref[...]
    xpad_ref[0:CONV_PAD, :] = jnp.zeros((CONV_PAD, xpad_ref.shape[1]), F32)
    xpad_ref[CONV_PAD:, :] = x_ref[...].astype(F32)
    y = xpad_ref[CONV_PAD:, :] * w[CONV_WIDTH - 1:CONV_WIDTH, :]
    for k in range(1, CONV_WIDTH):
        y = y + xpad_ref[pl.ds(CONV_PAD - k, seq_len), :] * w[CONV_WIDTH - 1 - k:CONV_WIDTH - k, :]
    y = _silu(y)
    if mode in ("q", "k"):
        outs = []
        for h in range(y.shape[1] // DN_DK):
            yh = y[:, h * DN_DK:(h + 1) * DN_DK]
            ss = jnp.sum(yh * yh, axis=-1, keepdims=True)
            yh = yh * lax.rsqrt(ss + EPS)
            if mode == "q":
                yh = yh * (DN_DK ** -0.5)
            outs.append(yh)
        y = jnp.concatenate(outs, axis=1)
    o_ref[...] = y.astype(o_ref.dtype)
    if mode == "k":
        kt_ref = rest[0]
        kt_ref[...] = y.T.astype(kt_ref.dtype)


DN_PREP_COLS = 128
CONV_PAD = 8


def _dn_prep(pbig, conv_w, col0, mode, B, S):
    M = B * S
    tw = DN_PREP_COLS
    col0_blocks = col0 // tw
    nj = DN_WIDTH // tw
    wcol0 = {"q": 0, "k": nj, "v": 2 * nj}[mode]
    kern = functools.partial(_dn_prep_kernel, mode=mode, seq_len=S)
    out_specs = [pl.BlockSpec((S, tw), lambda b, j: (b, j))]
    out_shape = [jax.ShapeDtypeStruct((M, DN_WIDTH), BF16)]
    if mode == "k":
        out_specs.append(pl.BlockSpec((tw, S), lambda b, j: (j, b)))
        out_shape.append(jax.ShapeDtypeStruct((DN_WIDTH, M), BF16))
    return pl.pallas_call(
        kern,
        grid=(B, nj),
        in_specs=[pl.BlockSpec((S, tw), lambda b, j: (b, col0_blocks + j)),
                  pl.BlockSpec((CONV_WIDTH, tw), lambda b, j: (0, wcol0 + j))],
        out_specs=out_specs,
        out_shape=out_shape,
        scratch_shapes=[pltpu.VMEM((CONV_PAD + S, tw), F32)],
        compiler_params=_cparams(("arbitrary", "arbitrary")),
        name="dn_prep_" + mode,
    )(pbig, conv_w)


def _dn_gate_kernel(ab_ref, alog_ref, dtb_ref, o_ref, ot_ref, *, tm):
    ab = ab_ref[...]
    x = ab + dtb_ref[...]
    softplus = jnp.maximum(x, 0.0) + jnp.log(1.0 + jnp.exp(-jnp.abs(x)))
    g = -jnp.exp(alog_ref[...]) * softplus
    beta = _sigmoid(ab)
    lane = lax.broadcasted_iota(jnp.int32, (DN_CHUNK, LANES), 1)
    r = lax.broadcasted_iota(jnp.int32, (DN_CHUNK, DN_CHUNK), 0)
    c = lax.broadcasted_iota(jnp.int32, (DN_CHUNK, DN_CHUNK), 1)
    tril = jnp.where(r >= c, 1.0, 0.0).astype(F32)
    for ci in range(tm // DN_CHUNK):
        rows = slice(ci * DN_CHUNK, (ci + 1) * DN_CHUNK)
        dec = _dot_f32(tril, g[rows])
        out = jnp.where(lane < DN_HEADS, dec, beta[rows])
        o_ref[rows, :] = out
        ot_ref[:, rows] = out.T


def _dn_gate(psmall, alog_row, dtb_row):
    M = psmall.shape[0]
    tm = 512
    kern = functools.partial(_dn_gate_kernel, tm=tm)
    return pl.pallas_call(
        kern,
        grid=(M // tm,),
        in_specs=[pl.BlockSpec((tm, LANES), lambda i: (i, SMALL_AB_BLOCK)),
                  pl.BlockSpec((1, LANES), lambda i: (0, 0)),
                  pl.BlockSpec((1, LANES), lambda i: (0, 0))],
        out_specs=[pl.BlockSpec((tm, LANES), lambda i: (i, 0)),
                   pl.BlockSpec((LANES, tm), lambda i: (0, i))],
        out_shape=[jax.ShapeDtypeStruct((M, LANES), F32),
                   jax.ShapeDtypeStruct((LANES, M), F32)],
        compiler_params=_cparams(("arbitrary",)),
        name="dn_gate",
    )(psmall, alog_row, dtb_row)


def _dn_scan_kernel(q_ref, k_ref, v_ref, kt_ref, dec_ref, dect_ref, z_ref, gain_ref, o_ref, st_ref):
    C = DN_CHUNK

    @pl.when(pl.program_id(1) == 0)
    def _():
        st_ref[...] = jnp.zeros(st_ref.shape, F32)

    dec = dec_ref[...]
    dect = dect_ref[...]
    r = lax.broadcasted_iota(jnp.int32, (C, C), 0)
    c = lax.broadcasted_iota(jnp.int32, (C, C), 1)
    tril = r >= c
    strict = r > c
    gain = gain_ref[...]
    eye = jnp.where(r == c, 1.0, 0.0).astype(F32)
    lvl_masks = []
    for lg in range(int(math.log2(C))):
        lvl_masks.append(((r >> (lg + 1)) == (c >> (lg + 1))) & ((r >> lg) != (c >> lg)))

    for h0 in range(0, DN_HEADS, DN_HEAD_GROUP):
        heads = range(h0, h0 + DN_HEAD_GROUP)
        cols = lambda ref: jnp.stack([ref[:, h * DN_DK:(h + 1) * DN_DK] for h in heads])
        qh = cols(q_ref).astype(F32)
        kh = cols(k_ref).astype(F32)
        vh = cols(v_ref).astype(F32)
        kth = jnp.stack([kt_ref[h * DN_DK:(h + 1) * DN_DK, :] for h in heads])
        dcol = jnp.stack([jnp.broadcast_to(dec[:, h:h + 1], (C, C)) for h in heads])
        bcol = jnp.stack([jnp.broadcast_to(dec[:, DN_HEADS + h:DN_HEADS + h + 1], (C, C)) for h in heads])
        drow = jnp.stack([jnp.broadcast_to(dect[h:h + 1, :], (C, C)) for h in heads])
        dlast = jnp.stack([jnp.broadcast_to(dect[h:h + 1, C - 1:C], (C, C)) for h in heads])
        lmat = jnp.exp(jnp.where(tril[None], dcol - drow, NEG))
        e_d = jnp.exp(dcol)
        kb = kh * bcol
        vb = vh * bcol
        a = jnp.where(strict[None], _bmm(kb.astype(BF16), kth) * lmat, 0.0)
        attn = jnp.where(tril[None], _bmm(qh.astype(BF16), kth) * lmat, 0.0)
        tinv = eye[None] - jnp.where(lvl_masks[0][None], a, 0.0)
        for lm in lvl_masks[1:]:
            t16 = tinv.astype(BF16)
            lo = jnp.where(lm[None], a, 0.0).astype(BF16)
            tinv = tinv - _bmm(t16, _bmm(lo, t16).astype(BF16))
        t16 = tinv.astype(BF16)
        u = _bmm(t16, vb.astype(BF16))
        w = _bmm(t16, (kb * e_d).astype(BF16))
        st = st_ref[h0:h0 + DN_HEAD_GROUP]
        st16 = st.astype(BF16)
        v_new = u - _bmm(w.astype(BF16), st16)
        v_new16 = v_new.astype(BF16)
        o = _bmm((qh * e_d).astype(BF16), st16) + _bmm(attn.astype(BF16), v_new16)
        kdt = (kth.astype(F32) * jnp.exp(dlast - drow)).astype(BF16)
        st_ref[h0:h0 + DN_HEAD_GROUP] = st * jnp.exp(dlast) + _bmm(kdt, v_new16)
        ms = jnp.mean(o * o, axis=-1, keepdims=True)
        y = o * lax.rsqrt(ms + EPS) * gain[None]
        for i, h in enumerate(heads):
            sl = slice(h * DN_DV, (h + 1) * DN_DV)
            o_ref[:, sl] = (y[i] * _silu(z_ref[:, sl].astype(F32))).astype(o_ref.dtype)


def _dn_scan(dq, dk, dv, dkt, dec, dect, pbig, z_col_block, gain_row, B, S):
    M = B * S
    C = DN_CHUNK
    nc = S // C
    tok = lambda b, n: (b * nc + n, 0)
    return pl.pallas_call(
        _dn_scan_kernel,
        grid=(B, nc),
        in_specs=[pl.BlockSpec((C, DN_WIDTH), tok),
                  pl.BlockSpec((C, DN_WIDTH), tok),
                  pl.BlockSpec((C, DN_WIDTH), tok),
                  pl.BlockSpec((DN_WIDTH, C), lambda b, n: (0, b * nc + n)),
                  pl.BlockSpec((C, LANES), tok),
                  pl.BlockSpec((LANES, C), lambda b, n: (0, b * nc + n)),
                  pl.BlockSpec((C, DN_WIDTH), lambda b, n: (b * nc + n, z_col_block)),
                  pl.BlockSpec((1, DN_DV), lambda b, n: (0, 0))],
        out_specs=pl.BlockSpec((C, DN_WIDTH), tok),
        out_shape=jax.ShapeDtypeStruct((M, DN_WIDTH), BF16),
        scratch_shapes=[pltpu.VMEM((DN_HEADS, DN_DK, DN_DV), F32)],
        compiler_params=_cparams(("arbitrary", "arbitrary")),
        name="dn_scan",
    )(dq, dk, dv, dkt, dec, dect, pbig, gain_row)


def _out1_kernel(oa_ref, ob_ref, wa_hbm, wb_hbm, ga_ref, gb_ref, o_ref, wf32_ref, w16_ref, sems, *, tn):
    j = pl.program_id(0)
    i = pl.program_id(1)
    hw = tn // 2

    def half_copy(jj, b, h):
        src = (wa_hbm, wb_hbm)[b].at[:, pl.ds(pl.multiple_of(jj * tn + h * hw, hw), hw)]
        return pltpu.make_async_copy(src, wf32_ref.at[b], sems.at[b])

    def start_half(jj, h):
        half_copy(jj, 0, h).start()
        half_copy(jj, 1, h).start()

    def merged(cols):
        ya = _dot(oa_ref[...], w16_ref[0, :, cols])
        yb = _dot(ob_ref[...], w16_ref[1, :, cols])
        mix = _sigmoid(ga_ref[:, cols].astype(F32)) * ya + _sigmoid(gb_ref[:, cols].astype(F32)) * yb
        o_ref[:, cols] = mix.astype(o_ref.dtype)

    @pl.when((j == 0) & (i == 0))
    def _():
        start_half(0, 0)

    @pl.when(i == 0)
    def _():
        cc = min(MM_CAST_COLS, hw)
        for h in range(2):
            half_copy(j, 0, h).wait()
            half_copy(j, 1, h).wait()
            for c0 in range(0, hw, cc):
                cols = slice(h * hw + c0, h * hw + c0 + cc)
                w16_ref[0, :, cols] = wf32_ref[0, :, c0:c0 + cc].astype(BF16)
                w16_ref[1, :, cols] = wf32_ref[1, :, c0:c0 + cc].astype(BF16)
                if h == 0 and c0 + cc == hw:
                    start_half(j, 1)
                merged(cols)

        @pl.when(j + 1 < pl.num_programs(0))
        def _():
            start_half(j + 1, 0)

    @pl.when(i > 0)
    def _():
        merged(slice(0, tn))


OUT1_TM = 1024


def _out1(o_a, o_b, wa, wb, pbig, D):
    M, K = o_a.shape
    tm = min(OUT1_TM, M)
    tn = min(MM_TN, D)
    nbd = D // tn
    return pl.pallas_call(
        functools.partial(_out1_kernel, tn=tn),
        grid=(D // tn, M // tm),
        in_specs=[pl.BlockSpec((tm, K), lambda j, i: (i, 0)),
                  pl.BlockSpec((tm, K), lambda j, i: (i, 0)),
                  pl.BlockSpec(memory_space=pl.ANY),
                  pl.BlockSpec(memory_space=pl.ANY),
                  pl.BlockSpec((tm, tn), lambda j, i: (i, j)),
                  pl.BlockSpec((tm, tn), lambda j, i: (i, nbd + j))],
        out_specs=pl.BlockSpec((tm, tn), lambda j, i: (i, j)),
        out_shape=jax.ShapeDtypeStruct((M, D), BF16),
        scratch_shapes=[pltpu.VMEM((2, K, tn // 2), F32), pltpu.VMEM((2, K, tn), BF16),
                        pltpu.SemaphoreType.DMA((2,))],
        compiler_params=_cparams(("arbitrary", "arbitrary")),
        name="out1",
    )(o_a, o_b, wa, wb, pbig, pbig)


def _final_kernel(mix_ref, x_ref, gate_ref, fg_ref, o_ref):
    xn = x_ref[...] + gate_ref[0] * mix_ref[...].astype(F32)
    ms = jnp.mean(xn * xn, axis=-1, keepdims=True)
    o_ref[...] = xn * lax.rsqrt(ms + EPS) * fg_ref[...]


def _final(mixed, x2, mod3, final_gain, S):
    M, D = x2.shape
    tm = ROW_TILE
    nb = S // tm
    return pl.pallas_call(
        _final_kernel,
        grid=(M // tm,),
        in_specs=[pl.BlockSpec((tm, D), lambda i: (i, 0)),
                  pl.BlockSpec((tm, D), lambda i: (i, 0)),
                  pl.BlockSpec((1, 1, D), lambda i: (i // nb, 0, 2)),
                  pl.BlockSpec((1, D), lambda i: (0, 0))],
        out_specs=pl.BlockSpec((tm, D), lambda i: (i, 0)),
        out_shape=jax.ShapeDtypeStruct((M, D), F32),
        compiler_params=_cparams(("arbitrary",)),
        name="final",
    )(mixed, x2, mod3, final_gain)


def _pad_cols(w, width):
    return jnp.pad(w, ((0, 0), (0, width - w.shape[1])))


def _proj_row_offsets(D):
    widths = (NSA_WIDTH, 6 * KV_WIDTH, 3 * NSA_HEADS, NSA_WIDTH, 3 * DN_WIDTH, DN_HEADS, DN_HEADS, DN_WIDTH, 2 * D)
    offs = [int(o) for o in np.concatenate([[0], np.cumsum(widths)])]
    tiles = lambda seg: [offs[seg] + t * MM_TN for t in range(widths[seg] // MM_TN)]
    big = tiles(8) + tiles(4) + tiles(0) + tiles(3) + tiles(7)
    kv0 = offs[1]
    tail = [kv0, kv0 + TAIL_TN]
    small = [offs[2], offs[5]]
    return big, tail, small


TAIL_TN = 3 * KV_WIDTH
TAIL_TM = 1024
TAIL_CMP_BLOCK = 0
SMALL_GATE_BLOCK = 0
SMALL_AB_BLOCK = 1


def kernel(x, c, positions, w_ada, b_ada, norm_gain, w_in, cmp_pos_k, cmp_pos_v, w_cmp_k1, w_cmp_k2,
           w_cmp_v1, w_cmp_v2, conv_w, dt_bias, a_log, dn_norm_gain, w_proj_a, w_proj_b, w_out, final_gain):
    B, S, D = x.shape
    M = B * S
    depth = w_in.shape[0]
    assert S % SEL_TILE == 0 and S % DN_CHUNK == 0 and S >= WINDOW + Q_BLOCK and D % 512 == 0 and B <= 8
    assert S // SLC_LEN <= LANES and D % 1024 == 0 and DN_DK == DN_CHUNK
    assert depth == 1, "the final RMSNorm is fused into the last layer's output kernel"

    off_dq = 2 * D
    off_q = off_dq + 3 * DN_WIDTH
    off_z = off_q + NSA_WIDTH
    off_dz = off_z + NSA_WIDTH

    x2 = x.reshape(M, D)
    pos_col = positions.reshape(M, 1)
    cmp_end = np.arange(S // CMP_STRIDE - 1) * CMP_STRIDE + CMP_LEN - 1
    pos_cmp = jnp.pad(positions[:, cmp_end], ((0, 0), (0, 1)))[:, :, None]
    c8 = jnp.pad(c, ((0, 8 - B), (0, 0)))

    for l in range(depth):
        mod = _ada(c8, w_ada[l], b_ada[l][None, :])
        mod3 = mod[:B].reshape(B, 1, 3 * D)
        h = _norm(x2, norm_gain[l][None, :], mod3, S)

        w_nk = jnp.transpose(w_in[l])
        big_rows, tail_rows, small_rows = _proj_row_offsets(D)
        pbig = _proj(h, w_nk, big_rows, MM_TM, MM_TN, BF16, "proj_big")
        ptail = _proj(h, w_nk, tail_rows, TAIL_TM, TAIL_TN, F32, "proj_tail")
        psmall = _proj(h, w_nk, small_rows, TAIL_TM, 2 * LANES, F32, "proj_small", pieces=2)

        q_t, keys, v_t = _nsa_prep(pbig, ptail, pos_col, off_q // NSA_WIDTH, S)
        kcmp = _compress(ptail, TAIL_CMP_BLOCK, cmp_pos_k[l], w_cmp_k1[l], w_cmp_k2[l], pos_cmp, B, S, True)
        vcmp = _compress(ptail, TAIL_CMP_BLOCK + NSA_GROUPS, cmp_pos_v[l], w_cmp_v1[l], w_cmp_v2[l].T,
                         pos_cmp, B, S, False)
        o_a = _nsa_attn(q_t, kcmp, vcmp, keys, v_t, psmall, pbig, off_z // NSA_WIDTH, B, S)

        cw = conv_w[l]
        dq = _dn_prep(pbig, cw, off_dq, "q", B, S)[0]
        dk, dkt = _dn_prep(pbig, cw, off_dq + DN_WIDTH, "k", B, S)
        dv = _dn_prep(pbig, cw, off_dq + 2 * DN_WIDTH, "v", B, S)[0]
        alog_row = _pad_cols(a_log[l][None, :].astype(F32), LANES)
        dtb_row = _pad_cols(dt_bias[l][None, :].astype(F32), LANES)
        dec, dect = _dn_gate(psmall, alog_row, dtb_row)
        o_b = _dn_scan(dq, dk, dv, dkt, dec, dect, pbig, off_dz // DN_WIDTH,
                       dn_norm_gain[l][None, :], B, S)

        mixed_in = _out1(o_a, o_b, w_proj_a[l], w_proj_b[l], pbig, D)
        mixed = _matmul(mixed_in, w_out[l], BF16, "out2")
        x2 = _final(mixed, x2, mod3, final_gain[None, :], S)
    return x2.reshape(B, S, D)
```

```python
import functools
import math

import numpy as np
import jax
import jax.numpy as jnp
from jax import lax
from jax.experimental import pallas as pl
from jax.experimental.pallas import tpu as pltpu

F32 = jnp.float32
BF16 = jnp.bfloat16

NSA_HEADS = 16
NSA_GROUPS = 2
NSA_HPG = NSA_HEADS // NSA_GROUPS
HEAD_DIM = 128
ROT_DIM = HEAD_DIM // 4
ROT_HALF = ROT_DIM // 2
ROPE_THETA = 500000.0
CMP_LEN = 32
CMP_STRIDE = 16
CMP_HIDDEN = 256
SLC_LEN = 64
SLC_TOPK = 16
WINDOW = 512
Q_BLOCK = 128
NSA_WIDTH = NSA_HEADS * HEAD_DIM
KV_WIDTH = NSA_GROUPS * HEAD_DIM
DN_HEADS = 16
DN_DK = 128
DN_DV = 128
DN_WIDTH = DN_HEADS * DN_DV
CONV_WIDTH = 4
EPS = 1e-6

LANES = 128
VMEM_LIMIT_BYTES = 56 * 1024 * 1024
ROW_TILE = 512

DN_CHUNK = 128
DN_HEAD_GROUP = 16
SEL_TILE = 512
VT_ROWS = HEAD_DIM + 16
NEG = -1e30


def _cparams(sem):
    return pltpu.CompilerParams(dimension_semantics=sem, vmem_limit_bytes=VMEM_LIMIT_BYTES)


def _sigmoid(x):
    return 1.0 / (1.0 + jnp.exp(-x))


def _silu(x):
    return x * _sigmoid(x)


def _dot(a, b):
    return jnp.dot(a, b, preferred_element_type=F32)


def _dot_nt(a, b):
    return lax.dot_general(a, b, (((1,), (1,)), ((), ())), preferred_element_type=F32)


def _bmm(a, b):
    return lax.dot_general(a, b, (((2,), (1,)), ((0,), (0,))), preferred_element_type=F32)


def _dot_f32(a, b):
    return jnp.dot(a, b, preferred_element_type=F32, precision=lax.Precision.HIGHEST)


def _ada_kernel(c_ref, w_ref, b_ref, o_ref):
    c = c_ref[...]
    c_hi = c.astype(BF16)
    c_lo = (c - c_hi.astype(F32)).astype(BF16)
    w = w_ref[...]
    w_hi = w.astype(BF16)
    w_lo = (w - w_hi.astype(F32)).astype(BF16)
    acc = _dot(c_hi, w_hi) + _dot(c_lo, w_hi) + _dot(c_hi, w_lo)
    o_ref[...] = acc + b_ref[...]


def _ada(c8, w_ada, b_ada):
    D, N = w_ada.shape
    tn = min(512, N)
    return pl.pallas_call(
        _ada_kernel,
        grid=(N // tn,),
        in_specs=[pl.BlockSpec((8, D), lambda j: (0, 0)),
                  pl.BlockSpec((D, tn), lambda j: (0, j)),
                  pl.BlockSpec((1, tn), lambda j: (0, j))],
        out_specs=pl.BlockSpec((8, tn), lambda j: (0, j)),
        out_shape=jax.ShapeDtypeStruct((8, N), F32),
        compiler_params=_cparams(("arbitrary",)),
        name="ada",
    )(c8, w_ada, b_ada)


NORM_SLOTS = 3


def _norm_kernel(x_hbm, gain_ref, shift_ref, scale_ref, o_ref, xbuf_ref, sems, *, tm, n_steps):
    s = pl.program_id(0)

    def tile_copy(step, slot):
        src = x_hbm.at[pl.ds(pl.multiple_of(step * tm, tm), tm), :]
        return pltpu.make_async_copy(src, xbuf_ref.at[slot], sems.at[slot])

    @pl.when(s == 0)
    def _():
        for t in range(min(NORM_SLOTS - 1, n_steps)):
            tile_copy(t, t).start()

    ahead = s + (NORM_SLOTS - 1)

    @pl.when(ahead < n_steps)
    def _():
        tile_copy(ahead, ahead % NORM_SLOTS).start()

    slot = s % NORM_SLOTS
    tile_copy(s, slot).wait()
    x = xbuf_ref[slot]
    ms = jnp.mean(x * x, axis=-1, keepdims=True)
    y = x * lax.rsqrt(ms + EPS) * gain_ref[...]
    o_ref[...] = (y * (1.0 + scale_ref[0]) + shift_ref[0]).astype(o_ref.dtype)


def _norm(x2, gain, mod3, S):
    M, D = x2.shape
    tm = ROW_TILE
    nb = S // tm
    n_steps = M // tm
    return pl.pallas_call(
        functools.partial(_norm_kernel, tm=tm, n_steps=n_steps),
        grid=(n_steps,),
        in_specs=[pl.BlockSpec(memory_space=pl.ANY),
                  pl.BlockSpec((1, D), lambda i: (0, 0)),
                  pl.BlockSpec((1, 1, D), lambda i: (i // nb, 0, 0)),
                  pl.BlockSpec((1, 1, D), lambda i: (i // nb, 0, 1))],
        out_specs=pl.BlockSpec((tm, D), lambda i: (i, 0)),
        out_shape=jax.ShapeDtypeStruct((M, D), BF16),
        scratch_shapes=[pltpu.VMEM((NORM_SLOTS, tm, D), F32), pltpu.SemaphoreType.DMA((NORM_SLOTS,))],
        compiler_params=_cparams(("arbitrary",)),
        name="norm",
    )(x2, gain, mod3, mod3)


MM_TM = 1024
MM_TN = 1024
W_ROW_ALIGN = 16
WT_CHUNK = 128
MM_CAST_COLS = 256


def _mm_kernel(offs_ref, a_ref, w_hbm, o_ref, wf32_ref, w16_ref, sems, *, w_is_nk, tn, pieces):
    j = pl.program_id(0)
    i = pl.program_id(1)
    pw = tn // pieces

    def piece_copy(jj, p):
        off = offs_ref[jj * pieces + p]
        if w_is_nk:
            src = w_hbm.at[pl.ds(pl.multiple_of(off * W_ROW_ALIGN, W_ROW_ALIGN), pw), :]
            dst = wf32_ref.at[p * pw:(p + 1) * pw, :]
        else:
            src = w_hbm.at[:, pl.ds(pl.multiple_of(off * LANES, LANES), pw)]
            dst = wf32_ref.at[:, p * pw:(p + 1) * pw]
        return pltpu.make_async_copy(src, dst, sems.at[p])

    def start_tile(jj):
        for p in range(pieces):
            piece_copy(jj, p).start()

    @pl.when((j == 0) & (i == 0))
    def _():
        start_tile(0)

    @pl.when(i == 0)
    def _():
        for p in range(pieces):
            piece_copy(j, p).wait()
        cc = min(MM_CAST_COLS, tn)
        for c0 in range(0, tn, cc):
            if w_is_nk:
                for c in range(c0, c0 + cc, WT_CHUNK):
                    w16_ref[:, c:c + WT_CHUNK] = wf32_ref[c:c + WT_CHUNK, :].T.astype(BF16)
            else:
                w16_ref[:, c0:c0 + cc] = wf32_ref[:, c0:c0 + cc].astype(BF16)
            o_ref[:, c0:c0 + cc] = _dot(a_ref[...], w16_ref[:, c0:c0 + cc]).astype(o_ref.dtype)

        @pl.when(j + 1 < pl.num_programs(0))
        def _():
            start_tile(j + 1)

    @pl.when(i > 0)
    def _():
        o_ref[...] = _dot(a_ref[...], w16_ref[...]).astype(o_ref.dtype)


def _mm_call(a, w, offs, w_is_nk, tm, tn, out_dtype, name, pieces=1):
    M, K = a.shape
    tm = min(tm, M)
    n_tiles = len(offs) // pieces
    grid_spec = pltpu.PrefetchScalarGridSpec(
        num_scalar_prefetch=1,
        grid=(n_tiles, M // tm),
        in_specs=[pl.BlockSpec((tm, K), lambda j, i, o: (i, 0)),
                  pl.BlockSpec(memory_space=pl.ANY)],
        out_specs=pl.BlockSpec((tm, tn), lambda j, i, o: (i, j)),
        scratch_shapes=[pltpu.VMEM((tn, K) if w_is_nk else (K, tn), F32),
                        pltpu.VMEM((K, tn), BF16),
                        pltpu.SemaphoreType.DMA((pieces,))],
    )
    return pl.pallas_call(
        functools.partial(_mm_kernel, w_is_nk=w_is_nk, tn=tn, pieces=pieces),
        grid_spec=grid_spec,
        out_shape=jax.ShapeDtypeStruct((M, n_tiles * tn), out_dtype),
        compiler_params=_cparams(("arbitrary", "arbitrary")),
        name=name,
    )(jnp.asarray(np.asarray(offs, np.int32)), a, w)


def _matmul(a, w, out_dtype, name):
    N = w.shape[1]
    tn = min(MM_TN, N)
    offs = [t * tn // LANES for t in range(N // tn)]
    return _mm_call(a, w, offs, False, MM_TM, tn, out_dtype, name)


def _proj(h, w_nk, row_offsets, tm, tn, out_dtype, name, pieces=1):
    assert all(int(o) % W_ROW_ALIGN == 0 and int(o) + tn // pieces <= w_nk.shape[0] for o in row_offsets)
    offs = [int(o) // W_ROW_ALIGN for o in row_offsets]
    return _mm_call(h, w_nk, offs, True, tm, tn, out_dtype, name, pieces)


def _rope_consts():
    inv = ROPE_THETA ** (-np.arange(ROT_HALF, dtype=np.float64) / ROT_HALF)
    invf = np.zeros((1, LANES), np.float32)
    invf[0, :ROT_HALF] = inv
    invf[0, ROT_HALF:ROT_DIM] = inv
    sgn = np.zeros((1, LANES), np.float32)
    sgn[0, :ROT_HALF] = -1.0
    sgn[0, ROT_HALF:ROT_DIM] = 1.0
    return jnp.asarray(invf), jnp.asarray(sgn)


def _rope_tables(pos_col, invf, sgn):
    ang = pos_col.astype(F32) * invf
    return jnp.cos(ang), jnp.sin(ang) * sgn


def _rope_apply(x, cos_t, sin_t):
    lane = lax.broadcasted_iota(jnp.int32, x.shape, 1)
    partner = jnp.where(lane < ROT_HALF,
                        pltpu.roll(x, LANES - ROT_HALF, 1),
                        pltpu.roll(x, ROT_HALF, 1))
    return x * cos_t + partner * sin_t


KEYS_WIDTH = 6 * HEAD_DIM


def _nsa_prep_kernel(q_ref, kvs_ref, kvw_ref, pos_ref, invf_ref, sgn_ref, qt_ref, keys_ref, vt_ref, *, seq_len):
    tm = q_ref.shape[0]
    cos_t, sin_t = _rope_tables(pos_ref[...], invf_ref[...], sgn_ref[...])
    qscale = (HEAD_DIM ** -0.5) * math.log2(math.e)
    for h in range(NSA_HEADS):
        sl = slice(h * HEAD_DIM, (h + 1) * HEAD_DIM)
        xq = _rope_apply(q_ref[:, sl].astype(F32), cos_t, sin_t) * qscale
        qt_ref[sl, :] = xq.T.astype(qt_ref.dtype)
    tok = (pl.program_id(0) * tm) % seq_len + lax.broadcasted_iota(jnp.int32, (tm, LANES), 0)
    lane = lax.broadcasted_iota(jnp.int32, (tm, LANES), 1)
    onehot = jnp.where(lane == (tok >> int(math.log2(SLC_LEN))), 1.0, 0.0).astype(keys_ref.dtype)
    blk = lambda i: (kvs_ref if i < 4 else kvw_ref)[:, (i % 4) * HEAD_DIM:(i % 4 + 1) * HEAD_DIM]
    for g in range(NSA_GROUPS):
        ks = _rope_apply(blk(g), cos_t, sin_t)
        keys_ref[:, (2 * g) * HEAD_DIM:(2 * g + 1) * HEAD_DIM] = ks.astype(keys_ref.dtype)
        keys_ref[:, (2 * g + 1) * HEAD_DIM:(2 * g + 2) * HEAD_DIM] = onehot
        kw = _rope_apply(blk(4 + g), cos_t, sin_t)
        keys_ref[:, (4 + g) * HEAD_DIM:(5 + g) * HEAD_DIM] = kw.astype(keys_ref.dtype)
        ones = jnp.ones((VT_ROWS - HEAD_DIM, tm), vt_ref.dtype)
        for c, src in ((g, 2 + g), (2 + g, 6 + g)):
            vt_ref[c * VT_ROWS:c * VT_ROWS + HEAD_DIM, :] = blk(src).T.astype(vt_ref.dtype)
            vt_ref[c * VT_ROWS + HEAD_DIM:(c + 1) * VT_ROWS, :] = ones


def _nsa_prep(pbig, ptail, pos_col, q_col_block, S):
    M = pbig.shape[0]
    tm = ROW_TILE
    invf, sgn = _rope_consts()
    return pl.pallas_call(
        functools.partial(_nsa_prep_kernel, seq_len=S),
        grid=(M // tm,),
        in_specs=[pl.BlockSpec((tm, NSA_WIDTH), lambda i: (i, q_col_block)),
                  pl.BlockSpec((tm, 4 * HEAD_DIM), lambda i: (i, 1)),
                  pl.BlockSpec((tm, 4 * HEAD_DIM), lambda i: (i, 2)),
                  pl.BlockSpec((tm, 1), lambda i: (i, 0)),
                  pl.BlockSpec((1, LANES), lambda i: (0, 0)),
                  pl.BlockSpec((1, LANES), lambda i: (0, 0))],
        out_specs=[pl.BlockSpec((NSA_WIDTH, tm), lambda i: (0, i)),
                   pl.BlockSpec((tm, KEYS_WIDTH), lambda i: (i, 0)),
                   pl.BlockSpec((4 * VT_ROWS, tm), lambda i: (0, i))],
        out_shape=[jax.ShapeDtypeStruct((NSA_WIDTH, M), BF16),
                   jax.ShapeDtypeStruct((M, KEYS_WIDTH), BF16),
                   jax.ShapeDtypeStruct((4 * VT_ROWS, M), BF16)],
        compiler_params=_cparams(("arbitrary",)),
        name="nsa_prep",
    )(pbig, ptail, ptail, pos_col, invf, sgn)


def _compress_kernel(x_ref, pe_ref, w1_ref, w2_ref, pos_ref, invf_ref, sgn_ref, o_ref, *, rope, n_rows):
    half = CMP_LEN // 2
    top = jnp.zeros((n_rows, CMP_HIDDEN), F32)
    bot = jnp.zeros((n_rows, CMP_HIDDEN), F32)
    for l in range(half):
        xl = x_ref[pl.ds(l, n_rows, stride=CMP_STRIDE), :]
        w_top = w1_ref[l * HEAD_DIM:(l + 1) * HEAD_DIM, :].astype(BF16)
        w_bot = w1_ref[(half + l) * HEAD_DIM:(half + l + 1) * HEAD_DIM, :].astype(BF16)
        top = top + _dot((xl + pe_ref[l:l + 1, :]).astype(BF16), w_top)
        bot = bot + _dot((xl + pe_ref[half + l:half + l + 1, :]).astype(BF16), w_bot)
    hid = top + pltpu.roll(bot, n_rows - 1, 0)
    act = _silu(hid).astype(BF16)
    if rope:
        out = _dot(act, w2_ref[...].astype(BF16))
        cos_t, sin_t = _rope_tables(pos_ref[0], invf_ref[...], sgn_ref[...])
        o_ref[0, 0] = _rope_apply(out, cos_t, sin_t).astype(o_ref.dtype)
    else:
        o_ref[0, 0, 0:HEAD_DIM, :] = _dot_nt(w2_ref[...].astype(BF16), act).astype(o_ref.dtype)
        o_ref[0, 0, HEAD_DIM:VT_ROWS, :] = jnp.ones((VT_ROWS - HEAD_DIM, n_rows), o_ref.dtype)


def _compress(ptail, col_block0, pe, w1, w2, pos_cmp, B, S, rope):
    n_rows = S // CMP_STRIDE
    invf, sgn = _rope_consts()
    kern = functools.partial(_compress_kernel, rope=rope, n_rows=n_rows)
    out_dims = (n_rows, HEAD_DIM) if rope else (VT_ROWS, n_rows)
    return pl.pallas_call(
        kern,
        grid=(B, NSA_GROUPS),
        in_specs=[pl.BlockSpec((S, HEAD_DIM), lambda b, g: (b, col_block0 + g)),
                  pl.BlockSpec((CMP_LEN, HEAD_DIM), lambda b, g: (0, 0)),
                  pl.BlockSpec((CMP_LEN * HEAD_DIM, CMP_HIDDEN), lambda b, g: (0, 0)),
                  pl.BlockSpec(w2.shape, lambda b, g: (0, 0)),
                  pl.BlockSpec((1, n_rows, 1), lambda b, g: (b, 0, 0)),
                  pl.BlockSpec((1, LANES), lambda b, g: (0, 0)),
                  pl.BlockSpec((1, LANES), lambda b, g: (0, 0))],
        out_specs=pl.BlockSpec((1, 1) + out_dims, lambda b, g: (b, g, 0, 0)),
        out_shape=jax.ShapeDtypeStruct((B, NSA_GROUPS) + out_dims, BF16),
        compiler_params=_cparams(("arbitrary", "arbitrary")),
        name="compress_k" if rope else "compress_v",
    )(ptail, pe, w1, w2, pos_cmp, invf, sgn)


def _tile_lanes(x, n):
    return jnp.concatenate([x] * n, axis=1)


def _nsa_attn_kernel(qt_ref, kc_ref, vct_ref, ks_ref, vst_ref, kw_ref, vwt_ref, g_ref, z_ref,
                     ovt_ref, o_ref, m_s, acc_s, qa_s, sa_s, sb_s, sc_s, *, seq_len):
    T = Q_BLOCK
    P = NSA_HPG
    R = P * T
    n_cmp_pad = seq_len // CMP_STRIDE
    n_cmp = n_cmp_pad - 1
    nb = seq_len // SLC_LEN
    qi = pl.program_id(2)
    qs = qi * T
    d0 = pl.multiple_of(qs, T)
    wlen = WINDOW + T
    w0 = pl.multiple_of(jnp.maximum(qs - WINDOW, 0), T)

    qt = jnp.concatenate([qt_ref[p * HEAD_DIM:(p + 1) * HEAD_DIM, :] for p in range(P)], axis=1)

    s_c = _dot(kc_ref[0, 0], qt)
    s_w = _dot(kw_ref[pl.ds(w0, wlen), :], qt)
    s_d = _dot(ks_ref[pl.ds(d0, T), 0:HEAD_DIM], qt)

    tok = lambda n: qs + lax.broadcasted_iota(jnp.int32, (n, T), 1)
    row = lambda n: lax.broadcasted_iota(jnp.int32, (n, T), 0)
    ok_c = (row(n_cmp_pad) * CMP_STRIDE + (CMP_LEN - 1) <= tok(n_cmp_pad)) & (row(n_cmp_pad) < n_cmp)
    key_w = w0 + row(wlen)
    ok_w = (key_w <= tok(wlen)) & (tok(wlen) - key_w < WINDOW)
    ok_d = row(T) <= lax.broadcasted_iota(jnp.int32, (T, T), 1)
    addmask = lambda ok: _tile_lanes(jnp.where(ok, 0.0, NEG).astype(F32), P)
    s_c = s_c + addmask(ok_c)
    s_w = s_w + addmask(ok_w)
    s_d = s_d + addmask(ok_d)

    m_c = jnp.max(s_c, axis=0, keepdims=True)
    m_w = jnp.max(s_w, axis=0, keepdims=True)
    m_d = jnp.max(s_d, axis=0, keepdims=True)
    m_c = jnp.where(m_c > 0.5 * NEG, m_c, 0.0)
    p_c = jnp.exp2(s_c - m_c)
    p_w = jnp.exp2(s_w - m_w)
    p_d = jnp.exp2(s_d - m_d)
    o_c = _dot(vct_ref[0, 0], p_c.astype(BF16))
    o_w = _dot(vwt_ref[:, pl.ds(w0, wlen)], p_w.astype(BF16))
    inv_c = 1.0 / jnp.maximum(jnp.sum(p_c, axis=0, keepdims=True), 1e-30)
    inv_w = 1.0 / o_w[HEAD_DIM:HEAD_DIM + 1, :]
    m_s[...] = m_d
    acc_s[...] = _dot(vst_ref[:, pl.ds(d0, T)], p_d.astype(BF16))

    pn = p_c * inv_c
    p_sum = pn[:, 0:T]
    for p in range(1, P):
        p_sum = p_sum + pn[:, p * T:(p + 1) * T]
    ps_hi = p_sum.astype(BF16)
    ps_lo = (p_sum - ps_hi.astype(F32)).astype(BF16)
    imp = (_dot(ovt_ref[...], ps_hi) + _dot(ovt_ref[...], ps_lo))[0:nb]
    jrow = row(nb)
    t_lane = tok(nb)
    tb = t_lane >> int(math.log2(SLC_LEN))
    visible = jrow * SLC_LEN <= t_lane
    forced = (jrow == 0) | (jrow == tb) | (jrow == tb - 1)
    score = jnp.where(forced, 1e9, jnp.where(visible, imp, -jnp.inf))
    cnt = jnp.zeros((nb, T), F32)
    for k in range(nb):
        rk = score[k:k + 1, :]
        cnt = cnt + jnp.where(jrow > k, jnp.where(rk >= score, 1.0, 0.0), jnp.where(rk > score, 1.0, 0.0))
    keep = (cnt < float(min(SLC_TOPK, nb))) & visible & (jrow < 2 * qi)
    bias = jnp.where(keep, 0.0, NEG).astype(F32)
    bias = jnp.concatenate([bias, jnp.full((LANES - nb, T), NEG, F32)], axis=0) if nb < LANES else bias
    qa_s[0:HEAD_DIM, :] = qt
    qa_s[HEAD_DIM:2 * HEAD_DIM, :] = _tile_lanes(bias.astype(BF16), P)

    n_t = (qs + SEL_TILE - 1) // SEL_TILE

    def scores_into(kt, dst):
        dst[...] = _dot(ks_ref[kt * SEL_TILE:(kt + 1) * SEL_TILE, :], qa_s[...])

    def consume(kt, src):
        k0 = kt * SEL_TILE
        s = src[...]
        m_prev = m_s[...]
        m_new = jnp.maximum(m_prev, jnp.max(s, axis=0, keepdims=True))
        alpha = jnp.exp2(m_prev - m_new)
        p = jnp.exp2(s - m_new)
        acc_s[...] = alpha * acc_s[...] + _dot(vst_ref[:, pl.ds(k0, SEL_TILE)], p.astype(BF16))
        m_s[...] = m_new

    bufs = (sa_s, sb_s, sc_s)

    def finish():
        gates_t = _sigmoid(g_ref[...]).T
        first_group = pl.program_id(1) == 0

        def grow(r):
            rows = []
            for p in range(P):
                h0 = r * NSA_HEADS + p
                rows.append(jnp.where(first_group, gates_t[h0:h0 + 1, :], gates_t[h0 + P:h0 + P + 1, :]))
            return jnp.concatenate(rows, axis=1)

        o_s = acc_s[...]
        hd = slice(0, HEAD_DIM)
        o_t = (o_c[hd] * (grow(0) * inv_c) + o_s[hd] * (grow(1) / o_s[HEAD_DIM:HEAD_DIM + 1, :])
               + o_w[hd] * (grow(2) * inv_w))
        for p in range(P):
            sl = slice(p * HEAD_DIM, (p + 1) * HEAD_DIM)
            o_p = o_t[:, p * T:(p + 1) * T].T
            o_ref[:, sl] = (o_p * _silu(z_ref[:, sl].astype(F32))).astype(o_ref.dtype)

    for r in range(0, seq_len // SEL_TILE + 1):
        @pl.when(n_t == r)
        def _(r=r):
            if r > 0:
                scores_into(0, bufs[0])
            for t in range(r):
                if t + 1 < r:
                    scores_into(t + 1, bufs[(t + 1) % len(bufs)])
                consume(t, bufs[t % len(bufs)])
            finish()


def _nsa_consts(S):
    n_cmp_pad = S // CMP_STRIDE
    nb = S // SLC_LEN
    cmp_start = np.arange(n_cmp_pad) * CMP_STRIDE
    slc_start = np.arange(LANES) * SLC_LEN
    ov = ((cmp_start[:, None] <= slc_start[None, :] + SLC_LEN - 1)
          & (cmp_start[:, None] + CMP_LEN - 1 >= slc_start[None, :])
          & (np.arange(LANES)[None, :] < nb)
          & (np.arange(n_cmp_pad)[:, None] < n_cmp_pad - 1)).astype(np.float32)
    return jnp.asarray(ov.T, dtype=BF16)


def _nsa_attn(q_t, kcmp, vcmp_t, keys, v_t, psmall, pbig, z_col_block, B, S):
    M = B * S
    nq = S // Q_BLOCK
    gw = NSA_HPG * HEAD_DIM
    ovt = _nsa_consts(S)
    n_cmp_pad = S // CMP_STRIDE
    R = NSA_HPG * Q_BLOCK
    kern = functools.partial(_nsa_attn_kernel, seq_len=S)
    return pl.pallas_call(
        kern,
        grid=(B, NSA_GROUPS, nq),
        in_specs=[pl.BlockSpec((gw, Q_BLOCK), lambda b, g, i: (g, b * nq + i)),
                  pl.BlockSpec((1, 1, n_cmp_pad, HEAD_DIM), lambda b, g, i: (b, g, 0, 0)),
                  pl.BlockSpec((1, 1, VT_ROWS, n_cmp_pad), lambda b, g, i: (b, g, 0, 0)),
                  pl.BlockSpec((S, 2 * HEAD_DIM), lambda b, g, i: (b, g)),
                  pl.BlockSpec((VT_ROWS, S), lambda b, g, i: (g, b)),
                  pl.BlockSpec((S, HEAD_DIM), lambda b, g, i: (b, 4 + g)),
                  pl.BlockSpec((VT_ROWS, S), lambda b, g, i: (2 + g, b)),
                  pl.BlockSpec((Q_BLOCK, LANES), lambda b, g, i: (b * nq + i, SMALL_GATE_BLOCK)),
                  pl.BlockSpec((Q_BLOCK, gw), lambda b, g, i: (b * nq + i, z_col_block * NSA_GROUPS + g)),
                  pl.BlockSpec((LANES, n_cmp_pad), lambda b, g, i: (0, 0))],
        out_specs=pl.BlockSpec((Q_BLOCK, gw), lambda b, g, i: (b * nq + i, g)),
        out_shape=jax.ShapeDtypeStruct((M, NSA_WIDTH), BF16),
        scratch_shapes=[pltpu.VMEM((1, R), F32),
                        pltpu.VMEM((VT_ROWS, R), F32),
                        pltpu.VMEM((2 * HEAD_DIM, R), BF16),
                        pltpu.VMEM((SEL_TILE, R), F32),
                        pltpu.VMEM((SEL_TILE, R), F32),
                        pltpu.VMEM((SEL_TILE, R), F32)],
        compiler_params=_cparams(("arbitrary", "arbitrary", "arbitrary")),
        name="nsa_attn",
    )(q_t, kcmp, vcmp_t, keys, v_t, keys, v_t, psmall, pbig, ovt)


def _dn_prep_kernel(x_ref, w_ref, o_ref, *rest, mode, seq_len):
    xpad_ref = rest[-1]
    w = w_ref[...]
    xpad_ref[0:CONV_PAD, :] = jnp.zeros((CONV_PAD, xpad_ref.shape[1]), F32)
    xpad_ref[CONV_PAD:, :] = x_ref[...].astype(F32)
    y = xpad_ref[CONV_PAD:, :] * w[CONV_WIDTH - 1:CONV_WIDTH, :]
    for k in range(1, CONV_WIDTH):
        y = y + xpad_ref[pl.ds(CONV_PAD - k, seq_len), :] * w[CONV_WIDTH - 1 - k:CONV_WIDTH - k, :]
    y = _silu(y)
    if mode in ("q", "k"):
        outs = []
        for h in range(y.shape[1] // DN_DK):
            yh = y[:, h * DN_DK:(h + 1) * DN_DK]
            ss = jnp.sum(yh * yh, axis=-1, keepdims=True)
            yh = yh * lax.rsqrt(ss + EPS)
            if mode == "q":
                yh = yh * (DN_DK ** -0.5)
            outs.append(yh)
        y = jnp.concatenate(outs, axis=1)
    o_ref[...] = y.astype(o_ref.dtype)
    if mode == "k":
        kt_ref = rest[0]
        kt_ref[...] = y.T.astype(kt_ref.dtype)


DN_PREP_COLS = 128
CONV_PAD = 8


def _dn_prep(pbig, conv_w, col0, mode, B, S):
    M = B * S
    tw = DN_PREP_COLS
    col0_blocks = col0 // tw
    nj = DN_WIDTH // tw
    wcol0 = {"q": 0, "k": nj, "v": 2 * nj}[mode]
    kern = functools.partial(_dn_prep_kernel, mode=mode, seq_len=S)
    out_specs = [pl.BlockSpec((S, tw), lambda b, j: (b, j))]
    out_shape = [jax.ShapeDtypeStruct((M, DN_WIDTH), BF16)]
    if mode == "k":
        out_specs.append(pl.BlockSpec((tw, S), lambda b, j: (j, b)))
        out_shape.append(jax.ShapeDtypeStruct((DN_WIDTH, M), BF16))
    return pl.pallas_call(
        kern,
        grid=(B, nj),
        in_specs=[pl.BlockSpec((S, tw), lambda b, j: (b, col0_blocks + j)),
                  pl.BlockSpec((CONV_WIDTH, tw), lambda b, j: (0, wcol0 + j))],
        out_specs=out_specs,
        out_shape=out_shape,
        scratch_shapes=[pltpu.VMEM((CONV_PAD + S, tw), F32)],
        compiler_params=_cparams(("arbitrary", "arbitrary")),
        name="dn_prep_" + mode,
    )(pbig, conv_w)


def _dn_gate_kernel(ab_ref, alog_ref, dtb_ref, o_ref, ot_ref, *, tm):
    ab = ab_ref[...]
    x = ab + dtb_ref[...]
    softplus = jnp.maximum(x, 0.0) + jnp.log(1.0 + jnp.exp(-jnp.abs(x)))
    g = -jnp.exp(alog_ref[...]) * softplus
    beta = _sigmoid(ab)
    lane = lax.broadcasted_iota(jnp.int32, (DN_CHUNK, LANES), 1)
    r = lax.broadcasted_iota(jnp.int32, (DN_CHUNK, DN_CHUNK), 0)
    c = lax.broadcasted_iota(jnp.int32, (DN_CHUNK, DN_CHUNK), 1)
    tril = jnp.where(r >= c, 1.0, 0.0).astype(F32)
    for ci in range(tm // DN_CHUNK):
        rows = slice(ci * DN_CHUNK, (ci + 1) * DN_CHUNK)
        dec = _dot_f32(tril, g[rows])
        out = jnp.where(lane < DN_HEADS, dec, beta[rows])
        o_ref[rows, :] = out
        ot_ref[:, rows] = out.T


def _dn_gate(psmall, alog_row, dtb_row):
    M = psmall.shape[0]
    tm = 512
    kern = functools.partial(_dn_gate_kernel, tm=tm)
    return pl.pallas_call(
        kern,
        grid=(M // tm,),
        in_specs=[pl.BlockSpec((tm, LANES), lambda i: (i, SMALL_AB_BLOCK)),
                  pl.BlockSpec((1, LANES), lambda i: (0, 0)),
                  pl.BlockSpec((1, LANES), lambda i: (0, 0))],
        out_specs=[pl.BlockSpec((tm, LANES), lambda i: (i, 0)),
                   pl.BlockSpec((LANES, tm), lambda i: (0, i))],
        out_shape=[jax.ShapeDtypeStruct((M, LANES), F32),
                   jax.ShapeDtypeStruct((LANES, M), F32)],
        compiler_params=_cparams(("arbitrary",)),
        name="dn_gate",
    )(psmall, alog_row, dtb_row)


def _dn_scan_kernel(q_ref, k_ref, v_ref, kt_ref, dec_ref, dect_ref, z_ref, gain_ref, o_ref, st_ref):
    C = DN_CHUNK

    @pl.when(pl.program_id(1) == 0)
    def _():
        st_ref[...] = jnp.zeros(st_ref.shape, F32)

    dec = dec_ref[...]
    dect = dect_ref[...]
    r = lax.broadcasted_iota(jnp.int32, (C, C), 0)
    c = lax.broadcasted_iota(jnp.int32, (C, C), 1)
    tril = r >= c
    strict = r > c
    gain = gain_ref[...]
    eye = jnp.where(r == c, 1.0, 0.0).astype(F32)
    lvl_masks = []
    for lg in range(int(math.log2(C))):
        lvl_masks.append(((r >> (lg + 1)) == (c >> (lg + 1))) & ((r >> lg) != (c >> lg)))

    for h0 in range(0, DN_HEADS, DN_HEAD_GROUP):
        heads = range(h0, h0 + DN_HEAD_GROUP)
        cols = lambda ref: jnp.stack([ref[:, h * DN_DK:(h + 1) * DN_DK] for h in heads])
        qh = cols(q_ref).astype(F32)
        kh = cols(k_ref).astype(F32)
        vh = cols(v_ref).astype(F32)
        kth = jnp.stack([kt_ref[h * DN_DK:(h + 1) * DN_DK, :] for h in heads])
        dcol = jnp.stack([jnp.broadcast_to(dec[:, h:h + 1], (C, C)) for h in heads])
        bcol = jnp.stack([jnp.broadcast_to(dec[:, DN_HEADS + h:DN_HEADS + h + 1], (C, C)) for h in heads])
        drow = jnp.stack([jnp.broadcast_to(dect[h:h + 1, :], (C, C)) for h in heads])
        dlast = jnp.stack([jnp.broadcast_to(dect[h:h + 1, C - 1:C], (C, C)) for h in heads])
        lmat = jnp.exp(jnp.where(tril[None], dcol - drow, NEG))
        e_d = jnp.exp(dcol)
        kb = kh * bcol
        vb = vh * bcol
        a = jnp.where(strict[None], _bmm(kb.astype(BF16), kth) * lmat, 0.0)
        attn = jnp.where(tril[None], _bmm(qh.astype(BF16), kth) * lmat, 0.0)
        tinv = eye[None] - jnp.where(lvl_masks[0][None], a, 0.0)
        for lm in lvl_masks[1:]:
            t16 = tinv.astype(BF16)
            lo = jnp.where(lm[None], a, 0.0).astype(BF16)
            tinv = tinv - _bmm(t16, _bmm(lo, t16).astype(BF16))
        t16 = tinv.astype(BF16)
        u = _bmm(t16, vb.astype(BF16))
        w = _bmm(t16, (kb * e_d).astype(BF16))
        st = st_ref[h0:h0 + DN_HEAD_GROUP]
        st16 = st.astype(BF16)
        v_new = u - _bmm(w.astype(BF16), st16)
        v_new16 = v_new.astype(BF16)
        o = _bmm((qh * e_d).astype(BF16), st16) + _bmm(attn.astype(BF16), v_new16)
        kdt = (kth.astype(F32) * jnp.exp(dlast - drow)).astype(BF16)
        st_ref[h0:h0 + DN_HEAD_GROUP] = st * jnp.exp(dlast) + _bmm(kdt, v_new16)
        ms = jnp.mean(o * o, axis=-1, keepdims=True)
        y = o * lax.rsqrt(ms + EPS) * gain[None]
        for i, h in enumerate(heads):
            sl = slice(h * DN_DV, (h + 1) * DN_DV)
            o_ref[:, sl] = (y[i] * _silu(z_ref[:, sl].astype(F32))).astype(o_ref.dtype)


def _dn_scan(dq, dk, dv, dkt, dec, dect, pbig, z_col_block, gain_row, B, S):
    M = B * S
    C = DN_CHUNK
    nc = S // C
    tok = lambda b, n: (b * nc + n, 0)
    return pl.pallas_call(
        _dn_scan_kernel,
        grid=(B, nc),
        in_specs=[pl.BlockSpec((C, DN_WIDTH), tok),
                  pl.BlockSpec((C, DN_WIDTH), tok),
                  pl.BlockSpec((C, DN_WIDTH), tok),
                  pl.BlockSpec((DN_WIDTH, C), lambda b, n: (0, b * nc + n)),
                  pl.BlockSpec((C, LANES), tok),
                  pl.BlockSpec((LANES, C), lambda b, n: (0, b * nc + n)),
                  pl.BlockSpec((C, DN_WIDTH), lambda b, n: (b * nc + n, z_col_block)),
                  pl.BlockSpec((1, DN_DV), lambda b, n: (0, 0))],
        out_specs=pl.BlockSpec((C, DN_WIDTH), tok),
        out_shape=jax.ShapeDtypeStruct((M, DN_WIDTH), BF16),
        scratch_shapes=[pltpu.VMEM((DN_HEADS, DN_DK, DN_DV), F32)],
        compiler_params=_cparams(("arbitrary", "arbitrary")),
        name="dn_scan",
    )(dq, dk, dv, dkt, dec, dect, pbig, gain_row)


def _out1_kernel(oa_ref, ob_ref, wa_hbm, wb_hbm, ga_ref, gb_ref, o_ref, wf32_ref, w16_ref, sems, *, tn):
    j = pl.program_id(0)
    i = pl.program_id(1)
    hw = tn // 2

    def half_copy(jj, b, h):
        src = (wa_hbm, wb_hbm)[b].at[:, pl.ds(pl.multiple_of(jj * tn + h * hw, hw), hw)]
        return pltpu.make_async_copy(src, wf32_ref.at[b], sems.at[b])

    def start_half(jj, h):
        half_copy(jj, 0, h).start()
        half_copy(jj, 1, h).start()

    def merged(cols):
        ya = _dot(oa_ref[...], w16_ref[0, :, cols])
        yb = _dot(ob_ref[...], w16_ref[1, :, cols])
        mix = _sigmoid(ga_ref[:, cols].astype(F32)) * ya + _sigmoid(gb_ref[:, cols].astype(F32)) * yb
        o_ref[:, cols] = mix.astype(o_ref.dtype)

    @pl.when((j == 0) & (i == 0))
    def _():
        start_half(0, 0)

    @pl.when(i == 0)
    def _():
        cc = min(MM_CAST_COLS, hw)
        for h in range(2):
            half_copy(j, 0, h).wait()
            half_copy(j, 1, h).wait()
            for c0 in range(0, hw, cc):
                cols = slice(h * hw + c0, h * hw + c0 + cc)
                w16_ref[0, :, cols] = wf32_ref[0, :, c0:c0 + cc].astype(BF16)
                w16_ref[1, :, cols] = wf32_ref[1, :, c0:c0 + cc].astype(BF16)
                if h == 0 and c0 + cc == hw:
                    start_half(j, 1)
                merged(cols)

        @pl.when(j + 1 < pl.num_programs(0))
        def _():
            start_half(j + 1, 0)

    @pl.when(i > 0)
    def _():
        merged(slice(0, tn))


OUT1_TM = 1024


def _out1(o_a, o_b, wa, wb, pbig, D):
    M, K = o_a.shape
    tm = min(OUT1_TM, M)
    tn = min(MM_TN, D)
    nbd = D // tn
    return pl.pallas_call(
        functools.partial(_out1_kernel, tn=tn),
        grid=(D // tn, M // tm),
        in_specs=[pl.BlockSpec((tm, K), lambda j, i: (i, 0)),
                  pl.BlockSpec((tm, K), lambda j, i: (i, 0)),
                  pl.BlockSpec(memory_space=pl.ANY),
                  pl.BlockSpec(memory_space=pl.ANY),
                  pl.BlockSpec((tm, tn), lambda j, i: (i, j)),
                  pl.BlockSpec((tm, tn), lambda j, i: (i, nbd + j))],
        out_specs=pl.BlockSpec((tm, tn), lambda j, i: (i, j)),
        out_shape=jax.ShapeDtypeStruct((M, D), BF16),
        scratch_shapes=[pltpu.VMEM((2, K, tn // 2), F32), pltpu.VMEM((2, K, tn), BF16),
                        pltpu.SemaphoreType.DMA((2,))],
        compiler_params=_cparams(("arbitrary", "arbitrary")),
        name="out1",
    )(o_a, o_b, wa, wb, pbig, pbig)


def _final_kernel(mix_ref, x_ref, gate_ref, fg_ref, o_ref):
    xn = x_ref[...] + gate_ref[0] * mix_ref[...].astype(F32)
    ms = jnp.mean(xn * xn, axis=-1, keepdims=True)
    o_ref[...] = xn * lax.rsqrt(ms + EPS) * fg_ref[...]


def _final(mixed, x2, mod3, final_gain, S):
    M, D = x2.shape
    tm = ROW_TILE
    nb = S // tm
    return pl.pallas_call(
        _final_kernel,
        grid=(M // tm,),
        in_specs=[pl.BlockSpec((tm, D), lambda i: (i, 0)),
                  pl.BlockSpec((tm, D), lambda i: (i, 0)),
                  pl.BlockSpec((1, 1, D), lambda i: (i // nb, 0, 2)),
                  pl.BlockSpec((1, D), lambda i: (0, 0))],
        out_specs=pl.BlockSpec((tm, D), lambda i: (i, 0)),
        out_shape=jax.ShapeDtypeStruct((M, D), F32),
        compiler_params=_cparams(("arbitrary",)),
        name="final",
    )(mixed, x2, mod3, final_gain)


def _pad_cols(w, width):
    return jnp.pad(w, ((0, 0), (0, width - w.shape[1])))


def _proj_row_offsets(D):
    widths = (NSA_WIDTH, 6 * KV_WIDTH, 3 * NSA_HEADS, NSA_WIDTH, 3 * DN_WIDTH, DN_HEADS, DN_HEADS, DN_WIDTH, 2 * D)
    offs = [int(o) for o in np.concatenate([[0], np.cumsum(widths)])]
    tiles = lambda seg: [offs[seg] + t * MM_TN for t in range(widths[seg] // MM_TN)]
    big = tiles(8) + tiles(4) + tiles(0) + tiles(3) + tiles(7)
    kv0 = offs[1]
    tail = [kv0, kv0 + TAIL_TN]
    small = [offs[2], offs[5]]
    return big, tail, small


TAIL_TN = 3 * KV_WIDTH
TAIL_TM = 1024
TAIL_CMP_BLOCK = 0
SMALL_GATE_BLOCK = 0
SMALL_AB_BLOCK = 1


def kernel(x, c, positions, w_ada, b_ada, norm_gain, w_in, cmp_pos_k, cmp_pos_v, w_cmp_k1, w_cmp_k2,
           w_cmp_v1, w_cmp_v2, conv_w, dt_bias, a_log, dn_norm_gain, w_proj_a, w_proj_b, w_out, final_gain):
    B, S, D = x.shape
    M = B * S
    depth = w_in.shape[0]
    assert S % SEL_TILE == 0 and S % DN_CHUNK == 0 and S >= WINDOW + Q_BLOCK and D % 512 == 0 and B <= 8
    assert S // SLC_LEN <= LANES and D % 1024 == 0 and DN_DK == DN_CHUNK
    assert depth == 1, "the final RMSNorm is fused into the last layer's output kernel"

    off_dq = 2 * D
    off_q = off_dq + 3 * DN_WIDTH
    off_z = off_q + NSA_WIDTH
    off_dz = off_z + NSA_WIDTH

    x2 = x.reshape(M, D)
    pos_col = positions.reshape(M, 1)
    cmp_end = np.arange(S // CMP_STRIDE - 1) * CMP_STRIDE + CMP_LEN - 1
    pos_cmp = jnp.pad(positions[:, cmp_end], ((0, 0), (0, 1)))[:, :, None]
    c8 = jnp.pad(c, ((0, 8 - B), (0, 0)))

    for l in range(depth):
        mod = _ada(c8, w_ada[l], b_ada[l][None, :])
        mod3 = mod[:B].reshape(B, 1, 3 * D)
        h = _norm(x2, norm_gain[l][None, :], mod3, S)

        w_nk = jnp.transpose(w_in[l])
        big_rows, tail_rows, small_rows = _proj_row_offsets(D)
        pbig = _proj(h, w_nk, big_rows, MM_TM, MM_TN, BF16, "proj_big")
        ptail = _proj(h, w_nk, tail_rows, TAIL_TM, TAIL_TN, F32, "proj_tail")
        psmall = _proj(h, w_nk, small_rows, TAIL_TM, 2 * LANES, F32, "proj_small", pieces=2)

        q_t, keys, v_t = _nsa_prep(pbig, ptail, pos_col, off_q // NSA_WIDTH, S)
        kcmp = _compress(ptail, TAIL_CMP_BLOCK, cmp_pos_k[l], w_cmp_k1[l], w_cmp_k2[l], pos_cmp, B, S, True)
        vcmp = _compress(ptail, TAIL_CMP_BLOCK + NSA_GROUPS, cmp_pos_v[l], w_cmp_v1[l], w_cmp_v2[l].T,
                         pos_cmp, B, S, False)
        o_a = _nsa_attn(q_t, kcmp, vcmp, keys, v_t, psmall, pbig, off_z // NSA_WIDTH, B, S)

        cw = conv_w[l]
        dq = _dn_prep(pbig, cw, off_dq, "q", B, S)[0]
        dk, dkt = _dn_prep(pbig, cw, off_dq + DN_WIDTH, "k", B, S)
        dv = _dn_prep(pbig, cw, off_dq + 2 * DN_WIDTH, "v", B, S)[0]
        alog_row = _pad_cols(a_log[l][None, :].astype(F32), LANES)
        dtb_row = _pad_cols(dt_bias[l][None, :].astype(F32), LANES)
        dec, dect = _dn_gate(psmall, alog_row, dtb_row)
        o_b = _dn_scan(dq, dk, dv, dkt, dec, dect, pbig, off_dz // DN_WIDTH,
                       dn_norm_gain[l][None, :], B, S)

        mixed_in = _out1(o_a, o_b, w_proj_a[l], w_proj_b[l], pbig, D)
        mixed = _matmul(mixed_in, w_out[l], BF16, "out2")
        x2 = _final(mixed, x2, mod3, final_gain[None, :], S)
    return x2.reshape(B, S, D)
```
